```python
import jax
import jax.numpy as jnp
from jax import lax
import numpy as np

D_MODEL = 2048
BATCH = 8
SEQ = 2048
DEPTH = 4

CHUNK = 64
QBLOCK = 128
ROPE_BASE = 10000.0
MAX_POS_OFFSET = 4096
NORM_EPS = 1e-6

RET_HEADS = 8
RET_QK_DIM = 128
RET_V_DIM = 128
RET_WIDTH = RET_HEADS * RET_V_DIM

LRU_WIDTH = 1024
LRU_BLOCKS = 8
LRU_BLOCK_DIM = LRU_WIDTH // LRU_BLOCKS
CONV_WIDTH = 4
LRU_C = 8.0

MLA_HEADS = 8
MLA_NOPE_DIM = 128
MLA_ROPE_DIM = 64
MLA_V_DIM = 128
MLA_Q_LORA = 512
MLA_KV_LORA = 512
MLA_WIDTH = MLA_HEADS * MLA_V_DIM

N_BRANCH = 3
MIX_WIDTH = RET_WIDTH + LRU_WIDTH + MLA_WIDTH
IN_SPLITS = (
    RET_HEADS * RET_QK_DIM,
    RET_HEADS * RET_QK_DIM,
    RET_WIDTH,
    RET_WIDTH,
    LRU_WIDTH,
    LRU_WIDTH,
    MLA_Q_LORA,
    MLA_KV_LORA,
    MLA_ROPE_DIM,
    MLA_WIDTH,
    N_BRANCH * D_MODEL,
)
IN_WIDTH = sum(IN_SPLITS)

kernel_name = 'hybrid_retention_rglru_mla_streaming_block'


def rms_norm(x, gain):
    xf = x.astype(jnp.float32)
    y = xf * lax.rsqrt(jnp.mean(xf * xf, axis=-1, keepdims=True) + NORM_EPS)
    return (y * gain.astype(jnp.float32)).astype(x.dtype)


def rope_tables(positions, dim):
    inv_freq = ROPE_BASE ** (-jnp.arange(0, dim, 2, dtype=jnp.float32) / dim)
    ang = positions.astype(jnp.float32)[:, :, None, None] * inv_freq
    return jnp.cos(ang), jnp.sin(ang)


def apply_rope(x, cos, sin):
    half = x.shape[-1] // 2
    xf = x.astype(jnp.float32)
    x1, x2 = xf[..., :half], xf[..., half:]
    return jnp.concatenate([x1 * cos - x2 * sin, x2 * cos + x1 * sin], axis=-1).astype(x.dtype)


def retention_branch(q, k, v, gate, gn, cos, sin):
    B, S = q.shape[:2]
    NC = S // CHUNK
    q = apply_rope(q.reshape(B, S, RET_HEADS, RET_QK_DIM), cos, sin) * (RET_QK_DIM ** -0.5)
    k = apply_rope(k.reshape(B, S, RET_HEADS, RET_QK_DIM), cos, sin)
    q = q.reshape(B, NC, CHUNK, RET_HEADS, RET_QK_DIM)
    k = k.reshape(B, NC, CHUNK, RET_HEADS, RET_QK_DIM)
    v = v.reshape(B, NC, CHUNK, RET_HEADS, RET_V_DIM)

    log_gamma = jnp.log1p(-jnp.exp2(-5.0 - jnp.arange(RET_HEADS, dtype=jnp.float32)))
    idx = jnp.arange(CHUNK, dtype=jnp.float32)
    intra_decay = jnp.exp(log_gamma[:, None, None] * jnp.abs(idx[:, None] - idx[None, :]))

    scores = jnp.einsum('bnihd,bnjhd->bhnij', q, k) * intra_decay[None, :, None]
    o_intra = jnp.einsum('bhnij,bnjhe->bnihe', scores, v)

    k_dec = k * jnp.exp(log_gamma[None, :] * (CHUNK - 1 - idx)[:, None])[None, None, :, :, None]
    kv_chunk = jnp.einsum('bnjhd,bnjhe->nbhde', k_dec, v)
    chunk_decay = jnp.exp(log_gamma * CHUNK)[None, :, None, None]

    def step(state, kv_n):
        return state * chunk_decay + kv_n, state

    _, prev_state = lax.scan(step, jnp.zeros(kv_chunk.shape[1:], kv_chunk.dtype), kv_chunk)
    q_dec = q * jnp.exp(log_gamma[None, :] * (idx + 1.0)[:, None])[None, None, :, :, None]
    o_inter = jnp.einsum('bnihd,nbhde->bnihe', q_dec, prev_state)

    o = (o_intra + o_inter).reshape(B, S, RET_HEADS, RET_V_DIM).astype(jnp.float32)
    mean = jnp.mean(o, axis=-1, keepdims=True)
    var = jnp.mean(jnp.square(o - mean), axis=-1, keepdims=True)
    o = ((o - mean) * lax.rsqrt(var + NORM_EPS)).reshape(B, S, RET_WIDTH) * gn.astype(jnp.float32)
    return o.astype(gate.dtype) * jax.nn.silu(gate)


def rglru_branch(xb, gate, conv_w, conv_b, wa, ba, wx, bx, lam):
    B, S, W = xb.shape
    xc = lax.conv_general_dilated(
        xb, conv_w[:, None, :].astype(xb.dtype), window_strides=(1,),
        padding=[(CONV_WIDTH - 1, 0)], dimension_numbers=('NWC', 'WIO', 'NWC'),
        feature_group_count=W) + conv_b
    xr = xc.reshape(B, S, LRU_BLOCKS, LRU_BLOCK_DIM)
    r = jax.nn.sigmoid(jnp.einsum('bsnc,ncd->bsnd', xr, wa).reshape(B, S, W) + ba)
    i = jax.nn.sigmoid(jnp.einsum('bsnc,ncd->bsnd', xr, wx).reshape(B, S, W) + bx)
    log_a = -LRU_C * r.astype(jnp.float32) * jax.nn.softplus(-lam.astype(jnp.float32))
    a = jnp.exp(log_a)
    b = jnp.sqrt(-jnp.expm1(2.0 * log_a)) * (i * xc).astype(jnp.float32)

    def combine(left, right):
        a1, b1 = left
        a2, b2 = right
        return a1 * a2, a2 * b1 + b2

    _, h = lax.associative_scan(combine, (a, b), axis=1)
    return h.astype(xb.dtype) * jax.nn.silu(gate)


def mla_branch(q_lat, kv_lat, k_rope, gate, q_norm, w_uq, kv_norm, w_ukv, cos, sin):
    B, S = q_lat.shape[:2]
    q = (rms_norm(q_lat, q_norm) @ w_uq).reshape(B, S, MLA_HEADS, MLA_NOPE_DIM + MLA_ROPE_DIM)
    q_nope = q[..., :MLA_NOPE_DIM]
    q_rope = apply_rope(q[..., MLA_NOPE_DIM:], cos, sin)
    kv = (rms_norm(kv_lat, kv_norm) @ w_ukv).reshape(B, S, MLA_HEADS, MLA_NOPE_DIM + MLA_V_DIM)
    k_nope, v = kv[..., :MLA_NOPE_DIM], kv[..., MLA_NOPE_DIM:]
    k_rope = apply_rope(k_rope[:, :, None, :], cos, sin)[:, :, 0]
    scale = (MLA_NOPE_DIM + MLA_ROPE_DIM) ** -0.5

    outs = []
    for qb in range(S // QBLOCK):
        qs, qe = qb * QBLOCK, (qb + 1) * QBLOCK
        s = (jnp.einsum('bqhd,bkhd->bhqk', q_nope[:, qs:qe], k_nope[:, :qe])
             + jnp.einsum('bqhr,bkr->bhqk', q_rope[:, qs:qe], k_rope[:, :qe]))
        s = s.astype(jnp.float32) * scale
        q_chunk = (qs + jnp.arange(QBLOCK)) // CHUNK
        k_chunk = jnp.arange(qe) // CHUNK
        mask = k_chunk[None, :] <= q_chunk[:, None]
        p = jax.nn.softmax(jnp.where(mask, s, -1e30), axis=-1).astype(v.dtype)
        outs.append(jnp.einsum('bhqk,bkhd->bqhd', p, v[:, :qe]))
    o = jnp.concatenate(outs, axis=1).reshape(B, S, MLA_WIDTH)
    return o * jax.nn.silu(gate)


def hybrid_layer(x, c_act, ada_w, ada_b, norm_pre, norm_post, w_in, ret_gn,
                 lru_conv_w, lru_conv_b, lru_wa, lru_ba, lru_wx, lru_bx, lru_lambda,
                 mla_q_norm, mla_w_uq, mla_kv_norm, mla_w_ukv, w_branch, w_out,
                 cos_ret, sin_ret, cos_mla, sin_mla):
    B, S, _ = x.shape
    mod = c_act @ ada_w + ada_b
    shift, scale, res_gate = jnp.split(mod, 3, axis=-1)
    h = rms_norm(x, norm_pre) * (1.0 + scale[:, None, :]) + shift[:, None, :]

    proj = h @ w_in
    offsets = [int(o) for o in np.cumsum(IN_SPLITS)[:-1]]
    (rq, rk, rv, rg, lx, lg, mq, mkv, mkr, mg, merge_logits) = jnp.split(proj, offsets, axis=-1)

    y_ret = retention_branch(rq, rk, rv, rg, ret_gn, cos_ret, sin_ret)
    y_lru = rglru_branch(lx, lg, lru_conv_w, lru_conv_b, lru_wa, lru_ba, lru_wx, lru_bx, lru_lambda)
    y_mla = mla_branch(mq, mkv, mkr, mg, mla_q_norm, mla_w_uq, mla_kv_norm, mla_w_ukv, cos_mla, sin_mla)

    gates = jax.nn.sigmoid(merge_logits.astype(jnp.float32)).astype(x.dtype).reshape(B, S, N_BRANCH, D_MODEL)
    wb_ret = w_branch[:RET_WIDTH]
    wb_lru = w_branch[RET_WIDTH:RET_WIDTH + LRU_WIDTH]
    wb_mla = w_branch[RET_WIDTH + LRU_WIDTH:]
    merged = (gates[:, :, 0] * (y_ret @ wb_ret)
              + gates[:, :, 1] * (y_lru @ wb_lru)
              + gates[:, :, 2] * (y_mla @ wb_mla))
    y = merged @ w_out
    return x + (1.0 + res_gate[:, None, :]) * rms_norm(y, norm_post)


def _fwd_setup_inputs(seed: int = 0) -> dict:
    key = jax.random.key(seed)
    ks = jax.random.split(key, 24)
    f32 = jnp.float32

    def nrm(k, shape, s):
        return jax.random.normal(k, shape, f32) * s

    x = nrm(ks[0], (BATCH, SEQ, D_MODEL), 1.0)
    c = nrm(ks[1], (BATCH, D_MODEL), 1.0)
    positions = (jnp.arange(SEQ, dtype=jnp.int32)[None, :]
                 + jax.random.randint(ks[2], (BATCH, 1), 0, MAX_POS_OFFSET, dtype=jnp.int32))
    ada_w = nrm(ks[3], (DEPTH, D_MODEL, 3 * D_MODEL), 0.5 * D_MODEL ** -0.5)
    ada_b = nrm(ks[4], (DEPTH, 3 * D_MODEL), 0.01)
    norm_pre = 1.0 + nrm(ks[5], (DEPTH, D_MODEL), 0.01)
    norm_post = 1.0 + nrm(ks[6], (DEPTH, D_MODEL), 0.01)
    w_in = nrm(ks[7], (DEPTH, D_MODEL, IN_WIDTH), D_MODEL ** -0.5)
    ret_gn = 1.0 + nrm(ks[8], (DEPTH, RET_WIDTH), 0.01)
    lru_conv_w = nrm(ks[9], (DEPTH, CONV_WIDTH, LRU_WIDTH), CONV_WIDTH ** -0.5)
    lru_conv_b = nrm(ks[10], (DEPTH, LRU_WIDTH), 0.01)
    lru_wa = nrm(ks[11], (DEPTH, LRU_BLOCKS, LRU_BLOCK_DIM, LRU_BLOCK_DIM), LRU_BLOCK_DIM ** -0.5)
    lru_ba = nrm(ks[12], (DEPTH, LRU_WIDTH), 0.01)
    lru_wx = nrm(ks[13], (DEPTH, LRU_BLOCKS, LRU_BLOCK_DIM, LRU_BLOCK_DIM), LRU_BLOCK_DIM ** -0.5)
    lru_bx = nrm(ks[14], (DEPTH, LRU_WIDTH), 0.01)
    u = jax.random.uniform(ks[15], (DEPTH, LRU_WIDTH), f32, 0.9, 0.999)
    a0 = u ** (1.0 / LRU_C)
    lru_lambda = jnp.log(a0) - jnp.log1p(-a0)
    mla_q_norm = 1.0 + nrm(ks[16], (DEPTH, MLA_Q_LORA), 0.01)
    mla_w_uq = nrm(ks[17], (DEPTH, MLA_Q_LORA, MLA_HEADS * (MLA_NOPE_DIM + MLA_ROPE_DIM)), MLA_Q_LORA ** -0.5)
    mla_kv_norm = 1.0 + nrm(ks[18], (DEPTH, MLA_KV_LORA), 0.01)
    mla_w_ukv = nrm(ks[19], (DEPTH, MLA_KV_LORA, MLA_HEADS * (MLA_NOPE_DIM + MLA_V_DIM)), MLA_KV_LORA ** -0.5)
    w_branch = nrm(ks[20], (DEPTH, MIX_WIDTH, D_MODEL), (MIX_WIDTH // N_BRANCH) ** -0.5)
    w_out = nrm(ks[21], (DEPTH, D_MODEL, D_MODEL), D_MODEL ** -0.5)
    return {'x': x, 'c': c, 'positions': positions, 'ada_w': ada_w, 'ada_b': ada_b,
            'norm_pre': norm_pre, 'norm_post': norm_post, 'w_in': w_in, 'ret_gn': ret_gn,
            'lru_conv_w': lru_conv_w, 'lru_conv_b': lru_conv_b, 'lru_wa': lru_wa, 'lru_ba': lru_ba,
            'lru_wx': lru_wx, 'lru_bx': lru_bx, 'lru_lambda': lru_lambda,
            'mla_q_norm': mla_q_norm, 'mla_w_uq': mla_w_uq, 'mla_kv_norm': mla_kv_norm,
            'mla_w_ukv': mla_w_ukv, 'w_branch': w_branch, 'w_out': w_out}


def _fwd_reference(x, c, positions, ada_w, ada_b, norm_pre, norm_post, w_in, ret_gn,
              lru_conv_w, lru_conv_b, lru_wa, lru_ba, lru_wx, lru_bx, lru_lambda,
              mla_q_norm, mla_w_uq, mla_kv_norm, mla_w_ukv, w_branch, w_out):
    c_act = jax.nn.silu(c)
    cos_ret, sin_ret = rope_tables(positions, RET_QK_DIM)
    cos_mla, sin_mla = rope_tables(positions, MLA_ROPE_DIM)
    for l in range(DEPTH):
        x = hybrid_layer(x, c_act, ada_w[l], ada_b[l], norm_pre[l], norm_post[l], w_in[l], ret_gn[l],
                         lru_conv_w[l], lru_conv_b[l], lru_wa[l], lru_ba[l], lru_wx[l], lru_bx[l],
                         lru_lambda[l], mla_q_norm[l], mla_w_uq[l], mla_kv_norm[l], mla_w_ukv[l],
                         w_branch[l], w_out[l], cos_ret, sin_ret, cos_mla, sin_mla)
    return x


import jax as _jax
import jax.numpy as _jnp

TWIN_FORMAT = 'train_step'
FWD_PARAMS = ['x', 'c', 'positions', 'ada_w', 'ada_b', 'norm_pre', 'norm_post', 'w_in', 'ret_gn', 'lru_conv_w', 'lru_conv_b', 'lru_wa', 'lru_ba', 'lru_wx', 'lru_bx', 'lru_lambda', 'mla_q_norm', 'mla_w_uq', 'mla_kv_norm', 'mla_w_ukv', 'w_branch', 'w_out']
TWIN_WEIGHTS = ['ada_w', 'ada_b', 'norm_pre', 'norm_post', 'w_in', 'ret_gn', 'lru_conv_w', 'lru_conv_b', 'lru_wa', 'lru_ba', 'lru_wx', 'lru_bx', 'lru_lambda', 'mla_q_norm', 'mla_w_uq', 'mla_kv_norm', 'mla_w_ukv', 'w_branch', 'w_out']
TWIN_DIFF_INPUT = 'x'
TWIN_INPUTS = ['x', 'c', 'positions', 'ada_w', 'ada_b', 'norm_pre', 'norm_post', 'w_in', 'ret_gn', 'lru_conv_w', 'lru_conv_b', 'lru_wa', 'lru_ba', 'lru_wx', 'lru_bx', 'lru_lambda', 'mla_q_norm', 'mla_w_uq', 'mla_kv_norm', 'mla_w_ukv', 'w_branch', 'w_out', 'loss_target', 'm_ada_w', 'm_ada_b', 'm_norm_pre', 'm_norm_post', 'm_w_in', 'm_ret_gn', 'm_lru_conv_w', 'm_lru_conv_b', 'm_lru_wa', 'm_lru_ba', 'm_lru_wx', 'm_lru_bx', 'm_lru_lambda', 'm_mla_q_norm', 'm_mla_w_uq', 'm_mla_kv_norm', 'm_mla_w_ukv', 'm_w_branch', 'm_w_out', 'v_ada_w', 'v_ada_b', 'v_norm_pre', 'v_norm_post', 'v_w_in', 'v_ret_gn', 'v_lru_conv_w', 'v_lru_conv_b', 'v_lru_wa', 'v_lru_ba', 'v_lru_wx', 'v_lru_bx', 'v_lru_lambda', 'v_mla_q_norm', 'v_mla_w_uq', 'v_mla_kv_norm', 'v_mla_w_ukv', 'v_w_branch', 'v_w_out']
TWIN_OUTPUTS = ['loss', 'grad_x', 'grad_ada_w', 'grad_ada_b', 'grad_norm_pre', 'grad_norm_post', 'grad_w_in', 'grad_ret_gn', 'grad_lru_conv_w', 'grad_lru_conv_b', 'grad_lru_wa', 'grad_lru_ba', 'grad_lru_wx', 'grad_lru_bx', 'grad_lru_lambda', 'grad_mla_q_norm', 'grad_mla_w_uq', 'grad_mla_kv_norm', 'grad_mla_w_ukv', 'grad_w_branch', 'grad_w_out', 'delta_ada_w', 'delta_ada_b', 'delta_norm_pre', 'delta_norm_post', 'delta_w_in', 'delta_ret_gn', 'delta_lru_conv_w', 'delta_lru_conv_b', 'delta_lru_wa', 'delta_lru_ba', 'delta_lru_wx', 'delta_lru_bx', 'delta_lru_lambda', 'delta_mla_q_norm', 'delta_mla_w_uq', 'delta_mla_kv_norm', 'delta_mla_w_ukv', 'delta_w_branch', 'delta_w_out', 'new_m_ada_w', 'new_m_ada_b', 'new_m_norm_pre', 'new_m_norm_post', 'new_m_w_in', 'new_m_ret_gn', 'new_m_lru_conv_w', 'new_m_lru_conv_b', 'new_m_lru_wa', 'new_m_lru_ba', 'new_m_lru_wx', 'new_m_lru_bx', 'new_m_lru_lambda', 'new_m_mla_q_norm', 'new_m_mla_w_uq', 'new_m_mla_kv_norm', 'new_m_mla_w_ukv', 'new_m_w_branch', 'new_m_w_out', 'new_v_ada_w', 'new_v_ada_b', 'new_v_norm_pre', 'new_v_norm_post', 'new_v_w_in', 'new_v_ret_gn', 'new_v_lru_conv_w', 'new_v_lru_conv_b', 'new_v_lru_wa', 'new_v_lru_ba', 'new_v_lru_wx', 'new_v_lru_bx', 'new_v_lru_lambda', 'new_v_mla_q_norm', 'new_v_mla_w_uq', 'new_v_mla_kv_norm', 'new_v_mla_w_ukv', 'new_v_w_branch', 'new_v_w_out']
TWIN_LEAF_KINDS = {'loss': 'loss', 'grad_x': 'grad_x', 'grad_ada_w': 'grad_w', 'grad_ada_b': 'grad_w', 'grad_norm_pre': 'grad_w', 'grad_norm_post': 'grad_w', 'grad_w_in': 'grad_w', 'grad_ret_gn': 'grad_w', 'grad_lru_conv_w': 'grad_w', 'grad_lru_conv_b': 'grad_w', 'grad_lru_wa': 'grad_w', 'grad_lru_ba': 'grad_w', 'grad_lru_wx': 'grad_w', 'grad_lru_bx': 'grad_w', 'grad_lru_lambda': 'grad_w', 'grad_mla_q_norm': 'grad_w', 'grad_mla_w_uq': 'grad_w', 'grad_mla_kv_norm': 'grad_w', 'grad_mla_w_ukv': 'grad_w', 'grad_w_branch': 'grad_w', 'grad_w_out': 'grad_w', 'delta_ada_w': 'delta_w', 'delta_ada_b': 'delta_w', 'delta_norm_pre': 'delta_w', 'delta_norm_post': 'delta_w', 'delta_w_in': 'delta_w', 'delta_ret_gn': 'delta_w', 'delta_lru_conv_w': 'delta_w', 'delta_lru_conv_b': 'delta_w', 'delta_lru_wa': 'delta_w', 'delta_lru_ba': 'delta_w', 'delta_lru_wx': 'delta_w', 'delta_lru_bx': 'delta_w', 'delta_lru_lambda': 'delta_w', 'delta_mla_q_norm': 'delta_w', 'delta_mla_w_uq': 'delta_w', 'delta_mla_kv_norm': 'delta_w', 'delta_mla_w_ukv': 'delta_w', 'delta_w_branch': 'delta_w', 'delta_w_out': 'delta_w', 'new_m_ada_w': 'new_m', 'new_m_ada_b': 'new_m', 'new_m_norm_pre': 'new_m', 'new_m_norm_post': 'new_m', 'new_m_w_in': 'new_m', 'new_m_ret_gn': 'new_m', 'new_m_lru_conv_w': 'new_m', 'new_m_lru_conv_b': 'new_m', 'new_m_lru_wa': 'new_m', 'new_m_lru_ba': 'new_m', 'new_m_lru_wx': 'new_m', 'new_m_lru_bx': 'new_m', 'new_m_lru_lambda': 'new_m', 'new_m_mla_q_norm': 'new_m', 'new_m_mla_w_uq': 'new_m', 'new_m_mla_kv_norm': 'new_m', 'new_m_mla_w_ukv': 'new_m', 'new_m_w_branch': 'new_m', 'new_m_w_out': 'new_m', 'new_v_ada_w': 'new_v', 'new_v_ada_b': 'new_v', 'new_v_norm_pre': 'new_v', 'new_v_norm_post': 'new_v', 'new_v_w_in': 'new_v', 'new_v_ret_gn': 'new_v', 'new_v_lru_conv_w': 'new_v', 'new_v_lru_conv_b': 'new_v', 'new_v_lru_wa': 'new_v', 'new_v_lru_ba': 'new_v', 'new_v_lru_wx': 'new_v', 'new_v_lru_bx': 'new_v', 'new_v_lru_lambda': 'new_v', 'new_v_mla_q_norm': 'new_v', 'new_v_mla_w_uq': 'new_v', 'new_v_mla_kv_norm': 'new_v', 'new_v_mla_w_ukv': 'new_v', 'new_v_w_branch': 'new_v', 'new_v_w_out': 'new_v'}


def _forward(args):
    return _fwd_reference(*[args[k] for k in FWD_PARAMS])


def _output_shape():
    out = _jax.eval_shape(lambda: _forward(_fwd_setup_inputs(0)))
    return out.shape, out.dtype

N_MICROBATCH = 1
ADAM_LR = 0.001
ADAM_B1 = 0.9
ADAM_B2 = 0.999
ADAM_EPS = 1e-08
ADAM_WD = 0.01
ADAM_STEP = 10
PER_EXAMPLE_BATCH_AXIS = {'x': 0, 'c': 0, 'positions': 0, 'loss_target': 0}
SHARED_INPUTS = []
_WEIGHT_DTYPES = {'ada_w': _jnp.float32, 'ada_b': _jnp.float32, 'norm_pre': _jnp.float32, 'norm_post': _jnp.float32, 'w_in': _jnp.float32, 'ret_gn': _jnp.float32, 'lru_conv_w': _jnp.float32, 'lru_conv_b': _jnp.float32, 'lru_wa': _jnp.float32, 'lru_ba': _jnp.float32, 'lru_wx': _jnp.float32, 'lru_bx': _jnp.float32, 'lru_lambda': _jnp.float32, 'mla_q_norm': _jnp.float32, 'mla_w_uq': _jnp.float32, 'mla_kv_norm': _jnp.float32, 'mla_w_ukv': _jnp.float32, 'w_branch': _jnp.float32, 'w_out': _jnp.float32}
MOMENT_SCALE = {'ada_w': 2.547020e+00, 'ada_b': 5.680263e+00, 'norm_pre': 9.026939e-01, 'norm_post': 9.365547e+00, 'w_in': 6.562278e-01, 'ret_gn': 2.539944e-01, 'lru_conv_w': 1.910806e+00, 'lru_conv_b': 5.488610e+00, 'lru_wa': 2.154791e-01, 'lru_ba': 2.351586e-01, 'lru_wx': 4.583727e-01, 'lru_bx': 7.172623e-01, 'lru_lambda': 5.701537e-01, 'mla_q_norm': 5.367516e-02, 'mla_w_uq': 3.067795e-02, 'mla_kv_norm': 4.214874e-01, 'mla_w_ukv': 2.177389e-01, 'w_branch': 9.346449e-01, 'w_out': 1.625005e+00}


def _to_microbatches(a, axis):
    t = _jnp.moveaxis(a, axis, 0)
    t = t.reshape((N_MICROBATCH, t.shape[0] // N_MICROBATCH) + t.shape[1:])
    return _jnp.moveaxis(t, 1, axis + 1)


def setup_inputs(seed: int = 0) -> dict:
    inp = _fwd_setup_inputs(seed)
    key = _jax.random.fold_in(_jax.random.key(seed), 7919)
    shape, _ = _output_shape()
    out = dict(inp)
    out["loss_target"] = _jax.random.normal(_jax.random.fold_in(key, 0), shape, _jnp.float32)
    for i, name in enumerate(TWIN_WEIGHTS):
        w = inp[name].astype(_jnp.float32)
        if MOMENT_SCALE is None:
            s = _jnp.sqrt(_jnp.mean(_jnp.square(w)) + 1e-30)
        else:
            s = MOMENT_SCALE[name]
        km, kv = _jax.random.split(_jax.random.fold_in(key, i + 1))
        out[name] = w
        out["m_" + name] = s * _jax.random.normal(km, w.shape, _jnp.float32)
        out["v_" + name] = (s * s) * _jax.random.uniform(kv, w.shape, _jnp.float32, 0.5, 1.5)
    if N_MICROBATCH > 1:
        for name, axis in PER_EXAMPLE_BATCH_AXIS.items():
            out[name] = _to_microbatches(out[name], axis)
    return {'x': out['x'], 'c': out['c'], 'positions': out['positions'], 'ada_w': out['ada_w'], 'ada_b': out['ada_b'], 'norm_pre': out['norm_pre'], 'norm_post': out['norm_post'], 'w_in': out['w_in'], 'ret_gn': out['ret_gn'], 'lru_conv_w': out['lru_conv_w'], 'lru_conv_b': out['lru_conv_b'], 'lru_wa': out['lru_wa'], 'lru_ba': out['lru_ba'], 'lru_wx': out['lru_wx'], 'lru_bx': out['lru_bx'], 'lru_lambda': out['lru_lambda'], 'mla_q_norm': out['mla_q_norm'], 'mla_w_uq': out['mla_w_uq'], 'mla_kv_norm': out['mla_kv_norm'], 'mla_w_ukv': out['mla_w_ukv'], 'w_branch': out['w_branch'], 'w_out': out['w_out'], 'loss_target': out['loss_target'], 'm_ada_w': out['m_ada_w'], 'm_ada_b': out['m_ada_b'], 'm_norm_pre': out['m_norm_pre'], 'm_norm_post': out['m_norm_post'], 'm_w_in': out['m_w_in'], 'm_ret_gn': out['m_ret_gn'], 'm_lru_conv_w': out['m_lru_conv_w'], 'm_lru_conv_b': out['m_lru_conv_b'], 'm_lru_wa': out['m_lru_wa'], 'm_lru_ba': out['m_lru_ba'], 'm_lru_wx': out['m_lru_wx'], 'm_lru_bx': out['m_lru_bx'], 'm_lru_lambda': out['m_lru_lambda'], 'm_mla_q_norm': out['m_mla_q_norm'], 'm_mla_w_uq': out['m_mla_w_uq'], 'm_mla_kv_norm': out['m_mla_kv_norm'], 'm_mla_w_ukv': out['m_mla_w_ukv'], 'm_w_branch': out['m_w_branch'], 'm_w_out': out['m_w_out'], 'v_ada_w': out['v_ada_w'], 'v_ada_b': out['v_ada_b'], 'v_norm_pre': out['v_norm_pre'], 'v_norm_post': out['v_norm_post'], 'v_w_in': out['v_w_in'], 'v_ret_gn': out['v_ret_gn'], 'v_lru_conv_w': out['v_lru_conv_w'], 'v_lru_conv_b': out['v_lru_conv_b'], 'v_lru_wa': out['v_lru_wa'], 'v_lru_ba': out['v_lru_ba'], 'v_lru_wx': out['v_lru_wx'], 'v_lru_bx': out['v_lru_bx'], 'v_lru_lambda': out['v_lru_lambda'], 'v_mla_q_norm': out['v_mla_q_norm'], 'v_mla_w_uq': out['v_mla_w_uq'], 'v_mla_kv_norm': out['v_mla_kv_norm'], 'v_mla_w_ukv': out['v_mla_w_ukv'], 'v_w_branch': out['v_w_branch'], 'v_w_out': out['v_w_out']}


def _loss(weights, diff, rest, loss_target):
    with _jax.named_scope("forward"):
        args = {**rest, TWIN_DIFF_INPUT: diff, **{k: w.astype(_WEIGHT_DTYPES[k]) for k, w in weights.items()}}
        y = _forward(args)
    with _jax.named_scope("loss_head"):
        err = _jnp.square(y.astype(_jnp.float32) - loss_target)
        return 0.5 * _jnp.sum(_jnp.mean(err, axis=-1)) if err.ndim else 0.5 * err


def _adamw(w, g, m, v):
    m = ADAM_B1 * m + (1.0 - ADAM_B1) * g
    v = ADAM_B2 * v + (1.0 - ADAM_B2) * _jnp.square(g)
    m_hat = m / (1.0 - ADAM_B1 ** ADAM_STEP)
    v_hat = v / (1.0 - ADAM_B2 ** ADAM_STEP)
    delta = -ADAM_LR * (m_hat / (_jnp.sqrt(v_hat) + ADAM_EPS) + ADAM_WD * w)
    return delta, m, v


def reference(x, c, positions, ada_w, ada_b, norm_pre, norm_post, w_in, ret_gn, lru_conv_w, lru_conv_b, lru_wa, lru_ba, lru_wx, lru_bx, lru_lambda, mla_q_norm, mla_w_uq, mla_kv_norm, mla_w_ukv, w_branch, w_out, loss_target, m_ada_w, m_ada_b, m_norm_pre, m_norm_post, m_w_in, m_ret_gn, m_lru_conv_w, m_lru_conv_b, m_lru_wa, m_lru_ba, m_lru_wx, m_lru_bx, m_lru_lambda, m_mla_q_norm, m_mla_w_uq, m_mla_kv_norm, m_mla_w_ukv, m_w_branch, m_w_out, v_ada_w, v_ada_b, v_norm_pre, v_norm_post, v_w_in, v_ret_gn, v_lru_conv_w, v_lru_conv_b, v_lru_wa, v_lru_ba, v_lru_wx, v_lru_bx, v_lru_lambda, v_mla_q_norm, v_mla_w_uq, v_mla_kv_norm, v_mla_w_ukv, v_w_branch, v_w_out):
    given = dict(x=x, c=c, positions=positions, ada_w=ada_w, ada_b=ada_b, norm_pre=norm_pre, norm_post=norm_post, w_in=w_in, ret_gn=ret_gn, lru_conv_w=lru_conv_w, lru_conv_b=lru_conv_b, lru_wa=lru_wa, lru_ba=lru_ba, lru_wx=lru_wx, lru_bx=lru_bx, lru_lambda=lru_lambda, mla_q_norm=mla_q_norm, mla_w_uq=mla_w_uq, mla_kv_norm=mla_kv_norm, mla_w_ukv=mla_w_ukv, w_branch=w_branch, w_out=w_out, loss_target=loss_target, m_ada_w=m_ada_w, m_ada_b=m_ada_b, m_norm_pre=m_norm_pre, m_norm_post=m_norm_post, m_w_in=m_w_in, m_ret_gn=m_ret_gn, m_lru_conv_w=m_lru_conv_w, m_lru_conv_b=m_lru_conv_b, m_lru_wa=m_lru_wa, m_lru_ba=m_lru_ba, m_lru_wx=m_lru_wx, m_lru_bx=m_lru_bx, m_lru_lambda=m_lru_lambda, m_mla_q_norm=m_mla_q_norm, m_mla_w_uq=m_mla_w_uq, m_mla_kv_norm=m_mla_kv_norm, m_mla_w_ukv=m_mla_w_ukv, m_w_branch=m_w_branch, m_w_out=m_w_out, v_ada_w=v_ada_w, v_ada_b=v_ada_b, v_norm_pre=v_norm_pre, v_norm_post=v_norm_post, v_w_in=v_w_in, v_ret_gn=v_ret_gn, v_lru_conv_w=v_lru_conv_w, v_lru_conv_b=v_lru_conv_b, v_lru_wa=v_lru_wa, v_lru_ba=v_lru_ba, v_lru_wx=v_lru_wx, v_lru_bx=v_lru_bx, v_lru_lambda=v_lru_lambda, v_mla_q_norm=v_mla_q_norm, v_mla_w_uq=v_mla_w_uq, v_mla_kv_norm=v_mla_kv_norm, v_mla_w_ukv=v_mla_w_ukv, v_w_branch=v_w_branch, v_w_out=v_w_out)
    weights = {n: given[n] for n in TWIN_WEIGHTS}
    shared = {n: given[n] for n in SHARED_INPUTS}
    per_example = {n: given[n] for n in ['x', 'c', 'positions']}
    grad_fn = _jax.value_and_grad(_loss, argnums=(0, 1))

    def one_microbatch(ex, loss_target):
        ex = dict(ex)
        diff = ex.pop(TWIN_DIFF_INPUT)
        return grad_fn(weights, diff, {**shared, **ex}, loss_target)

    if N_MICROBATCH == 1:
        loss, (grad_w, grad_x) = one_microbatch(per_example, given["loss_target"])
    else:
        def body(carry, xs):
            loss_sum, grad_sum = carry
            l_k, (gw_k, gx_k) = one_microbatch(xs[0], xs[1])
            with _jax.named_scope("update"):
                return (loss_sum + l_k, _jax.tree.map(_jnp.add, grad_sum, gw_k)), gx_k

        init = (_jnp.zeros((), _jnp.float32), _jax.tree.map(_jnp.zeros_like, weights))
        (loss, grad_w), grad_x = _jax.lax.scan(body, init, (per_example, given["loss_target"]))
    with _jax.named_scope("update"):
        delta_w, new_m, new_v = {}, {}, {}
        for n in TWIN_WEIGHTS:
            delta_w[n], new_m[n], new_v[n] = _adamw(weights[n], grad_w[n], given["m_" + n], given["v_" + n])
    return (loss, grad_x, *[grad_w[n] for n in TWIN_WEIGHTS], *[delta_w[n] for n in TWIN_WEIGHTS],
            *[new_m[n] for n in TWIN_WEIGHTS], *[new_v[n] for n in TWIN_WEIGHTS])
```

```python
import functools
import math

import numpy as np
import jax
import jax.numpy as jnp
from jax import lax
from jax.experimental import pallas as pl
from jax.experimental.pallas import tpu as pltpu

f32 = jnp.float32
bf16 = jnp.bfloat16
SDS = jax.ShapeDtypeStruct

DEPTH = 4
D = 2048
HEADS = 8
DH = 128
BW = HEADS * DH
LANE = 128
CHUNK = 64
EPS = 1e-6
LRU_C = 8.0
NDEV = 8
VMEM_V7X = 64 * 1024 * 1024

Q0, K0, V0, RG0, LX0, LG0, MQ0, MKV0, MKR0, MG0, ML0 = 0, 8, 16, 24, 32, 40, 48, 52, 56, 57, 65
NB = 114
NP = NB * LANE
IN_W = 14400
SHARD_W = IN_W // NDEV
GAP_COL = 7232
GAP = 64
WIN = 1920
WIN_BLKS = WIN // LANE
WIN_STRIDE = 14
LORA = 512
QPAD = 256

ADAM_LR, ADAM_B1, ADAM_B2, ADAM_EPS, ADAM_WD, ADAM_STEP = 0.001, 0.9, 0.999, 1e-08, 0.01, 10
ADAM_BLOCK_ELEMS = 256 * 1024
MLA_SCALE = (128 + 64) ** -0.5
RET_SCALE = 128 ** -0.5


def _cp(vmem_mb=None, **kw):
    if vmem_mb is not None:
        kw["vmem_limit_bytes"] = min(vmem_mb * 1024 * 1024, VMEM_V7X - 8 * 1024 * 1024)
    return pltpu.CompilerParams(**kw)


def _sigmoid(x):
    return 1.0 / (1.0 + jnp.exp(-x))


def _silu(x):
    return x * _sigmoid(x)


def _dsilu(x):
    s = _sigmoid(x)
    return s * (1.0 + x * (1.0 - s))


def _softplus(x):
    return jnp.maximum(x, 0.0) + jnp.log(1.0 + jnp.exp(-jnp.abs(x)))


def _one_minus_exp(y):
    series = -y * (1.0 + y * (0.5 + y * (1.0 / 6.0)))
    return jnp.where(y > -1e-2, series, 1.0 - jnp.exp(y))


def _acc(ref, val, first):
    @pl.when(first)
    def _():
        ref[...] = val

    @pl.when(jnp.logical_not(first))
    def _():
        ref[...] += val


_DIMS = {"nn": (((1,), (0,)), ((), ())), "nt": (((1,), (1,)), ((), ())), "tn": (((0,), (0,)), ((), ()))}


def matmul(a, b, *, dims, M, N, K, tm, tn, tk, out_dtype, name, a_blk0=(0, 0), b_blk0=(0, 0), vmem_mb=48):
    nk = K // tk
    assert M % tm == 0 and N % tn == 0 and K % tk == 0
    dn = _DIMS[dims]

    def body(a_ref, b_ref, o_ref, *scr):
        part = lax.dot_general(a_ref[...].astype(bf16), b_ref[...].astype(bf16), dn, preferred_element_type=f32)
        if nk == 1:
            o_ref[...] = part.astype(out_dtype)
        else:
            acc = scr[0]
            k = pl.program_id(2)
            _acc(acc, part, k == 0)

            @pl.when(k == nk - 1)
            def _():
                o_ref[...] = acc[...].astype(out_dtype)

    ar, ac = a_blk0
    br, bc = b_blk0
    if dims == "nn":
        a_spec = pl.BlockSpec((tm, tk), lambda i, j, k: (i + ar, k + ac))
        b_spec = pl.BlockSpec((tk, tn), lambda i, j, k: (k + br, j + bc))
    elif dims == "nt":
        a_spec = pl.BlockSpec((tm, tk), lambda i, j, k: (i + ar, k + ac))
        b_spec = pl.BlockSpec((tn, tk), lambda i, j, k: (j + br, k + bc))
    else:
        a_spec = pl.BlockSpec((tk, tm), lambda i, j, k: (k + ar, i + ac))
        b_spec = pl.BlockSpec((tk, tn), lambda i, j, k: (k + br, j + bc))
    return pl.pallas_call(
        body, name=name, grid=(M // tm, N // tn, nk),
        in_specs=[a_spec, b_spec], out_specs=pl.BlockSpec((tm, tn), lambda i, j, k: (i, j)),
        out_shape=SDS((M, N), out_dtype),
        scratch_shapes=[] if nk == 1 else [pltpu.VMEM((tm, tn), f32)],
        compiler_params=_cp(vmem_mb, dimension_semantics=("parallel", "parallel", "arbitrary")),
    )(a, b)


def pre_fwd(x, g, sc, sh, *, tm=256):
    S = x.shape[0]

    def body(x_ref, g_ref, sc_ref, sh_ref, h_ref):
        xv = x_ref[...]
        r = lax.rsqrt(jnp.mean(xv * xv, axis=-1, keepdims=True) + EPS)
        h_ref[...] = (((xv * r) * g_ref[...]) * (1.0 + sc_ref[...]) + sh_ref[...]).astype(bf16)

    row = pl.BlockSpec((tm, D), lambda i: (i, 0))
    vec = pl.BlockSpec((1, D), lambda i: (0, 0))
    return pl.pallas_call(body, name="pre_fwd", grid=(S // tm,), in_specs=[row, vec, vec, vec], out_specs=row,
                          out_shape=SDS((S, D), bf16), compiler_params=_cp(32))(x, g, sc, sh)


def pre_bwd(dh, x, g, sc, dxo, *, tm=256):
    S = x.shape[0]

    def body(dh_ref, x_ref, g_ref, sc_ref, dxo_ref, dx_ref, dsh_ref, dsc_ref, dg_ref):
        first = pl.program_id(0) == 0
        xv, dhv, gv = x_ref[...], dh_ref[...], g_ref[...]
        one_sc = 1.0 + sc_ref[...]
        r = lax.rsqrt(jnp.mean(xv * xv, axis=-1, keepdims=True) + EPS)
        xh = xv * r
        t = dhv * xh
        dxh = dhv * gv * one_sc
        dx_ref[...] = r * (dxh - xh * jnp.mean(dxh * xh, axis=-1, keepdims=True)) + dxo_ref[...]
        _acc(dsh_ref, jnp.sum(dhv, axis=0, keepdims=True), first)
        _acc(dsc_ref, jnp.sum(t * gv, axis=0, keepdims=True), first)
        _acc(dg_ref, jnp.sum(t * one_sc, axis=0, keepdims=True), first)

    row = pl.BlockSpec((tm, D), lambda i: (i, 0))
    vec = pl.BlockSpec((1, D), lambda i: (0, 0))
    return pl.pallas_call(
        body, name="pre_bwd", grid=(S // tm,), in_specs=[row, row, vec, vec, row], out_specs=[row, vec, vec, vec],
        out_shape=[SDS((S, D), f32), SDS((1, D), f32), SDS((1, D), f32), SDS((1, D), f32)],
        compiler_params=_cp(40))(dh, x, g, sc, dxo)


def out_fwd(merged, w_out, x, rg, gp, *, tm=256):
    S = x.shape[0]

    def body(m_ref, w_ref, x_ref, rg_ref, gp_ref, y_ref, xn_ref):
        y = jnp.dot(m_ref[...], w_ref[...], preferred_element_type=f32)
        y_ref[...] = y
        r = lax.rsqrt(jnp.mean(y * y, axis=-1, keepdims=True) + EPS)
        xn_ref[...] = x_ref[...] + (1.0 + rg_ref[...]) * ((y * r) * gp_ref[...])

    row = pl.BlockSpec((tm, D), lambda i: (i, 0))
    vec = pl.BlockSpec((1, D), lambda i: (0, 0))
    return pl.pallas_call(
        body, name="out_fwd", grid=(S // tm,),
        in_specs=[row, pl.BlockSpec((D, D), lambda i: (0, 0)), row, vec, vec], out_specs=[row, row],
        out_shape=[SDS((S, D), f32), SDS((S, D), f32)], compiler_params=_cp(48))(merged, w_out, x, rg, gp)


def out_bwd(dxo, y, rg, gp, *, tm=256):
    S = y.shape[0]

    def body(dxo_ref, y_ref, rg_ref, gp_ref, dy_ref, drg_ref, dgp_ref):
        first = pl.program_id(0) == 0
        yv, dv, gv = y_ref[...], dxo_ref[...], gp_ref[...]
        r = lax.rsqrt(jnp.mean(yv * yv, axis=-1, keepdims=True) + EPS)
        yh = yv * r
        dn = dv * (1.0 + rg_ref[...])
        dyh = dn * gv
        dy_ref[...] = (r * (dyh - yh * jnp.mean(dyh * yh, axis=-1, keepdims=True))).astype(bf16)
        _acc(drg_ref, jnp.sum(dv * (yh * gv), axis=0, keepdims=True), first)
        _acc(dgp_ref, jnp.sum(dn * yh, axis=0, keepdims=True), first)

    row = pl.BlockSpec((tm, D), lambda i: (i, 0))
    vec = pl.BlockSpec((1, D), lambda i: (0, 0))
    return pl.pallas_call(
        body, name="out_bwd", grid=(S // tm,), in_specs=[row, row, vec, vec], out_specs=[row, vec, vec],
        out_shape=[SDS((S, D), bf16), SDS((1, D), f32), SDS((1, D), f32)], compiler_params=_cp(40))(dxo, y, rg, gp)


def gate_fwd(proj, us, *, tm=1024):
    S = proj.shape[0]
    tm = min(tm, S)
    nj = D // LANE

    def body(ml0, ml1, ml2, u_ref, m_ref):
        acc = None
        for b, ml in enumerate((ml0, ml1, ml2)):
            t = _sigmoid(ml[...]) * u_ref[b].astype(f32)
            acc = t if acc is None else acc + t
        m_ref[...] = acc.astype(bf16)

    def ml_spec(b):
        return pl.BlockSpec((tm, LANE), lambda i, j: (i, ML0 + b * nj + j))

    return pl.pallas_call(
        body, name="gate_fwd", grid=(S // tm, nj),
        in_specs=[ml_spec(0), ml_spec(1), ml_spec(2), pl.BlockSpec((3, tm, LANE), lambda i, j: (0, i, j))],
        out_specs=pl.BlockSpec((tm, LANE), lambda i, j: (i, j)),
        out_shape=SDS((S, D), bf16), compiler_params=_cp(32),
    )(proj, proj, proj, us)


def gate_bwd(dmerged, proj, us, *, tm=1024):
    S = proj.shape[0]
    tm = min(tm, S)
    nj = D // LANE

    def body(dm_ref, ml_ref, u_ref, du_ref, dl_ref):
        dm = dm_ref[...]
        s = _sigmoid(ml_ref[...])
        du_ref[0] = (dm * s).astype(bf16)
        dl_ref[...] = (dm * u_ref[0].astype(f32) * (s * (1.0 - s))).astype(bf16)

    return pl.pallas_call(
        body, name="gate_bwd", grid=(S // tm, 3, nj),
        in_specs=[pl.BlockSpec((tm, LANE), lambda i, b, j: (i, j)),
                  pl.BlockSpec((tm, LANE), lambda i, b, j: (i, ML0 + b * nj + j)),
                  pl.BlockSpec((1, tm, LANE), lambda i, b, j: (b, i, j))],
        out_specs=[pl.BlockSpec((1, tm, LANE), lambda i, b, j: (b, i, j)),
                   pl.BlockSpec((tm, LANE), lambda i, b, j: (i, b * nj + j))],
        out_shape=[SDS((3, S, D), bf16), SDS((S, 3 * D), bf16)], compiler_params=_cp(32),
    )(dmerged, proj, us)


def rope_tables(positions):
    pos = positions.astype(f32)[:, None]

    def cs(dim):
        inv = 10000.0 ** (-jnp.arange(0, dim, 2, dtype=f32) / dim)
        ang = pos * inv
        return jnp.cos(ang), jnp.sin(ang)

    c, s = cs(128)
    ret = (jnp.concatenate([c, c], 1), jnp.concatenate([-s, s], 1))
    c, s = cs(64)
    z32, z64 = jnp.zeros_like(c), jnp.zeros((c.shape[0], 64), f32)
    mla = (jnp.concatenate([c, c, z64], 1), jnp.concatenate([-s, z32, z64], 1), jnp.concatenate([z32, s, z64], 1))
    return ret, mla


def _rope_ret(x, c, s):
    return x * c + pltpu.roll(x, 64, 1) * s


def _rope_ret_t(dy, c, s):
    return dy * c + pltpu.roll(dy * s, 64, 1)


def _rope_mla(x, c, sa, sb):
    return x * c + pltpu.roll(x, 96, 1) * sa + pltpu.roll(x, 32, 1) * sb


def _rope_mla_t(dy, c, sa, sb):
    return dy * c + pltpu.roll(dy * sa, 32, 1) + pltpu.roll(dy * sb, 96, 1)


def ret_prep_fwd(proj, c, s, *, tm=512):
    S = proj.shape[0]
    tm = min(tm, S)

    def body(p_ref, c_ref, s_ref, o_ref):
        scale = jnp.where(pl.program_id(1) < HEADS, RET_SCALE, 1.0)
        o_ref[...] = (_rope_ret(p_ref[...], c_ref[...], s_ref[...]) * scale).astype(bf16)

    blk = pl.BlockSpec((tm, DH), lambda i, j: (i, j))
    tab = pl.BlockSpec((tm, DH), lambda i, j: (i, 0))
    return pl.pallas_call(body, name="ret_prep_fwd", grid=(S // tm, 2 * HEADS), in_specs=[blk, tab, tab], out_specs=blk,
                          out_shape=SDS((S, 2 * BW), bf16), compiler_params=_cp(32))(proj, c, s)


def ret_prep_bwd(dqk, c, s, *, tm=512):
    S = dqk.shape[0]
    tm = min(tm, S)

    def body(d_ref, c_ref, s_ref, o_ref):
        scale = jnp.where(pl.program_id(1) < HEADS, RET_SCALE, 1.0)
        o_ref[...] = _rope_ret_t(d_ref[...] * scale, c_ref[...], s_ref[...]).astype(bf16)

    blk = pl.BlockSpec((tm, DH), lambda i, j: (i, j))
    tab = pl.BlockSpec((tm, DH), lambda i, j: (i, 0))
    return pl.pallas_call(body, name="ret_prep_bwd", grid=(S // tm, 2 * HEADS), in_specs=[blk, tab, tab], out_specs=blk,
                          out_shape=SDS((S, 2 * BW), bf16), compiler_params=_cp(32))(dqk, c, s)


def _scores(q, k, qi, bq, S, lg, softmax):
    s = lax.dot_general(q, k, _DIMS["nt"], preferred_element_type=f32)
    ti = qi * bq + lax.broadcasted_iota(jnp.int32, (bq, S), 0)
    tj = lax.broadcasted_iota(jnp.int32, (bq, S), 1)
    mask = (tj // CHUNK) <= (ti // CHUNK)
    if softmax:
        s = jnp.where(mask, s, -1e30)
        e = jnp.exp(s - jnp.max(s, axis=-1, keepdims=True))
        return e / jnp.sum(e, axis=-1, keepdims=True), None
    w = jnp.where(mask, jnp.exp(lg * jnp.abs(ti - tj).astype(f32)), 0.0)
    return s * w, w


def attn_fwd(q, k, v, gate_src, *, softmax, dq, q_blk0, k_blk0, v_blk0, gate_blk0, lgam=None, gn=None, bq=256, name):
    S = q.shape[0]
    bq = min(bq, S)

    def body(*refs):
        if softmax:
            q_ref, k_ref, v_ref, g_ref, o_ref, y_ref = refs
            lg = None
        else:
            lg_ref, q_ref, k_ref, v_ref, g_ref, gn_ref, o_ref, y_ref = refs
            lg = lg_ref[pl.program_id(0)]
        p, _ = _scores(q_ref[...], k_ref[...], pl.program_id(1), bq, S, lg, softmax)
        o = jnp.dot(p.astype(bf16), v_ref[...].astype(bf16), preferred_element_type=f32)
        o_ref[...] = o
        if softmax:
            z = o
        else:
            oc = o - jnp.mean(o, axis=-1, keepdims=True)
            z = oc * lax.rsqrt(jnp.mean(oc * oc, axis=-1, keepdims=True) + EPS) * gn_ref[...]
        y_ref[...] = (z * _silu(g_ref[...])).astype(bf16)

    q_spec = pl.BlockSpec((bq, dq), lambda h, i: (i, q_blk0 + h))
    k_spec = pl.BlockSpec((S, dq), lambda h, i: (0, k_blk0 + h))
    v_spec = pl.BlockSpec((S, DH), lambda h, i: (0, v_blk0 + h))
    g_spec = pl.BlockSpec((bq, DH), lambda h, i: (i, gate_blk0 + h))
    o_spec = pl.BlockSpec((bq, DH), lambda h, i: (i, h))
    in_specs, args = [q_spec, k_spec, v_spec, g_spec], [q, k, v, gate_src]
    if not softmax:
        in_specs = [pl.BlockSpec(memory_space=pltpu.SMEM)] + in_specs + [pl.BlockSpec((1, DH), lambda h, i: (0, h))]
        args = [lgam] + args + [gn]
    return pl.pallas_call(
        body, name=name, grid=(HEADS, S // bq), in_specs=in_specs, out_specs=[o_spec, o_spec],
        out_shape=[SDS((S, BW), f32), SDS((S, BW), bf16)], compiler_params=_cp(48))(*args)


def attn_bwd(q, k, v, do, o, *, softmax, dq, q_blk0, k_blk0, v_blk0, lgam=None, bq=256, name):
    S = q.shape[0]
    bq = min(bq, S)

    def body(*refs):
        if softmax:
            q_ref, k_ref, v_ref, do_ref, o_ref, dq_ref, dk_ref, dv_ref = refs
            lg = None
        else:
            lg_ref, q_ref, k_ref, v_ref, do_ref, dq_ref, dk_ref, dv_ref = refs
            lg = lg_ref[pl.program_id(0)]
        first = pl.program_id(1) == 0
        qv, kv, dov = q_ref[...], k_ref[...], do_ref[...]
        p, w = _scores(qv, kv, pl.program_id(1), bq, S, lg, softmax)
        dp = lax.dot_general(dov, v_ref[...].astype(bf16), _DIMS["nt"], preferred_element_type=f32)
        if softmax:
            delta = jnp.sum(dov.astype(f32) * o_ref[...], axis=-1, keepdims=True)
            ds = p * (dp - delta)
        else:
            ds = dp * w
        dsb = ds.astype(bf16)
        dq_ref[...] = jnp.dot(dsb, kv, preferred_element_type=f32)
        _acc(dv_ref, lax.dot_general(p.astype(bf16), dov, _DIMS["tn"], preferred_element_type=f32), first)
        _acc(dk_ref, lax.dot_general(dsb, qv, _DIMS["tn"], preferred_element_type=f32), first)

    q_spec = pl.BlockSpec((bq, dq), lambda h, i: (i, q_blk0 + h))
    k_spec = pl.BlockSpec((S, dq), lambda h, i: (0, k_blk0 + h))
    v_spec = pl.BlockSpec((S, DH), lambda h, i: (0, v_blk0 + h))
    o_spec = pl.BlockSpec((bq, DH), lambda h, i: (i, h))
    in_specs, args = [q_spec, k_spec, v_spec, o_spec], [q, k, v, do]
    if softmax:
        in_specs, args = in_specs + [o_spec], args + [o]
    else:
        in_specs, args = [pl.BlockSpec(memory_space=pltpu.SMEM)] + in_specs, [lgam] + args
    return pl.pallas_call(
        body, name=name, grid=(HEADS, S // bq), in_specs=in_specs,
        out_specs=[pl.BlockSpec((bq, dq), lambda h, i: (i, h)), pl.BlockSpec((S, dq), lambda h, i: (0, h)),
                   pl.BlockSpec((S, DH), lambda h, i: (0, h))],
        out_shape=[SDS((S, HEADS * dq), f32), SDS((S, HEADS * dq), f32), SDS((S, BW), f32)],
        compiler_params=_cp(52))(*args)


def ret_post_bwd(dy, o, proj, gn, *, tm=512):
    S = dy.shape[0]
    tm = min(tm, S)

    def body(dy_ref, o_ref, g_ref, gn_ref, do_ref, drg_ref, dgn_ref):
        ov, g, gnv, dyv = o_ref[...], g_ref[...], gn_ref[...], dy_ref[...]
        oc = ov - jnp.mean(ov, axis=-1, keepdims=True)
        rs = lax.rsqrt(jnp.mean(oc * oc, axis=-1, keepdims=True) + EPS)
        oh = oc * rs
        dz = dyv * _silu(g)
        drg_ref[...] = (dyv * (oh * gnv) * _dsilu(g)).astype(bf16)
        doh = dz * gnv
        do_ref[...] = (rs * (doh - jnp.mean(doh, axis=-1, keepdims=True)
                             - oh * jnp.mean(doh * oh, axis=-1, keepdims=True))).astype(bf16)
        _acc(dgn_ref, jnp.sum(dz * oh, axis=0, keepdims=True), pl.program_id(1) == 0)

    blk = pl.BlockSpec((tm, DH), lambda h, i: (i, h))
    vec = pl.BlockSpec((1, DH), lambda h, i: (0, h))
    return pl.pallas_call(
        body, name="ret_post_bwd", grid=(HEADS, S // tm),
        in_specs=[blk, blk, pl.BlockSpec((tm, DH), lambda h, i: (i, RG0 + h)), vec], out_specs=[blk, blk, vec],
        out_shape=[SDS((S, BW), bf16), SDS((S, BW), bf16), SDS((1, BW), f32)], compiler_params=_cp(32))(dy, o, proj, gn)


def mla_post_bwd(dy, o, proj, *, tm=512):
    S = dy.shape[0]
    tm = min(tm, S)

    def body(dy_ref, o_ref, g_ref, do_ref, dg_ref):
        g, dyv = g_ref[...], dy_ref[...]
        do_ref[...] = (dyv * _silu(g)).astype(bf16)
        dg_ref[...] = (dyv * o_ref[...] * _dsilu(g)).astype(bf16)

    blk = pl.BlockSpec((tm, DH), lambda i, h: (i, h))
    return pl.pallas_call(
        body, name="mla_post_bwd", grid=(S // tm, HEADS),
        in_specs=[blk, blk, pl.BlockSpec((tm, DH), lambda i, h: (i, MG0 + h))], out_specs=[blk, blk],
        out_shape=[SDS((S, BW), bf16), SDS((S, BW), bf16)], compiler_params=_cp(32))(dy, o, proj)


def mla_prep_fwd(proj, qnorm, kvnorm, wuq, wukv, tabs, *, tm=256):
    S = proj.shape[0]
    tm = min(tm, S)

    def body(mq_ref, mkv_ref, mkr_ref, qn_ref, kvn_ref, wuq_ref, wukv_ref, c_ref, sa_ref, sb_ref, q_ref, k_ref, v_ref):
        c, sa, sb = c_ref[...], sa_ref[...], sb_ref[...]
        mq, mkv = mq_ref[...], mkv_ref[...]
        qn = (mq * lax.rsqrt(jnp.mean(mq * mq, axis=-1, keepdims=True) + EPS) * qn_ref[...]).astype(bf16)
        kvn = (mkv * lax.rsqrt(jnp.mean(mkv * mkv, axis=-1, keepdims=True) + EPS) * kvn_ref[...]).astype(bf16)
        kr = _rope_mla(mkr_ref[...], c, sa, sb).astype(bf16)
        for h in range(HEADS):
            qh = jnp.dot(qn, wuq_ref[h], preferred_element_type=f32)
            q_ref[:, pl.ds(h * QPAD, DH)] = (qh[:, :DH] * MLA_SCALE).astype(bf16)
            q_ref[:, pl.ds(h * QPAD + DH, DH)] = (_rope_mla(qh[:, DH:], c, sa, sb) * MLA_SCALE).astype(bf16)
            kvh = jnp.dot(kvn, wukv_ref[h], preferred_element_type=f32)
            k_ref[:, pl.ds(h * QPAD, DH)] = kvh[:, :DH].astype(bf16)
            k_ref[:, pl.ds(h * QPAD + DH, DH)] = kr
            v_ref[:, pl.ds(h * DH, DH)] = kvh[:, DH:].astype(bf16)

    lat = lambda b: pl.BlockSpec((tm, LORA), lambda i: (i, b))
    tab = pl.BlockSpec((tm, DH), lambda i: (i, 0))
    vec = pl.BlockSpec((1, LORA), lambda i: (0, 0))
    wsp = pl.BlockSpec((HEADS, LORA, QPAD), lambda i: (0, 0, 0))
    return pl.pallas_call(
        body, name="mla_prep_fwd", grid=(S // tm,),
        in_specs=[lat(MQ0 // 4), lat(MKV0 // 4), pl.BlockSpec((tm, DH), lambda i: (i, MKR0)), vec, vec, wsp, wsp,
                  tab, tab, tab],
        out_specs=[pl.BlockSpec((tm, HEADS * QPAD), lambda i: (i, 0))] * 2 + [pl.BlockSpec((tm, BW), lambda i: (i, 0))],
        out_shape=[SDS((S, HEADS * QPAD), bf16)] * 2 + [SDS((S, BW), bf16)], compiler_params=_cp(48),
    )(proj, proj, proj, qnorm, kvnorm, wuq, wukv, *tabs)


def mla_prep_bwd(dq256, dk256, dv, proj, qnorm, kvnorm, wuq, wukv, tabs, *, tm=256):
    S = proj.shape[0]
    tm = min(tm, S)

    def body(dq_ref, dk_ref, dv_ref, mq_ref, mkv_ref, qn_ref, kvn_ref, wuq_ref, wukv_ref, c_ref, sa_ref, sb_ref,
             dm_ref, dwuq_ref, dwukv_ref, dqn_ref, dkvn_ref):
        first = pl.program_id(0) == 0
        c, sa, sb = c_ref[...], sa_ref[...], sb_ref[...]
        mq, mkv = mq_ref[...], mkv_ref[...]
        rq = lax.rsqrt(jnp.mean(mq * mq, axis=-1, keepdims=True) + EPS)
        rkv = lax.rsqrt(jnp.mean(mkv * mkv, axis=-1, keepdims=True) + EPS)
        mqh, mkvh = mq * rq, mkv * rkv
        qn = (mqh * qn_ref[...]).astype(bf16)
        kvn = (mkvh * kvn_ref[...]).astype(bf16)
        dqn = jnp.zeros((tm, LORA), f32)
        dkvn = jnp.zeros((tm, LORA), f32)
        dkr = jnp.zeros((tm, DH), f32)
        for h in range(HEADS):
            da = dq_ref[:, pl.ds(h * QPAD, DH)] * MLA_SCALE
            db = _rope_mla_t(dq_ref[:, pl.ds(h * QPAD + DH, DH)] * MLA_SCALE, c, sa, sb)
            dqh = jnp.concatenate([da, db], axis=1).astype(bf16)
            dqn += lax.dot_general(dqh, wuq_ref[h], _DIMS["nt"], preferred_element_type=f32)
            _acc(dwuq_ref.at[h], lax.dot_general(qn, dqh, _DIMS["tn"], preferred_element_type=f32), first)
            dkr += dk_ref[:, pl.ds(h * QPAD + DH, DH)]
            dkvh = jnp.concatenate([dk_ref[:, pl.ds(h * QPAD, DH)], dv_ref[:, pl.ds(h * DH, DH)]], axis=1).astype(bf16)
            dkvn += lax.dot_general(dkvh, wukv_ref[h], _DIMS["nt"], preferred_element_type=f32)
            _acc(dwukv_ref.at[h], lax.dot_general(kvn, dkvh, _DIMS["tn"], preferred_element_type=f32), first)
        dmh = dqn * qn_ref[...]
        dm_ref[:, pl.ds(0, LORA)] = (rq * (dmh - mqh * jnp.mean(dmh * mqh, axis=-1, keepdims=True))).astype(bf16)
        dmh = dkvn * kvn_ref[...]
        dm_ref[:, pl.ds(LORA, LORA)] = (rkv * (dmh - mkvh * jnp.mean(dmh * mkvh, axis=-1, keepdims=True))).astype(bf16)
        dm_ref[:, pl.ds(2 * LORA, DH)] = _rope_mla_t(dkr, c, sa, sb).astype(bf16)
        _acc(dqn_ref, jnp.sum(dqn * mqh, axis=0, keepdims=True), first)
        _acc(dkvn_ref, jnp.sum(dkvn * mkvh, axis=0, keepdims=True), first)

    lat = lambda b: pl.BlockSpec((tm, LORA), lambda i: (i, b))
    tab = pl.BlockSpec((tm, DH), lambda i: (i, 0))
    vec = pl.BlockSpec((1, LORA), lambda i: (0, 0))
    wsp = pl.BlockSpec((HEADS, LORA, QPAD), lambda i: (0, 0, 0))
    wide = pl.BlockSpec((tm, HEADS * QPAD), lambda i: (i, 0))
    return pl.pallas_call(
        body, name="mla_prep_bwd", grid=(S // tm,),
        in_specs=[wide, wide, pl.BlockSpec((tm, BW), lambda i: (i, 0)), lat(MQ0 // 4), lat(MKV0 // 4), vec, vec, wsp, wsp,
                  tab, tab, tab],
        out_specs=[pl.BlockSpec((tm, 2 * LORA + DH), lambda i: (i, 0)), wsp, wsp, vec, vec],
        out_shape=[SDS((S, 2 * LORA + DH), bf16), SDS((HEADS, LORA, QPAD), f32), SDS((HEADS, LORA, QPAD), f32),
                   SDS((1, LORA), f32), SDS((1, LORA), f32)],
        compiler_params=_cp(52),
    )(dq256, dk256, dv, proj, proj, qnorm, kvnorm, wuq, wukv, *tabs)


SUB = 8


def _scan_tiles(a_s, b_s, out, S, reverse):
    nt = S // SUB
    rows = lax.broadcasted_iota(jnp.int32, (SUB, LANE), 0)

    def tile(t, carry):
        base = pl.multiple_of((nt - 1 - t if reverse else t) * SUB, SUB)
        a, b = a_s[pl.ds(base, SUB), :], b_s[pl.ds(base, SUB), :]
        for d in (1, 2, 4):
            sh = SUB - d if reverse else d
            inside = rows < SUB - d if reverse else rows >= d
            a_n = jnp.where(inside, pltpu.roll(a, sh, 0), 1.0)
            b_n = jnp.where(inside, pltpu.roll(b, sh, 0), 0.0)
            b = a * b_n + b
            a = a * a_n
        res = a * carry + b
        out[pl.ds(base, SUB), :] = res
        edge = res[0:1, :] if reverse else res[SUB - 1:SUB, :]
        return jnp.broadcast_to(edge, (SUB, LANE))

    lax.fori_loop(0, nt, tile, jnp.zeros((SUB, LANE), f32))


def _shift_down(x, n, rows):
    return x if n == 0 else jnp.where(rows >= n, pltpu.roll(x, n, 0), 0.0)


def _shift_up(x, n, rows, S):
    return x if n == 0 else jnp.where(rows < S - n, pltpu.roll(x, S - n, 0), 0.0)


def _lru_gates(xb, cw, cb, wa, ba, wx, bx, lam, rows):
    xc = cb + cw[3:4, :] * xb
    for w in range(3):
        xc = xc + cw[w:w + 1, :] * _shift_down(xb, 3 - w, rows)
    xcb = xc.astype(bf16)
    r = _sigmoid(jnp.dot(xcb, wa, preferred_element_type=f32) + ba)
    i = _sigmoid(jnp.dot(xcb, wx, preferred_element_type=f32) + bx)
    sp = _softplus(-lam)
    la = (-LRU_C * r) * sp
    return xc, xcb, r, i, sp, la, jnp.exp(la)


def _lru_specs(S):
    col = lambda b0: pl.BlockSpec((S, LANE), lambda n: (0, b0 + n))
    vec = pl.BlockSpec((1, LANE), lambda n: (0, n))
    return col, vec, pl.BlockSpec((4, LANE), lambda n: (0, n)), pl.BlockSpec((1, LANE, LANE), lambda n: (n, 0, 0))


def lru_fwd(proj, cw, cb, wa, ba, wx, bx, lam):
    S = proj.shape[0]

    def body(x_ref, g_ref, cw_ref, cb_ref, wa_ref, ba_ref, wx_ref, bx_ref, lam_ref, h_ref, y_ref, a_s, b_s):
        rows = lax.broadcasted_iota(jnp.int32, (S, LANE), 0)
        xc, _, _, i, _, la, a = _lru_gates(x_ref[...], cw_ref[...], cb_ref[...], wa_ref[0].astype(bf16), ba_ref[...],
                                           wx_ref[0].astype(bf16), bx_ref[...], lam_ref[...], rows)
        a_s[...] = a
        b_s[...] = jnp.sqrt(_one_minus_exp(2.0 * la)) * (i * xc)
        _scan_tiles(a_s, b_s, h_ref, S, reverse=False)
        y_ref[...] = (h_ref[...] * _silu(g_ref[...])).astype(bf16)

    col, vec, cws, wsp = _lru_specs(S)
    return pl.pallas_call(
        body, name="lru_fwd", grid=(HEADS,),
        in_specs=[col(LX0), col(LG0), cws, vec, wsp, vec, wsp, vec, vec], out_specs=[col(0), col(0)],
        out_shape=[SDS((S, BW), f32), SDS((S, BW), bf16)],
        scratch_shapes=[pltpu.VMEM((S, LANE), f32), pltpu.VMEM((S, LANE), f32)], compiler_params=_cp(40),
    )(proj, proj, cw, cb, wa, ba, wx, bx, lam)


def lru_bwd(dy, h, proj, cw, cb, wa, ba, wx, bx, lam):
    S = proj.shape[0]

    def body(dy_ref, h_ref, x_ref, g_ref, cw_ref, cb_ref, wa_ref, ba_ref, wx_ref, bx_ref, lam_ref,
             dx_ref, dg_ref, dcw_ref, dcb_ref, dba_ref, dbx_ref, dlam_ref, dwa_ref, dwx_ref, a_s, b_s, l_s):
        rows = lax.broadcasted_iota(jnp.int32, (S, LANE), 0)
        xb, g, hv, dyv, cw, lam = x_ref[...], g_ref[...], h_ref[...], dy_ref[...], cw_ref[...], lam_ref[...]
        wa, wx = wa_ref[0].astype(bf16), wx_ref[0].astype(bf16)
        xc, xcb, r, i, sp, la, a = _lru_gates(xb, cw, cb_ref[...], wa, ba_ref[...], wx, bx_ref[...], lam, rows)
        dg_ref[...] = (dyv * hv * _dsilu(g)).astype(bf16)
        a_s[...] = _shift_up(a, 1, rows, S)
        b_s[...] = dyv * _silu(g)
        _scan_tiles(a_s, b_s, l_s, S, reverse=True)
        lmb = l_s[...]
        gated = i * xc
        sq = jnp.sqrt(_one_minus_exp(2.0 * la))
        dla = lmb * _shift_down(hv, 1, rows) * a - (lmb * gated) * (a * a) / sq
        dgated = lmb * sq
        dzr = (dla * (-LRU_C * sp)) * (r * (1.0 - r))
        dzi = (dgated * xc) * (i * (1.0 - i))
        dzrb, dzib = dzr.astype(bf16), dzi.astype(bf16)
        dxc = (dgated * i + lax.dot_general(dzrb, wa, _DIMS["nt"], preferred_element_type=f32)
               + lax.dot_general(dzib, wx, _DIMS["nt"], preferred_element_type=f32))
        dwa_ref[0] = lax.dot_general(xcb, dzrb, _DIMS["tn"], preferred_element_type=f32)
        dwx_ref[0] = lax.dot_general(xcb, dzib, _DIMS["tn"], preferred_element_type=f32)
        dba_ref[...] = jnp.sum(dzr, axis=0, keepdims=True)
        dbx_ref[...] = jnp.sum(dzi, axis=0, keepdims=True)
        dlam_ref[...] = jnp.sum(dla * (-LRU_C * r), axis=0, keepdims=True) * (-_sigmoid(-lam))
        dcb_ref[...] = jnp.sum(dxc, axis=0, keepdims=True)
        dxb = cw[3:4, :] * dxc
        dcw_ref[3:4, :] = jnp.sum(dxc * xb, axis=0, keepdims=True)
        for w in range(3):
            dxb = dxb + cw[w:w + 1, :] * _shift_up(dxc, 3 - w, rows, S)
            dcw_ref[w:w + 1, :] = jnp.sum(dxc * _shift_down(xb, 3 - w, rows), axis=0, keepdims=True)
        dx_ref[...] = dxb.astype(bf16)

    col, vec, cws, wsp = _lru_specs(S)
    scr = pltpu.VMEM((S, LANE), f32)
    return pl.pallas_call(
        body, name="lru_bwd", grid=(HEADS,),
        in_specs=[col(0), col(0), col(LX0), col(LG0), cws, vec, wsp, vec, wsp, vec, vec],
        out_specs=[col(0), col(0), cws, vec, vec, vec, vec, wsp, wsp],
        out_shape=[SDS((S, BW), bf16), SDS((S, BW), bf16), SDS((4, BW), f32)] + [SDS((1, BW), f32)] * 4
        + [SDS((HEADS, LANE, LANE), f32)] * 2,
        scratch_shapes=[scr, scr, scr], compiler_params=_cp(48),
    )(dy, h, proj, proj, cw, cb, wa, ba, wx, bx, lam)


def loss_head(y, target, *, tm=256):
    S = y.shape[0]

    def body(y_ref, t_ref, l_ref, d_ref):
        err = y_ref[...] - t_ref[...]
        d_ref[...] = err * (1.0 / D)
        part = jnp.sum(jnp.sum(err * err, axis=1, keepdims=True), axis=0, keepdims=True) * (0.5 / D)
        _acc(l_ref, jnp.broadcast_to(part, (1, LANE)), pl.program_id(0) == 0)

    row = pl.BlockSpec((tm, D), lambda i: (i, 0))
    return pl.pallas_call(
        body, name="loss_head", grid=(S // tm,), in_specs=[row, row],
        out_specs=[pl.BlockSpec((1, LANE), lambda i: (0, 0)), row],
        out_shape=[SDS((1, LANE), f32), SDS((S, D), f32)], compiler_params=_cp(32))(y, target)


def adamw(w, g, m, v):
    shape = w.shape
    cols = shape[-1]
    rows = math.prod(shape[:-1])
    tr = rows
    for cand in (2048, 1024, 512, 256, 128, 64, 32, 16, 8):
        if rows % cand == 0 and rows > cand and cand * cols <= ADAM_BLOCK_ELEMS:
            tr = cand
            break

    def body(w_ref, g_ref, m_ref, v_ref, d_ref, mo_ref, vo_ref):
        gv = g_ref[...]
        mn = ADAM_B1 * m_ref[...] + (1.0 - ADAM_B1) * gv
        vn = ADAM_B2 * v_ref[...] + (1.0 - ADAM_B2) * (gv * gv)
        mo_ref[...] = mn
        vo_ref[...] = vn
        m_hat = mn / (1.0 - ADAM_B1 ** ADAM_STEP)
        v_hat = vn / (1.0 - ADAM_B2 ** ADAM_STEP)
        d_ref[...] = -ADAM_LR * (m_hat / (jnp.sqrt(v_hat) + ADAM_EPS) + ADAM_WD * w_ref[...])

    blk = pl.BlockSpec((tr, cols), lambda i: (i, 0))
    flat = [t.reshape(rows, cols) for t in (w, g, m, v)]
    outs = pl.pallas_call(
        body, name="adamw", grid=(rows // tr,), in_specs=[blk] * 4, out_specs=[blk] * 3,
        out_shape=[SDS((rows, cols), f32)] * 3, compiler_params=_cp(48))(*flat)
    return tuple(o.reshape(shape) for o in outs)


ADA_SHARD = 3 * D // NDEV
ROWS16 = 16


def ada_fwd(c_all, ada_w, ada_b_mine):
    def body(c_ref, w_ref, b_ref, o_ref):
        o_ref[0] = jnp.dot(_silu(c_ref[...]).astype(bf16), w_ref[0].astype(bf16), preferred_element_type=f32) + b_ref[0]

    return pl.pallas_call(
        body, name="ada_fwd", grid=(DEPTH,),
        in_specs=[pl.BlockSpec((ROWS16, D), lambda l: (0, 0)), pl.BlockSpec((1, D, ADA_SHARD), lambda l: (l, 0, 0)),
                  pl.BlockSpec((1, 1, ADA_SHARD), lambda l: (l, 0, 0))],
        out_specs=pl.BlockSpec((1, ROWS16, ADA_SHARD), lambda l: (l, 0, 0)),
        out_shape=SDS((DEPTH, ROWS16, ADA_SHARD), f32), compiler_params=_cp(40))(c_all, ada_w, ada_b_mine)


def ada_bwd(c_all, dmod):
    def body(c_ref, d_ref, o_ref):
        o_ref[0] = lax.dot_general(_silu(c_ref[...]).astype(bf16), d_ref[0].astype(bf16), _DIMS["tn"],
                                   preferred_element_type=f32)

    return pl.pallas_call(
        body, name="ada_bwd", grid=(DEPTH,),
        in_specs=[pl.BlockSpec((ROWS16, D), lambda l: (0, 0)), pl.BlockSpec((1, ROWS16, ADA_SHARD), lambda l: (l, 0, 0))],
        out_specs=pl.BlockSpec((1, D, ADA_SHARD), lambda l: (l, 0, 0)),
        out_shape=SDS((DEPTH, D, ADA_SHARD), f32), compiler_params=_cp(40))(c_all, dmod)


def shift_params(me):
    start = SHARD_W * me
    return jnp.stack([start % LANE, (start + GAP) % LANE, jnp.clip(GAP_COL - start, 0, SHARD_W)]).astype(jnp.int32)


def win_pack(w, sidx, *, tm=256):
    def body(s_ref, w_ref, o_ref, scr):
        s1, s2, gi = s_ref[0], s_ref[1], s_ref[2]
        scr[...] = jnp.zeros_like(scr)
        scr[:, pl.ds(0, SHARD_W)] = w_ref[0]
        v = scr[...]
        j = lax.broadcasted_iota(jnp.int32, v.shape, 1)
        o_ref[0] = jnp.where(j - s1 < gi, pltpu.roll(v, s1, 1),
                             jnp.where(j - s2 >= gi, pltpu.roll(v, s2, 1), 0.0)).astype(bf16)

    return pl.pallas_call(
        body, name="win_pack", grid=(DEPTH, D // tm),
        in_specs=[pl.BlockSpec(memory_space=pltpu.SMEM), pl.BlockSpec((1, tm, SHARD_W), lambda l, i: (l, i, 0))],
        out_specs=pl.BlockSpec((1, tm, WIN), lambda l, i: (l, i, 0)),
        out_shape=SDS((DEPTH, D, WIN), bf16), scratch_shapes=[pltpu.VMEM((tm, WIN), f32)],
        compiler_params=_cp(32))(sidx, w)


def win_assemble(g, l):
    def body(a_ref, b_ref, o_ref):
        t = pl.program_id(0)
        a_ok = t < NDEV * WIN_STRIDE
        b_ok = jnp.logical_and(t % WIN_STRIDE == 0, jnp.logical_and(t >= WIN_STRIDE, t <= NDEV * WIN_STRIDE))
        zero = jnp.zeros((D, LANE), bf16)
        o_ref[...] = jnp.where(a_ok, a_ref[0, 0], zero) + jnp.where(b_ok, b_ref[0, 0], zero)

    def a_map(t):
        k = jnp.minimum(t // WIN_STRIDE, NDEV - 1)
        return (k, l, 0, jnp.minimum(t - WIN_STRIDE * k, WIN_BLKS - 1))

    def b_map(t):
        return (jnp.clip(t // WIN_STRIDE - 1, 0, NDEV - 1), l, 0, WIN_BLKS - 1)

    return pl.pallas_call(
        body, name="win_assemble", grid=(NB,),
        in_specs=[pl.BlockSpec((1, 1, D, LANE), a_map), pl.BlockSpec((1, 1, D, LANE), b_map)],
        out_specs=pl.BlockSpec((D, LANE), lambda t: (0, t)),
        out_shape=SDS((D, NP), bf16), compiler_params=_cp(32))(g, g)


def win_reduce(stag, sidx, *, tm=128):
    def body(s_ref, g_ref, o_ref, scr):
        s1, s2, gi = s_ref[0], s_ref[1], s_ref[2]
        tot = g_ref[0].astype(f32)
        for d in range(1, NDEV):
            tot = tot + g_ref[d].astype(f32)
        i = lax.broadcasted_iota(jnp.int32, tot.shape, 1)
        scr[...] = jnp.where(i < gi, pltpu.roll(tot, WIN - s1, 1), pltpu.roll(tot, WIN - s2, 1))
        o_ref[...] = scr[:, pl.ds(0, SHARD_W)]

    return pl.pallas_call(
        body, name="win_reduce", grid=(D // tm,),
        in_specs=[pl.BlockSpec(memory_space=pltpu.SMEM), pl.BlockSpec((NDEV, tm, WIN), lambda i: (0, i, 0))],
        out_specs=pl.BlockSpec((tm, SHARD_W), lambda i: (i, 0)),
        out_shape=SDS((D, SHARD_W), f32), scratch_shapes=[pltpu.VMEM((tm, WIN), f32)],
        compiler_params=_cp(40))(sidx, stag)


def cast_pad(w, cols_out, name):
    L, R, C = w.shape

    def body(w_ref, o_ref, *scr):
        if cols_out == C:
            o_ref[0] = w_ref[0].astype(bf16)
        else:
            scr[0][...] = jnp.zeros_like(scr[0])
            scr[0][:, pl.ds(0, C)] = w_ref[0]
            o_ref[0] = scr[0][...].astype(bf16)

    return pl.pallas_call(
        body, name=name, grid=(L,), in_specs=[pl.BlockSpec((1, R, C), lambda l: (l, 0, 0))],
        out_specs=pl.BlockSpec((1, R, cols_out), lambda l: (l, 0, 0)), out_shape=SDS((L, R, cols_out), bf16),
        scratch_shapes=[] if cols_out == C else [pltpu.VMEM((R, cols_out), f32)], compiler_params=_cp(32))(w)


def sum_parts(stag, cols_out, name, *, tr=None):
    _, R, C = stag.shape
    tr = R if tr is None else tr

    def body(g_ref, o_ref, *scr):
        tot = g_ref[0].astype(f32)
        for d in range(1, NDEV):
            tot = tot + g_ref[d].astype(f32)
        if cols_out == C:
            o_ref[...] = tot
        else:
            scr[0][...] = tot
            o_ref[...] = scr[0][:, pl.ds(0, cols_out)]

    return pl.pallas_call(
        body, name=name, grid=(R // tr,), in_specs=[pl.BlockSpec((NDEV, tr, C), lambda i: (0, i, 0))],
        out_specs=pl.BlockSpec((tr, cols_out), lambda i: (i, 0)), out_shape=SDS((R, cols_out), f32),
        scratch_shapes=[] if cols_out == C else [pltpu.VMEM((tr, C), f32)], compiler_params=_cp(40))(stag)


MESH_ID = pl.DeviceIdType.MESH
HBM_SPEC = pl.BlockSpec(memory_space=pltpu.HBM)


def _place():
    return lax.axis_index("x"), lax.axis_index("y"), lax.axis_index("c")


def all_gather(arrs, name):
    n = len(arrs)

    def body(*refs):
        ins, outs = refs[:n], refs[n:2 * n]
        send_sems, recv_sems, local_sems = refs[2 * n:]
        x, y, c = _place()
        me, sibling = (x, y, c), (x, y, 1 - c)
        chips = [(1 - x, y), (x, 1 - y), (1 - x, 1 - y)]

        def copy(a, k, block, to, src=None):
            slot = outs[a].at[4 * block[0] + 2 * block[1] + block[2]]
            return pltpu.make_async_remote_copy(
                src_ref=slot if src is None else src, dst_ref=slot, send_sem=send_sems.at[7 * a + k],
                recv_sem=recv_sems.at[7 * a + k], device_id=to, device_id_type=MESH_ID)

        mine = [pltpu.make_async_copy(ins[a], outs[a].at[4 * x + 2 * y + c], local_sems.at[a]) for a in range(n)]
        for cp in mine:
            cp.start()
        first = []
        for a in range(n):
            first.append(copy(a, 0, me, sibling, src=ins[a]))
            first += [copy(a, 1 + j, me, (*chip, c), src=ins[a]) for j, chip in enumerate(chips)]
        for cp in first:
            cp.start()
        passed = []
        for j, chip in enumerate(chips):
            for a in range(n):
                copy(a, 1 + j, (*chip, c), me).wait_recv()
                cp = copy(a, 4 + j, (*chip, c), sibling)
                cp.start()
                passed.append(cp)
        for a in range(n):
            copy(a, 0, sibling, me).wait_recv()
        for j, chip in enumerate(chips):
            for a in range(n):
                copy(a, 4 + j, (*chip, 1 - c), me).wait_recv()
        for cp in first + passed:
            cp.wait_send()
        for cp in mine:
            cp.wait()

    return pl.pallas_call(
        body, name=name, in_specs=[HBM_SPEC] * n, out_specs=[HBM_SPEC] * n,
        out_shape=[SDS((NDEV,) + a.shape, a.dtype) for a in arrs],
        scratch_shapes=[pltpu.SemaphoreType.DMA((7 * n,)), pltpu.SemaphoreType.DMA((7 * n,)),
                        pltpu.SemaphoreType.DMA((n,))],
    )(*arrs)


def reduce_scatter_parts(arrs, pick, shapes, name):
    n = len(arrs)

    def body(*refs):
        ins, outs = refs[:n], refs[n:2 * n]
        send_sems, recv_sems, local_sems = refs[2 * n:]
        x, y, c = _place()
        me = 4 * x + 2 * y + c
        mine = [pltpu.make_async_copy(pick[a](ins[a], me), outs[a].at[me], local_sems.at[a]) for a in range(n)]
        for cp in mine:
            cp.start()
        sent = []
        for r in range(1, NDEV):
            peer = (x ^ (r >> 2), y ^ ((r >> 1) & 1), c ^ (r & 1))
            pid = 4 * peer[0] + 2 * peer[1] + peer[2]
            for a in range(n):
                cp = pltpu.make_async_remote_copy(
                    src_ref=pick[a](ins[a], pid), dst_ref=outs[a].at[me], send_sem=send_sems.at[7 * a + r - 1],
                    recv_sem=recv_sems.at[7 * a + r - 1], device_id=peer, device_id_type=MESH_ID)
                cp.start()
                sent.append((cp, a, r, pid))
        for cp, a, r, pid in sent:
            pltpu.make_async_remote_copy(
                src_ref=pick[a](ins[a], pid), dst_ref=outs[a].at[pid], send_sem=send_sems.at[7 * a + r - 1],
                recv_sem=recv_sems.at[7 * a + r - 1], device_id=(x, y, c), device_id_type=MESH_ID).wait_recv()
        for cp, _, _, _ in sent:
            cp.wait_send()
        for cp in mine:
            cp.wait()

    return pl.pallas_call(
        body, name=name, in_specs=[HBM_SPEC] * n, out_specs=[HBM_SPEC] * n,
        out_shape=[SDS((NDEV,) + tuple(s), a.dtype) for s, a in zip(shapes, arrs)],
        scratch_shapes=[pltpu.SemaphoreType.DMA((7 * n,)), pltpu.SemaphoreType.DMA((7 * n,)),
                        pltpu.SemaphoreType.DMA((n,))],
    )(*arrs)


WEIGHTS = ("ada_w", "ada_b", "norm_pre", "norm_post", "w_in", "ret_gn", "lru_conv_w", "lru_conv_b", "lru_wa", "lru_ba",
           "lru_wx", "lru_bx", "lru_lambda", "mla_q_norm", "mla_w_uq", "mla_kv_norm", "mla_w_ukv", "w_branch", "w_out")
SMALL = ("norm_pre", "norm_post", "ret_gn", "lru_conv_w", "lru_conv_b", "lru_wa", "lru_ba", "lru_wx", "lru_bx",
         "lru_lambda", "mla_q_norm", "mla_kv_norm")
BR_ROWS = 3 * BW // NDEV
OUT_ROWS = D // NDEV
WUQ_COLS = 192


def _row(v):
    return v.reshape(1, -1)


def layer_fwd(xl, mod, p, wts, tabs, lgam):
    S = xl.shape[0]
    tm = min(S, 2048)
    sh, sc, rg = _row(mod[:D]), _row(mod[D:2 * D]), _row(mod[2 * D:])
    ret_tabs, mla_tabs = tabs
    h = pre_fwd(xl, _row(p["norm_pre"]), sc, sh)
    proj = matmul(h, wts["w_in"], dims="nn", M=S, N=NP, K=D, tm=tm, tn=768, tk=D, out_dtype=f32, name="mm_in")
    qk = ret_prep_fwd(proj, *ret_tabs)
    o_ret, y_ret = attn_fwd(qk, qk, proj, proj, softmax=False, dq=DH, q_blk0=0, k_blk0=HEADS, v_blk0=V0, gate_blk0=RG0,
                            lgam=lgam, gn=_row(p["ret_gn"]), name="ret_attn_fwd")
    h_lru, y_lru = lru_fwd(proj, p["conv_w"], _row(p["lru_conv_b"]), p["lru_wa"], _row(p["lru_ba"]), p["lru_wx"],
                           _row(p["lru_bx"]), _row(p["lru_lambda"]))
    q256, k256, vm = mla_prep_fwd(proj, _row(p["mla_q_norm"]), _row(p["mla_kv_norm"]), wts["w_uq"], wts["w_ukv"], mla_tabs)
    o_mla, y_mla = attn_fwd(q256, k256, vm, proj, softmax=True, dq=QPAD, q_blk0=0, k_blk0=0, v_blk0=0, gate_blk0=MG0,
                            name="mla_attn_fwd")
    ys = jnp.stack([y_ret, y_lru, y_mla])
    us = jnp.stack([matmul(ys[b], wts["w_branch"], dims="nn", M=S, N=D, K=BW, tm=min(S, 1024), tn=1024, tk=BW,
                           out_dtype=bf16, name="mm_branch", b_blk0=(b, 0)) for b in range(3)])
    merged = gate_fwd(proj, us)
    y, x_next = out_fwd(merged, wts["w_out"], xl, rg, _row(p["norm_post"]))
    saved = dict(x=xl, h=h, proj=proj, qk=qk, o_ret=o_ret, h_lru=h_lru, q256=q256, k256=k256, vm=vm, o_mla=o_mla,
                 ys=ys, us=us, merged=merged, y=y, sc=sc, rg=rg)
    return x_next, saved


def layer_bwd(dx, sv, p, wts, tabs, lgam):
    S = dx.shape[0]
    tm = min(S, 2048)
    ret_tabs, mla_tabs = tabs
    proj = sv["proj"]
    dy, d_rg, d_gpost = out_bwd(dx, sv["y"], sv["rg"], _row(p["norm_post"]))
    dmerged = matmul(dy, wts["w_out"], dims="nt", M=S, N=D, K=D, tm=min(S, 1024), tn=1024, tk=D, out_dtype=f32, name="mm_dmerged")
    dw_out = matmul(sv["merged"], dy, dims="tn", M=D, N=D, K=S, tm=1024, tn=1024, tk=min(S, 1024), out_dtype=bf16, name="mm_dwout")
    du, dml = gate_bwd(dmerged, proj, sv["us"])
    du2, ys2 = du.reshape(3 * S, D), sv["ys"].reshape(3 * S, BW)
    tmb, tkb = min(S, 1024), min(S, 1024)
    dys = [matmul(du2, wts["w_branch"], dims="nt", M=S, N=BW, K=D, tm=tmb, tn=BW, tk=D, out_dtype=f32, name="mm_dybranch",
                  a_blk0=(b * (S // tmb), 0), b_blk0=(b, 0)) for b in range(3)]
    dw_branch = jnp.concatenate(
        [matmul(ys2, du2, dims="tn", M=BW, N=D, K=S, tm=BW, tn=1024, tk=tkb, out_dtype=bf16, name="mm_dwbranch",
                a_blk0=(b * (S // tkb), 0), b_blk0=(b * (S // tkb), 0)) for b in range(3)], axis=0)
    do, d_rgate, d_gn = ret_post_bwd(dys[0], sv["o_ret"], proj, _row(p["ret_gn"]))
    dq, dk, dv = attn_bwd(sv["qk"], sv["qk"], proj, do, None, softmax=False, dq=DH, q_blk0=0, k_blk0=HEADS, v_blk0=V0,
                          lgam=lgam, name="ret_attn_bwd")
    d_qk = ret_prep_bwd(jnp.concatenate([dq, dk], axis=1), *ret_tabs)
    d_lx, d_lg, d_cw, d_cb, d_ba, d_bx, d_lam, d_wa, d_wx = lru_bwd(
        dys[1], sv["h_lru"], proj, p["conv_w"], _row(p["lru_conv_b"]), p["lru_wa"], _row(p["lru_ba"]), p["lru_wx"],
        _row(p["lru_bx"]), _row(p["lru_lambda"]))
    do, d_mg = mla_post_bwd(dys[2], sv["o_mla"], proj)
    dq256, dk256, dvm = attn_bwd(sv["q256"], sv["k256"], sv["vm"], do, sv["o_mla"], softmax=True, dq=QPAD, q_blk0=0,
                                 k_blk0=0, v_blk0=0, name="mla_attn_bwd")
    d_lat, dw_uq, dw_ukv, d_qn, d_kvn = mla_prep_bwd(dq256, dk256, dvm, proj, _row(p["mla_q_norm"]), _row(p["mla_kv_norm"]),
                                                       wts["w_uq"], wts["w_ukv"], mla_tabs)
    dproj = jnp.concatenate([d_qk, dv.astype(bf16), d_rgate, d_lx, d_lg, d_lat, d_mg, dml, jnp.zeros((S, LANE), bf16)], axis=1)
    dh = matmul(dproj, wts["w_in"], dims="nt", M=S, N=D, K=NP, tm=tm, tn=1024, tk=768, out_dtype=f32, name="mm_dh")
    dw_in = matmul(sv["h"], dproj, dims="tn", M=D, N=NP, K=S, tm=D, tn=768, tk=tm, out_dtype=bf16, name="mm_dwin")
    dxl, d_sh, d_sc, d_gpre = pre_bwd(dh, sv["x"], _row(p["norm_pre"]), sv["sc"], dx)
    dmod = jnp.concatenate([d_sh, d_sc, d_rg], axis=1).reshape(-1)
    big = dict(w_in=dw_in, w_branch=dw_branch, w_out=dw_out, w_uq=dw_uq.astype(bf16), w_ukv=dw_ukv.astype(bf16))
    small = dict(norm_pre=d_gpre, norm_post=d_gpost, ret_gn=d_gn, lru_conv_w=d_cw, lru_conv_b=d_cb, lru_wa=d_wa, lru_ba=d_ba,
                 lru_wx=d_wx, lru_bx=d_bx, lru_lambda=d_lam, mla_q_norm=d_qn, mla_kv_norm=d_kvn)
    return dxl, dmod, big, small


def exchange_grads(big, l):
    picks = [lambda r, d: r.at[:, pl.ds(pl.multiple_of(d * (WIN_STRIDE * LANE), LANE), WIN)],
             lambda r, d: r.at[pl.ds(pl.multiple_of(d * BR_ROWS, 8), BR_ROWS), :],
             lambda r, d: r.at[pl.ds(pl.multiple_of(d * OUT_ROWS, 8), OUT_ROWS), :],
             lambda r, d: r.at[d], lambda r, d: r.at[d]]
    shapes = [(D, WIN), (BR_ROWS, D), (OUT_ROWS, D), (LORA, QPAD), (LORA, QPAD)]
    arrs = [big["w_in"], big["w_branch"], big["w_out"], big["w_uq"], big["w_ukv"]]
    return reduce_scatter_parts(arrs, picks, shapes, "rs_grads")


def kernel(x, c, positions, ada_w, ada_b, norm_pre, norm_post, w_in, ret_gn, lru_conv_w, lru_conv_b, lru_wa, lru_ba, lru_wx, lru_bx, lru_lambda, mla_q_norm, mla_w_uq, mla_kv_norm, mla_w_ukv, w_branch, w_out, loss_target, m_ada_w, m_ada_b, m_norm_pre, m_norm_post, m_w_in, m_ret_gn, m_lru_conv_w, m_lru_conv_b, m_lru_wa, m_lru_ba, m_lru_wx, m_lru_bx, m_lru_lambda, m_mla_q_norm, m_mla_w_uq, m_mla_kv_norm, m_mla_w_ukv, m_w_branch, m_w_out, v_ada_w, v_ada_b, v_norm_pre, v_norm_post, v_w_in, v_ret_gn, v_lru_conv_w, v_lru_conv_b, v_lru_wa, v_lru_ba, v_lru_wx, v_lru_bx, v_lru_lambda, v_mla_q_norm, v_mla_w_uq, v_mla_kv_norm, v_mla_w_ukv, v_w_branch, v_w_out):
    given = dict(locals())
    xi, yi, ci = _place()
    me = 4 * xi + 2 * yi + ci
    S = x.shape[1]
    sidx = shift_params(me)
    lgam = jnp.asarray(np.log1p(-np.exp2(-5.0 - np.arange(HEADS))), f32)
    tabs = rope_tables(positions[0])

    (g_small,) = all_gather([jnp.concatenate([c.reshape(16, LANE), lru_conv_w.reshape(16, LANE)], axis=0)], "ag_small")
    c16 = jnp.concatenate([g_small[:, :16].reshape(NDEV, D), jnp.zeros((ROWS16 - NDEV, D), f32)], axis=0)
    conv_w_all = g_small[:, 16:].reshape(NDEV, DEPTH, 4, LANE).transpose(1, 2, 0, 3).reshape(DEPTH, 4, BW)
    ada_b_mine = lax.dynamic_slice_in_dim(ada_b, me * ADA_SHARD, ADA_SHARD, axis=1).reshape(DEPTH, 1, ADA_SHARD)
    (g_mod,) = all_gather([ada_fwd(c16, ada_w, ada_b_mine)[:, :NDEV]], "ag_mod")
    mods = lax.dynamic_index_in_dim(g_mod, me, axis=2, keepdims=False).transpose(1, 0, 2).reshape(DEPTH, 3 * D)
    g_win, g_uq, g_ukv, g_br, g_out = all_gather(
        [win_pack(w_in, sidx), cast_pad(mla_w_uq, QPAD, "pack_wuq"), cast_pad(mla_w_ukv, QPAD, "pack_wukv"),
         cast_pad(w_branch, D, "pack_wbranch"), cast_pad(w_out, D, "pack_wout")], "ag_weights")

    params, wts = [], []
    for l in range(DEPTH):
        p = {n: given[n][l] for n in SMALL if n != "lru_conv_w"}
        p["conv_w"] = conv_w_all[l]
        params.append(p)
        wts.append(dict(w_in=win_assemble(g_win, l), w_uq=g_uq[:, l], w_ukv=g_ukv[:, l],
                        w_branch=g_br[:, l].reshape(3 * BW, D), w_out=g_out[:, l].reshape(D, D)))

    xl, saved = x[0], []
    for l in range(DEPTH):
        xl, sv = layer_fwd(xl, mods[l], params[l], wts[l], tabs, lgam)
        saved.append(sv)
    my_loss, dx = loss_head(xl, loss_target[0])
    loss = lax.psum(my_loss[0, 0], ("x", "y", "c"))

    dmods, smalls, grads = [None] * DEPTH, [None] * DEPTH, {n: [None] * DEPTH for n in WEIGHTS}
    for l in reversed(range(DEPTH)):
        dx, dmods[l], big, smalls[l] = layer_bwd(dx, saved[l], params[l], wts[l], tabs, lgam)
        st_win, st_br, st_out, st_uq, st_ukv = exchange_grads(big, l)
        grads["w_in"][l] = win_reduce(st_win, sidx)
        grads["w_branch"][l] = sum_parts(st_br, D, "sum_wbranch", tr=128)
        grads["w_out"][l] = sum_parts(st_out, D, "sum_wout", tr=128)
        grads["mla_w_uq"][l] = sum_parts(st_uq, WUQ_COLS, "sum_wuq")
        grads["mla_w_ukv"][l] = sum_parts(st_ukv, QPAD, "sum_wukv")

    flat = [jnp.stack(dmods).reshape(-1)] + [smalls[l][n].reshape(-1) for l in range(DEPTH) for n in SMALL]
    sizes = [int(t.shape[0]) for t in flat]
    (g_pack,) = all_gather([jnp.concatenate(flat).reshape(-1, LANE)], "ag_small_grads")
    rows = g_pack.shape[1]
    tot = sum_parts(g_pack, LANE, "sum_small_grads", tr=rows // 8).reshape(-1)
    offs = np.concatenate([[0], np.cumsum(sizes)])
    pieces = [tot[int(offs[i]):int(offs[i + 1])] for i in range(len(sizes))]
    grads["ada_b"] = pieces[0].reshape(DEPTH, 3 * D)
    for l in range(DEPTH):
        for j, n in enumerate(SMALL):
            piece = pieces[1 + l * len(SMALL) + j]
            if n == "lru_conv_w":
                piece = lax.dynamic_slice_in_dim(piece.reshape(4, BW), me * LANE, LANE, axis=1)
            grads[n][l] = piece.reshape(given[n].shape[1:])
    dmod_all = g_pack[:, :DEPTH * 3 * D // LANE].reshape(NDEV, DEPTH, 3 * D)
    dmod_mine = lax.dynamic_slice_in_dim(dmod_all, me * ADA_SHARD, ADA_SHARD, axis=2).transpose(1, 0, 2)
    dmod16 = jnp.concatenate([dmod_mine, jnp.zeros((DEPTH, ROWS16 - NDEV, ADA_SHARD), f32)], axis=1)
    grads["ada_w"] = ada_bwd(c16, dmod16)

    outs = {"grad": [], "delta": [], "m": [], "v": []}
    for n in WEIGHTS:
        g = grads[n] if not isinstance(grads[n], list) else jnp.stack(grads[n])
        delta, new_m, new_v = adamw(given[n], g, given["m_" + n], given["v_" + n])
        outs["grad"].append(g)
        outs["delta"].append(delta)
        outs["m"].append(new_m)
        outs["v"].append(new_v)
    return (loss, dx[None], *outs["grad"], *outs["delta"], *outs["m"], *outs["v"])
```

```python
import functools
import math

import numpy as np
import jax
import jax.numpy as jnp
from jax import lax
from jax.experimental import pallas as pl
from jax.experimental.pallas import tpu as pltpu
from jax.experimental.pallas import tpu_sc as plsc

f32 = jnp.float32
bf16 = jnp.bfloat16
SDS = jax.ShapeDtypeStruct

DEPTH = 4
D = 2048
HEADS = 8
DH = 128
BW = HEADS * DH
LANE = 128
CHUNK = 64
EPS = 1e-6
LRU_C = 8.0
NDEV = 8
VMEM_V7X = 64 * 1024 * 1024

Q0, K0, V0, RG0, LX0, LG0, MQ0, MKV0, MKR0, MG0, ML0 = 0, 8, 16, 24, 32, 40, 48, 52, 56, 57, 65
NB = 114
NP = NB * LANE
IN_W = 14400
SHARD_W = IN_W // NDEV
GAP_COL = 7232
GAP = 64
WIN = 1920
WIN_BLKS = WIN // LANE
WIN_STRIDE = 14
LORA = 512
QPAD = 256

ADAM_LR, ADAM_B1, ADAM_B2, ADAM_EPS, ADAM_WD, ADAM_STEP = 0.001, 0.9, 0.999, 1e-08, 0.01, 10
ADAM_BLOCK_ELEMS = 256 * 1024
MLA_SCALE = (128 + 64) ** -0.5
RET_SCALE = 128 ** -0.5


def _cp(vmem_mb=None, **kw):
    if vmem_mb is not None:
        kw["vmem_limit_bytes"] = min(vmem_mb * 1024 * 1024, VMEM_V7X - 8 * 1024 * 1024)
    return pltpu.CompilerParams(**kw)


def _sigmoid(x):
    return 1.0 / (1.0 + jnp.exp(-x))


def _silu(x):
    return x * _sigmoid(x)


def _dsilu(x):
    s = _sigmoid(x)
    return s * (1.0 + x * (1.0 - s))


def _softplus(x):
    return jnp.maximum(x, 0.0) + jnp.log(1.0 + jnp.exp(-jnp.abs(x)))


def _one_minus_exp(y):
    series = -y * (1.0 + y * (0.5 + y * (1.0 / 6.0)))
    return jnp.where(y > -1e-2, series, 1.0 - jnp.exp(y))


def _acc(ref, val, first):
    @pl.when(first)
    def _():
        ref[...] = val

    @pl.when(jnp.logical_not(first))
    def _():
        ref[...] += val


_DIMS = {"nn": (((1,), (0,)), ((), ())), "nt": (((1,), (1,)), ((), ())), "tn": (((0,), (0,)), ((), ()))}


def matmul(a, b, *, dims, M, N, K, tm, tn, tk, out_dtype, name, a_blk0=(0, 0), b_blk0=(0, 0), vmem_mb=48):
    nk = K // tk
    assert M % tm == 0 and N % tn == 0 and K % tk == 0
    dn = _DIMS[dims]

    def body(a_ref, b_ref, o_ref, *scr):
        part = lax.dot_general(a_ref[...].astype(bf16), b_ref[...].astype(bf16), dn, preferred_element_type=f32)
        if nk == 1:
            o_ref[...] = part.astype(out_dtype)
        else:
            acc = scr[0]
            k = pl.program_id(2)
            _acc(acc, part, k == 0)

            @pl.when(k == nk - 1)
            def _():
                o_ref[...] = acc[...].astype(out_dtype)

    ar, ac = a_blk0
    br, bc = b_blk0
    if dims == "nn":
        a_spec = pl.BlockSpec((tm, tk), lambda i, j, k: (i + ar, k + ac))
        b_spec = pl.BlockSpec((tk, tn), lambda i, j, k: (k + br, j + bc))
    elif dims == "nt":
        a_spec = pl.BlockSpec((tm, tk), lambda i, j, k: (i + ar, k + ac))
        b_spec = pl.BlockSpec((tn, tk), lambda i, j, k: (j + br, k + bc))
    else:
        a_spec = pl.BlockSpec((tk, tm), lambda i, j, k: (k + ar, i + ac))
        b_spec = pl.BlockSpec((tk, tn), lambda i, j, k: (k + br, j + bc))
    return pl.pallas_call(
        body, name=name, grid=(M // tm, N // tn, nk),
        in_specs=[a_spec, b_spec], out_specs=pl.BlockSpec((tm, tn), lambda i, j, k: (i, j)),
        out_shape=SDS((M, N), out_dtype),
        scratch_shapes=[] if nk == 1 else [pltpu.VMEM((tm, tn), f32)],
        compiler_params=_cp(vmem_mb, dimension_semantics=("parallel", "parallel", "arbitrary")),
    )(a, b)


def pre_fwd(x, g, sc, sh, *, tm=256):
    S = x.shape[0]

    def body(x_ref, g_ref, sc_ref, sh_ref, h_ref):
        xv = x_ref[...]
        r = lax.rsqrt(jnp.mean(xv * xv, axis=-1, keepdims=True) + EPS)
        h_ref[...] = (((xv * r) * g_ref[...]) * (1.0 + sc_ref[...]) + sh_ref[...]).astype(bf16)

    row = pl.BlockSpec((tm, D), lambda i: (i, 0))
    vec = pl.BlockSpec((1, D), lambda i: (0, 0))
    return pl.pallas_call(body, name="pre_fwd", grid=(S // tm,), in_specs=[row, vec, vec, vec], out_specs=row,
                          out_shape=SDS((S, D), bf16), compiler_params=_cp(32))(x, g, sc, sh)


def pre_bwd(dh, x, g, sc, dxo, *, tm=256):
    S = x.shape[0]

    def body(dh_ref, x_ref, g_ref, sc_ref, dxo_ref, dx_ref, dsh_ref, dsc_ref, dg_ref):
        first = pl.program_id(0) == 0
        xv, dhv, gv = x_ref[...], dh_ref[...], g_ref[...]
        one_sc = 1.0 + sc_ref[...]
        r = lax.rsqrt(jnp.mean(xv * xv, axis=-1, keepdims=True) + EPS)
        xh = xv * r
        t = dhv * xh
        dxh = dhv * gv * one_sc
        dx_ref[...] = r * (dxh - xh * jnp.mean(dxh * xh, axis=-1, keepdims=True)) + dxo_ref[...]
        _acc(dsh_ref, jnp.sum(dhv, axis=0, keepdims=True), first)
        _acc(dsc_ref, jnp.sum(t * gv, axis=0, keepdims=True), first)
        _acc(dg_ref, jnp.sum(t * one_sc, axis=0, keepdims=True), first)

    row = pl.BlockSpec((tm, D), lambda i: (i, 0))
    vec = pl.BlockSpec((1, D), lambda i: (0, 0))
    return pl.pallas_call(
        body, name="pre_bwd", grid=(S // tm,), in_specs=[row, row, vec, vec, row], out_specs=[row, vec, vec, vec],
        out_shape=[SDS((S, D), f32), SDS((1, D), f32), SDS((1, D), f32), SDS((1, D), f32)],
        compiler_params=_cp(40))(dh, x, g, sc, dxo)


def out_fwd(merged, w_out, x, rg, gp, *, tm=256):
    S = x.shape[0]

    def body(m_ref, w_ref, x_ref, rg_ref, gp_ref, y_ref, xn_ref):
        y = jnp.dot(m_ref[...], w_ref[...], preferred_element_type=f32)
        y_ref[...] = y
        r = lax.rsqrt(jnp.mean(y * y, axis=-1, keepdims=True) + EPS)
        xn_ref[...] = x_ref[...] + (1.0 + rg_ref[...]) * ((y * r) * gp_ref[...])

    row = pl.BlockSpec((tm, D), lambda i: (i, 0))
    vec = pl.BlockSpec((1, D), lambda i: (0, 0))
    return pl.pallas_call(
        body, name="out_fwd", grid=(S // tm,),
        in_specs=[row, pl.BlockSpec((D, D), lambda i: (0, 0)), row, vec, vec], out_specs=[row, row],
        out_shape=[SDS((S, D), f32), SDS((S, D), f32)], compiler_params=_cp(48))(merged, w_out, x, rg, gp)


def out_bwd(dxo, y, rg, gp, *, tm=256):
    S = y.shape[0]

    def body(dxo_ref, y_ref, rg_ref, gp_ref, dy_ref, drg_ref, dgp_ref):
        first = pl.program_id(0) == 0
        yv, dv, gv = y_ref[...], dxo_ref[...], gp_ref[...]
        r = lax.rsqrt(jnp.mean(yv * yv, axis=-1, keepdims=True) + EPS)
        yh = yv * r
        dn = dv * (1.0 + rg_ref[...])
        dyh = dn * gv
        dy_ref[...] = (r * (dyh - yh * jnp.mean(dyh * yh, axis=-1, keepdims=True))).astype(bf16)
        _acc(drg_ref, jnp.sum(dv * (yh * gv), axis=0, keepdims=True), first)
        _acc(dgp_ref, jnp.sum(dn * yh, axis=0, keepdims=True), first)

    row = pl.BlockSpec((tm, D), lambda i: (i, 0))
    vec = pl.BlockSpec((1, D), lambda i: (0, 0))
    return pl.pallas_call(
        body, name="out_bwd", grid=(S // tm,), in_specs=[row, row, vec, vec], out_specs=[row, vec, vec],
        out_shape=[SDS((S, D), bf16), SDS((1, D), f32), SDS((1, D), f32)], compiler_params=_cp(40))(dxo, y, rg, gp)


def _ml_spec(b, tm):
    return pl.BlockSpec((tm, LANE), lambda i, j: (i, ML0 + b * (D // LANE) + j))


def gate_fwd(proj, us, *, tm=2048):
    S = proj.shape[0]
    tm = min(tm, S)

    def body(ml0, ml1, ml2, u0, u1, u2, m_ref):
        acc = None
        for ml, u in ((ml0, u0), (ml1, u1), (ml2, u2)):
            t = _sigmoid(ml[...]) * u[...].astype(f32)
            acc = t if acc is None else acc + t
        m_ref[...] = acc.astype(bf16)

    blk = pl.BlockSpec((tm, LANE), lambda i, j: (i, j))
    return pl.pallas_call(
        body, name="gate_fwd", grid=(S // tm, D // LANE),
        in_specs=[_ml_spec(0, tm), _ml_spec(1, tm), _ml_spec(2, tm), blk, blk, blk], out_specs=blk,
        out_shape=SDS((S, D), bf16), compiler_params=_cp(32),
    )(proj, proj, proj, *us)


def gate_bwd(dmerged, proj, us, *, tm=2048):
    S = proj.shape[0]
    tm = min(tm, S)

    def body(dm_ref, ml0, ml1, ml2, u0, u1, u2, du0, du1, du2, dl0, dl1, dl2):
        dm = dm_ref[...]
        for ml, u, du, dl in ((ml0, u0, du0, dl0), (ml1, u1, du1, dl1), (ml2, u2, du2, dl2)):
            s = _sigmoid(ml[...])
            du[...] = (dm * s).astype(bf16)
            dl[...] = (dm * u[...].astype(f32) * (s * (1.0 - s))).astype(bf16)

    blk = pl.BlockSpec((tm, LANE), lambda i, j: (i, j))
    outs = pl.pallas_call(
        body, name="gate_bwd", grid=(S // tm, D // LANE),
        in_specs=[blk, _ml_spec(0, tm), _ml_spec(1, tm), _ml_spec(2, tm), blk, blk, blk], out_specs=[blk] * 6,
        out_shape=[SDS((S, D), bf16)] * 6, compiler_params=_cp(40),
    )(dmerged, proj, proj, proj, *us)
    return outs[:3], outs[3:]


def rope_tables(positions):
    pos = positions.astype(f32)[:, None]

    def cs(dim):
        inv = 10000.0 ** (-jnp.arange(0, dim, 2, dtype=f32) / dim)
        ang = pos * inv
        return jnp.cos(ang), jnp.sin(ang)

    c, s = cs(128)
    ret = (jnp.concatenate([c, c], 1), jnp.concatenate([-s, s], 1))
    c, s = cs(64)
    z32, z64 = jnp.zeros_like(c), jnp.zeros((c.shape[0], 64), f32)
    mla = (jnp.concatenate([c, c, z64], 1), jnp.concatenate([-s, z32, z64], 1), jnp.concatenate([z32, s, z64], 1))
    return ret, mla


def _rope_ret(x, c, s):
    return x * c + pltpu.roll(x, 64, 1) * s


def _rope_ret_t(dy, c, s):
    return dy * c + pltpu.roll(dy * s, 64, 1)


def _rope_mla(x, c, sa, sb):
    return x * c + pltpu.roll(x, 96, 1) * sa + pltpu.roll(x, 32, 1) * sb


def _rope_mla_t(dy, c, sa, sb):
    return dy * c + pltpu.roll(dy * sa, 32, 1) + pltpu.roll(dy * sb, 96, 1)


def ret_prep_fwd(proj, c, s, *, tm=512):
    S = proj.shape[0]
    tm = min(tm, S)

    def body(p_ref, c_ref, s_ref, o_ref):
        scale = jnp.where(pl.program_id(1) < HEADS, RET_SCALE, 1.0)
        o_ref[...] = (_rope_ret(p_ref[...], c_ref[...], s_ref[...]) * scale).astype(bf16)

    blk = pl.BlockSpec((tm, DH), lambda i, j: (i, j))
    tab = pl.BlockSpec((tm, DH), lambda i, j: (i, 0))
    return pl.pallas_call(body, name="ret_prep_fwd", grid=(S // tm, 2 * HEADS), in_specs=[blk, tab, tab], out_specs=blk,
                          out_shape=SDS((S, 2 * BW), bf16), compiler_params=_cp(32))(proj, c, s)


def ret_prep_bwd(dqk, c, s, *, tm=512):
    S = dqk.shape[0]
    tm = min(tm, S)

    def body(d_ref, c_ref, s_ref, o_ref):
        scale = jnp.where(pl.program_id(1) < HEADS, RET_SCALE, 1.0)
        o_ref[...] = _rope_ret_t(d_ref[...] * scale, c_ref[...], s_ref[...]).astype(bf16)

    blk = pl.BlockSpec((tm, DH), lambda i, j: (i, j))
    tab = pl.BlockSpec((tm, DH), lambda i, j: (i, 0))
    return pl.pallas_call(body, name="ret_prep_bwd", grid=(S // tm, 2 * HEADS), in_specs=[blk, tab, tab], out_specs=blk,
                          out_shape=SDS((S, 2 * BW), bf16), compiler_params=_cp(32))(dqk, c, s)


def _scores(q, k, qi, bq, lg, softmax):
    nk = k.shape[0]
    s = lax.dot_general(q, k, _DIMS["nt"], preferred_element_type=f32)
    ti = qi * bq + lax.broadcasted_iota(jnp.int32, (bq, nk), 0)
    tj = lax.broadcasted_iota(jnp.int32, (bq, nk), 1)
    mask = (tj // CHUNK) <= (ti // CHUNK)
    if softmax:
        s = jnp.where(mask, s, -1e30)
        e = jnp.exp(s - jnp.max(s, axis=-1, keepdims=True))
        return e / jnp.sum(e, axis=-1, keepdims=True), None
    w = jnp.where(mask, jnp.exp(lg * jnp.abs(ti - tj).astype(f32)), 0.0)
    return s * w, w


def _per_query_block(nq, fn):
    for qi in range(nq):
        pl.when(pl.program_id(1) == qi)(functools.partial(fn, qi))


def attn_fwd(q, k, v, gate_src, *, softmax, dq, q_blk0, k_blk0, v_blk0, gate_blk0, lgam=None, gn=None, bq=256, name):
    S = q.shape[0]
    bq = min(bq, S)
    assert bq % CHUNK == 0

    def body(*refs):
        if softmax:
            q_ref, k_ref, v_ref, g_ref, o_ref, y_ref = refs
            lg = None
        else:
            lg_ref, q_ref, k_ref, v_ref, g_ref, gn_ref, o_ref, y_ref = refs
            lg = lg_ref[pl.program_id(0)]

        def block(qi):
            keys = pl.ds(0, (qi + 1) * bq)
            p, _ = _scores(q_ref[...], k_ref[keys, :], qi, bq, lg, softmax)
            o = jnp.dot(p.astype(bf16), v_ref[keys, :].astype(bf16), preferred_element_type=f32)
            o_ref[...] = o
            if softmax:
                z = o
            else:
                oc = o - jnp.mean(o, axis=-1, keepdims=True)
                z = oc * lax.rsqrt(jnp.mean(oc * oc, axis=-1, keepdims=True) + EPS) * gn_ref[...]
            y_ref[...] = (z * _silu(g_ref[...])).astype(bf16)

        _per_query_block(S // bq, block)

    q_spec = pl.BlockSpec((bq, dq), lambda h, i: (i, q_blk0 + h))
    k_spec = pl.BlockSpec((S, dq), lambda h, i: (0, k_blk0 + h))
    v_spec = pl.BlockSpec((S, DH), lambda h, i: (0, v_blk0 + h))
    g_spec = pl.BlockSpec((bq, DH), lambda h, i: (i, gate_blk0 + h))
    o_spec = pl.BlockSpec((bq, DH), lambda h, i: (i, h))
    in_specs, args = [q_spec, k_spec, v_spec, g_spec], [q, k, v, gate_src]
    if not softmax:
        in_specs = [pl.BlockSpec(memory_space=pltpu.SMEM)] + in_specs + [pl.BlockSpec((1, DH), lambda h, i: (0, h))]
        args = [lgam] + args + [gn]
    return pl.pallas_call(
        body, name=name, grid=(HEADS, S // bq), in_specs=in_specs, out_specs=[o_spec, o_spec],
        out_shape=[SDS((S, BW), f32), SDS((S, BW), bf16)], compiler_params=_cp(48))(*args)


def attn_bwd(q, k, v, do, o, *, softmax, dq, q_blk0, k_blk0, v_blk0, lgam=None, bq=256, name):
    S = q.shape[0]
    bq = min(bq, S)

    def body(*refs):
        if softmax:
            q_ref, k_ref, v_ref, do_ref, o_ref, dq_ref, dk_ref, dv_ref = refs
            lg = None
        else:
            lg_ref, q_ref, k_ref, v_ref, do_ref, dq_ref, dk_ref, dv_ref = refs
            lg = lg_ref[pl.program_id(0)]

        def block(qi):
            keys = pl.ds(0, (qi + 1) * bq)
            if qi == 0:
                dk_ref[...] = jnp.zeros_like(dk_ref)
                dv_ref[...] = jnp.zeros_like(dv_ref)
            qv, kv, dov = q_ref[...], k_ref[keys, :], do_ref[...]
            p, w = _scores(qv, kv, qi, bq, lg, softmax)
            dp = lax.dot_general(dov, v_ref[keys, :].astype(bf16), _DIMS["nt"], preferred_element_type=f32)
            if softmax:
                delta = jnp.sum(dov.astype(f32) * o_ref[...], axis=-1, keepdims=True)
                ds = p * (dp - delta)
            else:
                ds = dp * w
            dsb = ds.astype(bf16)
            dq_ref[...] = jnp.dot(dsb, kv, preferred_element_type=f32)
            dv_ref[keys, :] += lax.dot_general(p.astype(bf16), dov, _DIMS["tn"], preferred_element_type=f32)
            dk_ref[keys, :] += lax.dot_general(dsb, qv, _DIMS["tn"], preferred_element_type=f32)

        _per_query_block(S // bq, block)

    q_spec = pl.BlockSpec((bq, dq), lambda h, i: (i, q_blk0 + h))
    k_spec = pl.BlockSpec((S, dq), lambda h, i: (0, k_blk0 + h))
    v_spec = pl.BlockSpec((S, DH), lambda h, i: (0, v_blk0 + h))
    o_spec = pl.BlockSpec((bq, DH), lambda h, i: (i, h))
    in_specs, args = [q_spec, k_spec, v_spec, o_spec], [q, k, v, do]
    if softmax:
        in_specs, args = in_specs + [o_spec], args + [o]
    else:
        in_specs, args = [pl.BlockSpec(memory_space=pltpu.SMEM)] + in_specs, [lgam] + args
    return pl.pallas_call(
        body, name=name, grid=(HEADS, S // bq), in_specs=in_specs,
        out_specs=[pl.BlockSpec((bq, dq), lambda h, i: (i, h)), pl.BlockSpec((S, dq), lambda h, i: (0, h)),
                   pl.BlockSpec((S, DH), lambda h, i: (0, h))],
        out_shape=[SDS((S, HEADS * dq), f32), SDS((S, HEADS * dq), f32), SDS((S, BW), f32)],
        compiler_params=_cp(52))(*args)


def ret_post_bwd(dy, o, proj, gn, *, tm=512):
    S = dy.shape[0]
    tm = min(tm, S)

    def body(dy_ref, o_ref, g_ref, gn_ref, do_ref, drg_ref, dgn_ref):
        ov, g, gnv, dyv = o_ref[...], g_ref[...], gn_ref[...], dy_ref[...]
        oc = ov - jnp.mean(ov, axis=-1, keepdims=True)
        rs = lax.rsqrt(jnp.mean(oc * oc, axis=-1, keepdims=True) + EPS)
        oh = oc * rs
        dz = dyv * _silu(g)
        drg_ref[...] = (dyv * (oh * gnv) * _dsilu(g)).astype(bf16)
        doh = dz * gnv
        do_ref[...] = (rs * (doh - jnp.mean(doh, axis=-1, keepdims=True)
                             - oh * jnp.mean(doh * oh, axis=-1, keepdims=True))).astype(bf16)
        _acc(dgn_ref, jnp.sum(dz * oh, axis=0, keepdims=True), pl.program_id(1) == 0)

    blk = pl.BlockSpec((tm, DH), lambda h, i: (i, h))
    vec = pl.BlockSpec((1, DH), lambda h, i: (0, h))
    return pl.pallas_call(
        body, name="ret_post_bwd", grid=(HEADS, S // tm),
        in_specs=[blk, blk, pl.BlockSpec((tm, DH), lambda h, i: (i, RG0 + h)), vec], out_specs=[blk, blk, vec],
        out_shape=[SDS((S, BW), bf16), SDS((S, BW), bf16), SDS((1, BW), f32)], compiler_params=_cp(32))(dy, o, proj, gn)


def mla_post_bwd(dy, o, proj, *, tm=512):
    S = dy.shape[0]
    tm = min(tm, S)

    def body(dy_ref, o_ref, g_ref, do_ref, dg_ref):
        g, dyv = g_ref[...], dy_ref[...]
        do_ref[...] = (dyv * _silu(g)).astype(bf16)
        dg_ref[...] = (dyv * o_ref[...] * _dsilu(g)).astype(bf16)

    blk = pl.BlockSpec((tm, DH), lambda i, h: (i, h))
    return pl.pallas_call(
        body, name="mla_post_bwd", grid=(S // tm, HEADS),
        in_specs=[blk, blk, pl.BlockSpec((tm, DH), lambda i, h: (i, MG0 + h))], out_specs=[blk, blk],
        out_shape=[SDS((S, BW), bf16), SDS((S, BW), bf16)], compiler_params=_cp(32))(dy, o, proj)


def mla_prep_fwd(proj, qnorm, kvnorm, wuq, wukv, tabs, *, tm=256, kv_blk=0):
    S = proj.shape[0]
    tm = min(tm, S)

    def body(mq_ref, mkv_ref, mkr_ref, qn_ref, kvn_ref, wuq_ref, wukv_ref, c_ref, sa_ref, sb_ref, q_ref, k_ref, v_ref):
        c, sa, sb = c_ref[...], sa_ref[...], sb_ref[...]
        mq, mkv = mq_ref[...], mkv_ref[...]
        qn = (mq * lax.rsqrt(jnp.mean(mq * mq, axis=-1, keepdims=True) + EPS) * qn_ref[...]).astype(bf16)
        kvn = (mkv * lax.rsqrt(jnp.mean(mkv * mkv, axis=-1, keepdims=True) + EPS) * kvn_ref[...]).astype(bf16)
        kr = _rope_mla(mkr_ref[...], c, sa, sb).astype(bf16)
        for h in range(HEADS):
            qh = jnp.dot(qn, wuq_ref[h], preferred_element_type=f32)
            q_ref[:, pl.ds(h * QPAD, DH)] = (qh[:, :DH] * MLA_SCALE).astype(bf16)
            q_ref[:, pl.ds(h * QPAD + DH, DH)] = (_rope_mla(qh[:, DH:], c, sa, sb) * MLA_SCALE).astype(bf16)
            kvh = jnp.dot(kvn, wukv_ref[h], preferred_element_type=f32)
            k_ref[:, pl.ds(h * QPAD, DH)] = kvh[:, :DH].astype(bf16)
            k_ref[:, pl.ds(h * QPAD + DH, DH)] = kr
            v_ref[:, pl.ds(h * DH, DH)] = kvh[:, DH:].astype(bf16)

    lat = lambda b: pl.BlockSpec((tm, LORA), lambda i: (i, b))
    tab = pl.BlockSpec((tm, DH), lambda i: (i, 0))
    vec = pl.BlockSpec((1, LORA), lambda i: (0, 0))
    wsp = pl.BlockSpec((HEADS, LORA, QPAD), lambda i: (0, 0, 0))
    wkv = pl.BlockSpec((HEADS, LORA, QPAD), lambda i: (0, 0, kv_blk))
    return pl.pallas_call(
        body, name="mla_prep_fwd", grid=(S // tm,),
        in_specs=[lat(MQ0 // 4), lat(MKV0 // 4), pl.BlockSpec((tm, DH), lambda i: (i, MKR0)), vec, vec, wsp, wkv,
                  tab, tab, tab],
        out_specs=[pl.BlockSpec((tm, HEADS * QPAD), lambda i: (i, 0))] * 2 + [pl.BlockSpec((tm, BW), lambda i: (i, 0))],
        out_shape=[SDS((S, HEADS * QPAD), bf16)] * 2 + [SDS((S, BW), bf16)], compiler_params=_cp(48),
    )(proj, proj, proj, qnorm, kvnorm, wuq, wukv, *tabs)


def mla_prep_bwd(dq256, dk256, dv, proj, qnorm, kvnorm, wuq, wukv, tabs, *, tm=256, kv_blk=0):
    S = proj.shape[0]
    tm = min(tm, S)

    def body(dq_ref, dk_ref, dv_ref, mq_ref, mkv_ref, qn_ref, kvn_ref, wuq_ref, wukv_ref, c_ref, sa_ref, sb_ref,
             dm_ref, dwuq_ref, dwukv_ref, dqn_ref, dkvn_ref):
        first = pl.program_id(0) == 0
        c, sa, sb = c_ref[...], sa_ref[...], sb_ref[...]
        mq, mkv = mq_ref[...], mkv_ref[...]
        rq = lax.rsqrt(jnp.mean(mq * mq, axis=-1, keepdims=True) + EPS)
        rkv = lax.rsqrt(jnp.mean(mkv * mkv, axis=-1, keepdims=True) + EPS)
        mqh, mkvh = mq * rq, mkv * rkv
        qn = (mqh * qn_ref[...]).astype(bf16)
        kvn = (mkvh * kvn_ref[...]).astype(bf16)
        dqn = jnp.zeros((tm, LORA), f32)
        dkvn = jnp.zeros((tm, LORA), f32)
        dkr = jnp.zeros((tm, DH), f32)
        for h in range(HEADS):
            da = dq_ref[:, pl.ds(h * QPAD, DH)] * MLA_SCALE
            db = _rope_mla_t(dq_ref[:, pl.ds(h * QPAD + DH, DH)] * MLA_SCALE, c, sa, sb)
            dqh = jnp.concatenate([da, db], axis=1).astype(bf16)
            dqn += lax.dot_general(dqh, wuq_ref[h], _DIMS["nt"], preferred_element_type=f32)
            _acc(dwuq_ref.at[h], lax.dot_general(qn, dqh, _DIMS["tn"], preferred_element_type=f32), first)
            dkr += dk_ref[:, pl.ds(h * QPAD + DH, DH)]
            dkvh = jnp.concatenate([dk_ref[:, pl.ds(h * QPAD, DH)], dv_ref[:, pl.ds(h * DH, DH)]], axis=1).astype(bf16)
            dkvn += lax.dot_general(dkvh, wukv_ref[h], _DIMS["nt"], preferred_element_type=f32)
            _acc(dwukv_ref.at[h], lax.dot_general(kvn, dkvh, _DIMS["tn"], preferred_element_type=f32), first)
        dmh = dqn * qn_ref[...]
        dm_ref[:, pl.ds(0, LORA)] = (rq * (dmh - mqh * jnp.mean(dmh * mqh, axis=-1, keepdims=True))).astype(bf16)
        dmh = dkvn * kvn_ref[...]
        dm_ref[:, pl.ds(LORA, LORA)] = (rkv * (dmh - mkvh * jnp.mean(dmh * mkvh, axis=-1, keepdims=True))).astype(bf16)
        dm_ref[:, pl.ds(2 * LORA, DH)] = _rope_mla_t(dkr, c, sa, sb).astype(bf16)
        _acc(dqn_ref, jnp.sum(dqn * mqh, axis=0, keepdims=True), first)
        _acc(dkvn_ref, jnp.sum(dkvn * mkvh, axis=0, keepdims=True), first)

    lat = lambda b: pl.BlockSpec((tm, LORA), lambda i: (i, b))
    tab = pl.BlockSpec((tm, DH), lambda i: (i, 0))
    vec = pl.BlockSpec((1, LORA), lambda i: (0, 0))
    wsp = pl.BlockSpec((HEADS, LORA, QPAD), lambda i: (0, 0, 0))
    wkv = pl.BlockSpec((HEADS, LORA, QPAD), lambda i: (0, 0, kv_blk))
    wide = pl.BlockSpec((tm, HEADS * QPAD), lambda i: (i, 0))
    return pl.pallas_call(
        body, name="mla_prep_bwd", grid=(S // tm,),
        in_specs=[wide, wide, pl.BlockSpec((tm, BW), lambda i: (i, 0)), lat(MQ0 // 4), lat(MKV0 // 4), vec, vec, wsp, wkv,
                  tab, tab, tab],
        out_specs=[pl.BlockSpec((tm, 2 * LORA + DH), lambda i: (i, 0)), wsp, wsp, vec, vec],
        out_shape=[SDS((S, 2 * LORA + DH), bf16), SDS((HEADS, LORA, QPAD), f32), SDS((HEADS, LORA, QPAD), f32),
                   SDS((1, LORA), f32), SDS((1, LORA), f32)],
        compiler_params=_cp(52),
    )(dq256, dk256, dv, proj, proj, qnorm, kvnorm, wuq, wukv, *tabs)


SUB = 8


def _scan_tiles(a_s, b_s, out, S, reverse):
    nt = S // SUB
    rows = lax.broadcasted_iota(jnp.int32, (SUB, LANE), 0)

    def tile(t, carry):
        base = pl.multiple_of((nt - 1 - t if reverse else t) * SUB, SUB)
        a, b = a_s[pl.ds(base, SUB), :], b_s[pl.ds(base, SUB), :]
        for d in (1, 2, 4):
            sh = SUB - d if reverse else d
            inside = rows < SUB - d if reverse else rows >= d
            a_n = jnp.where(inside, pltpu.roll(a, sh, 0), 1.0)
            b_n = jnp.where(inside, pltpu.roll(b, sh, 0), 0.0)
            b = a * b_n + b
            a = a * a_n
        res = a * carry + b
        out[pl.ds(base, SUB), :] = res
        edge = res[0:1, :] if reverse else res[SUB - 1:SUB, :]
        return jnp.broadcast_to(edge, (SUB, LANE))

    lax.fori_loop(0, nt, tile, jnp.zeros((SUB, LANE), f32))


def _shift_down(x, n, rows):
    return x if n == 0 else jnp.where(rows >= n, pltpu.roll(x, n, 0), 0.0)


def _shift_up(x, n, rows, S):
    return x if n == 0 else jnp.where(rows < S - n, pltpu.roll(x, S - n, 0), 0.0)


def _lru_gates(xb, cw, cb, wa, ba, wx, bx, lam, rows):
    xc = cb + cw[3:4, :] * xb
    for w in range(3):
        xc = xc + cw[w:w + 1, :] * _shift_down(xb, 3 - w, rows)
    xcb = xc.astype(bf16)
    r = _sigmoid(jnp.dot(xcb, wa, preferred_element_type=f32) + ba)
    i = _sigmoid(jnp.dot(xcb, wx, preferred_element_type=f32) + bx)
    sp = _softplus(-lam)
    la = (-LRU_C * r) * sp
    return xc, xcb, r, i, sp, la, jnp.exp(la)


def _lru_specs(S):
    col = lambda b0: pl.BlockSpec((S, LANE), lambda n: (0, b0 + n))
    vec = pl.BlockSpec((1, LANE), lambda n: (0, n))
    return col, vec, pl.BlockSpec((4, LANE), lambda n: (0, n)), pl.BlockSpec((1, LANE, LANE), lambda n: (n, 0, 0))


def lru_fwd(proj, cw, cb, wa, ba, wx, bx, lam):
    S = proj.shape[0]

    def body(x_ref, g_ref, cw_ref, cb_ref, wa_ref, ba_ref, wx_ref, bx_ref, lam_ref, h_ref, y_ref, a_s, b_s):
        rows = lax.broadcasted_iota(jnp.int32, (S, LANE), 0)
        xc, _, _, i, _, la, a = _lru_gates(x_ref[...], cw_ref[...], cb_ref[...], wa_ref[0].astype(bf16), ba_ref[...],
                                           wx_ref[0].astype(bf16), bx_ref[...], lam_ref[...], rows)
        a_s[...] = a
        b_s[...] = jnp.sqrt(_one_minus_exp(2.0 * la)) * (i * xc)
        _scan_tiles(a_s, b_s, h_ref, S, reverse=False)
        y_ref[...] = (h_ref[...] * _silu(g_ref[...])).astype(bf16)

    col, vec, cws, wsp = _lru_specs(S)
    return pl.pallas_call(
        body, name="lru_fwd", grid=(HEADS,),
        in_specs=[col(LX0), col(LG0), cws, vec, wsp, vec, wsp, vec, vec], out_specs=[col(0), col(0)],
        out_shape=[SDS((S, BW), f32), SDS((S, BW), bf16)],
        scratch_shapes=[pltpu.VMEM((S, LANE), f32), pltpu.VMEM((S, LANE), f32)], compiler_params=_cp(40),
    )(proj, proj, cw, cb, wa, ba, wx, bx, lam)


def lru_bwd(dy, h, proj, cw, cb, wa, ba, wx, bx, lam):
    S = proj.shape[0]

    def body(dy_ref, h_ref, x_ref, g_ref, cw_ref, cb_ref, wa_ref, ba_ref, wx_ref, bx_ref, lam_ref,
             dx_ref, dg_ref, dcw_ref, dcb_ref, dba_ref, dbx_ref, dlam_ref, dwa_ref, dwx_ref, a_s, b_s, l_s):
        rows = lax.broadcasted_iota(jnp.int32, (S, LANE), 0)
        xb, g, hv, dyv, cw, lam = x_ref[...], g_ref[...], h_ref[...], dy_ref[...], cw_ref[...], lam_ref[...]
        wa, wx = wa_ref[0].astype(bf16), wx_ref[0].astype(bf16)
        xc, xcb, r, i, sp, la, a = _lru_gates(xb, cw, cb_ref[...], wa, ba_ref[...], wx, bx_ref[...], lam, rows)
        dg_ref[...] = (dyv * hv * _dsilu(g)).astype(bf16)
        a_s[...] = _shift_up(a, 1, rows, S)
        b_s[...] = dyv * _silu(g)
        _scan_tiles(a_s, b_s, l_s, S, reverse=True)
        lmb = l_s[...]
        gated = i * xc
        sq = jnp.sqrt(_one_minus_exp(2.0 * la))
        dla = lmb * _shift_down(hv, 1, rows) * a - (lmb * gated) * (a * a) / sq
        dgated = lmb * sq
        dzr = (dla * (-LRU_C * sp)) * (r * (1.0 - r))
        dzi = (dgated * xc) * (i * (1.0 - i))
        dzrb, dzib = dzr.astype(bf16), dzi.astype(bf16)
        dxc = (dgated * i + lax.dot_general(dzrb, wa, _DIMS["nt"], preferred_element_type=f32)
               + lax.dot_general(dzib, wx, _DIMS["nt"], preferred_element_type=f32))
        dwa_ref[0] = lax.dot_general(xcb, dzrb, _DIMS["tn"], preferred_element_type=f32)
        dwx_ref[0] = lax.dot_general(xcb, dzib, _DIMS["tn"], preferred_element_type=f32)
        dba_ref[...] = jnp.sum(dzr, axis=0, keepdims=True)
        dbx_ref[...] = jnp.sum(dzi, axis=0, keepdims=True)
        dlam_ref[...] = jnp.sum(dla * (-LRU_C * r), axis=0, keepdims=True) * (-_sigmoid(-lam))
        dcb_ref[...] = jnp.sum(dxc, axis=0, keepdims=True)
        dxb = cw[3:4, :] * dxc
        dcw_ref[3:4, :] = jnp.sum(dxc * xb, axis=0, keepdims=True)
        for w in range(3):
            dxb = dxb + cw[w:w + 1, :] * _shift_up(dxc, 3 - w, rows, S)
            dcw_ref[w:w + 1, :] = jnp.sum(dxc * _shift_down(xb, 3 - w, rows), axis=0, keepdims=True)
        dx_ref[...] = dxb.astype(bf16)

    col, vec, cws, wsp = _lru_specs(S)
    scr = pltpu.VMEM((S, LANE), f32)
    return pl.pallas_call(
        body, name="lru_bwd", grid=(HEADS,),
        in_specs=[col(0), col(0), col(LX0), col(LG0), cws, vec, wsp, vec, wsp, vec, vec],
        out_specs=[col(0), col(0), cws, vec, vec, vec, vec, wsp, wsp],
        out_shape=[SDS((S, BW), bf16), SDS((S, BW), bf16), SDS((4, BW), f32)] + [SDS((1, BW), f32)] * 4
        + [SDS((HEADS, LANE, LANE), f32)] * 2,
        scratch_shapes=[scr, scr, scr], compiler_params=_cp(48),
    )(dy, h, proj, proj, cw, cb, wa, ba, wx, bx, lam)


def loss_head(y, target, *, tm=256):
    S = y.shape[0]

    def body(y_ref, t_ref, l_ref, d_ref):
        err = y_ref[...] - t_ref[...]
        d_ref[...] = err * (1.0 / D)
        part = jnp.sum(jnp.sum(err * err, axis=1, keepdims=True), axis=0, keepdims=True) * (0.5 / D)
        _acc(l_ref, jnp.broadcast_to(part, (1, LANE)), pl.program_id(0) == 0)

    row = pl.BlockSpec((tm, D), lambda i: (i, 0))
    return pl.pallas_call(
        body, name="loss_head", grid=(S // tm,), in_specs=[row, row],
        out_specs=[pl.BlockSpec((1, LANE), lambda i: (0, 0)), row],
        out_shape=[SDS((1, LANE), f32), SDS((S, D), f32)], compiler_params=_cp(32))(y, target)


def adamw(w, g, m, v):
    shape = w.shape
    cols = shape[-1]
    rows = math.prod(shape[:-1])
    tr = rows
    for cand in (2048, 1024, 512, 256, 128, 64, 32, 16, 8):
        if rows % cand == 0 and rows > cand and cand * cols <= ADAM_BLOCK_ELEMS:
            tr = cand
            break

    def body(w_ref, g_ref, m_ref, v_ref, d_ref, mo_ref, vo_ref):
        gv = g_ref[...]
        mn = ADAM_B1 * m_ref[...] + (1.0 - ADAM_B1) * gv
        vn = ADAM_B2 * v_ref[...] + (1.0 - ADAM_B2) * (gv * gv)
        mo_ref[...] = mn
        vo_ref[...] = vn
        m_hat = mn / (1.0 - ADAM_B1 ** ADAM_STEP)
        v_hat = vn / (1.0 - ADAM_B2 ** ADAM_STEP)
        d_ref[...] = -ADAM_LR * (m_hat / (jnp.sqrt(v_hat) + ADAM_EPS) + ADAM_WD * w_ref[...])

    blk = pl.BlockSpec((tr, cols), lambda i: (i, 0))
    flat = [t.reshape(rows, cols) for t in (w, g, m, v)]
    outs = pl.pallas_call(
        body, name="adamw", grid=(rows // tr,), in_specs=[blk] * 4, out_specs=[blk] * 3,
        out_shape=[SDS((rows, cols), f32)] * 3, compiler_params=_cp(48))(*flat)
    return tuple(o.reshape(shape) for o in outs)


ADA_SHARD = 3 * D // NDEV
ROWS16 = 16


def ada_fwd(c_all, ada_w, ada_b_mine):
    def body(c_ref, w_ref, b_ref, o_ref):
        o_ref[0] = jnp.dot(_silu(c_ref[...]).astype(bf16), w_ref[0].astype(bf16), preferred_element_type=f32) + b_ref[0]

    return pl.pallas_call(
        body, name="ada_fwd", grid=(DEPTH,),
        in_specs=[pl.BlockSpec((ROWS16, D), lambda l: (0, 0)), pl.BlockSpec((1, D, ADA_SHARD), lambda l: (l, 0, 0)),
                  pl.BlockSpec((1, 1, ADA_SHARD), lambda l: (l, 0, 0))],
        out_specs=pl.BlockSpec((1, ROWS16, ADA_SHARD), lambda l: (l, 0, 0)),
        out_shape=SDS((DEPTH, ROWS16, ADA_SHARD), f32), compiler_params=_cp(40))(c_all, ada_w, ada_b_mine)


def ada_bwd(c_all, dmod):
    def body(c_ref, d_ref, o_ref):
        o_ref[0] = lax.dot_general(_silu(c_ref[...]).astype(bf16), d_ref[0].astype(bf16), _DIMS["tn"],
                                   preferred_element_type=f32)

    return pl.pallas_call(
        body, name="ada_bwd", grid=(DEPTH,),
        in_specs=[pl.BlockSpec((ROWS16, D), lambda l: (0, 0)), pl.BlockSpec((1, ROWS16, ADA_SHARD), lambda l: (l, 0, 0))],
        out_specs=pl.BlockSpec((1, D, ADA_SHARD), lambda l: (l, 0, 0)),
        out_shape=SDS((DEPTH, D, ADA_SHARD), f32), compiler_params=_cp(40))(c_all, dmod)


def shift_params(me):
    start = SHARD_W * me
    return jnp.stack([start % LANE, (start + GAP) % LANE, jnp.clip(GAP_COL - start, 0, SHARD_W)]).astype(jnp.int32)


def win_pack(w, sidx, *, tm=256):
    def body(s_ref, w_ref, o_ref, scr):
        s1, s2, gi = s_ref[0], s_ref[1], s_ref[2]
        scr[...] = jnp.zeros_like(scr)
        scr[:, pl.ds(0, SHARD_W)] = w_ref[0]
        v = scr[...]
        j = lax.broadcasted_iota(jnp.int32, v.shape, 1)
        o_ref[0] = jnp.where(j - s1 < gi, pltpu.roll(v, s1, 1),
                             jnp.where(j - s2 >= gi, pltpu.roll(v, s2, 1), 0.0)).astype(bf16)

    return pl.pallas_call(
        body, name="win_pack", grid=(DEPTH, D // tm),
        in_specs=[pl.BlockSpec(memory_space=pltpu.SMEM), pl.BlockSpec((1, tm, SHARD_W), lambda l, i: (l, i, 0))],
        out_specs=pl.BlockSpec((1, tm, WIN), lambda l, i: (l, i, 0)),
        out_shape=SDS((DEPTH, D, WIN), bf16), scratch_shapes=[pltpu.VMEM((tm, WIN), f32)],
        compiler_params=_cp(32))(sidx, w)


def win_assemble(g):
    own = WIN_STRIDE * LANE
    tail = NP - NDEV * own
    assert tail == 2 * LANE

    def body(a_ref, b_ref, o_ref):
        o_ref[...] = a_ref[0]

        @pl.when(pl.program_id(0) > 0)
        def _():
            o_ref[:, pl.ds(0, LANE)] = a_ref[0, :, pl.ds(0, LANE)] + b_ref[0]

    main = pl.pallas_call(
        body, name="win_assemble", grid=(NDEV,),
        in_specs=[pl.BlockSpec((1, D, own), lambda k: (k, 0, 0)),
                  pl.BlockSpec((1, D, LANE), lambda k: (jnp.maximum(k - 1, 0), 0, WIN_BLKS - 1))],
        out_specs=pl.BlockSpec((D, own), lambda k: (0, k)),
        out_shape=SDS((D, NP), bf16), compiler_params=_cp(48))(g, g)

    def tail_body(_, b_ref, o_ref):
        o_ref[:, pl.ds(0, LANE)] = b_ref[0]
        o_ref[:, pl.ds(LANE, LANE)] = jnp.zeros((D, LANE), bf16)

    return pl.pallas_call(
        tail_body, name="win_assemble_tail", grid=(1,),
        in_specs=[pl.BlockSpec(memory_space=pl.ANY), pl.BlockSpec((1, D, LANE), lambda t: (NDEV - 1, 0, WIN_BLKS - 1))],
        out_specs=pl.BlockSpec((D, tail), lambda t: (0, NDEV * own // tail)),
        out_shape=SDS((D, NP), bf16), input_output_aliases={0: 0}, compiler_params=_cp(32))(main, g)


def win_reduce(stag, sidx, *, tm=128):
    def body(s_ref, g_ref, o_ref, scr):
        s1, s2, gi = s_ref[0], s_ref[1], s_ref[2]
        tot = g_ref[0].astype(f32)
        for d in range(1, NDEV):
            tot = tot + g_ref[d].astype(f32)
        i = lax.broadcasted_iota(jnp.int32, tot.shape, 1)
        scr[...] = jnp.where(i < gi, pltpu.roll(tot, WIN - s1, 1), pltpu.roll(tot, WIN - s2, 1))
        o_ref[...] = scr[:, pl.ds(0, SHARD_W)]

    return pl.pallas_call(
        body, name="win_reduce", grid=(D // tm,),
        in_specs=[pl.BlockSpec(memory_space=pltpu.SMEM), pl.BlockSpec((NDEV, tm, WIN), lambda i: (0, i, 0))],
        out_specs=pl.BlockSpec((tm, SHARD_W), lambda i: (i, 0)),
        out_shape=SDS((D, SHARD_W), f32), scratch_shapes=[pltpu.VMEM((tm, WIN), f32)],
        compiler_params=_cp(40))(sidx, stag)


def cast_pad(w, cols_out, name):
    L, R, C = w.shape

    def body(w_ref, o_ref, *scr):
        if cols_out == C:
            o_ref[0] = w_ref[0].astype(bf16)
        else:
            scr[0][...] = jnp.zeros_like(scr[0])
            scr[0][:, pl.ds(0, C)] = w_ref[0]
            o_ref[0] = scr[0][...].astype(bf16)

    return pl.pallas_call(
        body, name=name, grid=(L,), in_specs=[pl.BlockSpec((1, R, C), lambda l: (l, 0, 0))],
        out_specs=pl.BlockSpec((1, R, cols_out), lambda l: (l, 0, 0)), out_shape=SDS((L, R, cols_out), bf16),
        scratch_shapes=[] if cols_out == C else [pltpu.VMEM((R, cols_out), f32)], compiler_params=_cp(32))(w)


def pack_qkv(w_uq, w_ukv):
    L = w_uq.shape[0]

    def body(q_ref, kv_ref, o_ref, scr):
        scr[...] = jnp.zeros_like(scr)
        scr[:, pl.ds(0, WUQ_COLS)] = q_ref[0]
        o_ref[0, :, pl.ds(0, QPAD)] = scr[...].astype(bf16)
        o_ref[0, :, pl.ds(QPAD, QPAD)] = kv_ref[0].astype(bf16)

    return pl.pallas_call(
        body, name="pack_qkv", grid=(L,),
        in_specs=[pl.BlockSpec((1, LORA, WUQ_COLS), lambda l: (l, 0, 0)), pl.BlockSpec((1, LORA, QPAD), lambda l: (l, 0, 0))],
        out_specs=pl.BlockSpec((1, LORA, 2 * QPAD), lambda l: (l, 0, 0)), out_shape=SDS((L, LORA, 2 * QPAD), bf16),
        scratch_shapes=[pltpu.VMEM((LORA, QPAD), f32)], compiler_params=_cp(32))(w_uq, w_ukv)


def sum_parts(stag, cols_out, name, *, tr=None):
    _, R, C = stag.shape
    tr = R if tr is None else tr

    def body(g_ref, o_ref, *scr):
        tot = g_ref[0].astype(f32)
        for d in range(1, NDEV):
            tot = tot + g_ref[d].astype(f32)
        if cols_out == C:
            o_ref[...] = tot
        else:
            scr[0][...] = tot
            o_ref[...] = scr[0][:, pl.ds(0, cols_out)]

    return pl.pallas_call(
        body, name=name, grid=(R // tr,), in_specs=[pl.BlockSpec((NDEV, tr, C), lambda i: (0, i, 0))],
        out_specs=pl.BlockSpec((tr, cols_out), lambda i: (i, 0)), out_shape=SDS((R, cols_out), f32),
        scratch_shapes=[] if cols_out == C else [pltpu.VMEM((tr, C), f32)], compiler_params=_cp(40))(stag)


MESH_ID = pl.DeviceIdType.MESH
HBM_SPEC = pl.BlockSpec(memory_space=pltpu.HBM)


def _place():
    return lax.axis_index("x"), lax.axis_index("y"), lax.axis_index("c")


def all_gather(arrs, name):
    n = len(arrs)

    def body(*refs):
        ins, outs = refs[:n], refs[n:2 * n]
        send_sems, recv_sems, local_sems = refs[2 * n:]
        x, y, c = _place()
        me, sibling = (x, y, c), (x, y, 1 - c)
        chips = [(1 - x, y), (x, 1 - y), (1 - x, 1 - y)]

        def copy(a, k, block, to, src=None):
            slot = outs[a].at[4 * block[0] + 2 * block[1] + block[2]]
            return pltpu.make_async_remote_copy(
                src_ref=slot if src is None else src, dst_ref=slot, send_sem=send_sems.at[7 * a + k],
                recv_sem=recv_sems.at[7 * a + k], device_id=to, device_id_type=MESH_ID)

        mine = [pltpu.make_async_copy(ins[a], outs[a].at[4 * x + 2 * y + c], local_sems.at[a]) for a in range(n)]
        for cp in mine:
            cp.start()
        first = []
        for a in range(n):
            first.append(copy(a, 0, me, sibling, src=ins[a]))
            first += [copy(a, 1 + j, me, (*chip, c), src=ins[a]) for j, chip in enumerate(chips)]
        for cp in first:
            cp.start()
        passed = []
        for j, chip in enumerate(chips):
            for a in range(n):
                copy(a, 1 + j, (*chip, c), me).wait_recv()
                cp = copy(a, 4 + j, (*chip, c), sibling)
                cp.start()
                passed.append(cp)
        for a in range(n):
            copy(a, 0, sibling, me).wait_recv()
        for j, chip in enumerate(chips):
            for a in range(n):
                copy(a, 4 + j, (*chip, 1 - c), me).wait_recv()
        for cp in first + passed:
            cp.wait_send()
        for cp in mine:
            cp.wait()

    return pl.pallas_call(
        body, name=name, in_specs=[HBM_SPEC] * n, out_specs=[HBM_SPEC] * n,
        out_shape=[SDS((NDEV,) + a.shape, a.dtype) for a in arrs],
        scratch_shapes=[pltpu.SemaphoreType.DMA((7 * n,)), pltpu.SemaphoreType.DMA((7 * n,)),
                        pltpu.SemaphoreType.DMA((n,))],
    )(*arrs)


def reduce_scatter_parts(arrs, pick, shapes, name):
    n = len(arrs)

    def body(*refs):
        ins, outs = refs[:n], refs[n:2 * n]
        send_sems, recv_sems, local_sems = refs[2 * n:]
        x, y, c = _place()
        me = 4 * x + 2 * y + c
        mine = [pltpu.make_async_copy(pick[a](ins[a], me), outs[a].at[me], local_sems.at[a]) for a in range(n)]
        for cp in mine:
            cp.start()
        sent = []
        for r in range(1, NDEV):
            peer = (x ^ (r >> 2), y ^ ((r >> 1) & 1), c ^ (r & 1))
            pid = 4 * peer[0] + 2 * peer[1] + peer[2]
            for a in range(n):
                cp = pltpu.make_async_remote_copy(
                    src_ref=pick[a](ins[a], pid), dst_ref=outs[a].at[me], send_sem=send_sems.at[7 * a + r - 1],
                    recv_sem=recv_sems.at[7 * a + r - 1], device_id=peer, device_id_type=MESH_ID)
                cp.start()
                sent.append((cp, a, r, pid))
        for cp, a, r, pid in sent:
            pltpu.make_async_remote_copy(
                src_ref=pick[a](ins[a], pid), dst_ref=outs[a].at[pid], send_sem=send_sems.at[7 * a + r - 1],
                recv_sem=recv_sems.at[7 * a + r - 1], device_id=(x, y, c), device_id_type=MESH_ID).wait_recv()
        for cp, _, _, _ in sent:
            cp.wait_send()
        for cp in mine:
            cp.wait()

    return pl.pallas_call(
        body, name=name, in_specs=[HBM_SPEC] * n, out_specs=[HBM_SPEC] * n,
        out_shape=[SDS((NDEV,) + tuple(s), a.dtype) for s, a in zip(shapes, arrs)],
        scratch_shapes=[pltpu.SemaphoreType.DMA((7 * n,)), pltpu.SemaphoreType.DMA((7 * n,)),
                        pltpu.SemaphoreType.DMA((n,))],
    )(*arrs)


AG_COLLECTIVE_ID = 0
RS_COLLECTIVE_ID = 1


def _everyone_else(x, y, c):
    return [(x ^ (r >> 2), y ^ ((r >> 1) & 1), c ^ (r & 1)) for r in range(1, NDEV)]


def _rendezvous(sem, peers):
    for peer in peers:
        pl.semaphore_signal(sem, inc=1, device_id=peer, device_id_type=MESH_ID)
    pl.semaphore_wait(sem, NDEV - 1)


def _sequencer_call(body, arrs, out_types, name, collective_id):
    n = len(arrs)
    return pl.kernel(
        body, name=name, out_type=out_types, mesh=plsc.ScalarSubcoreMesh(axis_name="sequencer", num_cores=1),
        scratch_types=[pltpu.SemaphoreType.DMA((7 * n,)), pltpu.SemaphoreType.DMA((7 * n,)),
                       pltpu.SemaphoreType.DMA((n,)), pltpu.SemaphoreType.REGULAR],
        compiler_params=pltpu.CompilerParams(collective_id=collective_id),
    )(*arrs)


def seq_all_gather(arrs, name):
    n = len(arrs)

    def body(*refs):
        ins, outs = refs[:n], refs[n:2 * n]
        send_sems, recv_sems, local_sems, exit_sem = refs[2 * n:]
        x, y, c = _place()
        peers = _everyone_else(x, y, c)
        _rendezvous(pltpu.get_barrier_semaphore(), peers)
        me, sibling = (x, y, c), (x, y, 1 - c)
        chips = [(1 - x, y), (x, 1 - y), (1 - x, 1 - y)]

        def copy(a, k, block, to, src=None):
            slot = outs[a].at[4 * block[0] + 2 * block[1] + block[2]]
            return pltpu.make_async_remote_copy(
                src_ref=slot if src is None else src, dst_ref=slot, send_sem=send_sems.at[7 * a + k],
                recv_sem=recv_sems.at[7 * a + k], device_id=to, device_id_type=MESH_ID)

        mine = [pltpu.make_async_copy(ins[a], outs[a].at[4 * x + 2 * y + c], local_sems.at[a]) for a in range(n)]
        for cp in mine:
            cp.start()
        first = []
        for a in range(n):
            first.append(copy(a, 0, me, sibling, src=ins[a]))
            first += [copy(a, 1 + j, me, (*chip, c), src=ins[a]) for j, chip in enumerate(chips)]
        for cp in first:
            cp.start()
        passed = []
        for j, chip in enumerate(chips):
            for a in range(n):
                copy(a, 1 + j, (*chip, c), me).wait_recv()
                cp = copy(a, 4 + j, (*chip, c), sibling)
                cp.start()
                passed.append(cp)
        for a in range(n):
            copy(a, 0, sibling, me).wait_recv()
        for j, chip in enumerate(chips):
            for a in range(n):
                copy(a, 4 + j, (*chip, 1 - c), me).wait_recv()
        for cp in first + passed:
            cp.wait_send()
        for cp in mine:
            cp.wait()
        _rendezvous(exit_sem, peers)

    return _sequencer_call(body, arrs, [SDS((NDEV,) + a.shape, a.dtype) for a in arrs], name, AG_COLLECTIVE_ID)


def seq_reduce_scatter_parts(arrs, pick, shapes, name):
    n = len(arrs)

    def body(*refs):
        ins, outs = refs[:n], refs[n:2 * n]
        send_sems, recv_sems, local_sems, exit_sem = refs[2 * n:]
        x, y, c = _place()
        peers = _everyone_else(x, y, c)
        _rendezvous(pltpu.get_barrier_semaphore(), peers)
        me = 4 * x + 2 * y + c
        mine = [pltpu.make_async_copy(pick[a](ins[a], me), outs[a].at[me], local_sems.at[a]) for a in range(n)]
        for cp in mine:
            cp.start()
        sent = []
        for r, peer in enumerate(peers):
            pid = 4 * peer[0] + 2 * peer[1] + peer[2]
            for a in range(n):
                cp = pltpu.make_async_remote_copy(
                    src_ref=pick[a](ins[a], pid), dst_ref=outs[a].at[me], send_sem=send_sems.at[7 * a + r],
                    recv_sem=recv_sems.at[7 * a + r], device_id=peer, device_id_type=MESH_ID)
                cp.start()
                sent.append((cp, a, r, pid))
        for cp, a, r, pid in sent:
            pltpu.make_async_remote_copy(
                src_ref=pick[a](ins[a], pid), dst_ref=outs[a].at[pid], send_sem=send_sems.at[7 * a + r],
                recv_sem=recv_sems.at[7 * a + r], device_id=(x, y, c), device_id_type=MESH_ID).wait_recv()
        for cp, _, _, _ in sent:
            cp.wait_send()
        for cp in mine:
            cp.wait()
        _rendezvous(exit_sem, peers)

    return _sequencer_call(body, arrs, [SDS((NDEV,) + tuple(s), a.dtype) for s, a in zip(shapes, arrs)], name,
                           RS_COLLECTIVE_ID)


WEIGHTS = ("ada_w", "ada_b", "norm_pre", "norm_post", "w_in", "ret_gn", "lru_conv_w", "lru_conv_b", "lru_wa", "lru_ba",
           "lru_wx", "lru_bx", "lru_lambda", "mla_q_norm", "mla_w_uq", "mla_kv_norm", "mla_w_ukv", "w_branch", "w_out")
SMALL = ("norm_pre", "norm_post", "ret_gn", "lru_conv_w", "lru_conv_b", "lru_wa", "lru_ba", "lru_wx", "lru_bx",
         "lru_lambda", "mla_q_norm", "mla_kv_norm")
BR_ROWS = 3 * BW // NDEV
OUT_ROWS = D // NDEV
WUQ_COLS = 192


def _row(v):
    return v.reshape(1, -1)


def layer_fwd(xl, mod, p, wts, tabs, lgam):
    S = xl.shape[0]
    tm = min(S, 2048)
    sh, sc, rg = _row(mod[:D]), _row(mod[D:2 * D]), _row(mod[2 * D:])
    ret_tabs, mla_tabs = tabs
    h = pre_fwd(xl, _row(p["norm_pre"]), sc, sh)
    proj = matmul(h, wts["w_in"], dims="nn", M=S, N=NP, K=D, tm=tm, tn=768, tk=D, out_dtype=f32, name="mm_in")
    qk = ret_prep_fwd(proj, *ret_tabs)
    o_ret, y_ret = attn_fwd(qk, qk, proj, proj, softmax=False, dq=DH, q_blk0=0, k_blk0=HEADS, v_blk0=V0, gate_blk0=RG0,
                            lgam=lgam, gn=_row(p["ret_gn"]), name="ret_attn_fwd")
    h_lru, y_lru = lru_fwd(proj, p["conv_w"], _row(p["lru_conv_b"]), p["lru_wa"], _row(p["lru_ba"]), p["lru_wx"],
                           _row(p["lru_bx"]), _row(p["lru_lambda"]))
    q256, k256, vm = mla_prep_fwd(proj, _row(p["mla_q_norm"]), _row(p["mla_kv_norm"]), wts["w_qkv"], wts["w_qkv"], mla_tabs,
                                  kv_blk=1)
    o_mla, y_mla = attn_fwd(q256, k256, vm, proj, softmax=True, dq=QPAD, q_blk0=0, k_blk0=0, v_blk0=0, gate_blk0=MG0,
                            name="mla_attn_fwd")
    ys = (y_ret, y_lru, y_mla)
    us = [matmul(ys[b], wts["w_branch"], dims="nn", M=S, N=D, K=BW, tm=min(S, 1024), tn=1024, tk=BW,
                 out_dtype=bf16, name="mm_branch", b_blk0=(b, 0)) for b in range(3)]
    merged = gate_fwd(proj, us)
    y, x_next = out_fwd(merged, wts["w_out"], xl, rg, _row(p["norm_post"]))
    saved = dict(x=xl, h=h, proj=proj, qk=qk, o_ret=o_ret, h_lru=h_lru, q256=q256, k256=k256, vm=vm, o_mla=o_mla,
                 ys=ys, us=us, merged=merged, y=y, sc=sc, rg=rg)
    return x_next, saved


def layer_bwd(dx, sv, p, wts, tabs, lgam):
    S = dx.shape[0]
    tm = min(S, 2048)
    ret_tabs, mla_tabs = tabs
    proj = sv["proj"]
    dy, d_rg, d_gpost = out_bwd(dx, sv["y"], sv["rg"], _row(p["norm_post"]))
    dmerged = matmul(dy, wts["w_out"], dims="nt", M=S, N=D, K=D, tm=min(S, 1024), tn=1024, tk=D, out_dtype=f32, name="mm_dmerged")
    dw_out = matmul(sv["merged"], dy, dims="tn", M=D, N=D, K=S, tm=1024, tn=1024, tk=min(S, 1024), out_dtype=bf16, name="mm_dwout")
    du, dml = gate_bwd(dmerged, proj, sv["us"])
    tmb, tkb = min(S, 1024), min(S, 1024)
    dys = [matmul(du[b], wts["w_branch"], dims="nt", M=S, N=BW, K=D, tm=tmb, tn=BW, tk=D, out_dtype=f32, name="mm_dybranch",
                  b_blk0=(b, 0)) for b in range(3)]
    dw_branch = jnp.concatenate(
        [matmul(sv["ys"][b], du[b], dims="tn", M=BW, N=D, K=S, tm=BW, tn=1024, tk=tkb, out_dtype=bf16, name="mm_dwbranch")
         for b in range(3)], axis=0)
    do, d_rgate, d_gn = ret_post_bwd(dys[0], sv["o_ret"], proj, _row(p["ret_gn"]))
    dq, dk, dv = attn_bwd(sv["qk"], sv["qk"], proj, do, None, softmax=False, dq=DH, q_blk0=0, k_blk0=HEADS, v_blk0=V0,
                          lgam=lgam, name="ret_attn_bwd")
    d_qk = ret_prep_bwd(jnp.concatenate([dq, dk], axis=1), *ret_tabs)
    d_lx, d_lg, d_cw, d_cb, d_ba, d_bx, d_lam, d_wa, d_wx = lru_bwd(
        dys[1], sv["h_lru"], proj, p["conv_w"], _row(p["lru_conv_b"]), p["lru_wa"], _row(p["lru_ba"]), p["lru_wx"],
        _row(p["lru_bx"]), _row(p["lru_lambda"]))
    do, d_mg = mla_post_bwd(dys[2], sv["o_mla"], proj)
    dq256, dk256, dvm = attn_bwd(sv["q256"], sv["k256"], sv["vm"], do, sv["o_mla"], softmax=True, dq=QPAD, q_blk0=0,
                                 k_blk0=0, v_blk0=0, name="mla_attn_bwd")
    d_lat, dw_uq, dw_ukv, d_qn, d_kvn = mla_prep_bwd(dq256, dk256, dvm, proj, _row(p["mla_q_norm"]), _row(p["mla_kv_norm"]),
                                                       wts["w_qkv"], wts["w_qkv"], mla_tabs, kv_blk=1)
    dproj = jnp.concatenate([d_qk, dv.astype(bf16), d_rgate, d_lx, d_lg, d_lat, d_mg, *dml, jnp.zeros((S, LANE), bf16)], axis=1)
    dh = matmul(dproj, wts["w_in"], dims="nt", M=S, N=D, K=NP, tm=tm, tn=1024, tk=768, out_dtype=f32, name="mm_dh")
    dw_in = matmul(sv["h"], dproj, dims="tn", M=D, N=NP, K=S, tm=D, tn=768, tk=tm, out_dtype=bf16, name="mm_dwin")
    dxl, d_sh, d_sc, d_gpre = pre_bwd(dh, sv["x"], _row(p["norm_pre"]), sv["sc"], dx)
    dmod = jnp.concatenate([d_sh, d_sc, d_rg], axis=1).reshape(-1)
    big = dict(w_in=dw_in, w_branch=dw_branch, w_out=dw_out, w_qkv=jnp.concatenate([dw_uq, dw_ukv], axis=2).astype(bf16))
    small = dict(norm_pre=d_gpre, norm_post=d_gpost, ret_gn=d_gn, lru_conv_w=d_cw, lru_conv_b=d_cb, lru_wa=d_wa, lru_ba=d_ba,
                 lru_wx=d_wx, lru_bx=d_bx, lru_lambda=d_lam, mla_q_norm=d_qn, mla_kv_norm=d_kvn)
    return dxl, dmod, big, small


def exchange_grads(big, l):
    picks = [lambda r, d: r.at[:, pl.ds(pl.multiple_of(d * (WIN_STRIDE * LANE), LANE), WIN)],
             lambda r, d: r.at[pl.ds(pl.multiple_of(d * BR_ROWS, 8), BR_ROWS), :],
             lambda r, d: r.at[pl.ds(pl.multiple_of(d * OUT_ROWS, 8), OUT_ROWS), :],
             lambda r, d: r.at[d]]
    shapes = [(D, WIN), (BR_ROWS, D), (OUT_ROWS, D), (LORA, 2 * QPAD)]
    arrs = [big["w_in"], big["w_branch"], big["w_out"], big["w_qkv"]]
    return seq_reduce_scatter_parts(arrs, picks, shapes, f"rs_grads_{l}")


def kernel(x, c, positions, ada_w, ada_b, norm_pre, norm_post, w_in, ret_gn, lru_conv_w, lru_conv_b, lru_wa, lru_ba, lru_wx, lru_bx, lru_lambda, mla_q_norm, mla_w_uq, mla_kv_norm, mla_w_ukv, w_branch, w_out, loss_target, m_ada_w, m_ada_b, m_norm_pre, m_norm_post, m_w_in, m_ret_gn, m_lru_conv_w, m_lru_conv_b, m_lru_wa, m_lru_ba, m_lru_wx, m_lru_bx, m_lru_lambda, m_mla_q_norm, m_mla_w_uq, m_mla_kv_norm, m_mla_w_ukv, m_w_branch, m_w_out, v_ada_w, v_ada_b, v_norm_pre, v_norm_post, v_w_in, v_ret_gn, v_lru_conv_w, v_lru_conv_b, v_lru_wa, v_lru_ba, v_lru_wx, v_lru_bx, v_lru_lambda, v_mla_q_norm, v_mla_w_uq, v_mla_kv_norm, v_mla_w_ukv, v_w_branch, v_w_out):
    given = dict(locals())
    xi, yi, ci = _place()
    me = 4 * xi + 2 * yi + ci
    S = x.shape[1]
    sidx = shift_params(me)
    lgam = jnp.asarray(np.log1p(-np.exp2(-5.0 - np.arange(HEADS))), f32)
    tabs = rope_tables(positions[0])

    (g_small,) = all_gather([jnp.concatenate([c.reshape(16, LANE), lru_conv_w.reshape(16, LANE)], axis=0)], "ag_small")
    c16 = jnp.concatenate([g_small[:, :16].reshape(NDEV, D), jnp.zeros((ROWS16 - NDEV, D), f32)], axis=0)
    conv_w_all = g_small[:, 16:].reshape(NDEV, DEPTH, 4, LANE).transpose(1, 2, 0, 3).reshape(DEPTH, 4, BW)
    ada_b_mine = lax.dynamic_slice_in_dim(ada_b, me * ADA_SHARD, ADA_SHARD, axis=1).reshape(DEPTH, 1, ADA_SHARD)
    (g_mod,) = all_gather([ada_fwd(c16, ada_w, ada_b_mine)[:, :NDEV]], "ag_mod")
    mods = lax.dynamic_index_in_dim(g_mod, me, axis=2, keepdims=False).transpose(1, 0, 2).reshape(DEPTH, 3 * D)
    packed = (win_pack(w_in, sidx), cast_pad(w_branch, D, "pack_wbranch"), cast_pad(w_out, D, "pack_wout"),
              pack_qkv(mla_w_uq, mla_w_ukv))
    gathered = [seq_all_gather([t[l] for t in packed], f"ag_weights_{l}") for l in range(DEPTH)]

    params, wts = [], []
    for l in range(DEPTH):
        p = {n: given[n][l] for n in SMALL if n != "lru_conv_w"}
        p["conv_w"] = conv_w_all[l]
        params.append(p)
        g_win, g_br, g_out, g_qkv = gathered[l]
        wts.append(dict(w_in=win_assemble(g_win), w_qkv=g_qkv, w_branch=g_br.reshape(3 * BW, D), w_out=g_out.reshape(D, D)))

    xl, saved = x[0], []
    for l in range(DEPTH):
        xl, sv = layer_fwd(xl, mods[l], params[l], wts[l], tabs, lgam)
        saved.append(sv)
    my_loss, dx = loss_head(xl, loss_target[0])
    loss = lax.psum(my_loss[0, 0], ("x", "y", "c"))

    dmods, smalls, staged, grads = [None] * DEPTH, [None] * DEPTH, [None] * DEPTH, {n: [None] * DEPTH for n in WEIGHTS}
    for l in reversed(range(DEPTH)):
        dx, dmods[l], big, smalls[l] = layer_bwd(dx, saved[l], params[l], wts[l], tabs, lgam)
        staged[l] = exchange_grads(big, l)
    for l in reversed(range(DEPTH)):
        st_win, st_br, st_out, st_qkv = staged[l]
        grads["w_in"][l] = win_reduce(st_win, sidx)
        grads["w_branch"][l] = sum_parts(st_br, D, "sum_wbranch", tr=128)
        grads["w_out"][l] = sum_parts(st_out, D, "sum_wout", tr=128)
        g_qkv = sum_parts(st_qkv, 2 * QPAD, "sum_wqkv")
        grads["mla_w_uq"][l] = g_qkv[:, :WUQ_COLS]
        grads["mla_w_ukv"][l] = g_qkv[:, QPAD:]

    flat = [jnp.stack(dmods).reshape(-1)] + [smalls[l][n].reshape(-1) for l in range(DEPTH) for n in SMALL]
    sizes = [int(t.shape[0]) for t in flat]
    (g_pack,) = all_gather([jnp.concatenate(flat).reshape(-1, LANE)], "ag_small_grads")
    rows = g_pack.shape[1]
    tot = sum_parts(g_pack, LANE, "sum_small_grads", tr=rows // 8).reshape(-1)
    offs = np.concatenate([[0], np.cumsum(sizes)])
    pieces = [tot[int(offs[i]):int(offs[i + 1])] for i in range(len(sizes))]
    grads["ada_b"] = pieces[0].reshape(DEPTH, 3 * D)
    for l in range(DEPTH):
        for j, n in enumerate(SMALL):
            piece = pieces[1 + l * len(SMALL) + j]
            if n == "lru_conv_w":
                piece = lax.dynamic_slice_in_dim(piece.reshape(4, BW), me * LANE, LANE, axis=1)
            grads[n][l] = piece.reshape(given[n].shape[1:])
    dmod_all = g_pack[:, :DEPTH * 3 * D // LANE].reshape(NDEV, DEPTH, 3 * D)
    dmod_mine = lax.dynamic_slice_in_dim(dmod_all, me * ADA_SHARD, ADA_SHARD, axis=2).transpose(1, 0, 2)
    dmod16 = jnp.concatenate([dmod_mine, jnp.zeros((DEPTH, ROWS16 - NDEV, ADA_SHARD), f32)], axis=1)
    grads["ada_w"] = ada_bwd(c16, dmod16)

    outs = {"grad": [], "delta": [], "m": [], "v": []}
    for n in WEIGHTS:
        g = grads[n] if not isinstance(grads[n], list) else jnp.stack(grads[n])
        delta, new_m, new_v = adamw(given[n], g, given["m_" + n], given["v_" + n])
        outs["grad"].append(g)
        outs["delta"].append(delta)
        outs["m"].append(new_m)
        outs["v"].append(new_v)
    return (loss, dx[None], *outs["grad"], *outs["delta"], *outs["m"], *outs["v"])
```

```python
import functools
import math

import numpy as np
import jax
import jax.numpy as jnp
from jax import lax
from jax.experimental import pallas as pl
from jax.experimental.pallas import tpu as pltpu
from jax.experimental.pallas import tpu_sc as plsc

f32 = jnp.float32
bf16 = jnp.bfloat16
SDS = jax.ShapeDtypeStruct

DEPTH = 4
D = 2048
HEADS = 8
DH = 128
BW = HEADS * DH
LANE = 128
CHUNK = 64
EPS = 1e-6
LRU_C = 8.0
NDEV = 8
VMEM_V7X = 64 * 1024 * 1024

Q0, K0, V0, RG0, LX0, LG0, MQ0, MKV0, MKR0, MG0, ML0 = 0, 8, 16, 24, 32, 40, 48, 52, 56, 57, 65
NB = 114
NP = NB * LANE
IN_W = 14400
SHARD_W = IN_W // NDEV
GAP_COL = 7232
GAP = 64
WIN = 1920
WIN_BLKS = WIN // LANE
WIN_STRIDE = 14
LORA = 512
QPAD = 256

ADAM_LR, ADAM_B1, ADAM_B2, ADAM_EPS, ADAM_WD, ADAM_STEP = 0.001, 0.9, 0.999, 1e-08, 0.01, 10
ADAM_BLOCK_ELEMS = 256 * 1024
MLA_SCALE = (128 + 64) ** -0.5
RET_SCALE = 128 ** -0.5


def _cp(vmem_mb=None, **kw):
    if vmem_mb is not None:
        kw["vmem_limit_bytes"] = min(vmem_mb * 1024 * 1024, VMEM_V7X - 8 * 1024 * 1024)
    return pltpu.CompilerParams(**kw)


def _sigmoid(x):
    return 1.0 / (1.0 + jnp.exp(-x))


def _silu(x):
    return x * _sigmoid(x)


def _dsilu(x):
    s = _sigmoid(x)
    return s * (1.0 + x * (1.0 - s))


def _softplus(x):
    return jnp.maximum(x, 0.0) + jnp.log(1.0 + jnp.exp(-jnp.abs(x)))


def _one_minus_exp(y):
    series = -y * (1.0 + y * (0.5 + y * (1.0 / 6.0)))
    return jnp.where(y > -1e-2, series, 1.0 - jnp.exp(y))


def _acc(ref, val, first):
    @pl.when(first)
    def _():
        ref[...] = val

    @pl.when(jnp.logical_not(first))
    def _():
        ref[...] += val


_DIMS = {"nn": (((1,), (0,)), ((), ())), "nt": (((1,), (1,)), ((), ())), "tn": (((0,), (0,)), ((), ()))}


def matmul(a, b, *, dims, M, N, K, tm, tn, tk, out_dtype, name, a_blk0=(0, 0), b_blk0=(0, 0), vmem_mb=48):
    nk = K // tk
    assert M % tm == 0 and N % tn == 0 and K % tk == 0
    dn = _DIMS[dims]

    def body(a_ref, b_ref, o_ref, *scr):
        part = lax.dot_general(a_ref[...].astype(bf16), b_ref[...].astype(bf16), dn, preferred_element_type=f32)
        if nk == 1:
            o_ref[...] = part.astype(out_dtype)
        else:
            acc = scr[0]
            k = pl.program_id(2)
            _acc(acc, part, k == 0)

            @pl.when(k == nk - 1)
            def _():
                o_ref[...] = acc[...].astype(out_dtype)

    ar, ac = a_blk0
    br, bc = b_blk0
    if dims == "nn":
        a_spec = pl.BlockSpec((tm, tk), lambda i, j, k: (i + ar, k + ac))
        b_spec = pl.BlockSpec((tk, tn), lambda i, j, k: (k + br, j + bc))
    elif dims == "nt":
        a_spec = pl.BlockSpec((tm, tk), lambda i, j, k: (i + ar, k + ac))
        b_spec = pl.BlockSpec((tn, tk), lambda i, j, k: (j + br, k + bc))
    else:
        a_spec = pl.BlockSpec((tk, tm), lambda i, j, k: (k + ar, i + ac))
        b_spec = pl.BlockSpec((tk, tn), lambda i, j, k: (k + br, j + bc))
    return pl.pallas_call(
        body, name=name, grid=(M // tm, N // tn, nk),
        in_specs=[a_spec, b_spec], out_specs=pl.BlockSpec((tm, tn), lambda i, j, k: (i, j)),
        out_shape=SDS((M, N), out_dtype),
        scratch_shapes=[] if nk == 1 else [pltpu.VMEM((tm, tn), f32)],
        compiler_params=_cp(vmem_mb, dimension_semantics=("parallel", "parallel", "arbitrary")),
    )(a, b)


def pre_fwd(x, g, sc, sh, *, tm=256):
    S = x.shape[0]

    def body(x_ref, g_ref, sc_ref, sh_ref, h_ref):
        xv = x_ref[...]
        r = lax.rsqrt(jnp.mean(xv * xv, axis=-1, keepdims=True) + EPS)
        h_ref[...] = (((xv * r) * g_ref[...]) * (1.0 + sc_ref[...]) + sh_ref[...]).astype(bf16)

    row = pl.BlockSpec((tm, D), lambda i: (i, 0))
    vec = pl.BlockSpec((1, D), lambda i: (0, 0))
    return pl.pallas_call(body, name="pre_fwd", grid=(S // tm,), in_specs=[row, vec, vec, vec], out_specs=row,
                          out_shape=SDS((S, D), bf16), compiler_params=_cp(32))(x, g, sc, sh)


def pre_bwd(dh, x, g, sc, dxo, *, tm=256, after=()):
    S = x.shape[0]

    def body(dh_ref, x_ref, g_ref, sc_ref, dxo_ref, *rest):
        dx_ref, dsh_ref, dsc_ref, dg_ref = rest[len(after):]
        first = pl.program_id(0) == 0
        xv, dhv, gv = x_ref[...], dh_ref[...], g_ref[...]
        one_sc = 1.0 + sc_ref[...]
        r = lax.rsqrt(jnp.mean(xv * xv, axis=-1, keepdims=True) + EPS)
        xh = xv * r
        t = dhv * xh
        dxh = dhv * gv * one_sc
        dx_ref[...] = r * (dxh - xh * jnp.mean(dxh * xh, axis=-1, keepdims=True)) + dxo_ref[...]
        _acc(dsh_ref, jnp.sum(dhv, axis=0, keepdims=True), first)
        _acc(dsc_ref, jnp.sum(t * gv, axis=0, keepdims=True), first)
        _acc(dg_ref, jnp.sum(t * one_sc, axis=0, keepdims=True), first)

    row = pl.BlockSpec((tm, D), lambda i: (i, 0))
    vec = pl.BlockSpec((1, D), lambda i: (0, 0))
    return pl.pallas_call(
        body, name="pre_bwd", grid=(S // tm,),
        in_specs=[row, row, vec, vec, row] + [pl.BlockSpec(memory_space=pl.ANY)] * len(after), out_specs=[row, vec, vec, vec],
        out_shape=[SDS((S, D), f32), SDS((1, D), f32), SDS((1, D), f32), SDS((1, D), f32)],
        compiler_params=_cp(40))(dh, x, g, sc, dxo, *after)


def out_fwd(merged, w_out, x, rg, gp, *, tm=256):
    S = x.shape[0]

    def body(m_ref, w_ref, x_ref, rg_ref, gp_ref, y_ref, xn_ref):
        y = jnp.dot(m_ref[...], w_ref[...], preferred_element_type=f32)
        y_ref[...] = y
        r = lax.rsqrt(jnp.mean(y * y, axis=-1, keepdims=True) + EPS)
        xn_ref[...] = x_ref[...] + (1.0 + rg_ref[...]) * ((y * r) * gp_ref[...])

    row = pl.BlockSpec((tm, D), lambda i: (i, 0))
    vec = pl.BlockSpec((1, D), lambda i: (0, 0))
    return pl.pallas_call(
        body, name="out_fwd", grid=(S // tm,),
        in_specs=[row, pl.BlockSpec((D, D), lambda i: (0, 0)), row, vec, vec], out_specs=[row, row],
        out_shape=[SDS((S, D), f32), SDS((S, D), f32)], compiler_params=_cp(48))(merged, w_out, x, rg, gp)


def out_bwd(dxo, y, rg, gp, *, tm=256):
    S = y.shape[0]

    def body(dxo_ref, y_ref, rg_ref, gp_ref, dy_ref, drg_ref, dgp_ref):
        first = pl.program_id(0) == 0
        yv, dv, gv = y_ref[...], dxo_ref[...], gp_ref[...]
        r = lax.rsqrt(jnp.mean(yv * yv, axis=-1, keepdims=True) + EPS)
        yh = yv * r
        dn = dv * (1.0 + rg_ref[...])
        dyh = dn * gv
        dy_ref[...] = (r * (dyh - yh * jnp.mean(dyh * yh, axis=-1, keepdims=True))).astype(bf16)
        _acc(drg_ref, jnp.sum(dv * (yh * gv), axis=0, keepdims=True), first)
        _acc(dgp_ref, jnp.sum(dn * yh, axis=0, keepdims=True), first)

    row = pl.BlockSpec((tm, D), lambda i: (i, 0))
    vec = pl.BlockSpec((1, D), lambda i: (0, 0))
    return pl.pallas_call(
        body, name="out_bwd", grid=(S // tm,), in_specs=[row, row, vec, vec], out_specs=[row, vec, vec],
        out_shape=[SDS((S, D), bf16), SDS((1, D), f32), SDS((1, D), f32)], compiler_params=_cp(40))(dxo, y, rg, gp)


def _ml_spec(b, tm):
    return pl.BlockSpec((tm, LANE), lambda i, j: (i, ML0 + b * (D // LANE) + j))


def gate_fwd(proj, us, *, tm=2048):
    S = proj.shape[0]
    tm = min(tm, S)

    def body(ml0, ml1, ml2, u0, u1, u2, m_ref):
        acc = None
        for ml, u in ((ml0, u0), (ml1, u1), (ml2, u2)):
            t = _sigmoid(ml[...]) * u[...].astype(f32)
            acc = t if acc is None else acc + t
        m_ref[...] = acc.astype(bf16)

    blk = pl.BlockSpec((tm, LANE), lambda i, j: (i, j))
    return pl.pallas_call(
        body, name="gate_fwd", grid=(S // tm, D // LANE),
        in_specs=[_ml_spec(0, tm), _ml_spec(1, tm), _ml_spec(2, tm), blk, blk, blk], out_specs=blk,
        out_shape=SDS((S, D), bf16), compiler_params=_cp(32),
    )(proj, proj, proj, *us)


def gate_bwd(dmerged, proj, us, *, tm=2048):
    S = proj.shape[0]
    tm = min(tm, S)

    def body(dm_ref, ml0, ml1, ml2, u0, u1, u2, du0, du1, du2, dl0, dl1, dl2):
        dm = dm_ref[...]
        for ml, u, du, dl in ((ml0, u0, du0, dl0), (ml1, u1, du1, dl1), (ml2, u2, du2, dl2)):
            s = _sigmoid(ml[...])
            du[...] = (dm * s).astype(bf16)
            dl[...] = (dm * u[...].astype(f32) * (s * (1.0 - s))).astype(bf16)

    blk = pl.BlockSpec((tm, LANE), lambda i, j: (i, j))
    outs = pl.pallas_call(
        body, name="gate_bwd", grid=(S // tm, D // LANE),
        in_specs=[blk, _ml_spec(0, tm), _ml_spec(1, tm), _ml_spec(2, tm), blk, blk, blk], out_specs=[blk] * 6,
        out_shape=[SDS((S, D), bf16)] * 6, compiler_params=_cp(40),
    )(dmerged, proj, proj, proj, *us)
    return outs[:3], outs[3:]


def rope_tables(positions):
    pos = positions.astype(f32)[:, None]

    def cs(dim):
        inv = 10000.0 ** (-jnp.arange(0, dim, 2, dtype=f32) / dim)
        ang = pos * inv
        return jnp.cos(ang), jnp.sin(ang)

    c, s = cs(128)
    ret = (jnp.concatenate([c, c], 1), jnp.concatenate([-s, s], 1))
    c, s = cs(64)
    z32, z64 = jnp.zeros_like(c), jnp.zeros((c.shape[0], 64), f32)
    mla = (jnp.concatenate([c, c, z64], 1), jnp.concatenate([-s, z32, z64], 1), jnp.concatenate([z32, s, z64], 1))
    return ret, mla


def _rope_ret(x, c, s):
    return x * c + pltpu.roll(x, 64, 1) * s


def _rope_ret_t(dy, c, s):
    return dy * c + pltpu.roll(dy * s, 64, 1)


def _rope_mla(x, c, sa, sb):
    return x * c + pltpu.roll(x, 96, 1) * sa + pltpu.roll(x, 32, 1) * sb


def _rope_mla_t(dy, c, sa, sb):
    return dy * c + pltpu.roll(dy * sa, 32, 1) + pltpu.roll(dy * sb, 96, 1)


def ret_prep_fwd(proj, c, s, *, tm=512):
    S = proj.shape[0]
    tm = min(tm, S)

    def body(p_ref, c_ref, s_ref, o_ref):
        scale = jnp.where(pl.program_id(1) < HEADS, RET_SCALE, 1.0)
        o_ref[...] = (_rope_ret(p_ref[...], c_ref[...], s_ref[...]) * scale).astype(bf16)

    blk = pl.BlockSpec((tm, DH), lambda i, j: (i, j))
    tab = pl.BlockSpec((tm, DH), lambda i, j: (i, 0))
    return pl.pallas_call(body, name="ret_prep_fwd", grid=(S // tm, 2 * HEADS), in_specs=[blk, tab, tab], out_specs=blk,
                          out_shape=SDS((S, 2 * BW), bf16), compiler_params=_cp(32))(proj, c, s)


def ret_prep_bwd(dqk, c, s, *, tm=512):
    S = dqk.shape[0]
    tm = min(tm, S)

    def body(d_ref, c_ref, s_ref, o_ref):
        scale = jnp.where(pl.program_id(1) < HEADS, RET_SCALE, 1.0)
        o_ref[...] = _rope_ret_t(d_ref[...] * scale, c_ref[...], s_ref[...]).astype(bf16)

    blk = pl.BlockSpec((tm, DH), lambda i, j: (i, j))
    tab = pl.BlockSpec((tm, DH), lambda i, j: (i, 0))
    return pl.pallas_call(body, name="ret_prep_bwd", grid=(S // tm, 2 * HEADS), in_specs=[blk, tab, tab], out_specs=blk,
                          out_shape=SDS((S, 2 * BW), bf16), compiler_params=_cp(32))(dqk, c, s)


def _scores(q, k, qi, bq, lg, softmax):
    nk = k.shape[0]
    s = lax.dot_general(q, k, _DIMS["nt"], preferred_element_type=f32)
    ti = qi * bq + lax.broadcasted_iota(jnp.int32, (bq, nk), 0)
    tj = lax.broadcasted_iota(jnp.int32, (bq, nk), 1)
    mask = (tj // CHUNK) <= (ti // CHUNK)
    if softmax:
        s = jnp.where(mask, s, -1e30)
        e = jnp.exp(s - jnp.max(s, axis=-1, keepdims=True))
        return e / jnp.sum(e, axis=-1, keepdims=True), None
    w = jnp.where(mask, jnp.exp(lg * jnp.abs(ti - tj).astype(f32)), 0.0)
    return s * w, w


def _per_query_block(nq, fn):
    for qi in range(nq):
        pl.when(pl.program_id(1) == qi)(functools.partial(fn, qi))


def attn_fwd(q, k, v, gate_src, *, softmax, dq, q_blk0, k_blk0, v_blk0, gate_blk0, lgam=None, gn=None, bq=256, name):
    S = q.shape[0]
    bq = min(bq, S)
    assert bq % CHUNK == 0

    def body(*refs):
        if softmax:
            q_ref, k_ref, v_ref, g_ref, o_ref, y_ref = refs
            lg = None
        else:
            lg_ref, q_ref, k_ref, v_ref, g_ref, gn_ref, o_ref, y_ref = refs
            lg = lg_ref[pl.program_id(0)]

        def block(qi):
            keys = pl.ds(0, (qi + 1) * bq)
            p, _ = _scores(q_ref[...], k_ref[keys, :], qi, bq, lg, softmax)
            o = jnp.dot(p.astype(bf16), v_ref[keys, :].astype(bf16), preferred_element_type=f32)
            o_ref[...] = o
            if softmax:
                z = o
            else:
                oc = o - jnp.mean(o, axis=-1, keepdims=True)
                z = oc * lax.rsqrt(jnp.mean(oc * oc, axis=-1, keepdims=True) + EPS) * gn_ref[...]
            y_ref[...] = (z * _silu(g_ref[...])).astype(bf16)

        _per_query_block(S // bq, block)

    q_spec = pl.BlockSpec((bq, dq), lambda h, i: (i, q_blk0 + h))
    k_spec = pl.BlockSpec((S, dq), lambda h, i: (0, k_blk0 + h))
    v_spec = pl.BlockSpec((S, DH), lambda h, i: (0, v_blk0 + h))
    g_spec = pl.BlockSpec((bq, DH), lambda h, i: (i, gate_blk0 + h))
    o_spec = pl.BlockSpec((bq, DH), lambda h, i: (i, h))
    in_specs, args = [q_spec, k_spec, v_spec, g_spec], [q, k, v, gate_src]
    if not softmax:
        in_specs = [pl.BlockSpec(memory_space=pltpu.SMEM)] + in_specs + [pl.BlockSpec((1, DH), lambda h, i: (0, h))]
        args = [lgam] + args + [gn]
    return pl.pallas_call(
        body, name=name, grid=(HEADS, S // bq), in_specs=in_specs, out_specs=[o_spec, o_spec],
        out_shape=[SDS((S, BW), f32), SDS((S, BW), bf16)], compiler_params=_cp(48))(*args)


def attn_bwd(q, k, v, do, o, *, softmax, dq, q_blk0, k_blk0, v_blk0, lgam=None, bq=256, name):
    S = q.shape[0]
    bq = min(bq, S)

    def body(*refs):
        if softmax:
            q_ref, k_ref, v_ref, do_ref, o_ref, dq_ref, dk_ref, dv_ref = refs
            lg = None
        else:
            lg_ref, q_ref, k_ref, v_ref, do_ref, dq_ref, dk_ref, dv_ref = refs
            lg = lg_ref[pl.program_id(0)]

        def block(qi):
            keys = pl.ds(0, (qi + 1) * bq)
            if qi == 0:
                dk_ref[...] = jnp.zeros_like(dk_ref)
                dv_ref[...] = jnp.zeros_like(dv_ref)
            qv, kv, dov = q_ref[...], k_ref[keys, :], do_ref[...]
            p, w = _scores(qv, kv, qi, bq, lg, softmax)
            dp = lax.dot_general(dov, v_ref[keys, :].astype(bf16), _DIMS["nt"], preferred_element_type=f32)
            if softmax:
                delta = jnp.sum(dov.astype(f32) * o_ref[...], axis=-1, keepdims=True)
                ds = p * (dp - delta)
            else:
                ds = dp * w
            dsb = ds.astype(bf16)
            dq_ref[...] = jnp.dot(dsb, kv, preferred_element_type=f32)
            dv_ref[keys, :] += lax.dot_general(p.astype(bf16), dov, _DIMS["tn"], preferred_element_type=f32)
            dk_ref[keys, :] += lax.dot_general(dsb, qv, _DIMS["tn"], preferred_element_type=f32)

        _per_query_block(S // bq, block)

    q_spec = pl.BlockSpec((bq, dq), lambda h, i: (i, q_blk0 + h))
    k_spec = pl.BlockSpec((S, dq), lambda h, i: (0, k_blk0 + h))
    v_spec = pl.BlockSpec((S, DH), lambda h, i: (0, v_blk0 + h))
    o_spec = pl.BlockSpec((bq, DH), lambda h, i: (i, h))
    in_specs, args = [q_spec, k_spec, v_spec, o_spec], [q, k, v, do]
    if softmax:
        in_specs, args = in_specs + [o_spec], args + [o]
    else:
        in_specs, args = [pl.BlockSpec(memory_space=pltpu.SMEM)] + in_specs, [lgam] + args
    return pl.pallas_call(
        body, name=name, grid=(HEADS, S // bq), in_specs=in_specs,
        out_specs=[pl.BlockSpec((bq, dq), lambda h, i: (i, h)), pl.BlockSpec((S, dq), lambda h, i: (0, h)),
                   pl.BlockSpec((S, DH), lambda h, i: (0, h))],
        out_shape=[SDS((S, HEADS * dq), f32), SDS((S, HEADS * dq), f32), SDS((S, BW), f32)],
        compiler_params=_cp(52))(*args)


def ret_post_bwd(dy, o, proj, gn, *, tm=512):
    S = dy.shape[0]
    tm = min(tm, S)

    def body(dy_ref, o_ref, g_ref, gn_ref, do_ref, drg_ref, dgn_ref):
        ov, g, gnv, dyv = o_ref[...], g_ref[...], gn_ref[...], dy_ref[...]
        oc = ov - jnp.mean(ov, axis=-1, keepdims=True)
        rs = lax.rsqrt(jnp.mean(oc * oc, axis=-1, keepdims=True) + EPS)
        oh = oc * rs
        dz = dyv * _silu(g)
        drg_ref[...] = (dyv * (oh * gnv) * _dsilu(g)).astype(bf16)
        doh = dz * gnv
        do_ref[...] = (rs * (doh - jnp.mean(doh, axis=-1, keepdims=True)
                             - oh * jnp.mean(doh * oh, axis=-1, keepdims=True))).astype(bf16)
        _acc(dgn_ref, jnp.sum(dz * oh, axis=0, keepdims=True), pl.program_id(1) == 0)

    blk = pl.BlockSpec((tm, DH), lambda h, i: (i, h))
    vec = pl.BlockSpec((1, DH), lambda h, i: (0, h))
    return pl.pallas_call(
        body, name="ret_post_bwd", grid=(HEADS, S // tm),
        in_specs=[blk, blk, pl.BlockSpec((tm, DH), lambda h, i: (i, RG0 + h)), vec], out_specs=[blk, blk, vec],
        out_shape=[SDS((S, BW), bf16), SDS((S, BW), bf16), SDS((1, BW), f32)], compiler_params=_cp(32))(dy, o, proj, gn)


def mla_post_bwd(dy, o, proj, *, tm=512):
    S = dy.shape[0]
    tm = min(tm, S)

    def body(dy_ref, o_ref, g_ref, do_ref, dg_ref):
        g, dyv = g_ref[...], dy_ref[...]
        do_ref[...] = (dyv * _silu(g)).astype(bf16)
        dg_ref[...] = (dyv * o_ref[...] * _dsilu(g)).astype(bf16)

    blk = pl.BlockSpec((tm, DH), lambda i, h: (i, h))
    return pl.pallas_call(
        body, name="mla_post_bwd", grid=(S // tm, HEADS),
        in_specs=[blk, blk, pl.BlockSpec((tm, DH), lambda i, h: (i, MG0 + h))], out_specs=[blk, blk],
        out_shape=[SDS((S, BW), bf16), SDS((S, BW), bf16)], compiler_params=_cp(32))(dy, o, proj)


def mla_prep_fwd(proj, qnorm, kvnorm, wuq, wukv, tabs, *, tm=256, kv_blk=0):
    S = proj.shape[0]
    tm = min(tm, S)

    def body(mq_ref, mkv_ref, mkr_ref, qn_ref, kvn_ref, wuq_ref, wukv_ref, c_ref, sa_ref, sb_ref, q_ref, k_ref, v_ref):
        c, sa, sb = c_ref[...], sa_ref[...], sb_ref[...]
        mq, mkv = mq_ref[...], mkv_ref[...]
        qn = (mq * lax.rsqrt(jnp.mean(mq * mq, axis=-1, keepdims=True) + EPS) * qn_ref[...]).astype(bf16)
        kvn = (mkv * lax.rsqrt(jnp.mean(mkv * mkv, axis=-1, keepdims=True) + EPS) * kvn_ref[...]).astype(bf16)
        kr = _rope_mla(mkr_ref[...], c, sa, sb).astype(bf16)
        for h in range(HEADS):
            qh = jnp.dot(qn, wuq_ref[h], preferred_element_type=f32)
            q_ref[:, pl.ds(h * QPAD, DH)] = (qh[:, :DH] * MLA_SCALE).astype(bf16)
            q_ref[:, pl.ds(h * QPAD + DH, DH)] = (_rope_mla(qh[:, DH:], c, sa, sb) * MLA_SCALE).astype(bf16)
            kvh = jnp.dot(kvn, wukv_ref[h], preferred_element_type=f32)
            k_ref[:, pl.ds(h * QPAD, DH)] = kvh[:, :DH].astype(bf16)
            k_ref[:, pl.ds(h * QPAD + DH, DH)] = kr
            v_ref[:, pl.ds(h * DH, DH)] = kvh[:, DH:].astype(bf16)

    lat = lambda b: pl.BlockSpec((tm, LORA), lambda i: (i, b))
    tab = pl.BlockSpec((tm, DH), lambda i: (i, 0))
    vec = pl.BlockSpec((1, LORA), lambda i: (0, 0))
    wsp = pl.BlockSpec((HEADS, LORA, QPAD), lambda i: (0, 0, 0))
    wkv = pl.BlockSpec((HEADS, LORA, QPAD), lambda i: (0, 0, kv_blk))
    return pl.pallas_call(
        body, name="mla_prep_fwd", grid=(S // tm,),
        in_specs=[lat(MQ0 // 4), lat(MKV0 // 4), pl.BlockSpec((tm, DH), lambda i: (i, MKR0)), vec, vec, wsp, wkv,
                  tab, tab, tab],
        out_specs=[pl.BlockSpec((tm, HEADS * QPAD), lambda i: (i, 0))] * 2 + [pl.BlockSpec((tm, BW), lambda i: (i, 0))],
        out_shape=[SDS((S, HEADS * QPAD), bf16)] * 2 + [SDS((S, BW), bf16)], compiler_params=_cp(48),
    )(proj, proj, proj, qnorm, kvnorm, wuq, wukv, *tabs)


def mla_prep_bwd(dq256, dk256, dv, proj, qnorm, kvnorm, wuq, wukv, tabs, *, tm=256, kv_blk=0):
    S = proj.shape[0]
    tm = min(tm, S)

    def body(dq_ref, dk_ref, dv_ref, mq_ref, mkv_ref, qn_ref, kvn_ref, wuq_ref, wukv_ref, c_ref, sa_ref, sb_ref,
             dm_ref, dwuq_ref, dwukv_ref, dqn_ref, dkvn_ref):
        first = pl.program_id(0) == 0
        c, sa, sb = c_ref[...], sa_ref[...], sb_ref[...]
        mq, mkv = mq_ref[...], mkv_ref[...]
        rq = lax.rsqrt(jnp.mean(mq * mq, axis=-1, keepdims=True) + EPS)
        rkv = lax.rsqrt(jnp.mean(mkv * mkv, axis=-1, keepdims=True) + EPS)
        mqh, mkvh = mq * rq, mkv * rkv
        qn = (mqh * qn_ref[...]).astype(bf16)
        kvn = (mkvh * kvn_ref[...]).astype(bf16)
        dqn = jnp.zeros((tm, LORA), f32)
        dkvn = jnp.zeros((tm, LORA), f32)
        dkr = jnp.zeros((tm, DH), f32)
        for h in range(HEADS):
            da = dq_ref[:, pl.ds(h * QPAD, DH)] * MLA_SCALE
            db = _rope_mla_t(dq_ref[:, pl.ds(h * QPAD + DH, DH)] * MLA_SCALE, c, sa, sb)
            dqh = jnp.concatenate([da, db], axis=1).astype(bf16)
            dqn += lax.dot_general(dqh, wuq_ref[h], _DIMS["nt"], preferred_element_type=f32)
            _acc(dwuq_ref.at[h], lax.dot_general(qn, dqh, _DIMS["tn"], preferred_element_type=f32), first)
            dkr += dk_ref[:, pl.ds(h * QPAD + DH, DH)]
            dkvh = jnp.concatenate([dk_ref[:, pl.ds(h * QPAD, DH)], dv_ref[:, pl.ds(h * DH, DH)]], axis=1).astype(bf16)
            dkvn += lax.dot_general(dkvh, wukv_ref[h], _DIMS["nt"], preferred_element_type=f32)
            _acc(dwukv_ref.at[h], lax.dot_general(kvn, dkvh, _DIMS["tn"], preferred_element_type=f32), first)
        dmh = dqn * qn_ref[...]
        dm_ref[:, pl.ds(0, LORA)] = (rq * (dmh - mqh * jnp.mean(dmh * mqh, axis=-1, keepdims=True))).astype(bf16)
        dmh = dkvn * kvn_ref[...]
        dm_ref[:, pl.ds(LORA, LORA)] = (rkv * (dmh - mkvh * jnp.mean(dmh * mkvh, axis=-1, keepdims=True))).astype(bf16)
        dm_ref[:, pl.ds(2 * LORA, DH)] = _rope_mla_t(dkr, c, sa, sb).astype(bf16)
        _acc(dqn_ref, jnp.sum(dqn * mqh, axis=0, keepdims=True), first)
        _acc(dkvn_ref, jnp.sum(dkvn * mkvh, axis=0, keepdims=True), first)

    lat = lambda b: pl.BlockSpec((tm, LORA), lambda i: (i, b))
    tab = pl.BlockSpec((tm, DH), lambda i: (i, 0))
    vec = pl.BlockSpec((1, LORA), lambda i: (0, 0))
    wsp = pl.BlockSpec((HEADS, LORA, QPAD), lambda i: (0, 0, 0))
    wkv = pl.BlockSpec((HEADS, LORA, QPAD), lambda i: (0, 0, kv_blk))
    wide = pl.BlockSpec((tm, HEADS * QPAD), lambda i: (i, 0))
    return pl.pallas_call(
        body, name="mla_prep_bwd", grid=(S // tm,),
        in_specs=[wide, wide, pl.BlockSpec((tm, BW), lambda i: (i, 0)), lat(MQ0 // 4), lat(MKV0 // 4), vec, vec, wsp, wkv,
                  tab, tab, tab],
        out_specs=[pl.BlockSpec((tm, 2 * LORA + DH), lambda i: (i, 0)), wsp, wsp, vec, vec],
        out_shape=[SDS((S, 2 * LORA + DH), bf16), SDS((HEADS, LORA, QPAD), f32), SDS((HEADS, LORA, QPAD), f32),
                   SDS((1, LORA), f32), SDS((1, LORA), f32)],
        compiler_params=_cp(52),
    )(dq256, dk256, dv, proj, proj, qnorm, kvnorm, wuq, wukv, *tabs)


SUB = 8


def _scan_tiles(a_s, b_s, out, S, reverse):
    nt = S // SUB
    rows = lax.broadcasted_iota(jnp.int32, (SUB, LANE), 0)

    def tile(t, carry):
        base = pl.multiple_of((nt - 1 - t if reverse else t) * SUB, SUB)
        a, b = a_s[pl.ds(base, SUB), :], b_s[pl.ds(base, SUB), :]
        for d in (1, 2, 4):
            sh = SUB - d if reverse else d
            inside = rows < SUB - d if reverse else rows >= d
            a_n = jnp.where(inside, pltpu.roll(a, sh, 0), 1.0)
            b_n = jnp.where(inside, pltpu.roll(b, sh, 0), 0.0)
            b = a * b_n + b
            a = a * a_n
        res = a * carry + b
        out[pl.ds(base, SUB), :] = res
        edge = res[0:1, :] if reverse else res[SUB - 1:SUB, :]
        return jnp.broadcast_to(edge, (SUB, LANE))

    lax.fori_loop(0, nt, tile, jnp.zeros((SUB, LANE), f32))


def _shift_down(x, n, rows):
    return x if n == 0 else jnp.where(rows >= n, pltpu.roll(x, n, 0), 0.0)


def _shift_up(x, n, rows, S):
    return x if n == 0 else jnp.where(rows < S - n, pltpu.roll(x, S - n, 0), 0.0)


def _lru_gates(xb, cw, cb, wa, ba, wx, bx, lam, rows):
    xc = cb + cw[3:4, :] * xb
    for w in range(3):
        xc = xc + cw[w:w + 1, :] * _shift_down(xb, 3 - w, rows)
    xcb = xc.astype(bf16)
    r = _sigmoid(jnp.dot(xcb, wa, preferred_element_type=f32) + ba)
    i = _sigmoid(jnp.dot(xcb, wx, preferred_element_type=f32) + bx)
    sp = _softplus(-lam)
    la = (-LRU_C * r) * sp
    return xc, xcb, r, i, sp, la, jnp.exp(la)


def _lru_specs(S):
    col = lambda b0: pl.BlockSpec((S, LANE), lambda n: (0, b0 + n))
    vec = pl.BlockSpec((1, LANE), lambda n: (0, n))
    return col, vec, pl.BlockSpec((4, LANE), lambda n: (0, n)), pl.BlockSpec((1, LANE, LANE), lambda n: (n, 0, 0))


def lru_fwd(proj, cw, cb, wa, ba, wx, bx, lam):
    S = proj.shape[0]

    def body(x_ref, g_ref, cw_ref, cb_ref, wa_ref, ba_ref, wx_ref, bx_ref, lam_ref, h_ref, y_ref, a_s, b_s):
        rows = lax.broadcasted_iota(jnp.int32, (S, LANE), 0)
        xc, _, _, i, _, la, a = _lru_gates(x_ref[...], cw_ref[...], cb_ref[...], wa_ref[0].astype(bf16), ba_ref[...],
                                           wx_ref[0].astype(bf16), bx_ref[...], lam_ref[...], rows)
        a_s[...] = a
        b_s[...] = jnp.sqrt(_one_minus_exp(2.0 * la)) * (i * xc)
        _scan_tiles(a_s, b_s, h_ref, S, reverse=False)
        y_ref[...] = (h_ref[...] * _silu(g_ref[...])).astype(bf16)

    col, vec, cws, wsp = _lru_specs(S)
    return pl.pallas_call(
        body, name="lru_fwd", grid=(HEADS,),
        in_specs=[col(LX0), col(LG0), cws, vec, wsp, vec, wsp, vec, vec], out_specs=[col(0), col(0)],
        out_shape=[SDS((S, BW), f32), SDS((S, BW), bf16)],
        scratch_shapes=[pltpu.VMEM((S, LANE), f32), pltpu.VMEM((S, LANE), f32)], compiler_params=_cp(40),
    )(proj, proj, cw, cb, wa, ba, wx, bx, lam)


def lru_bwd(dy, h, proj, cw, cb, wa, ba, wx, bx, lam):
    S = proj.shape[0]

    def body(dy_ref, h_ref, x_ref, g_ref, cw_ref, cb_ref, wa_ref, ba_ref, wx_ref, bx_ref, lam_ref,
             dx_ref, dg_ref, dcw_ref, dcb_ref, dba_ref, dbx_ref, dlam_ref, dwa_ref, dwx_ref, a_s, b_s, l_s):
        rows = lax.broadcasted_iota(jnp.int32, (S, LANE), 0)
        xb, g, hv, dyv, cw, lam = x_ref[...], g_ref[...], h_ref[...], dy_ref[...], cw_ref[...], lam_ref[...]
        wa, wx = wa_ref[0].astype(bf16), wx_ref[0].astype(bf16)
        xc, xcb, r, i, sp, la, a = _lru_gates(xb, cw, cb_ref[...], wa, ba_ref[...], wx, bx_ref[...], lam, rows)
        dg_ref[...] = (dyv * hv * _dsilu(g)).astype(bf16)
        a_s[...] = _shift_up(a, 1, rows, S)
        b_s[...] = dyv * _silu(g)
        _scan_tiles(a_s, b_s, l_s, S, reverse=True)
        lmb = l_s[...]
        gated = i * xc
        sq = jnp.sqrt(_one_minus_exp(2.0 * la))
        dla = lmb * _shift_down(hv, 1, rows) * a - (lmb * gated) * (a * a) / sq
        dgated = lmb * sq
        dzr = (dla * (-LRU_C * sp)) * (r * (1.0 - r))
        dzi = (dgated * xc) * (i * (1.0 - i))
        dzrb, dzib = dzr.astype(bf16), dzi.astype(bf16)
        dxc = (dgated * i + lax.dot_general(dzrb, wa, _DIMS["nt"], preferred_element_type=f32)
               + lax.dot_general(dzib, wx, _DIMS["nt"], preferred_element_type=f32))
        dwa_ref[0] = lax.dot_general(xcb, dzrb, _DIMS["tn"], preferred_element_type=f32)
        dwx_ref[0] = lax.dot_general(xcb, dzib, _DIMS["tn"], preferred_element_type=f32)
        dba_ref[...] = jnp.sum(dzr, axis=0, keepdims=True)
        dbx_ref[...] = jnp.sum(dzi, axis=0, keepdims=True)
        dlam_ref[...] = jnp.sum(dla * (-LRU_C * r), axis=0, keepdims=True) * (-_sigmoid(-lam))
        dcb_ref[...] = jnp.sum(dxc, axis=0, keepdims=True)
        dxb = cw[3:4, :] * dxc
        dcw_ref[3:4, :] = jnp.sum(dxc * xb, axis=0, keepdims=True)
        for w in range(3):
            dxb = dxb + cw[w:w + 1, :] * _shift_up(dxc, 3 - w, rows, S)
            dcw_ref[w:w + 1, :] = jnp.sum(dxc * _shift_down(xb, 3 - w, rows), axis=0, keepdims=True)
        dx_ref[...] = dxb.astype(bf16)

    col, vec, cws, wsp = _lru_specs(S)
    scr = pltpu.VMEM((S, LANE), f32)
    return pl.pallas_call(
        body, name="lru_bwd", grid=(HEADS,),
        in_specs=[col(0), col(0), col(LX0), col(LG0), cws, vec, wsp, vec, wsp, vec, vec],
        out_specs=[col(0), col(0), cws, vec, vec, vec, vec, wsp, wsp],
        out_shape=[SDS((S, BW), bf16), SDS((S, BW), bf16), SDS((4, BW), f32)] + [SDS((1, BW), f32)] * 4
        + [SDS((HEADS, LANE, LANE), f32)] * 2,
        scratch_shapes=[scr, scr, scr], compiler_params=_cp(48),
    )(dy, h, proj, proj, cw, cb, wa, ba, wx, bx, lam)


def loss_head(y, target, *, tm=256):
    S = y.shape[0]

    def body(y_ref, t_ref, l_ref, d_ref):
        err = y_ref[...] - t_ref[...]
        d_ref[...] = err * (1.0 / D)
        part = jnp.sum(jnp.sum(err * err, axis=1, keepdims=True), axis=0, keepdims=True) * (0.5 / D)
        _acc(l_ref, jnp.broadcast_to(part, (1, LANE)), pl.program_id(0) == 0)

    row = pl.BlockSpec((tm, D), lambda i: (i, 0))
    return pl.pallas_call(
        body, name="loss_head", grid=(S // tm,), in_specs=[row, row],
        out_specs=[pl.BlockSpec((1, LANE), lambda i: (0, 0)), row],
        out_shape=[SDS((1, LANE), f32), SDS((S, D), f32)], compiler_params=_cp(32))(y, target)


def _adam_math(w, g, m, v):
    mn = ADAM_B1 * m + (1.0 - ADAM_B1) * g
    vn = ADAM_B2 * v + (1.0 - ADAM_B2) * (g * g)
    m_hat = mn / (1.0 - ADAM_B1 ** ADAM_STEP)
    v_hat = vn / (1.0 - ADAM_B2 ** ADAM_STEP)
    return -ADAM_LR * (m_hat / (jnp.sqrt(v_hat) + ADAM_EPS) + ADAM_WD * w), mn, vn


def _adam_rows(rows, cols):
    for cand in (2048, 1024, 512, 256, 128, 64, 32, 16, 8):
        if rows % cand == 0 and rows > cand and cand * cols <= ADAM_BLOCK_ELEMS:
            return cand
    return rows


def adamw(w, g, m, v):
    shape = w.shape
    cols = shape[-1]
    rows = math.prod(shape[:-1])
    tr = _adam_rows(rows, cols)

    def body(w_ref, g_ref, m_ref, v_ref, d_ref, mo_ref, vo_ref):
        d_ref[...], mo_ref[...], vo_ref[...] = _adam_math(w_ref[...], g_ref[...], m_ref[...], v_ref[...])

    blk = pl.BlockSpec((tr, cols), lambda i: (i, 0))
    flat = [t.reshape(rows, cols) for t in (w, g, m, v)]
    outs = pl.pallas_call(
        body, name="adamw", grid=(rows // tr,), in_specs=[blk] * 4, out_specs=[blk] * 3,
        out_shape=[SDS((rows, cols), f32)] * 3, compiler_params=_cp(48))(*flat)
    return tuple(o.reshape(shape) for o in outs)


ADA_SHARD = 3 * D // NDEV
ROWS16 = 16


def ada_fwd(c_all, ada_w, ada_b_mine):
    def body(c_ref, w_ref, b_ref, o_ref):
        o_ref[0] = jnp.dot(_silu(c_ref[...]).astype(bf16), w_ref[0].astype(bf16), preferred_element_type=f32) + b_ref[0]

    return pl.pallas_call(
        body, name="ada_fwd", grid=(DEPTH,),
        in_specs=[pl.BlockSpec((ROWS16, D), lambda l: (0, 0)), pl.BlockSpec((1, D, ADA_SHARD), lambda l: (l, 0, 0)),
                  pl.BlockSpec((1, 1, ADA_SHARD), lambda l: (l, 0, 0))],
        out_specs=pl.BlockSpec((1, ROWS16, ADA_SHARD), lambda l: (l, 0, 0)),
        out_shape=SDS((DEPTH, ROWS16, ADA_SHARD), f32), compiler_params=_cp(40))(c_all, ada_w, ada_b_mine)


def ada_bwd(c_all, dmod):
    def body(c_ref, d_ref, o_ref):
        o_ref[0] = lax.dot_general(_silu(c_ref[...]).astype(bf16), d_ref[0].astype(bf16), _DIMS["tn"],
                                   preferred_element_type=f32)

    return pl.pallas_call(
        body, name="ada_bwd", grid=(DEPTH,),
        in_specs=[pl.BlockSpec((ROWS16, D), lambda l: (0, 0)), pl.BlockSpec((1, ROWS16, ADA_SHARD), lambda l: (l, 0, 0))],
        out_specs=pl.BlockSpec((1, D, ADA_SHARD), lambda l: (l, 0, 0)),
        out_shape=SDS((DEPTH, D, ADA_SHARD), f32), compiler_params=_cp(40))(c_all, dmod)


def shift_params(me):
    start = SHARD_W * me
    return jnp.stack([start % LANE, (start + GAP) % LANE, jnp.clip(GAP_COL - start, 0, SHARD_W)]).astype(jnp.int32)


def win_pack(w, sidx, *, tm=256, after=()):
    def body(s_ref, w_ref, *rest):
        o_ref, scr = rest[len(after):]
        s1, s2, gi = s_ref[0], s_ref[1], s_ref[2]
        scr[...] = jnp.zeros_like(scr)
        scr[:, pl.ds(0, SHARD_W)] = w_ref[0]
        v = scr[...]
        j = lax.broadcasted_iota(jnp.int32, v.shape, 1)
        o_ref[0] = jnp.where(j - s1 < gi, pltpu.roll(v, s1, 1),
                             jnp.where(j - s2 >= gi, pltpu.roll(v, s2, 1), 0.0)).astype(bf16)

    return pl.pallas_call(
        body, name="win_pack", grid=(DEPTH, D // tm),
        in_specs=[pl.BlockSpec(memory_space=pltpu.SMEM), pl.BlockSpec((1, tm, SHARD_W), lambda l, i: (l, i, 0))]
        + [pl.BlockSpec(memory_space=pl.ANY)] * len(after),
        out_specs=pl.BlockSpec((1, tm, WIN), lambda l, i: (l, i, 0)),
        out_shape=SDS((DEPTH, D, WIN), bf16), scratch_shapes=[pltpu.VMEM((tm, WIN), f32)],
        compiler_params=_cp(32))(sidx, w, *after)


def win_assemble(g):
    own = WIN_STRIDE * LANE
    tail = NP - NDEV * own
    assert tail == 2 * LANE

    def body(a_ref, b_ref, o_ref):
        o_ref[...] = a_ref[0]

        @pl.when(pl.program_id(0) > 0)
        def _():
            o_ref[:, pl.ds(0, LANE)] = a_ref[0, :, pl.ds(0, LANE)] + b_ref[0]

    main = pl.pallas_call(
        body, name="win_assemble", grid=(NDEV,),
        in_specs=[pl.BlockSpec((1, D, own), lambda k: (k, 0, 0)),
                  pl.BlockSpec((1, D, LANE), lambda k: (jnp.maximum(k - 1, 0), 0, WIN_BLKS - 1))],
        out_specs=pl.BlockSpec((D, own), lambda k: (0, k)),
        out_shape=SDS((D, NP), bf16), compiler_params=_cp(48))(g, g)

    def tail_body(_, b_ref, o_ref):
        o_ref[:, pl.ds(0, LANE)] = b_ref[0]
        o_ref[:, pl.ds(LANE, LANE)] = jnp.zeros((D, LANE), bf16)

    return pl.pallas_call(
        tail_body, name="win_assemble_tail", grid=(1,),
        in_specs=[pl.BlockSpec(memory_space=pl.ANY), pl.BlockSpec((1, D, LANE), lambda t: (NDEV - 1, 0, WIN_BLKS - 1))],
        out_specs=pl.BlockSpec((D, tail), lambda t: (0, NDEV * own // tail)),
        out_shape=SDS((D, NP), bf16), input_output_aliases={0: 0}, compiler_params=_cp(32))(main, g)


def reduce_adamw(stag, w, m, v, l, prev, name, *, sidx=None, tr=128):
    _, R, C = w.shape
    Cs = stag.shape[2]
    n_prev = 0 if prev is None else 4
    n_lead = 1 if sidx is not None else 0

    def body(*refs):
        refs = list(refs)
        s_ref = refs.pop(0) if sidx is not None else None
        g_ref, w_ref, m_ref, v_ref = refs[:4]
        rest = refs[4 + n_prev:]
        go_ref, d_ref, mo_ref, vo_ref = rest[:4]
        tot = g_ref[0].astype(f32)
        for d in range(1, NDEV):
            tot = tot + g_ref[d].astype(f32)
        if sidx is not None:
            scr = rest[4]
            s1, s2, gi = s_ref[0], s_ref[1], s_ref[2]
            i = lax.broadcasted_iota(jnp.int32, tot.shape, 1)
            scr[...] = jnp.where(i < gi, pltpu.roll(tot, WIN - s1, 1), pltpu.roll(tot, WIN - s2, 1))
            tot = scr[:, pl.ds(0, C)]
        go_ref[0] = tot
        d_ref[0], mo_ref[0], vo_ref[0] = _adam_math(w_ref[0], tot, m_ref[0], v_ref[0])

    blk3 = pl.BlockSpec((1, tr, C), lambda i: (l, i, 0))
    in_specs = ([pl.BlockSpec(memory_space=pltpu.SMEM)] * n_lead + [pl.BlockSpec((NDEV, tr, Cs), lambda i: (0, i, 0)), blk3, blk3, blk3]
                + [pl.BlockSpec(memory_space=pl.ANY)] * n_prev)
    args = ([sidx] if sidx is not None else []) + [stag, w, m, v] + list(prev or ())
    return pl.pallas_call(
        body, name=name, grid=(R // tr,), in_specs=in_specs, out_specs=[blk3] * 4, out_shape=[SDS(w.shape, f32)] * 4,
        scratch_shapes=[pltpu.VMEM((tr, Cs), f32)] if sidx is not None else [],
        input_output_aliases={n_lead + 4 + k: k for k in range(n_prev)}, compiler_params=_cp(48),
    )(*args)


def cast_pad(w, cols_out, name):
    L, R, C = w.shape

    def body(w_ref, o_ref, *scr):
        if cols_out == C:
            o_ref[0] = w_ref[0].astype(bf16)
        else:
            scr[0][...] = jnp.zeros_like(scr[0])
            scr[0][:, pl.ds(0, C)] = w_ref[0]
            o_ref[0] = scr[0][...].astype(bf16)

    return pl.pallas_call(
        body, name=name, grid=(L,), in_specs=[pl.BlockSpec((1, R, C), lambda l: (l, 0, 0))],
        out_specs=pl.BlockSpec((1, R, cols_out), lambda l: (l, 0, 0)), out_shape=SDS((L, R, cols_out), bf16),
        scratch_shapes=[] if cols_out == C else [pltpu.VMEM((R, cols_out), f32)], compiler_params=_cp(32))(w)


def pack_qkv(w_uq, w_ukv):
    L = w_uq.shape[0]

    def body(q_ref, kv_ref, o_ref, scr):
        scr[...] = jnp.zeros_like(scr)
        scr[:, pl.ds(0, WUQ_COLS)] = q_ref[0]
        o_ref[0, :, pl.ds(0, QPAD)] = scr[...].astype(bf16)
        o_ref[0, :, pl.ds(QPAD, QPAD)] = kv_ref[0].astype(bf16)

    return pl.pallas_call(
        body, name="pack_qkv", grid=(L,),
        in_specs=[pl.BlockSpec((1, LORA, WUQ_COLS), lambda l: (l, 0, 0)), pl.BlockSpec((1, LORA, QPAD), lambda l: (l, 0, 0))],
        out_specs=pl.BlockSpec((1, LORA, 2 * QPAD), lambda l: (l, 0, 0)), out_shape=SDS((L, LORA, 2 * QPAD), bf16),
        scratch_shapes=[pltpu.VMEM((LORA, QPAD), f32)], compiler_params=_cp(32))(w_uq, w_ukv)


def sum_parts(stag, cols_out, name, *, tr=None):
    _, R, C = stag.shape
    tr = R if tr is None else tr

    def body(g_ref, o_ref, *scr):
        tot = g_ref[0].astype(f32)
        for d in range(1, NDEV):
            tot = tot + g_ref[d].astype(f32)
        if cols_out == C:
            o_ref[...] = tot
        else:
            scr[0][...] = tot
            o_ref[...] = scr[0][:, pl.ds(0, cols_out)]

    return pl.pallas_call(
        body, name=name, grid=(R // tr,), in_specs=[pl.BlockSpec((NDEV, tr, C), lambda i: (0, i, 0))],
        out_specs=pl.BlockSpec((tr, cols_out), lambda i: (i, 0)), out_shape=SDS((R, cols_out), f32),
        scratch_shapes=[] if cols_out == C else [pltpu.VMEM((tr, C), f32)], compiler_params=_cp(40))(stag)


MESH_ID = pl.DeviceIdType.MESH
HBM_SPEC = pl.BlockSpec(memory_space=pltpu.HBM)


def _place():
    return lax.axis_index("x"), lax.axis_index("y"), lax.axis_index("c")


def all_gather(arrs, name):
    n = len(arrs)

    def body(*refs):
        ins, outs = refs[:n], refs[n:2 * n]
        send_sems, recv_sems, local_sems = refs[2 * n:]
        x, y, c = _place()
        me, sibling = (x, y, c), (x, y, 1 - c)
        chips = [(1 - x, y), (x, 1 - y), (1 - x, 1 - y)]

        def copy(a, k, block, to, src=None):
            slot = outs[a].at[4 * block[0] + 2 * block[1] + block[2]]
            return pltpu.make_async_remote_copy(
                src_ref=slot if src is None else src, dst_ref=slot, send_sem=send_sems.at[7 * a + k],
                recv_sem=recv_sems.at[7 * a + k], device_id=to, device_id_type=MESH_ID)

        mine = [pltpu.make_async_copy(ins[a], outs[a].at[4 * x + 2 * y + c], local_sems.at[a]) for a in range(n)]
        for cp in mine:
            cp.start()
        first = []
        for a in range(n):
            first.append(copy(a, 0, me, sibling, src=ins[a]))
            first += [copy(a, 1 + j, me, (*chip, c), src=ins[a]) for j, chip in enumerate(chips)]
        for cp in first:
            cp.start()
        passed = []
        for j, chip in enumerate(chips):
            for a in range(n):
                copy(a, 1 + j, (*chip, c), me).wait_recv()
                cp = copy(a, 4 + j, (*chip, c), sibling)
                cp.start()
                passed.append(cp)
        for a in range(n):
            copy(a, 0, sibling, me).wait_recv()
        for j, chip in enumerate(chips):
            for a in range(n):
                copy(a, 4 + j, (*chip, 1 - c), me).wait_recv()
        for cp in first + passed:
            cp.wait_send()
        for cp in mine:
            cp.wait()

    return pl.pallas_call(
        body, name=name, in_specs=[HBM_SPEC] * n, out_specs=[HBM_SPEC] * n,
        out_shape=[SDS((NDEV,) + a.shape, a.dtype) for a in arrs],
        scratch_shapes=[pltpu.SemaphoreType.DMA((7 * n,)), pltpu.SemaphoreType.DMA((7 * n,)),
                        pltpu.SemaphoreType.DMA((n,))],
    )(*arrs)


AG_COLLECTIVE_ID = 0
RS_COLLECTIVE_ID = 1


def _everyone_else(x, y, c):
    return [(x ^ (r >> 2), y ^ ((r >> 1) & 1), c ^ (r & 1)) for r in range(1, NDEV)]


def _rendezvous(sem, peers):
    for peer in peers:
        pl.semaphore_signal(sem, inc=1, device_id=peer, device_id_type=MESH_ID)
    pl.semaphore_wait(sem, NDEV - 1)


def _sequencer_call(body, arrs, out_types, name, collective_id):
    n = len(arrs)
    return pl.kernel(
        body, name=name, out_type=out_types, mesh=plsc.ScalarSubcoreMesh(axis_name="sequencer", num_cores=1),
        scratch_types=[pltpu.SemaphoreType.DMA((7 * n,)), pltpu.SemaphoreType.DMA((7 * n,)),
                       pltpu.SemaphoreType.DMA((n,)), pltpu.SemaphoreType.REGULAR],
        compiler_params=pltpu.CompilerParams(collective_id=collective_id),
    )(*arrs)


def seq_all_gather(arrs, name):
    n = len(arrs)

    def body(*refs):
        ins, outs = refs[:n], refs[n:2 * n]
        send_sems, recv_sems, local_sems, exit_sem = refs[2 * n:]
        x, y, c = _place()
        peers = _everyone_else(x, y, c)
        _rendezvous(pltpu.get_barrier_semaphore(), peers)
        me, sibling = (x, y, c), (x, y, 1 - c)
        chips = [(1 - x, y), (x, 1 - y), (1 - x, 1 - y)]

        def copy(a, k, block, to, src=None):
            slot = outs[a].at[4 * block[0] + 2 * block[1] + block[2]]
            return pltpu.make_async_remote_copy(
                src_ref=slot if src is None else src, dst_ref=slot, send_sem=send_sems.at[7 * a + k],
                recv_sem=recv_sems.at[7 * a + k], device_id=to, device_id_type=MESH_ID)

        mine = [pltpu.make_async_copy(ins[a], outs[a].at[4 * x + 2 * y + c], local_sems.at[a]) for a in range(n)]
        for cp in mine:
            cp.start()
        first = []
        for a in range(n):
            first.append(copy(a, 0, me, sibling, src=ins[a]))
            first += [copy(a, 1 + j, me, (*chip, c), src=ins[a]) for j, chip in enumerate(chips)]
        for cp in first:
            cp.start()
        passed = []
        for j, chip in enumerate(chips):
            for a in range(n):
                copy(a, 1 + j, (*chip, c), me).wait_recv()
                cp = copy(a, 4 + j, (*chip, c), sibling)
                cp.start()
                passed.append(cp)
        for a in range(n):
            copy(a, 0, sibling, me).wait_recv()
        for j, chip in enumerate(chips):
            for a in range(n):
                copy(a, 4 + j, (*chip, 1 - c), me).wait_recv()
        for cp in first + passed:
            cp.wait_send()
        for cp in mine:
            cp.wait()
        _rendezvous(exit_sem, peers)

    return _sequencer_call(body, arrs, [SDS((NDEV,) + a.shape, a.dtype) for a in arrs], name, AG_COLLECTIVE_ID)


def seq_reduce_scatter_parts(arrs, pick, shapes, name):
    n = len(arrs)

    def body(*refs):
        ins, outs = refs[:n], refs[n:2 * n]
        send_sems, recv_sems, local_sems, exit_sem = refs[2 * n:]
        x, y, c = _place()
        peers = _everyone_else(x, y, c)
        _rendezvous(pltpu.get_barrier_semaphore(), peers)
        me = 4 * x + 2 * y + c
        mine = [pltpu.make_async_copy(pick[a](ins[a], me), outs[a].at[me], local_sems.at[a]) for a in range(n)]
        for cp in mine:
            cp.start()
        sent = []
        for r, peer in enumerate(peers):
            pid = 4 * peer[0] + 2 * peer[1] + peer[2]
            for a in range(n):
                cp = pltpu.make_async_remote_copy(
                    src_ref=pick[a](ins[a], pid), dst_ref=outs[a].at[me], send_sem=send_sems.at[7 * a + r],
                    recv_sem=recv_sems.at[7 * a + r], device_id=peer, device_id_type=MESH_ID)
                cp.start()
                sent.append((cp, a, r, pid))
        for cp, a, r, pid in sent:
            pltpu.make_async_remote_copy(
                src_ref=pick[a](ins[a], pid), dst_ref=outs[a].at[pid], send_sem=send_sems.at[7 * a + r],
                recv_sem=recv_sems.at[7 * a + r], device_id=(x, y, c), device_id_type=MESH_ID).wait_recv()
        for cp, _, _, _ in sent:
            cp.wait_send()
        for cp in mine:
            cp.wait()
        _rendezvous(exit_sem, peers)

    return _sequencer_call(body, arrs, [SDS((NDEV,) + tuple(s), a.dtype) for s, a in zip(shapes, arrs)], name,
                           RS_COLLECTIVE_ID)


WEIGHTS = ("ada_w", "ada_b", "norm_pre", "norm_post", "w_in", "ret_gn", "lru_conv_w", "lru_conv_b", "lru_wa", "lru_ba",
           "lru_wx", "lru_bx", "lru_lambda", "mla_q_norm", "mla_w_uq", "mla_kv_norm", "mla_w_ukv", "w_branch", "w_out")
SMALL = ("norm_pre", "norm_post", "ret_gn", "lru_conv_w", "lru_conv_b", "lru_wa", "lru_ba", "lru_wx", "lru_bx",
         "lru_lambda", "mla_q_norm", "mla_kv_norm")
BR_ROWS = 3 * BW // NDEV
OUT_ROWS = D // NDEV
WUQ_COLS = 192


def _row(v):
    return v.reshape(1, -1)


def layer_fwd(xl, mod, p, wts, tabs, lgam):
    S = xl.shape[0]
    tm = min(S, 2048)
    sh, sc, rg = _row(mod[:D]), _row(mod[D:2 * D]), _row(mod[2 * D:])
    ret_tabs, mla_tabs = tabs
    h = pre_fwd(xl, _row(p["norm_pre"]), sc, sh)
    proj = matmul(h, wts["w_in"], dims="nn", M=S, N=NP, K=D, tm=tm, tn=768, tk=D, out_dtype=f32, name="mm_in")
    qk = ret_prep_fwd(proj, *ret_tabs)
    o_ret, y_ret = attn_fwd(qk, qk, proj, proj, softmax=False, dq=DH, q_blk0=0, k_blk0=HEADS, v_blk0=V0, gate_blk0=RG0,
                            lgam=lgam, gn=_row(p["ret_gn"]), name="ret_attn_fwd")
    h_lru, y_lru = lru_fwd(proj, p["conv_w"], _row(p["lru_conv_b"]), p["lru_wa"], _row(p["lru_ba"]), p["lru_wx"],
                           _row(p["lru_bx"]), _row(p["lru_lambda"]))
    q256, k256, vm = mla_prep_fwd(proj, _row(p["mla_q_norm"]), _row(p["mla_kv_norm"]), wts["w_qkv"], wts["w_qkv"], mla_tabs,
                                  kv_blk=1)
    o_mla, y_mla = attn_fwd(q256, k256, vm, proj, softmax=True, dq=QPAD, q_blk0=0, k_blk0=0, v_blk0=0, gate_blk0=MG0,
                            name="mla_attn_fwd")
    ys = (y_ret, y_lru, y_mla)
    us = [matmul(ys[b], wts["w_branch"], dims="nn", M=S, N=D, K=BW, tm=min(S, 1024), tn=1024, tk=BW,
                 out_dtype=bf16, name="mm_branch", b_blk0=(b, 0)) for b in range(3)]
    merged = gate_fwd(proj, us)
    y, x_next = out_fwd(merged, wts["w_out"], xl, rg, _row(p["norm_post"]))
    saved = dict(x=xl, h=h, proj=proj, qk=qk, o_ret=o_ret, h_lru=h_lru, q256=q256, k256=k256, vm=vm, o_mla=o_mla,
                 ys=ys, us=us, merged=merged, y=y, sc=sc, rg=rg)
    return x_next, saved


def layer_bwd(dx, sv, p, wts, tabs, lgam):
    S = dx.shape[0]
    tm = min(S, 2048)
    ret_tabs, mla_tabs = tabs
    proj = sv["proj"]
    dy, d_rg, d_gpost = out_bwd(dx, sv["y"], sv["rg"], _row(p["norm_post"]))
    dmerged = matmul(dy, wts["w_out"], dims="nt", M=S, N=D, K=D, tm=min(S, 1024), tn=1024, tk=D, out_dtype=f32, name="mm_dmerged")
    dw_out = matmul(sv["merged"], dy, dims="tn", M=D, N=D, K=S, tm=1024, tn=1024, tk=min(S, 1024), out_dtype=bf16, name="mm_dwout")
    du, dml = gate_bwd(dmerged, proj, sv["us"])
    tmb, tkb = min(S, 1024), min(S, 1024)
    dys = [matmul(du[b], wts["w_branch"], dims="nt", M=S, N=BW, K=D, tm=tmb, tn=BW, tk=D, out_dtype=f32, name="mm_dybranch",
                  b_blk0=(b, 0)) for b in range(3)]
    dw_branch = jnp.concatenate(
        [matmul(sv["ys"][b], du[b], dims="tn", M=BW, N=D, K=S, tm=BW, tn=1024, tk=tkb, out_dtype=bf16, name="mm_dwbranch")
         for b in range(3)], axis=0)
    do, d_rgate, d_gn = ret_post_bwd(dys[0], sv["o_ret"], proj, _row(p["ret_gn"]))
    dq, dk, dv = attn_bwd(sv["qk"], sv["qk"], proj, do, None, softmax=False, dq=DH, q_blk0=0, k_blk0=HEADS, v_blk0=V0,
                          lgam=lgam, name="ret_attn_bwd")
    d_qk = ret_prep_bwd(jnp.concatenate([dq, dk], axis=1), *ret_tabs)
    d_lx, d_lg, d_cw, d_cb, d_ba, d_bx, d_lam, d_wa, d_wx = lru_bwd(
        dys[1], sv["h_lru"], proj, p["conv_w"], _row(p["lru_conv_b"]), p["lru_wa"], _row(p["lru_ba"]), p["lru_wx"],
        _row(p["lru_bx"]), _row(p["lru_lambda"]))
    do, d_mg = mla_post_bwd(dys[2], sv["o_mla"], proj)
    dq256, dk256, dvm = attn_bwd(sv["q256"], sv["k256"], sv["vm"], do, sv["o_mla"], softmax=True, dq=QPAD, q_blk0=0,
                                 k_blk0=0, v_blk0=0, name="mla_attn_bwd")
    d_lat, dw_uq, dw_ukv, d_qn, d_kvn = mla_prep_bwd(dq256, dk256, dvm, proj, _row(p["mla_q_norm"]), _row(p["mla_kv_norm"]),
                                                       wts["w_qkv"], wts["w_qkv"], mla_tabs, kv_blk=1)
    dproj = jnp.concatenate([d_qk, dv.astype(bf16), d_rgate, d_lx, d_lg, d_lat, d_mg, *dml, jnp.zeros((S, LANE), bf16)], axis=1)
    dh = matmul(dproj, wts["w_in"], dims="nt", M=S, N=D, K=NP, tm=tm, tn=1024, tk=768, out_dtype=f32, name="mm_dh")
    dw_in = matmul(sv["h"], dproj, dims="tn", M=D, N=NP, K=S, tm=D, tn=768, tk=tm, out_dtype=bf16, name="mm_dwin")
    big = dict(w_in=dw_in, w_branch=dw_branch, w_out=dw_out, w_qkv=jnp.concatenate([dw_uq, dw_ukv], axis=2).astype(bf16))
    dxl, d_sh, d_sc, d_gpre = pre_bwd(dh, sv["x"], _row(p["norm_pre"]), sv["sc"], dx, after=tuple(big.values()))
    dmod = jnp.concatenate([d_sh, d_sc, d_rg], axis=1).reshape(-1)
    small = dict(norm_pre=d_gpre, norm_post=d_gpost, ret_gn=d_gn, lru_conv_w=d_cw, lru_conv_b=d_cb, lru_wa=d_wa, lru_ba=d_ba,
                 lru_wx=d_wx, lru_bx=d_bx, lru_lambda=d_lam, mla_q_norm=d_qn, mla_kv_norm=d_kvn)
    return dxl, dmod, big, small


def exchange_grads(big, l):
    picks = [lambda r, d: r.at[:, pl.ds(pl.multiple_of(d * (WIN_STRIDE * LANE), LANE), WIN)],
             lambda r, d: r.at[pl.ds(pl.multiple_of(d * BR_ROWS, 8), BR_ROWS), :],
             lambda r, d: r.at[pl.ds(pl.multiple_of(d * OUT_ROWS, 8), OUT_ROWS), :],
             lambda r, d: r.at[d]]
    shapes = [(D, WIN), (BR_ROWS, D), (OUT_ROWS, D), (LORA, 2 * QPAD)]
    arrs = [big["w_in"], big["w_branch"], big["w_out"], big["w_qkv"]]
    return seq_reduce_scatter_parts(arrs, picks, shapes, f"rs_grads_{l}")


def kernel(x, c, positions, ada_w, ada_b, norm_pre, norm_post, w_in, ret_gn, lru_conv_w, lru_conv_b, lru_wa, lru_ba, lru_wx, lru_bx, lru_lambda, mla_q_norm, mla_w_uq, mla_kv_norm, mla_w_ukv, w_branch, w_out, loss_target, m_ada_w, m_ada_b, m_norm_pre, m_norm_post, m_w_in, m_ret_gn, m_lru_conv_w, m_lru_conv_b, m_lru_wa, m_lru_ba, m_lru_wx, m_lru_bx, m_lru_lambda, m_mla_q_norm, m_mla_w_uq, m_mla_kv_norm, m_mla_w_ukv, m_w_branch, m_w_out, v_ada_w, v_ada_b, v_norm_pre, v_norm_post, v_w_in, v_ret_gn, v_lru_conv_w, v_lru_conv_b, v_lru_wa, v_lru_ba, v_lru_wx, v_lru_bx, v_lru_lambda, v_mla_q_norm, v_mla_w_uq, v_mla_kv_norm, v_mla_w_ukv, v_w_branch, v_w_out):
    given = dict(locals())
    xi, yi, ci = _place()
    me = 4 * xi + 2 * yi + ci
    S = x.shape[1]
    sidx = shift_params(me)
    lgam = jnp.asarray(np.log1p(-np.exp2(-5.0 - np.arange(HEADS))), f32)
    tabs = rope_tables(positions[0])

    (g_small,) = all_gather([jnp.concatenate([c.reshape(16, LANE), lru_conv_w.reshape(16, LANE)], axis=0)], "ag_small")
    c16 = jnp.concatenate([g_small[:, :16].reshape(NDEV, D), jnp.zeros((ROWS16 - NDEV, D), f32)], axis=0)
    conv_w_all = g_small[:, 16:].reshape(NDEV, DEPTH, 4, LANE).transpose(1, 2, 0, 3).reshape(DEPTH, 4, BW)
    ada_b_mine = lax.dynamic_slice_in_dim(ada_b, me * ADA_SHARD, ADA_SHARD, axis=1).reshape(DEPTH, 1, ADA_SHARD)
    (g_mod,) = all_gather([ada_fwd(c16, ada_w, ada_b_mine)[:, :NDEV]], "ag_mod")
    mods = lax.dynamic_index_in_dim(g_mod, me, axis=2, keepdims=False).transpose(1, 0, 2).reshape(DEPTH, 3 * D)
    packed = (win_pack(w_in, sidx, after=(g_mod,)), cast_pad(w_branch, D, "pack_wbranch"), cast_pad(w_out, D, "pack_wout"),
              pack_qkv(mla_w_uq, mla_w_ukv))
    gathered = [seq_all_gather([t[l] for t in packed], f"ag_weights_{l}") for l in range(DEPTH)]

    params, wts = [], []
    for l in range(DEPTH):
        p = {n: given[n][l] for n in SMALL if n != "lru_conv_w"}
        p["conv_w"] = conv_w_all[l]
        params.append(p)
        g_win, g_br, g_out, g_qkv = gathered[l]
        wts.append(dict(w_in=win_assemble(g_win), w_qkv=g_qkv, w_branch=g_br.reshape(3 * BW, D), w_out=g_out.reshape(D, D)))

    xl, saved = x[0], []
    for l in range(DEPTH):
        xl, sv = layer_fwd(xl, mods[l], params[l], wts[l], tabs, lgam)
        saved.append(sv)
    my_loss, dx = loss_head(xl, loss_target[0])
    loss = lax.psum(my_loss[0, 0], ("x", "y", "c"))

    dmods, smalls, staged, grads = [None] * DEPTH, [None] * DEPTH, [None] * DEPTH, {n: [None] * DEPTH for n in WEIGHTS}
    for l in reversed(range(DEPTH)):
        dx, dmods[l], big, smalls[l] = layer_bwd(dx, saved[l], params[l], wts[l], tabs, lgam)
        staged[l] = exchange_grads(big, l)
    chained = {"w_in": None, "w_branch": None, "w_out": None}
    for l in reversed(range(DEPTH)):
        st_win, st_br, st_out, st_qkv = staged[l]
        for n, st, sx in (("w_in", st_win, sidx), ("w_branch", st_br, None), ("w_out", st_out, None)):
            chained[n] = reduce_adamw(st, given[n], given["m_" + n], given["v_" + n], l, chained[n], "update_" + n, sidx=sx)
        g_qkv = sum_parts(st_qkv, 2 * QPAD, "sum_wqkv")
        grads["mla_w_uq"][l] = g_qkv[:, :WUQ_COLS]
        grads["mla_w_ukv"][l] = g_qkv[:, QPAD:]

    flat = [jnp.stack(dmods).reshape(-1)] + [smalls[l][n].reshape(-1) for l in range(DEPTH) for n in SMALL]
    sizes = [int(t.shape[0]) for t in flat]
    (g_pack,) = all_gather([jnp.concatenate(flat).reshape(-1, LANE)], "ag_small_grads")
    rows = g_pack.shape[1]
    tot = sum_parts(g_pack, LANE, "sum_small_grads", tr=rows // 8).reshape(-1)
    offs = np.concatenate([[0], np.cumsum(sizes)])
    pieces = [tot[int(offs[i]):int(offs[i + 1])] for i in range(len(sizes))]
    grads["ada_b"] = pieces[0].reshape(DEPTH, 3 * D)
    for l in range(DEPTH):
        for j, n in enumerate(SMALL):
            piece = pieces[1 + l * len(SMALL) + j]
            if n == "lru_conv_w":
                piece = lax.dynamic_slice_in_dim(piece.reshape(4, BW), me * LANE, LANE, axis=1)
            grads[n][l] = piece.reshape(given[n].shape[1:])
    dmod_all = g_pack[:, :DEPTH * 3 * D // LANE].reshape(NDEV, DEPTH, 3 * D)
    dmod_mine = lax.dynamic_slice_in_dim(dmod_all, me * ADA_SHARD, ADA_SHARD, axis=2).transpose(1, 0, 2)
    dmod16 = jnp.concatenate([dmod_mine, jnp.zeros((DEPTH, ROWS16 - NDEV, ADA_SHARD), f32)], axis=1)
    grads["ada_w"] = ada_bwd(c16, dmod16)

    outs = {"grad": [], "delta": [], "m": [], "v": []}
    for n in WEIGHTS:
        if n in chained:
            g, delta, new_m, new_v = chained[n]
        else:
            g = grads[n] if not isinstance(grads[n], list) else jnp.stack(grads[n])
            delta, new_m, new_v = adamw(given[n], g, given["m_" + n], given["v_" + n])
        outs["grad"].append(g)
        outs["delta"].append(delta)
        outs["m"].append(new_m)
        outs["v"].append(new_v)
    return (loss, dx[None], *outs["grad"], *outs["delta"], *outs["m"], *outs["v"])
```

```python
import functools
import math

import numpy as np
import jax
import jax.numpy as jnp
from jax import lax
from jax.experimental import pallas as pl
from jax.experimental.pallas import tpu as pltpu
from jax.experimental.pallas import tpu_sc as plsc

f32 = jnp.float32
bf16 = jnp.bfloat16
SDS = jax.ShapeDtypeStruct

DEPTH = 4
D = 2048
HEADS = 8
DH = 128
BW = HEADS * DH
LANE = 128
CHUNK = 64
EPS = 1e-6
LRU_C = 8.0
NDEV = 8
VMEM_V7X = 64 * 1024 * 1024

Q0, K0, V0, RG0, LX0, LG0, MQ0, MKV0, MKR0, MG0, ML0 = 0, 8, 16, 24, 32, 40, 48, 52, 56, 57, 65
NB = 114
NP = NB * LANE
IN_W = 14400
SHARD_W = IN_W // NDEV
GAP_COL = 7232
GAP = 64
WIN = 1920
WIN_BLKS = WIN // LANE
WIN_STRIDE = 14
LORA = 512
QPAD = 256

ADAM_LR, ADAM_B1, ADAM_B2, ADAM_EPS, ADAM_WD, ADAM_STEP = 0.001, 0.9, 0.999, 1e-08, 0.01, 10
ADAM_BLOCK_ELEMS = 256 * 1024
MLA_SCALE = (128 + 64) ** -0.5
RET_SCALE = 128 ** -0.5


def _cp(vmem_mb=None, **kw):
    if vmem_mb is not None:
        kw["vmem_limit_bytes"] = min(vmem_mb * 1024 * 1024, VMEM_V7X - 8 * 1024 * 1024)
    return pltpu.CompilerParams(**kw)


def _sigmoid(x):
    return 1.0 / (1.0 + jnp.exp(-x))


def _silu(x):
    return x * _sigmoid(x)


def _dsilu(x):
    s = _sigmoid(x)
    return s * (1.0 + x * (1.0 - s))


def _softplus(x):
    return jnp.maximum(x, 0.0) + jnp.log(1.0 + jnp.exp(-jnp.abs(x)))


def _one_minus_exp(y):
    series = -y * (1.0 + y * (0.5 + y * (1.0 / 6.0)))
    return jnp.where(y > -1e-2, series, 1.0 - jnp.exp(y))


def _acc(ref, val, first):
    @pl.when(first)
    def _():
        ref[...] = val

    @pl.when(jnp.logical_not(first))
    def _():
        ref[...] += val


_DIMS = {"nn": (((1,), (0,)), ((), ())), "nt": (((1,), (1,)), ((), ())), "tn": (((0,), (0,)), ((), ()))}


def matmul(a, b, *, dims, M, N, K, tm, tn, tk, out_dtype, name, a_blk0=(0, 0), b_blk0=(0, 0), vmem_mb=48):
    nk = K // tk
    assert M % tm == 0 and N % tn == 0 and K % tk == 0
    dn = _DIMS[dims]

    def body(a_ref, b_ref, o_ref, *scr):
        part = lax.dot_general(a_ref[...].astype(bf16), b_ref[...].astype(bf16), dn, preferred_element_type=f32)
        if nk == 1:
            o_ref[...] = part.astype(out_dtype)
        else:
            acc = scr[0]
            k = pl.program_id(2)
            _acc(acc, part, k == 0)

            @pl.when(k == nk - 1)
            def _():
                o_ref[...] = acc[...].astype(out_dtype)

    ar, ac = a_blk0
    br, bc = b_blk0
    if dims == "nn":
        a_spec = pl.BlockSpec((tm, tk), lambda i, j, k: (i + ar, k + ac))
        b_spec = pl.BlockSpec((tk, tn), lambda i, j, k: (k + br, j + bc))
    elif dims == "nt":
        a_spec = pl.BlockSpec((tm, tk), lambda i, j, k: (i + ar, k + ac))
        b_spec = pl.BlockSpec((tn, tk), lambda i, j, k: (j + br, k + bc))
    else:
        a_spec = pl.BlockSpec((tk, tm), lambda i, j, k: (k + ar, i + ac))
        b_spec = pl.BlockSpec((tk, tn), lambda i, j, k: (k + br, j + bc))
    return pl.pallas_call(
        body, name=name, grid=(M // tm, N // tn, nk),
        in_specs=[a_spec, b_spec], out_specs=pl.BlockSpec((tm, tn), lambda i, j, k: (i, j)),
        out_shape=SDS((M, N), out_dtype),
        scratch_shapes=[] if nk == 1 else [pltpu.VMEM((tm, tn), f32)],
        compiler_params=_cp(vmem_mb, dimension_semantics=("parallel", "parallel", "arbitrary")),
    )(a, b)


def pre_fwd(x, g, sc, sh, *, tm=256):
    S = x.shape[0]

    def body(x_ref, g_ref, sc_ref, sh_ref, h_ref):
        xv = x_ref[...]
        r = lax.rsqrt(jnp.mean(xv * xv, axis=-1, keepdims=True) + EPS)
        h_ref[...] = (((xv * r) * g_ref[...]) * (1.0 + sc_ref[...]) + sh_ref[...]).astype(bf16)

    row = pl.BlockSpec((tm, D), lambda i: (i, 0))
    vec = pl.BlockSpec((1, D), lambda i: (0, 0))
    return pl.pallas_call(body, name="pre_fwd", grid=(S // tm,), in_specs=[row, vec, vec, vec], out_specs=row,
                          out_shape=SDS((S, D), bf16), compiler_params=_cp(32))(x, g, sc, sh)


def pre_bwd(dh, x, g, sc, dxo, *, tm=256, after=()):
    S = x.shape[0]

    def body(dh_ref, x_ref, g_ref, sc_ref, dxo_ref, *rest):
        dx_ref, dsh_ref, dsc_ref, dg_ref = rest[len(after):]
        first = pl.program_id(0) == 0
        xv, dhv, gv = x_ref[...], dh_ref[...], g_ref[...]
        one_sc = 1.0 + sc_ref[...]
        r = lax.rsqrt(jnp.mean(xv * xv, axis=-1, keepdims=True) + EPS)
        xh = xv * r
        t = dhv * xh
        dxh = dhv * gv * one_sc
        dx_ref[...] = r * (dxh - xh * jnp.mean(dxh * xh, axis=-1, keepdims=True)) + dxo_ref[...]
        _acc(dsh_ref, jnp.sum(dhv, axis=0, keepdims=True), first)
        _acc(dsc_ref, jnp.sum(t * gv, axis=0, keepdims=True), first)
        _acc(dg_ref, jnp.sum(t * one_sc, axis=0, keepdims=True), first)

    row = pl.BlockSpec((tm, D), lambda i: (i, 0))
    vec = pl.BlockSpec((1, D), lambda i: (0, 0))
    return pl.pallas_call(
        body, name="pre_bwd", grid=(S // tm,),
        in_specs=[row, row, vec, vec, row] + [pl.BlockSpec(memory_space=pl.ANY)] * len(after), out_specs=[row, vec, vec, vec],
        out_shape=[SDS((S, D), f32), SDS((1, D), f32), SDS((1, D), f32), SDS((1, D), f32)],
        compiler_params=_cp(40))(dh, x, g, sc, dxo, *after)


def out_fwd(merged, w_out, x, rg, gp, *, tm=256):
    S = x.shape[0]

    def body(m_ref, w_ref, x_ref, rg_ref, gp_ref, y_ref, xn_ref):
        y = jnp.dot(m_ref[...], w_ref[...], preferred_element_type=f32)
        y_ref[...] = y
        r = lax.rsqrt(jnp.mean(y * y, axis=-1, keepdims=True) + EPS)
        xn_ref[...] = x_ref[...] + (1.0 + rg_ref[...]) * ((y * r) * gp_ref[...])

    row = pl.BlockSpec((tm, D), lambda i: (i, 0))
    vec = pl.BlockSpec((1, D), lambda i: (0, 0))
    return pl.pallas_call(
        body, name="out_fwd", grid=(S // tm,),
        in_specs=[row, pl.BlockSpec((D, D), lambda i: (0, 0)), row, vec, vec], out_specs=[row, row],
        out_shape=[SDS((S, D), f32), SDS((S, D), f32)], compiler_params=_cp(48))(merged, w_out, x, rg, gp)


def out_bwd(dxo, y, rg, gp, *, tm=256):
    S = y.shape[0]

    def body(dxo_ref, y_ref, rg_ref, gp_ref, dy_ref, drg_ref, dgp_ref):
        first = pl.program_id(0) == 0
        yv, dv, gv = y_ref[...], dxo_ref[...], gp_ref[...]
        r = lax.rsqrt(jnp.mean(yv * yv, axis=-1, keepdims=True) + EPS)
        yh = yv * r
        dn = dv * (1.0 + rg_ref[...])
        dyh = dn * gv
        dy_ref[...] = (r * (dyh - yh * jnp.mean(dyh * yh, axis=-1, keepdims=True))).astype(bf16)
        _acc(drg_ref, jnp.sum(dv * (yh * gv), axis=0, keepdims=True), first)
        _acc(dgp_ref, jnp.sum(dn * yh, axis=0, keepdims=True), first)

    row = pl.BlockSpec((tm, D), lambda i: (i, 0))
    vec = pl.BlockSpec((1, D), lambda i: (0, 0))
    return pl.pallas_call(
        body, name="out_bwd", grid=(S // tm,), in_specs=[row, row, vec, vec], out_specs=[row, vec, vec],
        out_shape=[SDS((S, D), bf16), SDS((1, D), f32), SDS((1, D), f32)], compiler_params=_cp(40))(dxo, y, rg, gp)


def _ml_spec(b, tm):
    return pl.BlockSpec((tm, LANE), lambda i, j: (i, ML0 + b * (D // LANE) + j))


def gate_fwd(proj, us, *, tm=2048):
    S = proj.shape[0]
    tm = min(tm, S)

    def body(ml0, ml1, ml2, u0, u1, u2, m_ref):
        acc = None
        for ml, u in ((ml0, u0), (ml1, u1), (ml2, u2)):
            t = _sigmoid(ml[...]) * u[...].astype(f32)
            acc = t if acc is None else acc + t
        m_ref[...] = acc.astype(bf16)

    blk = pl.BlockSpec((tm, LANE), lambda i, j: (i, j))
    return pl.pallas_call(
        body, name="gate_fwd", grid=(S // tm, D // LANE),
        in_specs=[_ml_spec(0, tm), _ml_spec(1, tm), _ml_spec(2, tm), blk, blk, blk], out_specs=blk,
        out_shape=SDS((S, D), bf16), compiler_params=_cp(32),
    )(proj, proj, proj, *us)


def gate_bwd(dmerged, proj, us, *, tm=2048):
    S = proj.shape[0]
    tm = min(tm, S)

    def body(dm_ref, ml0, ml1, ml2, u0, u1, u2, du0, du1, du2, dl0, dl1, dl2):
        dm = dm_ref[...]
        for ml, u, du, dl in ((ml0, u0, du0, dl0), (ml1, u1, du1, dl1), (ml2, u2, du2, dl2)):
            s = _sigmoid(ml[...])
            du[...] = (dm * s).astype(bf16)
            dl[...] = (dm * u[...].astype(f32) * (s * (1.0 - s))).astype(bf16)

    blk = pl.BlockSpec((tm, LANE), lambda i, j: (i, j))
    outs = pl.pallas_call(
        body, name="gate_bwd", grid=(S // tm, D // LANE),
        in_specs=[blk, _ml_spec(0, tm), _ml_spec(1, tm), _ml_spec(2, tm), blk, blk, blk], out_specs=[blk] * 6,
        out_shape=[SDS((S, D), bf16)] * 6, compiler_params=_cp(40),
    )(dmerged, proj, proj, proj, *us)
    return outs[:3], outs[3:]


def rope_tables(positions):
    pos = positions.astype(f32)[:, None]

    def cs(dim):
        inv = 10000.0 ** (-jnp.arange(0, dim, 2, dtype=f32) / dim)
        ang = pos * inv
        return jnp.cos(ang), jnp.sin(ang)

    c, s = cs(128)
    ret = (jnp.concatenate([c, c], 1), jnp.concatenate([-s, s], 1))
    c, s = cs(64)
    z32, z64 = jnp.zeros_like(c), jnp.zeros((c.shape[0], 64), f32)
    mla = (jnp.concatenate([c, c, z64], 1), jnp.concatenate([-s, z32, z64], 1), jnp.concatenate([z32, s, z64], 1))
    return ret, mla


def _rope_ret(x, c, s):
    return x * c + pltpu.roll(x, 64, 1) * s


def _rope_ret_t(dy, c, s):
    return dy * c + pltpu.roll(dy * s, 64, 1)


def _rope_mla(x, c, sa, sb):
    return x * c + pltpu.roll(x, 96, 1) * sa + pltpu.roll(x, 32, 1) * sb


def _rope_mla_t(dy, c, sa, sb):
    return dy * c + pltpu.roll(dy * sa, 32, 1) + pltpu.roll(dy * sb, 96, 1)


def ret_prep_fwd(proj, c, s, *, tm=512):
    S = proj.shape[0]
    tm = min(tm, S)

    def body(p_ref, c_ref, s_ref, o_ref):
        scale = jnp.where(pl.program_id(1) < HEADS, RET_SCALE, 1.0)
        o_ref[...] = (_rope_ret(p_ref[...], c_ref[...], s_ref[...]) * scale).astype(bf16)

    blk = pl.BlockSpec((tm, DH), lambda i, j: (i, j))
    tab = pl.BlockSpec((tm, DH), lambda i, j: (i, 0))
    return pl.pallas_call(body, name="ret_prep_fwd", grid=(S // tm, 2 * HEADS), in_specs=[blk, tab, tab], out_specs=blk,
                          out_shape=SDS((S, 2 * BW), bf16), compiler_params=_cp(32))(proj, c, s)


def ret_prep_bwd(dqk, c, s, *, tm=512):
    S = dqk.shape[0]
    tm = min(tm, S)

    def body(d_ref, c_ref, s_ref, o_ref):
        scale = jnp.where(pl.program_id(1) < HEADS, RET_SCALE, 1.0)
        o_ref[...] = _rope_ret_t(d_ref[...] * scale, c_ref[...], s_ref[...]).astype(bf16)

    blk = pl.BlockSpec((tm, DH), lambda i, j: (i, j))
    tab = pl.BlockSpec((tm, DH), lambda i, j: (i, 0))
    return pl.pallas_call(body, name="ret_prep_bwd", grid=(S // tm, 2 * HEADS), in_specs=[blk, tab, tab], out_specs=blk,
                          out_shape=SDS((S, 2 * BW), bf16), compiler_params=_cp(32))(dqk, c, s)


def _scores(q, k, qi, bq, lg, softmax):
    nk = k.shape[0]
    s = lax.dot_general(q, k, _DIMS["nt"], preferred_element_type=f32)
    ti = qi * bq + lax.broadcasted_iota(jnp.int32, (bq, nk), 0)
    tj = lax.broadcasted_iota(jnp.int32, (bq, nk), 1)
    mask = (tj // CHUNK) <= (ti // CHUNK)
    if softmax:
        s = jnp.where(mask, s, -1e30)
        e = jnp.exp(s - jnp.max(s, axis=-1, keepdims=True))
        return e / jnp.sum(e, axis=-1, keepdims=True), None
    w = jnp.where(mask, jnp.exp(lg * jnp.abs(ti - tj).astype(f32)), 0.0)
    return s * w, w


def _per_query_block(nq, fn):
    for qi in range(nq):
        pl.when(pl.program_id(1) == qi)(functools.partial(fn, qi))


def attn_fwd(q, k, v, gate_src, *, softmax, dq, q_blk0, k_blk0, v_blk0, gate_blk0, lgam=None, gn=None, bq=256, name):
    S = q.shape[0]
    bq = min(bq, S)
    assert bq % CHUNK == 0

    def body(*refs):
        if softmax:
            q_ref, k_ref, v_ref, g_ref, o_ref, y_ref = refs
            lg = None
        else:
            lg_ref, q_ref, k_ref, v_ref, g_ref, gn_ref, o_ref, y_ref = refs
            lg = lg_ref[pl.program_id(0)]

        def block(qi):
            keys = pl.ds(0, (qi + 1) * bq)
            p, _ = _scores(q_ref[...], k_ref[keys, :], qi, bq, lg, softmax)
            o = jnp.dot(p.astype(bf16), v_ref[keys, :].astype(bf16), preferred_element_type=f32)
            o_ref[...] = o
            if softmax:
                z = o
            else:
                oc = o - jnp.mean(o, axis=-1, keepdims=True)
                z = oc * lax.rsqrt(jnp.mean(oc * oc, axis=-1, keepdims=True) + EPS) * gn_ref[...]
            y_ref[...] = (z * _silu(g_ref[...])).astype(bf16)

        _per_query_block(S // bq, block)

    q_spec = pl.BlockSpec((bq, dq), lambda h, i: (i, q_blk0 + h))
    k_spec = pl.BlockSpec((S, dq), lambda h, i: (0, k_blk0 + h))
    v_spec = pl.BlockSpec((S, DH), lambda h, i: (0, v_blk0 + h))
    g_spec = pl.BlockSpec((bq, DH), lambda h, i: (i, gate_blk0 + h))
    o_spec = pl.BlockSpec((bq, DH), lambda h, i: (i, h))
    in_specs, args = [q_spec, k_spec, v_spec, g_spec], [q, k, v, gate_src]
    if not softmax:
        in_specs = [pl.BlockSpec(memory_space=pltpu.SMEM)] + in_specs + [pl.BlockSpec((1, DH), lambda h, i: (0, h))]
        args = [lgam] + args + [gn]
    return pl.pallas_call(
        body, name=name, grid=(HEADS, S // bq), in_specs=in_specs, out_specs=[o_spec, o_spec],
        out_shape=[SDS((S, BW), f32), SDS((S, BW), bf16)], compiler_params=_cp(48))(*args)


def attn_bwd(q, k, v, do, o, *, softmax, dq, q_blk0, k_blk0, v_blk0, lgam=None, bq=256, name):
    S = q.shape[0]
    bq = min(bq, S)

    def body(*refs):
        if softmax:
            q_ref, k_ref, v_ref, do_ref, o_ref, dq_ref, dk_ref, dv_ref = refs
            lg = None
        else:
            lg_ref, q_ref, k_ref, v_ref, do_ref, dq_ref, dk_ref, dv_ref = refs
            lg = lg_ref[pl.program_id(0)]

        def block(qi):
            keys = pl.ds(0, (qi + 1) * bq)
            if qi == 0:
                dk_ref[...] = jnp.zeros_like(dk_ref)
                dv_ref[...] = jnp.zeros_like(dv_ref)
            qv, kv, dov = q_ref[...], k_ref[keys, :], do_ref[...]
            p, w = _scores(qv, kv, qi, bq, lg, softmax)
            dp = lax.dot_general(dov, v_ref[keys, :].astype(bf16), _DIMS["nt"], preferred_element_type=f32)
            if softmax:
                delta = jnp.sum(dov.astype(f32) * o_ref[...], axis=-1, keepdims=True)
                ds = p * (dp - delta)
            else:
                ds = dp * w
            dsb = ds.astype(bf16)
            dq_ref[...] = jnp.dot(dsb, kv, preferred_element_type=f32)
            dv_ref[keys, :] += lax.dot_general(p.astype(bf16), dov, _DIMS["tn"], preferred_element_type=f32)
            dk_ref[keys, :] += lax.dot_general(dsb, qv, _DIMS["tn"], preferred_element_type=f32)

        _per_query_block(S // bq, block)

    q_spec = pl.BlockSpec((bq, dq), lambda h, i: (i, q_blk0 + h))
    k_spec = pl.BlockSpec((S, dq), lambda h, i: (0, k_blk0 + h))
    v_spec = pl.BlockSpec((S, DH), lambda h, i: (0, v_blk0 + h))
    o_spec = pl.BlockSpec((bq, DH), lambda h, i: (i, h))
    in_specs, args = [q_spec, k_spec, v_spec, o_spec], [q, k, v, do]
    if softmax:
        in_specs, args = in_specs + [o_spec], args + [o]
    else:
        in_specs, args = [pl.BlockSpec(memory_space=pltpu.SMEM)] + in_specs, [lgam] + args
    return pl.pallas_call(
        body, name=name, grid=(HEADS, S // bq), in_specs=in_specs,
        out_specs=[pl.BlockSpec((bq, dq), lambda h, i: (i, h)), pl.BlockSpec((S, dq), lambda h, i: (0, h)),
                   pl.BlockSpec((S, DH), lambda h, i: (0, h))],
        out_shape=[SDS((S, HEADS * dq), f32), SDS((S, HEADS * dq), f32), SDS((S, BW), f32)],
        compiler_params=_cp(52))(*args)


def ret_post_bwd(dy, o, proj, gn, *, tm=512):
    S = dy.shape[0]
    tm = min(tm, S)

    def body(dy_ref, o_ref, g_ref, gn_ref, do_ref, drg_ref, dgn_ref):
        ov, g, gnv, dyv = o_ref[...], g_ref[...], gn_ref[...], dy_ref[...]
        oc = ov - jnp.mean(ov, axis=-1, keepdims=True)
        rs = lax.rsqrt(jnp.mean(oc * oc, axis=-1, keepdims=True) + EPS)
        oh = oc * rs
        dz = dyv * _silu(g)
        drg_ref[...] = (dyv * (oh * gnv) * _dsilu(g)).astype(bf16)
        doh = dz * gnv
        do_ref[...] = (rs * (doh - jnp.mean(doh, axis=-1, keepdims=True)
                             - oh * jnp.mean(doh * oh, axis=-1, keepdims=True))).astype(bf16)
        _acc(dgn_ref, jnp.sum(dz * oh, axis=0, keepdims=True), pl.program_id(1) == 0)

    blk = pl.BlockSpec((tm, DH), lambda h, i: (i, h))
    vec = pl.BlockSpec((1, DH), lambda h, i: (0, h))
    return pl.pallas_call(
        body, name="ret_post_bwd", grid=(HEADS, S // tm),
        in_specs=[blk, blk, pl.BlockSpec((tm, DH), lambda h, i: (i, RG0 + h)), vec], out_specs=[blk, blk, vec],
        out_shape=[SDS((S, BW), bf16), SDS((S, BW), bf16), SDS((1, BW), f32)], compiler_params=_cp(32))(dy, o, proj, gn)


def mla_post_bwd(dy, o, proj, *, tm=512):
    S = dy.shape[0]
    tm = min(tm, S)

    def body(dy_ref, o_ref, g_ref, do_ref, dg_ref):
        g, dyv = g_ref[...], dy_ref[...]
        do_ref[...] = (dyv * _silu(g)).astype(bf16)
        dg_ref[...] = (dyv * o_ref[...] * _dsilu(g)).astype(bf16)

    blk = pl.BlockSpec((tm, DH), lambda i, h: (i, h))
    return pl.pallas_call(
        body, name="mla_post_bwd", grid=(S // tm, HEADS),
        in_specs=[blk, blk, pl.BlockSpec((tm, DH), lambda i, h: (i, MG0 + h))], out_specs=[blk, blk],
        out_shape=[SDS((S, BW), bf16), SDS((S, BW), bf16)], compiler_params=_cp(32))(dy, o, proj)


def mla_prep_fwd(proj, qnorm, kvnorm, wuq, wukv, tabs, *, tm=256, kv_blk=0):
    S = proj.shape[0]
    tm = min(tm, S)

    def body(mq_ref, mkv_ref, mkr_ref, qn_ref, kvn_ref, wuq_ref, wukv_ref, c_ref, sa_ref, sb_ref, q_ref, k_ref, v_ref):
        c, sa, sb = c_ref[...], sa_ref[...], sb_ref[...]
        mq, mkv = mq_ref[...], mkv_ref[...]
        qn = (mq * lax.rsqrt(jnp.mean(mq * mq, axis=-1, keepdims=True) + EPS) * qn_ref[...]).astype(bf16)
        kvn = (mkv * lax.rsqrt(jnp.mean(mkv * mkv, axis=-1, keepdims=True) + EPS) * kvn_ref[...]).astype(bf16)
        kr = _rope_mla(mkr_ref[...], c, sa, sb).astype(bf16)
        for h in range(HEADS):
            qh = jnp.dot(qn, wuq_ref[h], preferred_element_type=f32)
            q_ref[:, pl.ds(h * QPAD, DH)] = (qh[:, :DH] * MLA_SCALE).astype(bf16)
            q_ref[:, pl.ds(h * QPAD + DH, DH)] = (_rope_mla(qh[:, DH:], c, sa, sb) * MLA_SCALE).astype(bf16)
            kvh = jnp.dot(kvn, wukv_ref[h], preferred_element_type=f32)
            k_ref[:, pl.ds(h * QPAD, DH)] = kvh[:, :DH].astype(bf16)
            k_ref[:, pl.ds(h * QPAD + DH, DH)] = kr
            v_ref[:, pl.ds(h * DH, DH)] = kvh[:, DH:].astype(bf16)

    lat = lambda b: pl.BlockSpec((tm, LORA), lambda i: (i, b))
    tab = pl.BlockSpec((tm, DH), lambda i: (i, 0))
    vec = pl.BlockSpec((1, LORA), lambda i: (0, 0))
    wsp = pl.BlockSpec((HEADS, LORA, QPAD), lambda i: (0, 0, 0))
    wkv = pl.BlockSpec((HEADS, LORA, QPAD), lambda i: (0, 0, kv_blk))
    return pl.pallas_call(
        body, name="mla_prep_fwd", grid=(S // tm,),
        in_specs=[lat(MQ0 // 4), lat(MKV0 // 4), pl.BlockSpec((tm, DH), lambda i: (i, MKR0)), vec, vec, wsp, wkv,
                  tab, tab, tab],
        out_specs=[pl.BlockSpec((tm, HEADS * QPAD), lambda i: (i, 0))] * 2 + [pl.BlockSpec((tm, BW), lambda i: (i, 0))],
        out_shape=[SDS((S, HEADS * QPAD), bf16)] * 2 + [SDS((S, BW), bf16)], compiler_params=_cp(48),
    )(proj, proj, proj, qnorm, kvnorm, wuq, wukv, *tabs)


def mla_prep_bwd(dq256, dk256, dv, proj, qnorm, kvnorm, wuq, wukv, tabs, *, tm=256, kv_blk=0):
    S = proj.shape[0]
    tm = min(tm, S)

    def body(dq_ref, dk_ref, dv_ref, mq_ref, mkv_ref, qn_ref, kvn_ref, wuq_ref, wukv_ref, c_ref, sa_ref, sb_ref,
             dm_ref, dwuq_ref, dwukv_ref, dqn_ref, dkvn_ref):
        first = pl.program_id(0) == 0
        c, sa, sb = c_ref[...], sa_ref[...], sb_ref[...]
        mq, mkv = mq_ref[...], mkv_ref[...]
        rq = lax.rsqrt(jnp.mean(mq * mq, axis=-1, keepdims=True) + EPS)
        rkv = lax.rsqrt(jnp.mean(mkv * mkv, axis=-1, keepdims=True) + EPS)
        mqh, mkvh = mq * rq, mkv * rkv
        qn = (mqh * qn_ref[...]).astype(bf16)
        kvn = (mkvh * kvn_ref[...]).astype(bf16)
        dqn = jnp.zeros((tm, LORA), f32)
        dkvn = jnp.zeros((tm, LORA), f32)
        dkr = jnp.zeros((tm, DH), f32)
        for h in range(HEADS):
            da = dq_ref[:, pl.ds(h * QPAD, DH)] * MLA_SCALE
            db = _rope_mla_t(dq_ref[:, pl.ds(h * QPAD + DH, DH)] * MLA_SCALE, c, sa, sb)
            dqh = jnp.concatenate([da, db], axis=1).astype(bf16)
            dqn += lax.dot_general(dqh, wuq_ref[h], _DIMS["nt"], preferred_element_type=f32)
            _acc(dwuq_ref.at[h], lax.dot_general(qn, dqh, _DIMS["tn"], preferred_element_type=f32), first)
            dkr += dk_ref[:, pl.ds(h * QPAD + DH, DH)]
            dkvh = jnp.concatenate([dk_ref[:, pl.ds(h * QPAD, DH)], dv_ref[:, pl.ds(h * DH, DH)]], axis=1).astype(bf16)
            dkvn += lax.dot_general(dkvh, wukv_ref[h], _DIMS["nt"], preferred_element_type=f32)
            _acc(dwukv_ref.at[h], lax.dot_general(kvn, dkvh, _DIMS["tn"], preferred_element_type=f32), first)
        dmh = dqn * qn_ref[...]
        dm_ref[:, pl.ds(0, LORA)] = (rq * (dmh - mqh * jnp.mean(dmh * mqh, axis=-1, keepdims=True))).astype(bf16)
        dmh = dkvn * kvn_ref[...]
        dm_ref[:, pl.ds(LORA, LORA)] = (rkv * (dmh - mkvh * jnp.mean(dmh * mkvh, axis=-1, keepdims=True))).astype(bf16)
        dm_ref[:, pl.ds(2 * LORA, DH)] = _rope_mla_t(dkr, c, sa, sb).astype(bf16)
        _acc(dqn_ref, jnp.sum(dqn * mqh, axis=0, keepdims=True), first)
        _acc(dkvn_ref, jnp.sum(dkvn * mkvh, axis=0, keepdims=True), first)

    lat = lambda b: pl.BlockSpec((tm, LORA), lambda i: (i, b))
    tab = pl.BlockSpec((tm, DH), lambda i: (i, 0))
    vec = pl.BlockSpec((1, LORA), lambda i: (0, 0))
    wsp = pl.BlockSpec((HEADS, LORA, QPAD), lambda i: (0, 0, 0))
    wkv = pl.BlockSpec((HEADS, LORA, QPAD), lambda i: (0, 0, kv_blk))
    wide = pl.BlockSpec((tm, HEADS * QPAD), lambda i: (i, 0))
    return pl.pallas_call(
        body, name="mla_prep_bwd", grid=(S // tm,),
        in_specs=[wide, wide, pl.BlockSpec((tm, BW), lambda i: (i, 0)), lat(MQ0 // 4), lat(MKV0 // 4), vec, vec, wsp, wkv,
                  tab, tab, tab],
        out_specs=[pl.BlockSpec((tm, 2 * LORA + DH), lambda i: (i, 0)), wsp, wsp, vec, vec],
        out_shape=[SDS((S, 2 * LORA + DH), bf16), SDS((HEADS, LORA, QPAD), f32), SDS((HEADS, LORA, QPAD), f32),
                   SDS((1, LORA), f32), SDS((1, LORA), f32)],
        compiler_params=_cp(52),
    )(dq256, dk256, dv, proj, proj, qnorm, kvnorm, wuq, wukv, *tabs)


SUB = 8


def _scan_tiles(a_s, b_s, out, S, reverse):
    nt = S // SUB
    rows = lax.broadcasted_iota(jnp.int32, (SUB, LANE), 0)

    def tile(t, carry):
        base = pl.multiple_of((nt - 1 - t if reverse else t) * SUB, SUB)
        a, b = a_s[pl.ds(base, SUB), :], b_s[pl.ds(base, SUB), :]
        for d in (1, 2, 4):
            sh = SUB - d if reverse else d
            inside = rows < SUB - d if reverse else rows >= d
            a_n = jnp.where(inside, pltpu.roll(a, sh, 0), 1.0)
            b_n = jnp.where(inside, pltpu.roll(b, sh, 0), 0.0)
            b = a * b_n + b
            a = a * a_n
        res = a * carry + b
        out[pl.ds(base, SUB), :] = res
        edge = res[0:1, :] if reverse else res[SUB - 1:SUB, :]
        return jnp.broadcast_to(edge, (SUB, LANE))

    lax.fori_loop(0, nt, tile, jnp.zeros((SUB, LANE), f32))


def _shift_down(x, n, rows):
    return x if n == 0 else jnp.where(rows >= n, pltpu.roll(x, n, 0), 0.0)


def _shift_up(x, n, rows, S):
    return x if n == 0 else jnp.where(rows < S - n, pltpu.roll(x, S - n, 0), 0.0)


def _lru_gates(xb, cw, cb, wa, ba, wx, bx, lam, rows):
    xc = cb + cw[3:4, :] * xb
    for w in range(3):
        xc = xc + cw[w:w + 1, :] * _shift_down(xb, 3 - w, rows)
    xcb = xc.astype(bf16)
    r = _sigmoid(jnp.dot(xcb, wa, preferred_element_type=f32) + ba)
    i = _sigmoid(jnp.dot(xcb, wx, preferred_element_type=f32) + bx)
    sp = _softplus(-lam)
    la = (-LRU_C * r) * sp
    return xc, xcb, r, i, sp, la, jnp.exp(la)


def _lru_specs(S):
    col = lambda b0: pl.BlockSpec((S, LANE), lambda n: (0, b0 + n))
    vec = pl.BlockSpec((1, LANE), lambda n: (0, n))
    return col, vec, pl.BlockSpec((4, LANE), lambda n: (0, n)), pl.BlockSpec((1, LANE, LANE), lambda n: (n, 0, 0))


def lru_fwd(proj, cw, cb, wa, ba, wx, bx, lam):
    S = proj.shape[0]

    def body(x_ref, g_ref, cw_ref, cb_ref, wa_ref, ba_ref, wx_ref, bx_ref, lam_ref, h_ref, y_ref, a_s, b_s):
        rows = lax.broadcasted_iota(jnp.int32, (S, LANE), 0)
        xc, _, _, i, _, la, a = _lru_gates(x_ref[...], cw_ref[...], cb_ref[...], wa_ref[0].astype(bf16), ba_ref[...],
                                           wx_ref[0].astype(bf16), bx_ref[...], lam_ref[...], rows)
        a_s[...] = a
        b_s[...] = jnp.sqrt(_one_minus_exp(2.0 * la)) * (i * xc)
        _scan_tiles(a_s, b_s, h_ref, S, reverse=False)
        y_ref[...] = (h_ref[...] * _silu(g_ref[...])).astype(bf16)

    col, vec, cws, wsp = _lru_specs(S)
    return pl.pallas_call(
        body, name="lru_fwd", grid=(HEADS,),
        in_specs=[col(LX0), col(LG0), cws, vec, wsp, vec, wsp, vec, vec], out_specs=[col(0), col(0)],
        out_shape=[SDS((S, BW), f32), SDS((S, BW), bf16)],
        scratch_shapes=[pltpu.VMEM((S, LANE), f32), pltpu.VMEM((S, LANE), f32)], compiler_params=_cp(40),
    )(proj, proj, cw, cb, wa, ba, wx, bx, lam)


def lru_bwd(dy, h, proj, cw, cb, wa, ba, wx, bx, lam):
    S = proj.shape[0]

    def body(dy_ref, h_ref, x_ref, g_ref, cw_ref, cb_ref, wa_ref, ba_ref, wx_ref, bx_ref, lam_ref,
             dx_ref, dg_ref, dcw_ref, dcb_ref, dba_ref, dbx_ref, dlam_ref, dwa_ref, dwx_ref, a_s, b_s, l_s):
        rows = lax.broadcasted_iota(jnp.int32, (S, LANE), 0)
        xb, g, hv, dyv, cw, lam = x_ref[...], g_ref[...], h_ref[...], dy_ref[...], cw_ref[...], lam_ref[...]
        wa, wx = wa_ref[0].astype(bf16), wx_ref[0].astype(bf16)
        xc, xcb, r, i, sp, la, a = _lru_gates(xb, cw, cb_ref[...], wa, ba_ref[...], wx, bx_ref[...], lam, rows)
        dg_ref[...] = (dyv * hv * _dsilu(g)).astype(bf16)
        a_s[...] = _shift_up(a, 1, rows, S)
        b_s[...] = dyv * _silu(g)
        _scan_tiles(a_s, b_s, l_s, S, reverse=True)
        lmb = l_s[...]
        gated = i * xc
        sq = jnp.sqrt(_one_minus_exp(2.0 * la))
        dla = lmb * _shift_down(hv, 1, rows) * a - (lmb * gated) * (a * a) / sq
        dgated = lmb * sq
        dzr = (dla * (-LRU_C * sp)) * (r * (1.0 - r))
        dzi = (dgated * xc) * (i * (1.0 - i))
        dzrb, dzib = dzr.astype(bf16), dzi.astype(bf16)
        dxc = (dgated * i + lax.dot_general(dzrb, wa, _DIMS["nt"], preferred_element_type=f32)
               + lax.dot_general(dzib, wx, _DIMS["nt"], preferred_element_type=f32))
        dwa_ref[0] = lax.dot_general(xcb, dzrb, _DIMS["tn"], preferred_element_type=f32)
        dwx_ref[0] = lax.dot_general(xcb, dzib, _DIMS["tn"], preferred_element_type=f32)
        dba_ref[...] = jnp.sum(dzr, axis=0, keepdims=True)
        dbx_ref[...] = jnp.sum(dzi, axis=0, keepdims=True)
        dlam_ref[...] = jnp.sum(dla * (-LRU_C * r), axis=0, keepdims=True) * (-_sigmoid(-lam))
        dcb_ref[...] = jnp.sum(dxc, axis=0, keepdims=True)
        dxb = cw[3:4, :] * dxc
        dcw_ref[3:4, :] = jnp.sum(dxc * xb, axis=0, keepdims=True)
        for w in range(3):
            dxb = dxb + cw[w:w + 1, :] * _shift_up(dxc, 3 - w, rows, S)
            dcw_ref[w:w + 1, :] = jnp.sum(dxc * _shift_down(xb, 3 - w, rows), axis=0, keepdims=True)
        dx_ref[...] = dxb.astype(bf16)

    col, vec, cws, wsp = _lru_specs(S)
    scr = pltpu.VMEM((S, LANE), f32)
    return pl.pallas_call(
        body, name="lru_bwd", grid=(HEADS,),
        in_specs=[col(0), col(0), col(LX0), col(LG0), cws, vec, wsp, vec, wsp, vec, vec],
        out_specs=[col(0), col(0), cws, vec, vec, vec, vec, wsp, wsp],
        out_shape=[SDS((S, BW), bf16), SDS((S, BW), bf16), SDS((4, BW), f32)] + [SDS((1, BW), f32)] * 4
        + [SDS((HEADS, LANE, LANE), f32)] * 2,
        scratch_shapes=[scr, scr, scr], compiler_params=_cp(48),
    )(dy, h, proj, proj, cw, cb, wa, ba, wx, bx, lam)


def loss_head(y, target, *, tm=256):
    S = y.shape[0]

    def body(y_ref, t_ref, l_ref, d_ref):
        err = y_ref[...] - t_ref[...]
        d_ref[...] = err * (1.0 / D)
        part = jnp.sum(jnp.sum(err * err, axis=1, keepdims=True), axis=0, keepdims=True) * (0.5 / D)
        _acc(l_ref, jnp.broadcast_to(part, (1, LANE)), pl.program_id(0) == 0)

    row = pl.BlockSpec((tm, D), lambda i: (i, 0))
    return pl.pallas_call(
        body, name="loss_head", grid=(S // tm,), in_specs=[row, row],
        out_specs=[pl.BlockSpec((1, LANE), lambda i: (0, 0)), row],
        out_shape=[SDS((1, LANE), f32), SDS((S, D), f32)], compiler_params=_cp(32))(y, target)


def _adam_math(w, g, m, v):
    mn = ADAM_B1 * m + (1.0 - ADAM_B1) * g
    vn = ADAM_B2 * v + (1.0 - ADAM_B2) * (g * g)
    m_hat = mn / (1.0 - ADAM_B1 ** ADAM_STEP)
    v_hat = vn / (1.0 - ADAM_B2 ** ADAM_STEP)
    return -ADAM_LR * (m_hat / (jnp.sqrt(v_hat) + ADAM_EPS) + ADAM_WD * w), mn, vn


def _adam_rows(rows, cols):
    for cand in (2048, 1024, 512, 256, 128, 64, 32, 16, 8):
        if rows % cand == 0 and rows > cand and cand * cols <= ADAM_BLOCK_ELEMS:
            return cand
    return rows


def adamw(w, g, m, v):
    shape = w.shape
    cols = shape[-1]
    rows = math.prod(shape[:-1])
    tr = _adam_rows(rows, cols)

    def body(w_ref, g_ref, m_ref, v_ref, d_ref, mo_ref, vo_ref):
        d_ref[...], mo_ref[...], vo_ref[...] = _adam_math(w_ref[...], g_ref[...], m_ref[...], v_ref[...])

    blk = pl.BlockSpec((tr, cols), lambda i: (i, 0))
    flat = [t.reshape(rows, cols) for t in (w, g, m, v)]
    outs = pl.pallas_call(
        body, name="adamw", grid=(rows // tr,), in_specs=[blk] * 4, out_specs=[blk] * 3,
        out_shape=[SDS((rows, cols), f32)] * 3, compiler_params=_cp(48))(*flat)
    return tuple(o.reshape(shape) for o in outs)


ADA_SHARD = 3 * D // NDEV
ROWS16 = 16


def ada_fwd(c_all, ada_w, ada_b_mine):
    def body(c_ref, w_ref, b_ref, o_ref):
        o_ref[0] = jnp.dot(_silu(c_ref[...]).astype(bf16), w_ref[0].astype(bf16), preferred_element_type=f32) + b_ref[0]

    return pl.pallas_call(
        body, name="ada_fwd", grid=(DEPTH,),
        in_specs=[pl.BlockSpec((ROWS16, D), lambda l: (0, 0)), pl.BlockSpec((1, D, ADA_SHARD), lambda l: (l, 0, 0)),
                  pl.BlockSpec((1, 1, ADA_SHARD), lambda l: (l, 0, 0))],
        out_specs=pl.BlockSpec((1, ROWS16, ADA_SHARD), lambda l: (l, 0, 0)),
        out_shape=SDS((DEPTH, ROWS16, ADA_SHARD), f32), compiler_params=_cp(40))(c_all, ada_w, ada_b_mine)


def ada_bwd(c_all, dmod):
    def body(c_ref, d_ref, o_ref):
        o_ref[0] = lax.dot_general(_silu(c_ref[...]).astype(bf16), d_ref[0].astype(bf16), _DIMS["tn"],
                                   preferred_element_type=f32)

    return pl.pallas_call(
        body, name="ada_bwd", grid=(DEPTH,),
        in_specs=[pl.BlockSpec((ROWS16, D), lambda l: (0, 0)), pl.BlockSpec((1, ROWS16, ADA_SHARD), lambda l: (l, 0, 0))],
        out_specs=pl.BlockSpec((1, D, ADA_SHARD), lambda l: (l, 0, 0)),
        out_shape=SDS((DEPTH, D, ADA_SHARD), f32), compiler_params=_cp(40))(c_all, dmod)


def shift_params(me):
    start = SHARD_W * me
    return jnp.stack([start % LANE, (start + GAP) % LANE, jnp.clip(GAP_COL - start, 0, SHARD_W)]).astype(jnp.int32)


def win_pack(wt, sidx, *, after=()):
    def body(s_ref, w_ref, *rest):
        o_ref, scr = rest[len(after):]
        s1, s2, gi = s_ref[0], s_ref[1], s_ref[2]
        scr[pl.ds(SHARD_W, WIN - SHARD_W), :] = jnp.zeros((WIN - SHARD_W, LANE), f32)
        scr[pl.ds(0, SHARD_W), :] = w_ref[0]
        v = scr[...].T
        j = lax.broadcasted_iota(jnp.int32, v.shape, 1)
        o_ref[0] = jnp.where(j - s1 < gi, pltpu.roll(v, s1, 1),
                             jnp.where(j - s2 >= gi, pltpu.roll(v, s2, 1), 0.0)).astype(bf16)

    return pl.pallas_call(
        body, name="win_pack", grid=(DEPTH, D // LANE),
        in_specs=[pl.BlockSpec(memory_space=pltpu.SMEM), pl.BlockSpec((1, SHARD_W, LANE), lambda l, i: (l, 0, i))]
        + [pl.BlockSpec(memory_space=pl.ANY)] * len(after),
        out_specs=pl.BlockSpec((1, LANE, WIN), lambda l, i: (l, i, 0)),
        out_shape=SDS((DEPTH, D, WIN), bf16), scratch_shapes=[pltpu.VMEM((WIN, LANE), f32)],
        compiler_params=_cp(32))(sidx, wt, *after)


def win_assemble(g):
    own = WIN_STRIDE * LANE
    tail = NP - NDEV * own
    assert tail == 2 * LANE

    def body(a_ref, b_ref, o_ref):
        o_ref[...] = a_ref[0]

        @pl.when(pl.program_id(0) > 0)
        def _():
            o_ref[:, pl.ds(0, LANE)] = a_ref[0, :, pl.ds(0, LANE)] + b_ref[0]

    main = pl.pallas_call(
        body, name="win_assemble", grid=(NDEV,),
        in_specs=[pl.BlockSpec((1, D, own), lambda k: (k, 0, 0)),
                  pl.BlockSpec((1, D, LANE), lambda k: (jnp.maximum(k - 1, 0), 0, WIN_BLKS - 1))],
        out_specs=pl.BlockSpec((D, own), lambda k: (0, k)),
        out_shape=SDS((D, NP), bf16), compiler_params=_cp(48))(g, g)

    def tail_body(_, b_ref, o_ref):
        o_ref[:, pl.ds(0, LANE)] = b_ref[0]
        o_ref[:, pl.ds(LANE, LANE)] = jnp.zeros((D, LANE), bf16)

    return pl.pallas_call(
        tail_body, name="win_assemble_tail", grid=(1,),
        in_specs=[pl.BlockSpec(memory_space=pl.ANY), pl.BlockSpec((1, D, LANE), lambda t: (NDEV - 1, 0, WIN_BLKS - 1))],
        out_specs=pl.BlockSpec((D, tail), lambda t: (0, NDEV * own // tail)),
        out_shape=SDS((D, NP), bf16), input_output_aliases={0: 0}, compiler_params=_cp(32))(main, g)


def reduce_adamw(stag, w, m, v, l, prev, name, *, sidx=None, tr=128):
    Cs = stag.shape[2]
    n_prev = 0 if prev is None else 4
    n_lead = 1 if sidx is not None else 0

    def body(*refs):
        refs = list(refs)
        s_ref = refs.pop(0) if sidx is not None else None
        g_ref, w_ref, m_ref, v_ref = refs[:4]
        rest = refs[4 + n_prev:]
        go_ref, d_ref, mo_ref, vo_ref = rest[:4]
        tot = g_ref[0].astype(f32)
        for d in range(1, NDEV):
            tot = tot + g_ref[d].astype(f32)
        if sidx is not None:
            scr = rest[4]
            s1, s2, gi = s_ref[0], s_ref[1], s_ref[2]
            i = lax.broadcasted_iota(jnp.int32, tot.shape, 1)
            scr[...] = jnp.where(i < gi, pltpu.roll(tot, WIN - s1, 1), pltpu.roll(tot, WIN - s2, 1)).T
            tot = scr[pl.ds(0, SHARD_W), :]
        go_ref[0] = tot
        d_ref[0], mo_ref[0], vo_ref[0] = _adam_math(w_ref[0], tot, m_ref[0], v_ref[0])

    if sidx is not None:
        blk3 = pl.BlockSpec((1, SHARD_W, tr), lambda i: (l, 0, i))
        steps = w.shape[2] // tr
    else:
        blk3 = pl.BlockSpec((1, tr, w.shape[2]), lambda i: (l, i, 0))
        steps = w.shape[1] // tr
    in_specs = ([pl.BlockSpec(memory_space=pltpu.SMEM)] * n_lead + [pl.BlockSpec((NDEV, tr, Cs), lambda i: (0, i, 0)), blk3, blk3, blk3]
                + [pl.BlockSpec(memory_space=pl.ANY)] * n_prev)
    args = ([sidx] if sidx is not None else []) + [stag, w, m, v] + list(prev or ())
    return pl.pallas_call(
        body, name=name, grid=(steps,), in_specs=in_specs, out_specs=[blk3] * 4, out_shape=[SDS(w.shape, f32)] * 4,
        scratch_shapes=[pltpu.VMEM((Cs, tr), f32)] if sidx is not None else [],
        input_output_aliases={n_lead + 4 + k: k for k in range(n_prev)}, compiler_params=_cp(48),
    )(*args)


def cast_pad(w, cols_out, name):
    L, R, C = w.shape

    def body(w_ref, o_ref, *scr):
        if cols_out == C:
            o_ref[0] = w_ref[0].astype(bf16)
        else:
            scr[0][...] = jnp.zeros_like(scr[0])
            scr[0][:, pl.ds(0, C)] = w_ref[0]
            o_ref[0] = scr[0][...].astype(bf16)

    return pl.pallas_call(
        body, name=name, grid=(L,), in_specs=[pl.BlockSpec((1, R, C), lambda l: (l, 0, 0))],
        out_specs=pl.BlockSpec((1, R, cols_out), lambda l: (l, 0, 0)), out_shape=SDS((L, R, cols_out), bf16),
        scratch_shapes=[] if cols_out == C else [pltpu.VMEM((R, cols_out), f32)], compiler_params=_cp(32))(w)


def pack_qkv(w_uq, w_ukv):
    L = w_uq.shape[0]

    def body(q_ref, kv_ref, o_ref, scr):
        scr[...] = jnp.zeros_like(scr)
        scr[:, pl.ds(0, WUQ_COLS)] = q_ref[0]
        o_ref[0, :, pl.ds(0, QPAD)] = scr[...].astype(bf16)
        o_ref[0, :, pl.ds(QPAD, QPAD)] = kv_ref[0].astype(bf16)

    return pl.pallas_call(
        body, name="pack_qkv", grid=(L,),
        in_specs=[pl.BlockSpec((1, LORA, WUQ_COLS), lambda l: (l, 0, 0)), pl.BlockSpec((1, LORA, QPAD), lambda l: (l, 0, 0))],
        out_specs=pl.BlockSpec((1, LORA, 2 * QPAD), lambda l: (l, 0, 0)), out_shape=SDS((L, LORA, 2 * QPAD), bf16),
        scratch_shapes=[pltpu.VMEM((LORA, QPAD), f32)], compiler_params=_cp(32))(w_uq, w_ukv)


def sum_parts(stag, cols_out, name, *, tr=None):
    _, R, C = stag.shape
    tr = R if tr is None else tr

    def body(g_ref, o_ref, *scr):
        tot = g_ref[0].astype(f32)
        for d in range(1, NDEV):
            tot = tot + g_ref[d].astype(f32)
        if cols_out == C:
            o_ref[...] = tot
        else:
            scr[0][...] = tot
            o_ref[...] = scr[0][:, pl.ds(0, cols_out)]

    return pl.pallas_call(
        body, name=name, grid=(R // tr,), in_specs=[pl.BlockSpec((NDEV, tr, C), lambda i: (0, i, 0))],
        out_specs=pl.BlockSpec((tr, cols_out), lambda i: (i, 0)), out_shape=SDS((R, cols_out), f32),
        scratch_shapes=[] if cols_out == C else [pltpu.VMEM((tr, C), f32)], compiler_params=_cp(40))(stag)


MESH_ID = pl.DeviceIdType.MESH
HBM_SPEC = pl.BlockSpec(memory_space=pltpu.HBM)


def _place():
    return lax.axis_index("x"), lax.axis_index("y"), lax.axis_index("c")


def all_gather(arrs, name):
    n = len(arrs)

    def body(*refs):
        ins, outs = refs[:n], refs[n:2 * n]
        send_sems, recv_sems, local_sems = refs[2 * n:]
        x, y, c = _place()
        me, sibling = (x, y, c), (x, y, 1 - c)
        chips = [(1 - x, y), (x, 1 - y), (1 - x, 1 - y)]

        def copy(a, k, block, to, src=None):
            slot = outs[a].at[4 * block[0] + 2 * block[1] + block[2]]
            return pltpu.make_async_remote_copy(
                src_ref=slot if src is None else src, dst_ref=slot, send_sem=send_sems.at[7 * a + k],
                recv_sem=recv_sems.at[7 * a + k], device_id=to, device_id_type=MESH_ID)

        mine = [pltpu.make_async_copy(ins[a], outs[a].at[4 * x + 2 * y + c], local_sems.at[a]) for a in range(n)]
        for cp in mine:
            cp.start()
        first = []
        for a in range(n):
            first.append(copy(a, 0, me, sibling, src=ins[a]))
            first += [copy(a, 1 + j, me, (*chip, c), src=ins[a]) for j, chip in enumerate(chips)]
        for cp in first:
            cp.start()
        passed = []
        for j, chip in enumerate(chips):
            for a in range(n):
                copy(a, 1 + j, (*chip, c), me).wait_recv()
                cp = copy(a, 4 + j, (*chip, c), sibling)
                cp.start()
                passed.append(cp)
        for a in range(n):
            copy(a, 0, sibling, me).wait_recv()
        for j, chip in enumerate(chips):
            for a in range(n):
                copy(a, 4 + j, (*chip, 1 - c), me).wait_recv()
        for cp in first + passed:
            cp.wait_send()
        for cp in mine:
            cp.wait()

    return pl.pallas_call(
        body, name=name, in_specs=[HBM_SPEC] * n, out_specs=[HBM_SPEC] * n,
        out_shape=[SDS((NDEV,) + a.shape, a.dtype) for a in arrs],
        scratch_shapes=[pltpu.SemaphoreType.DMA((7 * n,)), pltpu.SemaphoreType.DMA((7 * n,)),
                        pltpu.SemaphoreType.DMA((n,))],
    )(*arrs)


AG_COLLECTIVE_ID = 0
RS_COLLECTIVE_ID = 1


def _everyone_else(x, y, c):
    return [(x ^ (r >> 2), y ^ ((r >> 1) & 1), c ^ (r & 1)) for r in range(1, NDEV)]


def _rendezvous(sem, peers):
    for peer in peers:
        pl.semaphore_signal(sem, inc=1, device_id=peer, device_id_type=MESH_ID)
    pl.semaphore_wait(sem, NDEV - 1)


def _sequencer_call(body, arrs, out_types, name, collective_id):
    n = len(arrs)
    return pl.kernel(
        body, name=name, out_type=out_types, mesh=plsc.ScalarSubcoreMesh(axis_name="sequencer", num_cores=1),
        scratch_types=[pltpu.SemaphoreType.DMA((7 * n,)), pltpu.SemaphoreType.DMA((7 * n,)),
                       pltpu.SemaphoreType.DMA((n,)), pltpu.SemaphoreType.REGULAR],
        compiler_params=pltpu.CompilerParams(collective_id=collective_id),
    )(*arrs)


def seq_all_gather(arrs, name):
    n = len(arrs)

    def body(*refs):
        ins, outs = refs[:n], refs[n:2 * n]
        send_sems, recv_sems, local_sems, exit_sem = refs[2 * n:]
        x, y, c = _place()
        peers = _everyone_else(x, y, c)
        _rendezvous(pltpu.get_barrier_semaphore(), peers)
        me, sibling = (x, y, c), (x, y, 1 - c)
        chips = [(1 - x, y), (x, 1 - y), (1 - x, 1 - y)]

        def copy(a, k, block, to, src=None):
            slot = outs[a].at[4 * block[0] + 2 * block[1] + block[2]]
            return pltpu.make_async_remote_copy(
                src_ref=slot if src is None else src, dst_ref=slot, send_sem=send_sems.at[7 * a + k],
                recv_sem=recv_sems.at[7 * a + k], device_id=to, device_id_type=MESH_ID)

        mine = [pltpu.make_async_copy(ins[a], outs[a].at[4 * x + 2 * y + c], local_sems.at[a]) for a in range(n)]
        for cp in mine:
            cp.start()
        first = []
        for a in range(n):
            first.append(copy(a, 0, me, sibling, src=ins[a]))
            first += [copy(a, 1 + j, me, (*chip, c), src=ins[a]) for j, chip in enumerate(chips)]
        for cp in first:
            cp.start()
        passed = []
        for j, chip in enumerate(chips):
            for a in range(n):
                copy(a, 1 + j, (*chip, c), me).wait_recv()
                cp = copy(a, 4 + j, (*chip, c), sibling)
                cp.start()
                passed.append(cp)
        for a in range(n):
            copy(a, 0, sibling, me).wait_recv()
        for j, chip in enumerate(chips):
            for a in range(n):
                copy(a, 4 + j, (*chip, 1 - c), me).wait_recv()
        for cp in first + passed:
            cp.wait_send()
        for cp in mine:
            cp.wait()
        _rendezvous(exit_sem, peers)

    return _sequencer_call(body, arrs, [SDS((NDEV,) + a.shape, a.dtype) for a in arrs], name, AG_COLLECTIVE_ID)


def seq_reduce_scatter_parts(arrs, pick, shapes, name):
    n = len(arrs)

    def body(*refs):
        ins, outs = refs[:n], refs[n:2 * n]
        send_sems, recv_sems, local_sems, exit_sem = refs[2 * n:]
        x, y, c = _place()
        peers = _everyone_else(x, y, c)
        _rendezvous(pltpu.get_barrier_semaphore(), peers)
        me = 4 * x + 2 * y + c
        mine = [pltpu.make_async_copy(pick[a](ins[a], me), outs[a].at[me], local_sems.at[a]) for a in range(n)]
        for cp in mine:
            cp.start()
        sent = []
        for r, peer in enumerate(peers):
            pid = 4 * peer[0] + 2 * peer[1] + peer[2]
            for a in range(n):
                cp = pltpu.make_async_remote_copy(
                    src_ref=pick[a](ins[a], pid), dst_ref=outs[a].at[me], send_sem=send_sems.at[7 * a + r],
                    recv_sem=recv_sems.at[7 * a + r], device_id=peer, device_id_type=MESH_ID)
                cp.start()
                sent.append((cp, a, r, pid))
        for cp, a, r, pid in sent:
            pltpu.make_async_remote_copy(
                src_ref=pick[a](ins[a], pid), dst_ref=outs[a].at[pid], send_sem=send_sems.at[7 * a + r],
                recv_sem=recv_sems.at[7 * a + r], device_id=(x, y, c), device_id_type=MESH_ID).wait_recv()
        for cp, _, _, _ in sent:
            cp.wait_send()
        for cp in mine:
            cp.wait()
        _rendezvous(exit_sem, peers)

    return _sequencer_call(body, arrs, [SDS((NDEV,) + tuple(s), a.dtype) for s, a in zip(shapes, arrs)], name,
                           RS_COLLECTIVE_ID)


WEIGHTS = ("ada_w", "ada_b", "norm_pre", "norm_post", "w_in", "ret_gn", "lru_conv_w", "lru_conv_b", "lru_wa", "lru_ba",
           "lru_wx", "lru_bx", "lru_lambda", "mla_q_norm", "mla_w_uq", "mla_kv_norm", "mla_w_ukv", "w_branch", "w_out")
SMALL = ("norm_pre", "norm_post", "ret_gn", "lru_conv_w", "lru_conv_b", "lru_ba", "lru_bx", "lru_lambda", "mla_q_norm",
         "mla_kv_norm")
QKV_ROWS = LORA
QKV_BLOCK = (LORA + LANE, 2 * QPAD)
BR_ROWS = 3 * BW // NDEV
OUT_ROWS = D // NDEV
WUQ_COLS = 192


def _row(v):
    return v.reshape(1, -1)


def layer_fwd(xl, mod, p, wts, tabs, lgam):
    S = xl.shape[0]
    tm = min(S, 2048)
    sh, sc, rg = _row(mod[:D]), _row(mod[D:2 * D]), _row(mod[2 * D:])
    ret_tabs, mla_tabs = tabs
    h = pre_fwd(xl, _row(p["norm_pre"]), sc, sh)
    proj = matmul(h, wts["w_in"], dims="nn", M=S, N=NP, K=D, tm=tm, tn=768, tk=D, out_dtype=f32, name="mm_in")
    qk = ret_prep_fwd(proj, *ret_tabs)
    o_ret, y_ret = attn_fwd(qk, qk, proj, proj, softmax=False, dq=DH, q_blk0=0, k_blk0=HEADS, v_blk0=V0, gate_blk0=RG0,
                            lgam=lgam, gn=_row(p["ret_gn"]), name="ret_attn_fwd")
    h_lru, y_lru = lru_fwd(proj, p["conv_w"], _row(p["lru_conv_b"]), p["lru_wa"], _row(p["lru_ba"]), p["lru_wx"],
                           _row(p["lru_bx"]), _row(p["lru_lambda"]))
    q256, k256, vm = mla_prep_fwd(proj, _row(p["mla_q_norm"]), _row(p["mla_kv_norm"]), wts["w_qkv"], wts["w_qkv"], mla_tabs,
                                  kv_blk=1)
    o_mla, y_mla = attn_fwd(q256, k256, vm, proj, softmax=True, dq=QPAD, q_blk0=0, k_blk0=0, v_blk0=0, gate_blk0=MG0,
                            name="mla_attn_fwd")
    ys = (y_ret, y_lru, y_mla)
    us = [matmul(ys[b], wts["w_branch"], dims="nn", M=S, N=D, K=BW, tm=min(S, 1024), tn=1024, tk=BW,
                 out_dtype=bf16, name="mm_branch", b_blk0=(b, 0)) for b in range(3)]
    merged = gate_fwd(proj, us)
    y, x_next = out_fwd(merged, wts["w_out"], xl, rg, _row(p["norm_post"]))
    saved = dict(x=xl, h=h, proj=proj, qk=qk, o_ret=o_ret, h_lru=h_lru, q256=q256, k256=k256, vm=vm, o_mla=o_mla,
                 ys=ys, us=us, merged=merged, y=y, sc=sc, rg=rg)
    return x_next, saved


def layer_bwd(dx, sv, p, wts, tabs, lgam):
    S = dx.shape[0]
    tm = min(S, 2048)
    ret_tabs, mla_tabs = tabs
    proj = sv["proj"]
    dy, d_rg, d_gpost = out_bwd(dx, sv["y"], sv["rg"], _row(p["norm_post"]))
    dmerged = matmul(dy, wts["w_out"], dims="nt", M=S, N=D, K=D, tm=min(S, 1024), tn=1024, tk=D, out_dtype=f32, name="mm_dmerged")
    dw_out = matmul(sv["merged"], dy, dims="tn", M=D, N=D, K=S, tm=1024, tn=1024, tk=min(S, 1024), out_dtype=bf16, name="mm_dwout")
    du, dml = gate_bwd(dmerged, proj, sv["us"])
    tmb, tkb = min(S, 1024), min(S, 1024)
    dys = [matmul(du[b], wts["w_branch"], dims="nt", M=S, N=BW, K=D, tm=tmb, tn=BW, tk=D, out_dtype=f32, name="mm_dybranch",
                  b_blk0=(b, 0)) for b in range(3)]
    dw_branch = jnp.concatenate(
        [matmul(sv["ys"][b], du[b], dims="tn", M=BW, N=D, K=S, tm=BW, tn=1024, tk=tkb, out_dtype=bf16, name="mm_dwbranch")
         for b in range(3)], axis=0)
    do, d_rgate, d_gn = ret_post_bwd(dys[0], sv["o_ret"], proj, _row(p["ret_gn"]))
    dq, dk, dv = attn_bwd(sv["qk"], sv["qk"], proj, do, None, softmax=False, dq=DH, q_blk0=0, k_blk0=HEADS, v_blk0=V0,
                          lgam=lgam, name="ret_attn_bwd")
    d_qk = ret_prep_bwd(jnp.concatenate([dq, dk], axis=1), *ret_tabs)
    d_lx, d_lg, d_cw, d_cb, d_ba, d_bx, d_lam, d_wa, d_wx = lru_bwd(
        dys[1], sv["h_lru"], proj, p["conv_w"], _row(p["lru_conv_b"]), p["lru_wa"], _row(p["lru_ba"]), p["lru_wx"],
        _row(p["lru_bx"]), _row(p["lru_lambda"]))
    do, d_mg = mla_post_bwd(dys[2], sv["o_mla"], proj)
    dq256, dk256, dvm = attn_bwd(sv["q256"], sv["k256"], sv["vm"], do, sv["o_mla"], softmax=True, dq=QPAD, q_blk0=0,
                                 k_blk0=0, v_blk0=0, name="mla_attn_bwd")
    d_lat, dw_uq, dw_ukv, d_qn, d_kvn = mla_prep_bwd(dq256, dk256, dvm, proj, _row(p["mla_q_norm"]), _row(p["mla_kv_norm"]),
                                                       wts["w_qkv"], wts["w_qkv"], mla_tabs, kv_blk=1)
    dproj = jnp.concatenate([d_qk, dv.astype(bf16), d_rgate, d_lx, d_lg, d_lat, d_mg, *dml, jnp.zeros((S, LANE), bf16)], axis=1)
    dh = matmul(dproj, wts["w_in"], dims="nt", M=S, N=D, K=NP, tm=min(S, 1024), tn=1024, tk=NP // 6, out_dtype=f32, name="mm_dh")
    dw_in = matmul(sv["h"], dproj, dims="tn", M=D, N=NP, K=S, tm=D, tn=768, tk=tm, out_dtype=bf16, name="mm_dwin")
    dw_qkv = jnp.concatenate([jnp.concatenate([dw_uq, dw_ukv], axis=2),
                              jnp.concatenate([d_wa, d_wx, jnp.zeros((HEADS, LANE, QPAD), f32)], axis=2)], axis=1).astype(bf16)
    big = dict(w_in=dw_in, w_branch=dw_branch, w_out=dw_out, w_qkv=dw_qkv)
    dxl, d_sh, d_sc, d_gpre = pre_bwd(dh, sv["x"], _row(p["norm_pre"]), sv["sc"], dx, after=tuple(big.values()))
    dmod = jnp.concatenate([d_sh, d_sc, d_rg], axis=1).reshape(-1)
    small = dict(norm_pre=d_gpre, norm_post=d_gpost, ret_gn=d_gn, lru_conv_w=d_cw, lru_conv_b=d_cb, lru_ba=d_ba,
                 lru_bx=d_bx, lru_lambda=d_lam, mla_q_norm=d_qn, mla_kv_norm=d_kvn)
    return dxl, dmod, big, small


def exchange_grads(big, l):
    picks = [lambda r, d: r.at[:, pl.ds(pl.multiple_of(d * (WIN_STRIDE * LANE), LANE), WIN)],
             lambda r, d: r.at[pl.ds(pl.multiple_of(d * BR_ROWS, 8), BR_ROWS), :],
             lambda r, d: r.at[pl.ds(pl.multiple_of(d * OUT_ROWS, 8), OUT_ROWS), :],
             lambda r, d: r.at[d]]
    shapes = [(D, WIN), (BR_ROWS, D), (OUT_ROWS, D), QKV_BLOCK]
    arrs = [big["w_in"], big["w_branch"], big["w_out"], big["w_qkv"]]
    return seq_reduce_scatter_parts(arrs, picks, shapes, f"rs_grads_{l}")


def kernel(x, c, positions, ada_w, ada_b, norm_pre, norm_post, w_in, ret_gn, lru_conv_w, lru_conv_b, lru_wa, lru_ba, lru_wx, lru_bx, lru_lambda, mla_q_norm, mla_w_uq, mla_kv_norm, mla_w_ukv, w_branch, w_out, loss_target, m_ada_w, m_ada_b, m_norm_pre, m_norm_post, m_w_in, m_ret_gn, m_lru_conv_w, m_lru_conv_b, m_lru_wa, m_lru_ba, m_lru_wx, m_lru_bx, m_lru_lambda, m_mla_q_norm, m_mla_w_uq, m_mla_kv_norm, m_mla_w_ukv, m_w_branch, m_w_out, v_ada_w, v_ada_b, v_norm_pre, v_norm_post, v_w_in, v_ret_gn, v_lru_conv_w, v_lru_conv_b, v_lru_wa, v_lru_ba, v_lru_wx, v_lru_bx, v_lru_lambda, v_mla_q_norm, v_mla_w_uq, v_mla_kv_norm, v_mla_w_ukv, v_w_branch, v_w_out):
    given = dict(locals())
    xi, yi, ci = _place()
    me = 4 * xi + 2 * yi + ci
    S = x.shape[1]
    sidx = shift_params(me)
    lgam = jnp.asarray(np.log1p(-np.exp2(-5.0 - np.arange(HEADS))), f32)
    tabs = rope_tables(positions[0])

    (g_small,) = all_gather([jnp.concatenate([c.reshape(16, LANE), lru_conv_w.reshape(16, LANE)], axis=0)], "ag_small")
    c16 = jnp.concatenate([g_small[:, :16].reshape(NDEV, D), jnp.zeros((ROWS16 - NDEV, D), f32)], axis=0)
    conv_w_all = g_small[:, 16:].reshape(NDEV, DEPTH, 4, LANE).transpose(1, 2, 0, 3).reshape(DEPTH, 4, BW)
    ada_b_mine = lax.dynamic_slice_in_dim(ada_b, me * ADA_SHARD, ADA_SHARD, axis=1).reshape(DEPTH, 1, ADA_SHARD)
    (g_mod,) = all_gather([ada_fwd(c16, ada_w, ada_b_mine)[:, :NDEV]], "ag_mod")
    mods = lax.dynamic_index_in_dim(g_mod, me, axis=2, keepdims=False).transpose(1, 0, 2).reshape(DEPTH, 3 * D)
    w_in_t = {n: jnp.swapaxes(given[n], 1, 2) for n in ("w_in", "m_w_in", "v_w_in")}
    packed = (win_pack(w_in_t["w_in"], sidx, after=(g_mod,)), cast_pad(w_branch, D, "pack_wbranch"), cast_pad(w_out, D, "pack_wout"),
              pack_qkv(mla_w_uq, mla_w_ukv))
    gathered = [seq_all_gather([t[l] for t in packed], f"ag_weights_{l}") for l in range(DEPTH)]

    params, wts = [], []
    for l in range(DEPTH):
        p = {n: given[n][l] for n in SMALL + ("lru_wa", "lru_wx") if n != "lru_conv_w"}
        p["conv_w"] = conv_w_all[l]
        params.append(p)
        g_win, g_br, g_out, g_qkv = gathered[l]
        wts.append(dict(w_in=win_assemble(g_win), w_qkv=g_qkv, w_branch=g_br.reshape(3 * BW, D), w_out=g_out.reshape(D, D)))

    xl, saved = x[0], []
    for l in range(DEPTH):
        xl, sv = layer_fwd(xl, mods[l], params[l], wts[l], tabs, lgam)
        saved.append(sv)
    my_loss, dx = loss_head(xl, loss_target[0])
    loss = lax.psum(my_loss[0, 0], ("x", "y", "c"))

    dmods, smalls, staged, grads = [None] * DEPTH, [None] * DEPTH, [None] * DEPTH, {n: [None] * DEPTH for n in WEIGHTS}
    for l in reversed(range(DEPTH)):
        dx, dmods[l], big, smalls[l] = layer_bwd(dx, saved[l], params[l], wts[l], tabs, lgam)
        staged[l] = exchange_grads(big, l)
    chained, lru_blocks = {"w_in": None, "w_branch": None, "w_out": None}, [None] * DEPTH
    for l in reversed(range(DEPTH)):
        st_win, st_br, st_out, st_qkv = staged[l]
        chained["w_in"] = reduce_adamw(st_win, w_in_t["w_in"], w_in_t["m_w_in"], w_in_t["v_w_in"], l, chained["w_in"],
                                       "update_w_in", sidx=sidx)
        for n, st in (("w_branch", st_br), ("w_out", st_out)):
            chained[n] = reduce_adamw(st, given[n], given["m_" + n], given["v_" + n], l, chained[n], "update_" + n)
        g_qkv = sum_parts(st_qkv, 2 * QPAD, "sum_wqkv")
        grads["mla_w_uq"][l] = g_qkv[:QKV_ROWS, :WUQ_COLS]
        grads["mla_w_ukv"][l] = g_qkv[:QKV_ROWS, QPAD:]
        lru_blocks[l] = g_qkv[QKV_ROWS:, :2 * LANE]
    (g_lru,) = all_gather([jnp.stack(lru_blocks)], "ag_lru_w")
    grads["lru_wa"] = g_lru[..., :LANE].transpose(1, 0, 2, 3)
    grads["lru_wx"] = g_lru[..., LANE:].transpose(1, 0, 2, 3)

    flat = [jnp.stack(dmods).reshape(-1)] + [smalls[l][n].reshape(-1) for l in range(DEPTH) for n in SMALL]
    sizes = [int(t.shape[0]) for t in flat]
    (g_pack,) = all_gather([jnp.concatenate(flat).reshape(-1, LANE)], "ag_small_grads")
    rows = g_pack.shape[1]
    tot = sum_parts(g_pack, LANE, "sum_small_grads", tr=rows // 8).reshape(-1)
    offs = np.concatenate([[0], np.cumsum(sizes)])
    pieces = [tot[int(offs[i]):int(offs[i + 1])] for i in range(len(sizes))]
    grads["ada_b"] = pieces[0].reshape(DEPTH, 3 * D)
    for l in range(DEPTH):
        for j, n in enumerate(SMALL):
            piece = pieces[1 + l * len(SMALL) + j]
            if n == "lru_conv_w":
                piece = lax.dynamic_slice_in_dim(piece.reshape(4, BW), me * LANE, LANE, axis=1)
            grads[n][l] = piece.reshape(given[n].shape[1:])
    dmod_all = g_pack[:, :DEPTH * 3 * D // LANE].reshape(NDEV, DEPTH, 3 * D)
    dmod_mine = lax.dynamic_slice_in_dim(dmod_all, me * ADA_SHARD, ADA_SHARD, axis=2).transpose(1, 0, 2)
    dmod16 = jnp.concatenate([dmod_mine, jnp.zeros((DEPTH, ROWS16 - NDEV, ADA_SHARD), f32)], axis=1)
    grads["ada_w"] = ada_bwd(c16, dmod16)

    outs = {"grad": [], "delta": [], "m": [], "v": []}
    for n in WEIGHTS:
        if n in chained:
            g, delta, new_m, new_v = (jnp.swapaxes(t, 1, 2) for t in chained[n]) if n == "w_in" else chained[n]
        else:
            g = grads[n] if not isinstance(grads[n], list) else jnp.stack(grads[n])
            delta, new_m, new_v = adamw(given[n], g, given["m_" + n], given["v_" + n])
        outs["grad"].append(g)
        outs["delta"].append(delta)
        outs["m"].append(new_m)
        outs["v"].append(new_v)
    return (loss, dx[None], *outs["grad"], *outs["delta"], *outs["m"], *outs["v"])
```

```python
import functools
import math

import numpy as np
import jax
import jax.numpy as jnp
from jax import lax
from jax.experimental import pallas as pl
from jax.experimental.pallas import tpu as pltpu
from jax.experimental.pallas import tpu_sc as plsc

f32 = jnp.float32
bf16 = jnp.bfloat16
SDS = jax.ShapeDtypeStruct

DEPTH = 4
D = 2048
HEADS = 8
DH = 128
BW = HEADS * DH
LANE = 128
CHUNK = 64
EPS = 1e-6
LRU_C = 8.0
NDEV = 8
VMEM_V7X = 64 * 1024 * 1024

Q0, K0, V0, RG0, LX0, LG0, MQ0, MKV0, MKR0, MG0, ML0 = 0, 8, 16, 24, 32, 40, 48, 52, 56, 57, 65
NB = 114
NP = NB * LANE
IN_W = 14400
SHARD_W = IN_W // NDEV
GAP_COL = 7232
GAP = 64
WIN = 1920
WIN_BLKS = WIN // LANE
WIN_STRIDE = 14
LORA = 512
QPAD = 256

ADAM_LR, ADAM_B1, ADAM_B2, ADAM_EPS, ADAM_WD, ADAM_STEP = 0.001, 0.9, 0.999, 1e-08, 0.01, 10
ADAM_BLOCK_ELEMS = 256 * 1024
MLA_SCALE = (128 + 64) ** -0.5
RET_SCALE = 128 ** -0.5


def _cp(vmem_mb=None, **kw):
    if vmem_mb is not None:
        kw["vmem_limit_bytes"] = min(vmem_mb * 1024 * 1024, VMEM_V7X - 8 * 1024 * 1024)
    return pltpu.CompilerParams(**kw)


def _sigmoid(x):
    return 1.0 / (1.0 + jnp.exp(-x))


def _silu(x):
    return x * _sigmoid(x)


def _dsilu(x):
    s = _sigmoid(x)
    return s * (1.0 + x * (1.0 - s))


def _softplus(x):
    return jnp.maximum(x, 0.0) + jnp.log(1.0 + jnp.exp(-jnp.abs(x)))


def _one_minus_exp(y):
    series = -y * (1.0 + y * (0.5 + y * (1.0 / 6.0)))
    return jnp.where(y > -1e-2, series, 1.0 - jnp.exp(y))


def _acc(ref, val, first):
    @pl.when(first)
    def _():
        ref[...] = val

    @pl.when(jnp.logical_not(first))
    def _():
        ref[...] += val


_DIMS = {"nn": (((1,), (0,)), ((), ())), "nt": (((1,), (1,)), ((), ())), "tn": (((0,), (0,)), ((), ()))}


def matmul(a, b, *, dims, M, N, K, tm, tn, tk, out_dtype, name, a_blk0=(0, 0), b_blk0=(0, 0), vmem_mb=48):
    nk = K // tk
    assert M % tm == 0 and N % tn == 0 and K % tk == 0
    dn = _DIMS[dims]

    def body(a_ref, b_ref, o_ref, *scr):
        part = lax.dot_general(a_ref[...].astype(bf16), b_ref[...].astype(bf16), dn, preferred_element_type=f32)
        if nk == 1:
            o_ref[...] = part.astype(out_dtype)
        else:
            acc = scr[0]
            k = pl.program_id(2)
            _acc(acc, part, k == 0)

            @pl.when(k == nk - 1)
            def _():
                o_ref[...] = acc[...].astype(out_dtype)

    ar, ac = a_blk0
    br, bc = b_blk0
    if dims == "nn":
        a_spec = pl.BlockSpec((tm, tk), lambda i, j, k: (i + ar, k + ac))
        b_spec = pl.BlockSpec((tk, tn), lambda i, j, k: (k + br, j + bc))
    elif dims == "nt":
        a_spec = pl.BlockSpec((tm, tk), lambda i, j, k: (i + ar, k + ac))
        b_spec = pl.BlockSpec((tn, tk), lambda i, j, k: (j + br, k + bc))
    else:
        a_spec = pl.BlockSpec((tk, tm), lambda i, j, k: (k + ar, i + ac))
        b_spec = pl.BlockSpec((tk, tn), lambda i, j, k: (k + br, j + bc))
    return pl.pallas_call(
        body, name=name, grid=(M // tm, N // tn, nk),
        in_specs=[a_spec, b_spec], out_specs=pl.BlockSpec((tm, tn), lambda i, j, k: (i, j)),
        out_shape=SDS((M, N), out_dtype),
        scratch_shapes=[] if nk == 1 else [pltpu.VMEM((tm, tn), f32)],
        compiler_params=_cp(vmem_mb, dimension_semantics=("parallel", "parallel", "arbitrary")),
    )(a, b)


def pre_fwd(x, g, sc, sh, *, tm=256):
    S = x.shape[0]

    def body(x_ref, g_ref, sc_ref, sh_ref, h_ref):
        xv = x_ref[...]
        r = lax.rsqrt(jnp.mean(xv * xv, axis=-1, keepdims=True) + EPS)
        h_ref[...] = (((xv * r) * g_ref[...]) * (1.0 + sc_ref[...]) + sh_ref[...]).astype(bf16)

    row = pl.BlockSpec((tm, D), lambda i: (i, 0))
    vec = pl.BlockSpec((1, D), lambda i: (0, 0))
    return pl.pallas_call(body, name="pre_fwd", grid=(S // tm,), in_specs=[row, vec, vec, vec], out_specs=row,
                          out_shape=SDS((S, D), bf16), compiler_params=_cp(32))(x, g, sc, sh)


def pre_bwd(dh, x, g, sc, dxo, *, tm=256, after=()):
    S = x.shape[0]

    def body(dh_ref, x_ref, g_ref, sc_ref, dxo_ref, *rest):
        dx_ref, dsh_ref, dsc_ref, dg_ref = rest[len(after):]
        first = pl.program_id(0) == 0
        xv, dhv, gv = x_ref[...], dh_ref[...], g_ref[...]
        one_sc = 1.0 + sc_ref[...]
        r = lax.rsqrt(jnp.mean(xv * xv, axis=-1, keepdims=True) + EPS)
        xh = xv * r
        t = dhv * xh
        dxh = dhv * gv * one_sc
        dx_ref[...] = r * (dxh - xh * jnp.mean(dxh * xh, axis=-1, keepdims=True)) + dxo_ref[...]
        _acc(dsh_ref, jnp.sum(dhv, axis=0, keepdims=True), first)
        _acc(dsc_ref, jnp.sum(t * gv, axis=0, keepdims=True), first)
        _acc(dg_ref, jnp.sum(t * one_sc, axis=0, keepdims=True), first)

    row = pl.BlockSpec((tm, D), lambda i: (i, 0))
    vec = pl.BlockSpec((1, D), lambda i: (0, 0))
    return pl.pallas_call(
        body, name="pre_bwd", grid=(S // tm,),
        in_specs=[row, row, vec, vec, row] + [pl.BlockSpec(memory_space=pl.ANY)] * len(after), out_specs=[row, vec, vec, vec],
        out_shape=[SDS((S, D), f32), SDS((1, D), f32), SDS((1, D), f32), SDS((1, D), f32)],
        compiler_params=_cp(40))(dh, x, g, sc, dxo, *after)


def out_fwd(merged, w_out, x, rg, gp, *, tm=256):
    S = x.shape[0]

    def body(m_ref, w_ref, x_ref, rg_ref, gp_ref, y_ref, xn_ref):
        y = jnp.dot(m_ref[...], w_ref[...], preferred_element_type=f32)
        y_ref[...] = y
        r = lax.rsqrt(jnp.mean(y * y, axis=-1, keepdims=True) + EPS)
        xn_ref[...] = x_ref[...] + (1.0 + rg_ref[...]) * ((y * r) * gp_ref[...])

    row = pl.BlockSpec((tm, D), lambda i: (i, 0))
    vec = pl.BlockSpec((1, D), lambda i: (0, 0))
    return pl.pallas_call(
        body, name="out_fwd", grid=(S // tm,),
        in_specs=[row, pl.BlockSpec((D, D), lambda i: (0, 0)), row, vec, vec], out_specs=[row, row],
        out_shape=[SDS((S, D), f32), SDS((S, D), f32)], compiler_params=_cp(48))(merged, w_out, x, rg, gp)


def out_bwd(dxo, y, rg, gp, *, tm=256):
    S = y.shape[0]

    def body(dxo_ref, y_ref, rg_ref, gp_ref, dy_ref, drg_ref, dgp_ref):
        first = pl.program_id(0) == 0
        yv, dv, gv = y_ref[...], dxo_ref[...], gp_ref[...]
        r = lax.rsqrt(jnp.mean(yv * yv, axis=-1, keepdims=True) + EPS)
        yh = yv * r
        dn = dv * (1.0 + rg_ref[...])
        dyh = dn * gv
        dy_ref[...] = (r * (dyh - yh * jnp.mean(dyh * yh, axis=-1, keepdims=True))).astype(bf16)
        _acc(drg_ref, jnp.sum(dv * (yh * gv), axis=0, keepdims=True), first)
        _acc(dgp_ref, jnp.sum(dn * yh, axis=0, keepdims=True), first)

    row = pl.BlockSpec((tm, D), lambda i: (i, 0))
    vec = pl.BlockSpec((1, D), lambda i: (0, 0))
    return pl.pallas_call(
        body, name="out_bwd", grid=(S // tm,), in_specs=[row, row, vec, vec], out_specs=[row, vec, vec],
        out_shape=[SDS((S, D), bf16), SDS((1, D), f32), SDS((1, D), f32)], compiler_params=_cp(40))(dxo, y, rg, gp)


def _ml_spec(b, tm):
    return pl.BlockSpec((tm, LANE), lambda i, j: (i, ML0 + b * (D // LANE) + j))


def gate_fwd(proj, us, *, tm=2048):
    S = proj.shape[0]
    tm = min(tm, S)

    def body(ml0, ml1, ml2, u0, u1, u2, m_ref):
        acc = None
        for ml, u in ((ml0, u0), (ml1, u1), (ml2, u2)):
            t = _sigmoid(ml[...]) * u[...].astype(f32)
            acc = t if acc is None else acc + t
        m_ref[...] = acc.astype(bf16)

    blk = pl.BlockSpec((tm, LANE), lambda i, j: (i, j))
    return pl.pallas_call(
        body, name="gate_fwd", grid=(S // tm, D // LANE),
        in_specs=[_ml_spec(0, tm), _ml_spec(1, tm), _ml_spec(2, tm), blk, blk, blk], out_specs=blk,
        out_shape=SDS((S, D), bf16), compiler_params=_cp(32),
    )(proj, proj, proj, *us)


def gate_bwd(dmerged, proj, us, *, tm=2048):
    S = proj.shape[0]
    tm = min(tm, S)

    def body(dm_ref, ml0, ml1, ml2, u0, u1, u2, du0, du1, du2, dl0, dl1, dl2):
        dm = dm_ref[...]
        for ml, u, du, dl in ((ml0, u0, du0, dl0), (ml1, u1, du1, dl1), (ml2, u2, du2, dl2)):
            s = _sigmoid(ml[...])
            du[...] = (dm * s).astype(bf16)
            dl[...] = (dm * u[...].astype(f32) * (s * (1.0 - s))).astype(bf16)

    blk = pl.BlockSpec((tm, LANE), lambda i, j: (i, j))
    outs = pl.pallas_call(
        body, name="gate_bwd", grid=(S // tm, D // LANE),
        in_specs=[blk, _ml_spec(0, tm), _ml_spec(1, tm), _ml_spec(2, tm), blk, blk, blk], out_specs=[blk] * 6,
        out_shape=[SDS((S, D), bf16)] * 6, compiler_params=_cp(40),
    )(dmerged, proj, proj, proj, *us)
    return outs[:3], outs[3:]


def rope_tables(positions):
    pos = positions.astype(f32)[:, None]

    def cs(dim):
        inv = 10000.0 ** (-jnp.arange(0, dim, 2, dtype=f32) / dim)
        ang = pos * inv
        return jnp.cos(ang), jnp.sin(ang)

    c, s = cs(128)
    ret = (jnp.concatenate([c, c], 1), jnp.concatenate([-s, s], 1))
    c, s = cs(64)
    z32, z64 = jnp.zeros_like(c), jnp.zeros((c.shape[0], 64), f32)
    mla = (jnp.concatenate([c, c, z64], 1), jnp.concatenate([-s, z32, z64], 1), jnp.concatenate([z32, s, z64], 1))
    return ret, mla


def _rope_ret(x, c, s):
    return x * c + pltpu.roll(x, 64, 1) * s


def _rope_ret_t(dy, c, s):
    return dy * c + pltpu.roll(dy * s, 64, 1)


def _rope_mla(x, c, sa, sb):
    return x * c + pltpu.roll(x, 96, 1) * sa + pltpu.roll(x, 32, 1) * sb


def _rope_mla_t(dy, c, sa, sb):
    return dy * c + pltpu.roll(dy * sa, 32, 1) + pltpu.roll(dy * sb, 96, 1)


def _scores(q, k, qi, bq, lg, softmax):
    nk = k.shape[0]
    s = lax.dot_general(q, k, _DIMS["nt"], preferred_element_type=f32)
    ti = qi * bq + lax.broadcasted_iota(jnp.int32, (bq, nk), 0)
    tj = lax.broadcasted_iota(jnp.int32, (bq, nk), 1)
    mask = (tj // CHUNK) <= (ti // CHUNK)
    if softmax:
        s = jnp.where(mask, s, -1e30)
        e = jnp.exp(s - jnp.max(s, axis=-1, keepdims=True))
        return e / jnp.sum(e, axis=-1, keepdims=True), None
    w = jnp.where(mask, jnp.exp(lg * jnp.abs(ti - tj).astype(f32)), 0.0)
    return s * w, w


def _per_query_block(nq, fn):
    for qi in range(nq):
        pl.when(pl.program_id(1) == qi)(functools.partial(fn, qi))


def attn_fwd(q, k, v, gate_src, *, softmax, dq, q_blk0, k_blk0, v_blk0, gate_blk0, lgam=None, gn=None, bq=256, name):
    S = q.shape[0]
    bq = min(bq, S)
    assert bq % CHUNK == 0

    def body(*refs):
        if softmax:
            q_ref, k_ref, v_ref, g_ref, o_ref, y_ref = refs
            lg = None
        else:
            lg_ref, q_ref, k_ref, v_ref, g_ref, gn_ref, o_ref, y_ref = refs
            lg = lg_ref[pl.program_id(0)]

        def block(qi):
            keys = pl.ds(0, (qi + 1) * bq)
            p, _ = _scores(q_ref[...], k_ref[keys, :], qi, bq, lg, softmax)
            o = jnp.dot(p.astype(bf16), v_ref[keys, :].astype(bf16), preferred_element_type=f32)
            o_ref[...] = o
            if softmax:
                z = o
            else:
                oc = o - jnp.mean(o, axis=-1, keepdims=True)
                z = oc * lax.rsqrt(jnp.mean(oc * oc, axis=-1, keepdims=True) + EPS) * gn_ref[...]
            y_ref[...] = (z * _silu(g_ref[...])).astype(bf16)

        _per_query_block(S // bq, block)

    q_spec = pl.BlockSpec((bq, dq), lambda h, i: (i, q_blk0 + h))
    k_spec = pl.BlockSpec((S, dq), lambda h, i: (0, k_blk0 + h))
    v_spec = pl.BlockSpec((S, DH), lambda h, i: (0, v_blk0 + h))
    g_spec = pl.BlockSpec((bq, DH), lambda h, i: (i, gate_blk0 + h))
    o_spec = pl.BlockSpec((bq, DH), lambda h, i: (i, h))
    in_specs, args = [q_spec, k_spec, v_spec, g_spec], [q, k, v, gate_src]
    if not softmax:
        in_specs = [pl.BlockSpec(memory_space=pltpu.SMEM)] + in_specs + [pl.BlockSpec((1, DH), lambda h, i: (0, h))]
        args = [lgam] + args + [gn]
    return pl.pallas_call(
        body, name=name, grid=(HEADS, S // bq), in_specs=in_specs, out_specs=[o_spec, o_spec],
        out_shape=[SDS((S, BW), f32), SDS((S, BW), bf16)], compiler_params=_cp(48))(*args)


def attn_bwd(q, k, v, do, o, *, softmax, dq, q_blk0, k_blk0, v_blk0, lgam=None, bq=256, name):
    S = q.shape[0]
    bq = min(bq, S)

    def body(*refs):
        if softmax:
            q_ref, k_ref, v_ref, do_ref, o_ref, dq_ref, dk_ref, dv_ref = refs
            lg = None
        else:
            lg_ref, q_ref, k_ref, v_ref, do_ref, dq_ref, dk_ref, dv_ref = refs
            lg = lg_ref[pl.program_id(0)]

        def block(qi):
            keys = pl.ds(0, (qi + 1) * bq)
            if qi == 0:
                dk_ref[...] = jnp.zeros_like(dk_ref)
                dv_ref[...] = jnp.zeros_like(dv_ref)
            qv, kv, dov = q_ref[...], k_ref[keys, :], do_ref[...]
            p, w = _scores(qv, kv, qi, bq, lg, softmax)
            dp = lax.dot_general(dov, v_ref[keys, :].astype(bf16), _DIMS["nt"], preferred_element_type=f32)
            if softmax:
                delta = jnp.sum(dov.astype(f32) * o_ref[...], axis=-1, keepdims=True)
                ds = p * (dp - delta)
            else:
                ds = dp * w
            dsb = ds.astype(bf16)
            dq_ref[...] = jnp.dot(dsb, kv, preferred_element_type=f32)
            dv_ref[keys, :] += lax.dot_general(p.astype(bf16), dov, _DIMS["tn"], preferred_element_type=f32)
            dk_ref[keys, :] += lax.dot_general(dsb, qv, _DIMS["tn"], preferred_element_type=f32)

        _per_query_block(S // bq, block)

    q_spec = pl.BlockSpec((bq, dq), lambda h, i: (i, q_blk0 + h))
    k_spec = pl.BlockSpec((S, dq), lambda h, i: (0, k_blk0 + h))
    v_spec = pl.BlockSpec((S, DH), lambda h, i: (0, v_blk0 + h))
    o_spec = pl.BlockSpec((bq, DH), lambda h, i: (i, h))
    in_specs, args = [q_spec, k_spec, v_spec, o_spec], [q, k, v, do]
    if softmax:
        in_specs, args = in_specs + [o_spec], args + [o]
    else:
        in_specs, args = [pl.BlockSpec(memory_space=pltpu.SMEM)] + in_specs, [lgam] + args
    return pl.pallas_call(
        body, name=name, grid=(HEADS, S // bq), in_specs=in_specs,
        out_specs=[pl.BlockSpec((bq, dq), lambda h, i: (i, h)), pl.BlockSpec((S, dq), lambda h, i: (0, h)),
                   pl.BlockSpec((S, DH), lambda h, i: (0, h))],
        out_shape=[SDS((S, HEADS * dq), f32), SDS((S, HEADS * dq), f32), SDS((S, BW), f32)],
        compiler_params=_cp(52))(*args)


def ret_attn_fwd(proj, tabs, lgam, gn, *, bq=256):
    S = proj.shape[0]
    bq = min(bq, S)
    assert bq % CHUNK == 0

    def body(lg_ref, q_ref, k_ref, v_ref, g_ref, gn_ref, cq_ref, sq_ref, ck_ref, sk_ref, o_ref, y_ref, k_s):
        lg = lg_ref[pl.program_id(0)]

        def block(qi):
            if qi == 0:
                k_s[...] = _rope_ret(k_ref[...], ck_ref[...], sk_ref[...]).astype(bf16)
            keys = pl.ds(0, (qi + 1) * bq)
            qv = (_rope_ret(q_ref[...], cq_ref[...], sq_ref[...]) * RET_SCALE).astype(bf16)
            p, _ = _scores(qv, k_s[keys, :], qi, bq, lg, False)
            o = jnp.dot(p.astype(bf16), v_ref[keys, :].astype(bf16), preferred_element_type=f32)
            o_ref[...] = o
            oc = o - jnp.mean(o, axis=-1, keepdims=True)
            z = oc * lax.rsqrt(jnp.mean(oc * oc, axis=-1, keepdims=True) + EPS) * gn_ref[...]
            y_ref[...] = (z * _silu(g_ref[...])).astype(bf16)

        _per_query_block(S // bq, block)

    row = lambda b0: pl.BlockSpec((bq, DH), lambda h, i: (i, b0 + h))
    full = lambda b0: pl.BlockSpec((S, DH), lambda h, i: (0, b0 + h))
    tq, tk = pl.BlockSpec((bq, DH), lambda h, i: (i, 0)), pl.BlockSpec((S, DH), lambda h, i: (0, 0))
    o_spec = pl.BlockSpec((bq, DH), lambda h, i: (i, h))
    return pl.pallas_call(
        body, name="ret_attn_fwd", grid=(HEADS, S // bq),
        in_specs=[pl.BlockSpec(memory_space=pltpu.SMEM), row(Q0), full(K0), full(V0), row(RG0),
                  pl.BlockSpec((1, DH), lambda h, i: (0, h)), tq, tq, tk, tk],
        out_specs=[o_spec, o_spec], out_shape=[SDS((S, BW), f32), SDS((S, BW), bf16)],
        scratch_shapes=[pltpu.VMEM((S, DH), bf16)], compiler_params=_cp(48),
    )(lgam, proj, proj, proj, proj, gn, tabs[0], tabs[1], tabs[0], tabs[1])


def ret_attn_bwd(proj, do, tabs, lgam, *, bq=256):
    S = proj.shape[0]
    bq = min(bq, S)
    nq = S // bq

    def body(lg_ref, q_ref, k_ref, v_ref, do_ref, cq_ref, sq_ref, ck_ref, sk_ref, dq_ref, dk_ref, dv_ref, k_s, dk_s, dv_s):
        lg = lg_ref[pl.program_id(0)]

        def block(qi):
            if qi == 0:
                k_s[...] = _rope_ret(k_ref[...], ck_ref[...], sk_ref[...]).astype(bf16)
                dk_s[...] = jnp.zeros_like(dk_s)
                dv_s[...] = jnp.zeros_like(dv_s)
            keys = pl.ds(0, (qi + 1) * bq)
            cq, sq = cq_ref[...], sq_ref[...]
            qv = (_rope_ret(q_ref[...], cq, sq) * RET_SCALE).astype(bf16)
            kv, dov = k_s[keys, :], do_ref[...]
            p, w = _scores(qv, kv, qi, bq, lg, False)
            dp = lax.dot_general(dov, v_ref[keys, :].astype(bf16), _DIMS["nt"], preferred_element_type=f32)
            dsb = (dp * w).astype(bf16)
            dq_ref[...] = _rope_ret_t(jnp.dot(dsb, kv, preferred_element_type=f32) * RET_SCALE, cq, sq).astype(bf16)
            dv_s[keys, :] += lax.dot_general(p.astype(bf16), dov, _DIMS["tn"], preferred_element_type=f32)
            dk_s[keys, :] += lax.dot_general(dsb, qv, _DIMS["tn"], preferred_element_type=f32)
            if qi == nq - 1:
                dk_ref[...] = _rope_ret_t(dk_s[...], ck_ref[...], sk_ref[...]).astype(bf16)
                dv_ref[...] = dv_s[...].astype(bf16)

        _per_query_block(nq, block)

    row = lambda b0: pl.BlockSpec((bq, DH), lambda h, i: (i, b0 + h))
    full = lambda b0: pl.BlockSpec((S, DH), lambda h, i: (0, b0 + h))
    tq, tk = pl.BlockSpec((bq, DH), lambda h, i: (i, 0)), pl.BlockSpec((S, DH), lambda h, i: (0, 0))
    return pl.pallas_call(
        body, name="ret_attn_bwd", grid=(HEADS, nq),
        in_specs=[pl.BlockSpec(memory_space=pltpu.SMEM), row(Q0), full(K0), full(V0), row(0), tq, tq, tk, tk],
        out_specs=[row(0), full(0), full(0)], out_shape=[SDS((S, BW), bf16)] * 3,
        scratch_shapes=[pltpu.VMEM((S, DH), bf16), pltpu.VMEM((S, DH), f32), pltpu.VMEM((S, DH), f32)],
        compiler_params=_cp(52),
    )(lgam, proj, proj, proj, do, tabs[0], tabs[1], tabs[0], tabs[1])


def ret_post_bwd(dy, o, proj, gn, *, tm=512):
    S = dy.shape[0]
    tm = min(tm, S)

    def body(dy_ref, o_ref, g_ref, gn_ref, do_ref, drg_ref, dgn_ref):
        ov, g, gnv, dyv = o_ref[...], g_ref[...], gn_ref[...], dy_ref[...]
        oc = ov - jnp.mean(ov, axis=-1, keepdims=True)
        rs = lax.rsqrt(jnp.mean(oc * oc, axis=-1, keepdims=True) + EPS)
        oh = oc * rs
        dz = dyv * _silu(g)
        drg_ref[...] = (dyv * (oh * gnv) * _dsilu(g)).astype(bf16)
        doh = dz * gnv
        do_ref[...] = (rs * (doh - jnp.mean(doh, axis=-1, keepdims=True)
                             - oh * jnp.mean(doh * oh, axis=-1, keepdims=True))).astype(bf16)
        _acc(dgn_ref, jnp.sum(dz * oh, axis=0, keepdims=True), pl.program_id(1) == 0)

    blk = pl.BlockSpec((tm, DH), lambda h, i: (i, h))
    vec = pl.BlockSpec((1, DH), lambda h, i: (0, h))
    return pl.pallas_call(
        body, name="ret_post_bwd", grid=(HEADS, S // tm),
        in_specs=[blk, blk, pl.BlockSpec((tm, DH), lambda h, i: (i, RG0 + h)), vec], out_specs=[blk, blk, vec],
        out_shape=[SDS((S, BW), bf16), SDS((S, BW), bf16), SDS((1, BW), f32)], compiler_params=_cp(32))(dy, o, proj, gn)


def mla_post_bwd(dy, o, proj, *, tm=512):
    S = dy.shape[0]
    tm = min(tm, S)

    def body(dy_ref, o_ref, g_ref, do_ref, dg_ref):
        g, dyv = g_ref[...], dy_ref[...]
        do_ref[...] = (dyv * _silu(g)).astype(bf16)
        dg_ref[...] = (dyv * o_ref[...] * _dsilu(g)).astype(bf16)

    blk = pl.BlockSpec((tm, DH), lambda i, h: (i, h))
    return pl.pallas_call(
        body, name="mla_post_bwd", grid=(S // tm, HEADS),
        in_specs=[blk, blk, pl.BlockSpec((tm, DH), lambda i, h: (i, MG0 + h))], out_specs=[blk, blk],
        out_shape=[SDS((S, BW), bf16), SDS((S, BW), bf16)], compiler_params=_cp(32))(dy, o, proj)


def mla_prep_fwd(proj, qnorm, kvnorm, wuq, wukv, tabs, *, tm=256, kv_blk=0):
    S = proj.shape[0]
    tm = min(tm, S)

    def body(mq_ref, mkv_ref, mkr_ref, qn_ref, kvn_ref, wuq_ref, wukv_ref, c_ref, sa_ref, sb_ref, q_ref, k_ref, v_ref):
        c, sa, sb = c_ref[...], sa_ref[...], sb_ref[...]
        mq, mkv = mq_ref[...], mkv_ref[...]
        qn = (mq * lax.rsqrt(jnp.mean(mq * mq, axis=-1, keepdims=True) + EPS) * qn_ref[...]).astype(bf16)
        kvn = (mkv * lax.rsqrt(jnp.mean(mkv * mkv, axis=-1, keepdims=True) + EPS) * kvn_ref[...]).astype(bf16)
        kr = _rope_mla(mkr_ref[...], c, sa, sb).astype(bf16)
        for h in range(HEADS):
            qh = jnp.dot(qn, wuq_ref[h], preferred_element_type=f32)
            q_ref[:, pl.ds(h * QPAD, DH)] = (qh[:, :DH] * MLA_SCALE).astype(bf16)
            q_ref[:, pl.ds(h * QPAD + DH, DH)] = (_rope_mla(qh[:, DH:], c, sa, sb) * MLA_SCALE).astype(bf16)
            kvh = jnp.dot(kvn, wukv_ref[h], preferred_element_type=f32)
            k_ref[:, pl.ds(h * QPAD, DH)] = kvh[:, :DH].astype(bf16)
            k_ref[:, pl.ds(h * QPAD + DH, DH)] = kr
            v_ref[:, pl.ds(h * DH, DH)] = kvh[:, DH:].astype(bf16)

    lat = lambda b: pl.BlockSpec((tm, LORA), lambda i: (i, b))
    tab = pl.BlockSpec((tm, DH), lambda i: (i, 0))
    vec = pl.BlockSpec((1, LORA), lambda i: (0, 0))
    wsp = pl.BlockSpec((HEADS, LORA, QPAD), lambda i: (0, 0, 0))
    wkv = pl.BlockSpec((HEADS, LORA, QPAD), lambda i: (0, 0, kv_blk))
    return pl.pallas_call(
        body, name="mla_prep_fwd", grid=(S // tm,),
        in_specs=[lat(MQ0 // 4), lat(MKV0 // 4), pl.BlockSpec((tm, DH), lambda i: (i, MKR0)), vec, vec, wsp, wkv,
                  tab, tab, tab],
        out_specs=[pl.BlockSpec((tm, HEADS * QPAD), lambda i: (i, 0))] * 2 + [pl.BlockSpec((tm, BW), lambda i: (i, 0))],
        out_shape=[SDS((S, HEADS * QPAD), bf16)] * 2 + [SDS((S, BW), bf16)], compiler_params=_cp(48),
    )(proj, proj, proj, qnorm, kvnorm, wuq, wukv, *tabs)


def mla_prep_bwd(dq256, dk256, dv, proj, qnorm, kvnorm, wuq, wukv, tabs, *, tm=256, kv_blk=0):
    S = proj.shape[0]
    tm = min(tm, S)

    def body(dq_ref, dk_ref, dv_ref, mq_ref, mkv_ref, qn_ref, kvn_ref, wuq_ref, wukv_ref, c_ref, sa_ref, sb_ref,
             dm_ref, dwuq_ref, dwukv_ref, dqn_ref, dkvn_ref):
        first = pl.program_id(0) == 0
        c, sa, sb = c_ref[...], sa_ref[...], sb_ref[...]
        mq, mkv = mq_ref[...], mkv_ref[...]
        rq = lax.rsqrt(jnp.mean(mq * mq, axis=-1, keepdims=True) + EPS)
        rkv = lax.rsqrt(jnp.mean(mkv * mkv, axis=-1, keepdims=True) + EPS)
        mqh, mkvh = mq * rq, mkv * rkv
        qn = (mqh * qn_ref[...]).astype(bf16)
        kvn = (mkvh * kvn_ref[...]).astype(bf16)
        dqn = jnp.zeros((tm, LORA), f32)
        dkvn = jnp.zeros((tm, LORA), f32)
        dkr = jnp.zeros((tm, DH), f32)
        for h in range(HEADS):
            da = dq_ref[:, pl.ds(h * QPAD, DH)] * MLA_SCALE
            db = _rope_mla_t(dq_ref[:, pl.ds(h * QPAD + DH, DH)] * MLA_SCALE, c, sa, sb)
            dqh = jnp.concatenate([da, db], axis=1).astype(bf16)
            dqn += lax.dot_general(dqh, wuq_ref[h], _DIMS["nt"], preferred_element_type=f32)
            _acc(dwuq_ref.at[h], lax.dot_general(qn, dqh, _DIMS["tn"], preferred_element_type=f32), first)
            dkr += dk_ref[:, pl.ds(h * QPAD + DH, DH)]
            dkvh = jnp.concatenate([dk_ref[:, pl.ds(h * QPAD, DH)], dv_ref[:, pl.ds(h * DH, DH)]], axis=1).astype(bf16)
            dkvn += lax.dot_general(dkvh, wukv_ref[h], _DIMS["nt"], preferred_element_type=f32)
            _acc(dwukv_ref.at[h], lax.dot_general(kvn, dkvh, _DIMS["tn"], preferred_element_type=f32), first)
        dmh = dqn * qn_ref[...]
        dm_ref[:, pl.ds(0, LORA)] = (rq * (dmh - mqh * jnp.mean(dmh * mqh, axis=-1, keepdims=True))).astype(bf16)
        dmh = dkvn * kvn_ref[...]
        dm_ref[:, pl.ds(LORA, LORA)] = (rkv * (dmh - mkvh * jnp.mean(dmh * mkvh, axis=-1, keepdims=True))).astype(bf16)
        dm_ref[:, pl.ds(2 * LORA, DH)] = _rope_mla_t(dkr, c, sa, sb).astype(bf16)
        _acc(dqn_ref, jnp.sum(dqn * mqh, axis=0, keepdims=True), first)
        _acc(dkvn_ref, jnp.sum(dkvn * mkvh, axis=0, keepdims=True), first)

    lat = lambda b: pl.BlockSpec((tm, LORA), lambda i: (i, b))
    tab = pl.BlockSpec((tm, DH), lambda i: (i, 0))
    vec = pl.BlockSpec((1, LORA), lambda i: (0, 0))
    wsp = pl.BlockSpec((HEADS, LORA, QPAD), lambda i: (0, 0, 0))
    wkv = pl.BlockSpec((HEADS, LORA, QPAD), lambda i: (0, 0, kv_blk))
    wide = pl.BlockSpec((tm, HEADS * QPAD), lambda i: (i, 0))
    return pl.pallas_call(
        body, name="mla_prep_bwd", grid=(S // tm,),
        in_specs=[wide, wide, pl.BlockSpec((tm, BW), lambda i: (i, 0)), lat(MQ0 // 4), lat(MKV0 // 4), vec, vec, wsp, wkv,
                  tab, tab, tab],
        out_specs=[pl.BlockSpec((tm, 2 * LORA + DH), lambda i: (i, 0)), wsp, wsp, vec, vec],
        out_shape=[SDS((S, 2 * LORA + DH), bf16), SDS((HEADS, LORA, QPAD), f32), SDS((HEADS, LORA, QPAD), f32),
                   SDS((1, LORA), f32), SDS((1, LORA), f32)],
        compiler_params=_cp(52),
    )(dq256, dk256, dv, proj, proj, qnorm, kvnorm, wuq, wukv, *tabs)


SUB = 8


def _scan_tiles(a_s, b_s, out, S, reverse):
    nt = S // SUB
    rows = lax.broadcasted_iota(jnp.int32, (SUB, LANE), 0)

    def tile(t, carry):
        base = pl.multiple_of((nt - 1 - t if reverse else t) * SUB, SUB)
        a, b = a_s[pl.ds(base, SUB), :], b_s[pl.ds(base, SUB), :]
        for d in (1, 2, 4):
            sh = SUB - d if reverse else d
            inside = rows < SUB - d if reverse else rows >= d
            a_n = jnp.where(inside, pltpu.roll(a, sh, 0), 1.0)
            b_n = jnp.where(inside, pltpu.roll(b, sh, 0), 0.0)
            b = a * b_n + b
            a = a * a_n
        res = a * carry + b
        out[pl.ds(base, SUB), :] = res
        edge = res[0:1, :] if reverse else res[SUB - 1:SUB, :]
        return jnp.broadcast_to(edge, (SUB, LANE))

    lax.fori_loop(0, nt, tile, jnp.zeros((SUB, LANE), f32))


def _shift_down(x, n, rows):
    return x if n == 0 else jnp.where(rows >= n, pltpu.roll(x, n, 0), 0.0)


def _shift_up(x, n, rows, S):
    return x if n == 0 else jnp.where(rows < S - n, pltpu.roll(x, S - n, 0), 0.0)


def _lru_gates(xb, cw, cb, wa, ba, wx, bx, lam, rows):
    xc = cb + cw[3:4, :] * xb
    for w in range(3):
        xc = xc + cw[w:w + 1, :] * _shift_down(xb, 3 - w, rows)
    xcb = xc.astype(bf16)
    r = _sigmoid(jnp.dot(xcb, wa, preferred_element_type=f32) + ba)
    i = _sigmoid(jnp.dot(xcb, wx, preferred_element_type=f32) + bx)
    sp = _softplus(-lam)
    la = (-LRU_C * r) * sp
    return xc, xcb, r, i, sp, la, jnp.exp(la)


def _lru_specs(S):
    col = lambda b0: pl.BlockSpec((S, LANE), lambda n: (0, b0 + n))
    vec = pl.BlockSpec((1, LANE), lambda n: (0, n))
    return col, vec, pl.BlockSpec((4, LANE), lambda n: (0, n)), pl.BlockSpec((1, LANE, LANE), lambda n: (n, 0, 0))


def lru_fwd(proj, cw, cb, wa, ba, wx, bx, lam):
    S = proj.shape[0]

    def body(x_ref, g_ref, cw_ref, cb_ref, wa_ref, ba_ref, wx_ref, bx_ref, lam_ref, h_ref, y_ref, a_s, b_s):
        rows = lax.broadcasted_iota(jnp.int32, (S, LANE), 0)
        xc, _, _, i, _, la, a = _lru_gates(x_ref[...], cw_ref[...], cb_ref[...], wa_ref[0].astype(bf16), ba_ref[...],
                                           wx_ref[0].astype(bf16), bx_ref[...], lam_ref[...], rows)
        a_s[...] = a
        b_s[...] = jnp.sqrt(_one_minus_exp(2.0 * la)) * (i * xc)
        _scan_tiles(a_s, b_s, h_ref, S, reverse=False)
        y_ref[...] = (h_ref[...] * _silu(g_ref[...])).astype(bf16)

    col, vec, cws, wsp = _lru_specs(S)
    return pl.pallas_call(
        body, name="lru_fwd", grid=(HEADS,),
        in_specs=[col(LX0), col(LG0), cws, vec, wsp, vec, wsp, vec, vec], out_specs=[col(0), col(0)],
        out_shape=[SDS((S, BW), f32), SDS((S, BW), bf16)],
        scratch_shapes=[pltpu.VMEM((S, LANE), f32), pltpu.VMEM((S, LANE), f32)], compiler_params=_cp(40),
    )(proj, proj, cw, cb, wa, ba, wx, bx, lam)


def lru_bwd(dy, h, proj, cw, cb, wa, ba, wx, bx, lam):
    S = proj.shape[0]

    def body(dy_ref, h_ref, x_ref, g_ref, cw_ref, cb_ref, wa_ref, ba_ref, wx_ref, bx_ref, lam_ref,
             dx_ref, dg_ref, dcw_ref, dcb_ref, dba_ref, dbx_ref, dlam_ref, dwa_ref, dwx_ref, a_s, b_s, l_s):
        rows = lax.broadcasted_iota(jnp.int32, (S, LANE), 0)
        xb, g, hv, dyv, cw, lam = x_ref[...], g_ref[...], h_ref[...], dy_ref[...], cw_ref[...], lam_ref[...]
        wa, wx = wa_ref[0].astype(bf16), wx_ref[0].astype(bf16)
        xc, xcb, r, i, sp, la, a = _lru_gates(xb, cw, cb_ref[...], wa, ba_ref[...], wx, bx_ref[...], lam, rows)
        dg_ref[...] = (dyv * hv * _dsilu(g)).astype(bf16)
        a_s[...] = _shift_up(a, 1, rows, S)
        b_s[...] = dyv * _silu(g)
        _scan_tiles(a_s, b_s, l_s, S, reverse=True)
        lmb = l_s[...]
        gated = i * xc
        sq = jnp.sqrt(_one_minus_exp(2.0 * la))
        dla = lmb * _shift_down(hv, 1, rows) * a - (lmb * gated) * (a * a) / sq
        dgated = lmb * sq
        dzr = (dla * (-LRU_C * sp)) * (r * (1.0 - r))
        dzi = (dgated * xc) * (i * (1.0 - i))
        dzrb, dzib = dzr.astype(bf16), dzi.astype(bf16)
        dxc = (dgated * i + lax.dot_general(dzrb, wa, _DIMS["nt"], preferred_element_type=f32)
               + lax.dot_general(dzib, wx, _DIMS["nt"], preferred_element_type=f32))
        dwa_ref[0] = lax.dot_general(xcb, dzrb, _DIMS["tn"], preferred_element_type=f32)
        dwx_ref[0] = lax.dot_general(xcb, dzib, _DIMS["tn"], preferred_element_type=f32)
        dba_ref[...] = jnp.sum(dzr, axis=0, keepdims=True)
        dbx_ref[...] = jnp.sum(dzi, axis=0, keepdims=True)
        dlam_ref[...] = jnp.sum(dla * (-LRU_C * r), axis=0, keepdims=True) * (-_sigmoid(-lam))
        dcb_ref[...] = jnp.sum(dxc, axis=0, keepdims=True)
        dxb = cw[3:4, :] * dxc
        dcw_ref[3:4, :] = jnp.sum(dxc * xb, axis=0, keepdims=True)
        for w in range(3):
            dxb = dxb + cw[w:w + 1, :] * _shift_up(dxc, 3 - w, rows, S)
            dcw_ref[w:w + 1, :] = jnp.sum(dxc * _shift_down(xb, 3 - w, rows), axis=0, keepdims=True)
        dx_ref[...] = dxb.astype(bf16)

    col, vec, cws, wsp = _lru_specs(S)
    scr = pltpu.VMEM((S, LANE), f32)
    return pl.pallas_call(
        body, name="lru_bwd", grid=(HEADS,),
        in_specs=[col(0), col(0), col(LX0), col(LG0), cws, vec, wsp, vec, wsp, vec, vec],
        out_specs=[col(0), col(0), cws, vec, vec, vec, vec, wsp, wsp],
        out_shape=[SDS((S, BW), bf16), SDS((S, BW), bf16), SDS((4, BW), f32)] + [SDS((1, BW), f32)] * 4
        + [SDS((HEADS, LANE, LANE), f32)] * 2,
        scratch_shapes=[scr, scr, scr], compiler_params=_cp(48),
    )(dy, h, proj, proj, cw, cb, wa, ba, wx, bx, lam)


def loss_head(y, target, *, tm=256):
    S = y.shape[0]

    def body(y_ref, t_ref, l_ref, d_ref):
        err = y_ref[...] - t_ref[...]
        d_ref[...] = err * (1.0 / D)
        part = jnp.sum(jnp.sum(err * err, axis=1, keepdims=True), axis=0, keepdims=True) * (0.5 / D)
        _acc(l_ref, jnp.broadcast_to(part, (1, LANE)), pl.program_id(0) == 0)

    row = pl.BlockSpec((tm, D), lambda i: (i, 0))
    return pl.pallas_call(
        body, name="loss_head", grid=(S // tm,), in_specs=[row, row],
        out_specs=[pl.BlockSpec((1, LANE), lambda i: (0, 0)), row],
        out_shape=[SDS((1, LANE), f32), SDS((S, D), f32)], compiler_params=_cp(32))(y, target)


def _adam_math(w, g, m, v):
    mn = ADAM_B1 * m + (1.0 - ADAM_B1) * g
    vn = ADAM_B2 * v + (1.0 - ADAM_B2) * (g * g)
    m_hat = mn / (1.0 - ADAM_B1 ** ADAM_STEP)
    v_hat = vn / (1.0 - ADAM_B2 ** ADAM_STEP)
    return -ADAM_LR * (m_hat / (jnp.sqrt(v_hat) + ADAM_EPS) + ADAM_WD * w), mn, vn


def _adam_rows(rows, cols):
    for cand in (2048, 1024, 512, 256, 128, 64, 32, 16, 8):
        if rows % cand == 0 and rows > cand and cand * cols <= ADAM_BLOCK_ELEMS:
            return cand
    return rows


def adamw(w, g, m, v):
    shape = w.shape
    cols = shape[-1]
    rows = math.prod(shape[:-1])
    tr = _adam_rows(rows, cols)

    def body(w_ref, g_ref, m_ref, v_ref, d_ref, mo_ref, vo_ref):
        d_ref[...], mo_ref[...], vo_ref[...] = _adam_math(w_ref[...], g_ref[...], m_ref[...], v_ref[...])

    blk = pl.BlockSpec((tr, cols), lambda i: (i, 0))
    flat = [t.reshape(rows, cols) for t in (w, g, m, v)]
    outs = pl.pallas_call(
        body, name="adamw", grid=(rows // tr,), in_specs=[blk] * 4, out_specs=[blk] * 3,
        out_shape=[SDS((rows, cols), f32)] * 3, compiler_params=_cp(48))(*flat)
    return tuple(o.reshape(shape) for o in outs)


ADA_SHARD = 3 * D // NDEV
ROWS16 = 16


def ada_fwd(c_all, ada_w, ada_b_mine):
    def body(c_ref, w_ref, b_ref, o_ref):
        o_ref[0] = jnp.dot(_silu(c_ref[...]).astype(bf16), w_ref[0].astype(bf16), preferred_element_type=f32) + b_ref[0]

    return pl.pallas_call(
        body, name="ada_fwd", grid=(DEPTH,),
        in_specs=[pl.BlockSpec((ROWS16, D), lambda l: (0, 0)), pl.BlockSpec((1, D, ADA_SHARD), lambda l: (l, 0, 0)),
                  pl.BlockSpec((1, 1, ADA_SHARD), lambda l: (l, 0, 0))],
        out_specs=pl.BlockSpec((1, ROWS16, ADA_SHARD), lambda l: (l, 0, 0)),
        out_shape=SDS((DEPTH, ROWS16, ADA_SHARD), f32), compiler_params=_cp(40))(c_all, ada_w, ada_b_mine)


def ada_bwd(c_all, dmod):
    def body(c_ref, d_ref, o_ref):
        o_ref[0] = lax.dot_general(_silu(c_ref[...]).astype(bf16), d_ref[0].astype(bf16), _DIMS["tn"],
                                   preferred_element_type=f32)

    return pl.pallas_call(
        body, name="ada_bwd", grid=(DEPTH,),
        in_specs=[pl.BlockSpec((ROWS16, D), lambda l: (0, 0)), pl.BlockSpec((1, ROWS16, ADA_SHARD), lambda l: (l, 0, 0))],
        out_specs=pl.BlockSpec((1, D, ADA_SHARD), lambda l: (l, 0, 0)),
        out_shape=SDS((DEPTH, D, ADA_SHARD), f32), compiler_params=_cp(40))(c_all, dmod)


def shift_params(me):
    start = SHARD_W * me
    return jnp.stack([start % LANE, (start + GAP) % LANE, jnp.clip(GAP_COL - start, 0, SHARD_W)]).astype(jnp.int32)


def win_pack(wt, sidx, *, after=()):
    def body(s_ref, w_ref, *rest):
        o_ref, scr = rest[len(after):]
        s1, s2, gi = s_ref[0], s_ref[1], s_ref[2]
        scr[pl.ds(SHARD_W, WIN - SHARD_W), :] = jnp.zeros((WIN - SHARD_W, LANE), f32)
        scr[pl.ds(0, SHARD_W), :] = w_ref[0]
        v = scr[...].T
        j = lax.broadcasted_iota(jnp.int32, v.shape, 1)
        o_ref[0] = jnp.where(j - s1 < gi, pltpu.roll(v, s1, 1),
                             jnp.where(j - s2 >= gi, pltpu.roll(v, s2, 1), 0.0)).astype(bf16)

    return pl.pallas_call(
        body, name="win_pack", grid=(DEPTH, D // LANE),
        in_specs=[pl.BlockSpec(memory_space=pltpu.SMEM), pl.BlockSpec((1, SHARD_W, LANE), lambda l, i: (l, 0, i))]
        + [pl.BlockSpec(memory_space=pl.ANY)] * len(after),
        out_specs=pl.BlockSpec((1, LANE, WIN), lambda l, i: (l, i, 0)),
        out_shape=SDS((DEPTH, D, WIN), bf16), scratch_shapes=[pltpu.VMEM((WIN, LANE), f32)],
        compiler_params=_cp(32))(sidx, wt, *after)


def win_assemble(g, *, after=()):
    own = WIN_STRIDE * LANE
    tail = NP - NDEV * own
    assert tail == 2 * LANE

    def body(a_ref, b_ref, *rest):
        o_ref = rest[len(after)]
        o_ref[...] = a_ref[0]

        @pl.when(pl.program_id(0) > 0)
        def _():
            o_ref[:, pl.ds(0, LANE)] = a_ref[0, :, pl.ds(0, LANE)] + b_ref[0]

    main = pl.pallas_call(
        body, name="win_assemble", grid=(NDEV,),
        in_specs=[pl.BlockSpec((1, D, own), lambda k: (k, 0, 0)),
                  pl.BlockSpec((1, D, LANE), lambda k: (jnp.maximum(k - 1, 0), 0, WIN_BLKS - 1))]
        + [pl.BlockSpec(memory_space=pl.ANY)] * len(after),
        out_specs=pl.BlockSpec((D, own), lambda k: (0, k)),
        out_shape=SDS((D, NP), bf16), compiler_params=_cp(48))(g, g, *after)

    def tail_body(_, b_ref, o_ref):
        o_ref[:, pl.ds(0, LANE)] = b_ref[0]
        o_ref[:, pl.ds(LANE, LANE)] = jnp.zeros((D, LANE), bf16)

    return pl.pallas_call(
        tail_body, name="win_assemble_tail", grid=(1,),
        in_specs=[pl.BlockSpec(memory_space=pl.ANY), pl.BlockSpec((1, D, LANE), lambda t: (NDEV - 1, 0, WIN_BLKS - 1))],
        out_specs=pl.BlockSpec((D, tail), lambda t: (0, NDEV * own // tail)),
        out_shape=SDS((D, NP), bf16), input_output_aliases={0: 0}, compiler_params=_cp(32))(main, g)


def reduce_adamw(stag, w, m, v, l, prev, name, *, sidx=None, tr=128):
    Cs = stag.shape[2]
    n_prev = 0 if prev is None else 4
    n_lead = 1 if sidx is not None else 0

    def body(*refs):
        refs = list(refs)
        s_ref = refs.pop(0) if sidx is not None else None
        g_ref, w_ref, m_ref, v_ref = refs[:4]
        rest = refs[4 + n_prev:]
        go_ref, d_ref, mo_ref, vo_ref = rest[:4]
        tot = g_ref[0].astype(f32)
        for d in range(1, NDEV):
            tot = tot + g_ref[d].astype(f32)
        if sidx is not None:
            scr = rest[4]
            s1, s2, gi = s_ref[0], s_ref[1], s_ref[2]
            i = lax.broadcasted_iota(jnp.int32, tot.shape, 1)
            scr[...] = jnp.where(i < gi, pltpu.roll(tot, WIN - s1, 1), pltpu.roll(tot, WIN - s2, 1)).T
            tot = scr[pl.ds(0, SHARD_W), :]
        go_ref[0] = tot
        d_ref[0], mo_ref[0], vo_ref[0] = _adam_math(w_ref[0], tot, m_ref[0], v_ref[0])

    if sidx is not None:
        blk3 = pl.BlockSpec((1, SHARD_W, tr), lambda i: (l, 0, i))
        steps = w.shape[2] // tr
    else:
        blk3 = pl.BlockSpec((1, tr, w.shape[2]), lambda i: (l, i, 0))
        steps = w.shape[1] // tr
    in_specs = ([pl.BlockSpec(memory_space=pltpu.SMEM)] * n_lead + [pl.BlockSpec((NDEV, tr, Cs), lambda i: (0, i, 0)), blk3, blk3, blk3]
                + [pl.BlockSpec(memory_space=pl.ANY)] * n_prev)
    args = ([sidx] if sidx is not None else []) + [stag, w, m, v] + list(prev or ())
    return pl.pallas_call(
        body, name=name, grid=(steps,), in_specs=in_specs, out_specs=[blk3] * 4, out_shape=[SDS(w.shape, f32)] * 4,
        scratch_shapes=[pltpu.VMEM((Cs, tr), f32)] if sidx is not None else [],
        input_output_aliases={n_lead + 4 + k: k for k in range(n_prev)}, compiler_params=_cp(48),
    )(*args)


def cast_pad(w, cols_out, name):
    L, R, C = w.shape

    def body(w_ref, o_ref, *scr):
        if cols_out == C:
            o_ref[0] = w_ref[0].astype(bf16)
        else:
            scr[0][...] = jnp.zeros_like(scr[0])
            scr[0][:, pl.ds(0, C)] = w_ref[0]
            o_ref[0] = scr[0][...].astype(bf16)

    return pl.pallas_call(
        body, name=name, grid=(L,), in_specs=[pl.BlockSpec((1, R, C), lambda l: (l, 0, 0))],
        out_specs=pl.BlockSpec((1, R, cols_out), lambda l: (l, 0, 0)), out_shape=SDS((L, R, cols_out), bf16),
        scratch_shapes=[] if cols_out == C else [pltpu.VMEM((R, cols_out), f32)], compiler_params=_cp(32))(w)


def pack_qkv(w_uq, w_ukv):
    L = w_uq.shape[0]

    def body(q_ref, kv_ref, o_ref, scr):
        scr[...] = jnp.zeros_like(scr)
        scr[:, pl.ds(0, WUQ_COLS)] = q_ref[0]
        o_ref[0, :, pl.ds(0, QPAD)] = scr[...].astype(bf16)
        o_ref[0, :, pl.ds(QPAD, QPAD)] = kv_ref[0].astype(bf16)

    return pl.pallas_call(
        body, name="pack_qkv", grid=(L,),
        in_specs=[pl.BlockSpec((1, LORA, WUQ_COLS), lambda l: (l, 0, 0)), pl.BlockSpec((1, LORA, QPAD), lambda l: (l, 0, 0))],
        out_specs=pl.BlockSpec((1, LORA, 2 * QPAD), lambda l: (l, 0, 0)), out_shape=SDS((L, LORA, 2 * QPAD), bf16),
        scratch_shapes=[pltpu.VMEM((LORA, QPAD), f32)], compiler_params=_cp(32))(w_uq, w_ukv)


def sum_parts(stag, cols_out, name, *, tr=None):
    _, R, C = stag.shape
    tr = R if tr is None else tr

    def body(g_ref, o_ref, *scr):
        tot = g_ref[0].astype(f32)
        for d in range(1, NDEV):
            tot = tot + g_ref[d].astype(f32)
        if cols_out == C:
            o_ref[...] = tot
        else:
            scr[0][...] = tot
            o_ref[...] = scr[0][:, pl.ds(0, cols_out)]

    return pl.pallas_call(
        body, name=name, grid=(R // tr,), in_specs=[pl.BlockSpec((NDEV, tr, C), lambda i: (0, i, 0))],
        out_specs=pl.BlockSpec((tr, cols_out), lambda i: (i, 0)), out_shape=SDS((R, cols_out), f32),
        scratch_shapes=[] if cols_out == C else [pltpu.VMEM((tr, C), f32)], compiler_params=_cp(40))(stag)


MESH_ID = pl.DeviceIdType.MESH
HBM_SPEC = pl.BlockSpec(memory_space=pltpu.HBM)


def _place():
    return lax.axis_index("x"), lax.axis_index("y"), lax.axis_index("c")


def all_gather(arrs, name):
    n = len(arrs)

    def body(*refs):
        ins, outs = refs[:n], refs[n:2 * n]
        send_sems, recv_sems, local_sems = refs[2 * n:]
        x, y, c = _place()
        me, sibling = (x, y, c), (x, y, 1 - c)
        chips = [(1 - x, y), (x, 1 - y), (1 - x, 1 - y)]

        def copy(a, k, block, to, src=None):
            slot = outs[a].at[4 * block[0] + 2 * block[1] + block[2]]
            return pltpu.make_async_remote_copy(
                src_ref=slot if src is None else src, dst_ref=slot, send_sem=send_sems.at[7 * a + k],
                recv_sem=recv_sems.at[7 * a + k], device_id=to, device_id_type=MESH_ID)

        mine = [pltpu.make_async_copy(ins[a], outs[a].at[4 * x + 2 * y + c], local_sems.at[a]) for a in range(n)]
        for cp in mine:
            cp.start()
        first = []
        for a in range(n):
            first.append(copy(a, 0, me, sibling, src=ins[a]))
            first += [copy(a, 1 + j, me, (*chip, c), src=ins[a]) for j, chip in enumerate(chips)]
        for cp in first:
            cp.start()
        passed = []
        for j, chip in enumerate(chips):
            for a in range(n):
                copy(a, 1 + j, (*chip, c), me).wait_recv()
                cp = copy(a, 4 + j, (*chip, c), sibling)
                cp.start()
                passed.append(cp)
        for a in range(n):
            copy(a, 0, sibling, me).wait_recv()
        for j, chip in enumerate(chips):
            for a in range(n):
                copy(a, 4 + j, (*chip, 1 - c), me).wait_recv()
        for cp in first + passed:
            cp.wait_send()
        for cp in mine:
            cp.wait()

    return pl.pallas_call(
        body, name=name, in_specs=[HBM_SPEC] * n, out_specs=[HBM_SPEC] * n,
        out_shape=[SDS((NDEV,) + a.shape, a.dtype) for a in arrs],
        scratch_shapes=[pltpu.SemaphoreType.DMA((7 * n,)), pltpu.SemaphoreType.DMA((7 * n,)),
                        pltpu.SemaphoreType.DMA((n,))],
    )(*arrs)


AG_COLLECTIVE_ID = 0
RS_COLLECTIVE_ID = 1


def _everyone_else(x, y, c):
    return [(x ^ (r >> 2), y ^ ((r >> 1) & 1), c ^ (r & 1)) for r in range(1, NDEV)]


def _rendezvous(sem, peers):
    for peer in peers:
        pl.semaphore_signal(sem, inc=1, device_id=peer, device_id_type=MESH_ID)
    pl.semaphore_wait(sem, NDEV - 1)


def _sequencer_call(body, arrs, out_types, name, collective_id):
    n = len(arrs)
    return pl.kernel(
        body, name=name, out_type=out_types, mesh=plsc.ScalarSubcoreMesh(axis_name="sequencer", num_cores=1),
        scratch_types=[pltpu.SemaphoreType.DMA((7 * n,)), pltpu.SemaphoreType.DMA((7 * n,)),
                       pltpu.SemaphoreType.DMA((n,)), pltpu.SemaphoreType.REGULAR],
        compiler_params=pltpu.CompilerParams(collective_id=collective_id),
    )(*arrs)


def seq_all_gather(arrs, name):
    n = len(arrs)

    def body(*refs):
        ins, outs = refs[:n], refs[n:2 * n]
        send_sems, recv_sems, local_sems, exit_sem = refs[2 * n:]
        x, y, c = _place()
        peers = _everyone_else(x, y, c)
        _rendezvous(pltpu.get_barrier_semaphore(), peers)
        me, sibling = (x, y, c), (x, y, 1 - c)
        chips = [(1 - x, y), (x, 1 - y), (1 - x, 1 - y)]

        def copy(a, k, block, to, src=None):
            slot = outs[a].at[4 * block[0] + 2 * block[1] + block[2]]
            return pltpu.make_async_remote_copy(
                src_ref=slot if src is None else src, dst_ref=slot, send_sem=send_sems.at[7 * a + k],
                recv_sem=recv_sems.at[7 * a + k], device_id=to, device_id_type=MESH_ID)

        mine = [pltpu.make_async_copy(ins[a], outs[a].at[4 * x + 2 * y + c], local_sems.at[a]) for a in range(n)]
        for cp in mine:
            cp.start()
        first = []
        for a in range(n):
            first.append(copy(a, 0, me, sibling, src=ins[a]))
            first += [copy(a, 1 + j, me, (*chip, c), src=ins[a]) for j, chip in enumerate(chips)]
        for cp in first:
            cp.start()
        passed = []
        for j, chip in enumerate(chips):
            for a in range(n):
                copy(a, 1 + j, (*chip, c), me).wait_recv()
                cp = copy(a, 4 + j, (*chip, c), sibling)
                cp.start()
                passed.append(cp)
        for a in range(n):
            copy(a, 0, sibling, me).wait_recv()
        for j, chip in enumerate(chips):
            for a in range(n):
                copy(a, 4 + j, (*chip, 1 - c), me).wait_recv()
        for cp in first + passed:
            cp.wait_send()
        for cp in mine:
            cp.wait()
        _rendezvous(exit_sem, peers)

    return _sequencer_call(body, arrs, [SDS((NDEV,) + a.shape, a.dtype) for a in arrs], name, AG_COLLECTIVE_ID)


def seq_reduce_scatter_parts(arrs, pick, shapes, name):
    n = len(arrs)

    def body(*refs):
        ins, outs = refs[:n], refs[n:2 * n]
        send_sems, recv_sems, local_sems, exit_sem = refs[2 * n:]
        x, y, c = _place()
        peers = _everyone_else(x, y, c)
        _rendezvous(pltpu.get_barrier_semaphore(), peers)
        me = 4 * x + 2 * y + c
        mine = [pltpu.make_async_copy(pick[a](ins[a], me), outs[a].at[me], local_sems.at[a]) for a in range(n)]
        for cp in mine:
            cp.start()
        sent = []
        for r, peer in enumerate(peers):
            pid = 4 * peer[0] + 2 * peer[1] + peer[2]
            for a in range(n):
                cp = pltpu.make_async_remote_copy(
                    src_ref=pick[a](ins[a], pid), dst_ref=outs[a].at[me], send_sem=send_sems.at[7 * a + r],
                    recv_sem=recv_sems.at[7 * a + r], device_id=peer, device_id_type=MESH_ID)
                cp.start()
                sent.append((cp, a, r, pid))
        for cp, a, r, pid in sent:
            pltpu.make_async_remote_copy(
                src_ref=pick[a](ins[a], pid), dst_ref=outs[a].at[pid], send_sem=send_sems.at[7 * a + r],
                recv_sem=recv_sems.at[7 * a + r], device_id=(x, y, c), device_id_type=MESH_ID).wait_recv()
        for cp, _, _, _ in sent:
            cp.wait_send()
        for cp in mine:
            cp.wait()
        _rendezvous(exit_sem, peers)

    return _sequencer_call(body, arrs, [SDS((NDEV,) + tuple(s), a.dtype) for s, a in zip(shapes, arrs)], name,
                           RS_COLLECTIVE_ID)


WEIGHTS = ("ada_w", "ada_b", "norm_pre", "norm_post", "w_in", "ret_gn", "lru_conv_w", "lru_conv_b", "lru_wa", "lru_ba",
           "lru_wx", "lru_bx", "lru_lambda", "mla_q_norm", "mla_w_uq", "mla_kv_norm", "mla_w_ukv", "w_branch", "w_out")
SMALL = ("norm_pre", "norm_post", "ret_gn", "lru_conv_w", "lru_conv_b", "lru_ba", "lru_bx", "lru_lambda", "mla_q_norm",
         "mla_kv_norm")
QKV_ROWS = LORA
QKV_BLOCK = (LORA + LANE, 2 * QPAD)
BR_ROWS = 3 * BW // NDEV
OUT_ROWS = D // NDEV
WUQ_COLS = 192


def _row(v):
    return v.reshape(1, -1)


def layer_fwd(xl, mod, p, wts, tabs, lgam):
    S = xl.shape[0]
    tm = min(S, 2048)
    sh, sc, rg = _row(mod[:D]), _row(mod[D:2 * D]), _row(mod[2 * D:])
    ret_tabs, mla_tabs = tabs
    h = pre_fwd(xl, _row(p["norm_pre"]), sc, sh)
    proj = matmul(h, wts["w_in"], dims="nn", M=S, N=NP, K=D, tm=tm, tn=768, tk=D, out_dtype=f32, name="mm_in")
    o_ret, y_ret = ret_attn_fwd(proj, ret_tabs, lgam, _row(p["ret_gn"]))
    h_lru, y_lru = lru_fwd(proj, p["conv_w"], _row(p["lru_conv_b"]), p["lru_wa"], _row(p["lru_ba"]), p["lru_wx"],
                           _row(p["lru_bx"]), _row(p["lru_lambda"]))
    q256, k256, vm = mla_prep_fwd(proj, _row(p["mla_q_norm"]), _row(p["mla_kv_norm"]), wts["w_qkv"], wts["w_qkv"], mla_tabs,
                                  kv_blk=1)
    o_mla, y_mla = attn_fwd(q256, k256, vm, proj, softmax=True, dq=QPAD, q_blk0=0, k_blk0=0, v_blk0=0, gate_blk0=MG0,
                            name="mla_attn_fwd")
    ys = (y_ret, y_lru, y_mla)
    us = [matmul(ys[b], wts["w_branch"], dims="nn", M=S, N=D, K=BW, tm=min(S, 1024), tn=1024, tk=BW,
                 out_dtype=bf16, name="mm_branch", b_blk0=(b, 0)) for b in range(3)]
    merged = gate_fwd(proj, us)
    y, x_next = out_fwd(merged, wts["w_out"], xl, rg, _row(p["norm_post"]))
    saved = dict(x=xl, h=h, proj=proj, o_ret=o_ret, h_lru=h_lru, q256=q256, k256=k256, vm=vm, o_mla=o_mla,
                 ys=ys, us=us, merged=merged, y=y, sc=sc, rg=rg)
    return x_next, saved


def layer_bwd(dx, sv, p, wts, tabs, lgam):
    S = dx.shape[0]
    tm = min(S, 2048)
    ret_tabs, mla_tabs = tabs
    proj = sv["proj"]
    dy, d_rg, d_gpost = out_bwd(dx, sv["y"], sv["rg"], _row(p["norm_post"]))
    dmerged = matmul(dy, wts["w_out"], dims="nt", M=S, N=D, K=D, tm=min(S, 1024), tn=1024, tk=D, out_dtype=f32, name="mm_dmerged")
    dw_out = matmul(sv["merged"], dy, dims="tn", M=D, N=D, K=S, tm=1024, tn=1024, tk=min(S, 1024), out_dtype=bf16, name="mm_dwout")
    du, dml = gate_bwd(dmerged, proj, sv["us"])
    tmb, tkb = min(S, 1024), min(S, 1024)
    dys = [matmul(du[b], wts["w_branch"], dims="nt", M=S, N=BW, K=D, tm=tmb, tn=BW, tk=D, out_dtype=f32, name="mm_dybranch",
                  b_blk0=(b, 0)) for b in range(3)]
    dw_branch = jnp.concatenate(
        [matmul(sv["ys"][b], du[b], dims="tn", M=BW, N=D, K=S, tm=BW, tn=1024, tk=tkb, out_dtype=bf16, name="mm_dwbranch")
         for b in range(3)], axis=0)
    do, d_rgate, d_gn = ret_post_bwd(dys[0], sv["o_ret"], proj, _row(p["ret_gn"]))
    d_q, d_k, d_v = ret_attn_bwd(proj, do, ret_tabs, lgam)
    d_lx, d_lg, d_cw, d_cb, d_ba, d_bx, d_lam, d_wa, d_wx = lru_bwd(
        dys[1], sv["h_lru"], proj, p["conv_w"], _row(p["lru_conv_b"]), p["lru_wa"], _row(p["lru_ba"]), p["lru_wx"],
        _row(p["lru_bx"]), _row(p["lru_lambda"]))
    do, d_mg = mla_post_bwd(dys[2], sv["o_mla"], proj)
    dq256, dk256, dvm = attn_bwd(sv["q256"], sv["k256"], sv["vm"], do, sv["o_mla"], softmax=True, dq=QPAD, q_blk0=0,
                                 k_blk0=0, v_blk0=0, name="mla_attn_bwd")
    d_lat, dw_uq, dw_ukv, d_qn, d_kvn = mla_prep_bwd(dq256, dk256, dvm, proj, _row(p["mla_q_norm"]), _row(p["mla_kv_norm"]),
                                                       wts["w_qkv"], wts["w_qkv"], mla_tabs, kv_blk=1)
    dproj = jnp.concatenate([d_q, d_k, d_v, d_rgate, d_lx, d_lg, d_lat, d_mg, *dml, jnp.zeros((S, LANE), bf16)], axis=1)
    dh = matmul(dproj, wts["w_in"], dims="nt", M=S, N=D, K=NP, tm=min(S, 1024), tn=1024, tk=NP // 6, out_dtype=f32, name="mm_dh")
    dw_in = matmul(sv["h"], dproj, dims="tn", M=D, N=NP, K=S, tm=D, tn=768, tk=tm, out_dtype=bf16, name="mm_dwin")
    dw_qkv = jnp.concatenate([jnp.concatenate([dw_uq, dw_ukv], axis=2),
                              jnp.concatenate([d_wa, d_wx, jnp.zeros((HEADS, LANE, QPAD), f32)], axis=2)], axis=1).astype(bf16)
    big = dict(w_in=dw_in, w_branch=dw_branch, w_out=dw_out, w_qkv=dw_qkv)
    dxl, d_sh, d_sc, d_gpre = pre_bwd(dh, sv["x"], _row(p["norm_pre"]), sv["sc"], dx, after=tuple(big.values()))
    dmod = jnp.concatenate([d_sh, d_sc, d_rg], axis=1).reshape(-1)
    small = dict(norm_pre=d_gpre, norm_post=d_gpost, ret_gn=d_gn, lru_conv_w=d_cw, lru_conv_b=d_cb, lru_ba=d_ba,
                 lru_bx=d_bx, lru_lambda=d_lam, mla_q_norm=d_qn, mla_kv_norm=d_kvn)
    return dxl, dmod, big, small


def exchange_grads(big, l):
    picks = [lambda r, d: r.at[:, pl.ds(pl.multiple_of(d * (WIN_STRIDE * LANE), LANE), WIN)],
             lambda r, d: r.at[pl.ds(pl.multiple_of(d * BR_ROWS, 8), BR_ROWS), :],
             lambda r, d: r.at[pl.ds(pl.multiple_of(d * OUT_ROWS, 8), OUT_ROWS), :],
             lambda r, d: r.at[d]]
    shapes = [(D, WIN), (BR_ROWS, D), (OUT_ROWS, D), QKV_BLOCK]
    arrs = [big["w_in"], big["w_branch"], big["w_out"], big["w_qkv"]]
    return seq_reduce_scatter_parts(arrs, picks, shapes, f"rs_grads_{l}")


def kernel(x, c, positions, ada_w, ada_b, norm_pre, norm_post, w_in, ret_gn, lru_conv_w, lru_conv_b, lru_wa, lru_ba, lru_wx, lru_bx, lru_lambda, mla_q_norm, mla_w_uq, mla_kv_norm, mla_w_ukv, w_branch, w_out, loss_target, m_ada_w, m_ada_b, m_norm_pre, m_norm_post, m_w_in, m_ret_gn, m_lru_conv_w, m_lru_conv_b, m_lru_wa, m_lru_ba, m_lru_wx, m_lru_bx, m_lru_lambda, m_mla_q_norm, m_mla_w_uq, m_mla_kv_norm, m_mla_w_ukv, m_w_branch, m_w_out, v_ada_w, v_ada_b, v_norm_pre, v_norm_post, v_w_in, v_ret_gn, v_lru_conv_w, v_lru_conv_b, v_lru_wa, v_lru_ba, v_lru_wx, v_lru_bx, v_lru_lambda, v_mla_q_norm, v_mla_w_uq, v_mla_kv_norm, v_mla_w_ukv, v_w_branch, v_w_out):
    given = dict(locals())
    xi, yi, ci = _place()
    me = 4 * xi + 2 * yi + ci
    S = x.shape[1]
    sidx = shift_params(me)
    lgam = jnp.asarray(np.log1p(-np.exp2(-5.0 - np.arange(HEADS))), f32)
    tabs = rope_tables(positions[0])

    (g_small,) = all_gather([jnp.concatenate([c.reshape(16, LANE), lru_conv_w.reshape(16, LANE)], axis=0)], "ag_small")
    c16 = jnp.concatenate([g_small[:, :16].reshape(NDEV, D), jnp.zeros((ROWS16 - NDEV, D), f32)], axis=0)
    conv_w_all = g_small[:, 16:].reshape(NDEV, DEPTH, 4, LANE).transpose(1, 2, 0, 3).reshape(DEPTH, 4, BW)
    ada_b_mine = lax.dynamic_slice_in_dim(ada_b, me * ADA_SHARD, ADA_SHARD, axis=1).reshape(DEPTH, 1, ADA_SHARD)
    (g_mod,) = all_gather([ada_fwd(c16, ada_w, ada_b_mine)[:, :NDEV]], "ag_mod")
    mods = lax.dynamic_index_in_dim(g_mod, me, axis=2, keepdims=False).transpose(1, 0, 2).reshape(DEPTH, 3 * D)
    w_in_t = {n: jnp.swapaxes(given[n], 1, 2) for n in ("w_in", "m_w_in", "v_w_in")}
    packed = (win_pack(w_in_t["w_in"], sidx, after=(g_mod,)), cast_pad(w_branch, D, "pack_wbranch"), cast_pad(w_out, D, "pack_wout"),
              pack_qkv(mla_w_uq, mla_w_ukv))
    gathered = [seq_all_gather([t[l] for t in packed], f"ag_weights_{l}") for l in range(DEPTH)]

    params, wts = [], []
    for l in range(DEPTH):
        p = {n: given[n][l] for n in SMALL + ("lru_wa", "lru_wx") if n != "lru_conv_w"}
        p["conv_w"] = conv_w_all[l]
        params.append(p)

    xl, saved = x[0], []
    for l in range(DEPTH):
        g_win, g_br, g_out, g_qkv = gathered[l]
        wts.append(dict(w_in=win_assemble(g_win, after=(xl,)), w_qkv=g_qkv, w_branch=g_br.reshape(3 * BW, D),
                        w_out=g_out.reshape(D, D)))
        xl, sv = layer_fwd(xl, mods[l], params[l], wts[l], tabs, lgam)
        saved.append(sv)
    my_loss, dx = loss_head(xl, loss_target[0])
    loss = lax.psum(my_loss[0, 0], ("x", "y", "c"))

    dmods, smalls, staged, grads = [None] * DEPTH, [None] * DEPTH, [None] * DEPTH, {n: [None] * DEPTH for n in WEIGHTS}
    for l in reversed(range(DEPTH)):
        dx, dmods[l], big, smalls[l] = layer_bwd(dx, saved[l], params[l], wts[l], tabs, lgam)
        staged[l] = exchange_grads(big, l)
    chained, lru_blocks = {"w_in": None, "w_branch": None, "w_out": None}, [None] * DEPTH
    for l in reversed(range(DEPTH)):
        st_win, st_br, st_out, st_qkv = staged[l]
        chained["w_in"] = reduce_adamw(st_win, w_in_t["w_in"], w_in_t["m_w_in"], w_in_t["v_w_in"], l, chained["w_in"],
                                       "update_w_in", sidx=sidx)
        for n, st in (("w_branch", st_br), ("w_out", st_out)):
            chained[n] = reduce_adamw(st, given[n], given["m_" + n], given["v_" + n], l, chained[n], "update_" + n)
        g_qkv = sum_parts(st_qkv, 2 * QPAD, "sum_wqkv")
        grads["mla_w_uq"][l] = g_qkv[:QKV_ROWS, :WUQ_COLS]
        grads["mla_w_ukv"][l] = g_qkv[:QKV_ROWS, QPAD:]
        lru_blocks[l] = g_qkv[QKV_ROWS:, :2 * LANE]
    (g_lru,) = all_gather([jnp.stack(lru_blocks)], "ag_lru_w")
    grads["lru_wa"] = g_lru[..., :LANE].transpose(1, 0, 2, 3)
    grads["lru_wx"] = g_lru[..., LANE:].transpose(1, 0, 2, 3)

    flat = [jnp.stack(dmods).reshape(-1)] + [smalls[l][n].reshape(-1) for l in range(DEPTH) for n in SMALL]
    sizes = [int(t.shape[0]) for t in flat]
    (g_pack,) = all_gather([jnp.concatenate(flat).reshape(-1, LANE)], "ag_small_grads")
    rows = g_pack.shape[1]
    tot = sum_parts(g_pack, LANE, "sum_small_grads", tr=rows // 8).reshape(-1)
    offs = np.concatenate([[0], np.cumsum(sizes)])
    pieces = [tot[int(offs[i]):int(offs[i + 1])] for i in range(len(sizes))]
    grads["ada_b"] = pieces[0].reshape(DEPTH, 3 * D)
    for l in range(DEPTH):
        for j, n in enumerate(SMALL):
            piece = pieces[1 + l * len(SMALL) + j]
            if n == "lru_conv_w":
                piece = lax.dynamic_slice_in_dim(piece.reshape(4, BW), me * LANE, LANE, axis=1)
            grads[n][l] = piece.reshape(given[n].shape[1:])
    dmod_all = g_pack[:, :DEPTH * 3 * D // LANE].reshape(NDEV, DEPTH, 3 * D)
    dmod_mine = lax.dynamic_slice_in_dim(dmod_all, me * ADA_SHARD, ADA_SHARD, axis=2).transpose(1, 0, 2)
    dmod16 = jnp.concatenate([dmod_mine, jnp.zeros((DEPTH, ROWS16 - NDEV, ADA_SHARD), f32)], axis=1)
    grads["ada_w"] = ada_bwd(c16, dmod16)

    outs = {"grad": [], "delta": [], "m": [], "v": []}
    for n in WEIGHTS:
        if n in chained:
            g, delta, new_m, new_v = (jnp.swapaxes(t, 1, 2) for t in chained[n]) if n == "w_in" else chained[n]
        else:
            g = grads[n] if not isinstance(grads[n], list) else jnp.stack(grads[n])
            delta, new_m, new_v = adamw(given[n], g, given["m_" + n], given["v_" + n])
        outs["grad"].append(g)
        outs["delta"].append(delta)
        outs["m"].append(new_m)
        outs["v"].append(new_v)
    return (loss, dx[None], *outs["grad"], *outs["delta"], *outs["m"], *outs["v"])
```

```python
import functools
import math

import numpy as np
import jax
import jax.numpy as jnp
from jax import lax
from jax.experimental import pallas as pl
from jax.experimental.pallas import tpu as pltpu
from jax.experimental.pallas import tpu_sc as plsc

f32 = jnp.float32
bf16 = jnp.bfloat16
SDS = jax.ShapeDtypeStruct

DEPTH = 4
D = 2048
HEADS = 8
DH = 128
BW = HEADS * DH
LANE = 128
CHUNK = 64
EPS = 1e-6
LRU_C = 8.0
NDEV = 8
VMEM_V7X = 64 * 1024 * 1024

Q0, K0, V0, RG0, LX0, LG0, MQ0, MKV0, MKR0, MG0, ML0 = 0, 8, 16, 24, 32, 40, 48, 52, 56, 57, 65
NB = 114
NP = NB * LANE
IN_W = 14400
SHARD_W = IN_W // NDEV
GAP_COL = 7232
GAP = 64
WIN = 1920
WIN_BLKS = WIN // LANE
WIN_STRIDE = 14
LORA = 512
QPAD = 256

ADAM_LR, ADAM_B1, ADAM_B2, ADAM_EPS, ADAM_WD, ADAM_STEP = 0.001, 0.9, 0.999, 1e-08, 0.01, 10
ADAM_BLOCK_ELEMS = 256 * 1024
MLA_SCALE = (128 + 64) ** -0.5
RET_SCALE = 128 ** -0.5


def _cp(vmem_mb=None, **kw):
    if vmem_mb is not None:
        kw["vmem_limit_bytes"] = min(vmem_mb * 1024 * 1024, VMEM_V7X - 8 * 1024 * 1024)
    return pltpu.CompilerParams(**kw)


def _sigmoid(x):
    return 1.0 / (1.0 + jnp.exp(-x))


def _silu(x):
    return x * _sigmoid(x)


def _dsilu(x):
    s = _sigmoid(x)
    return s * (1.0 + x * (1.0 - s))


def _softplus(x):
    return jnp.maximum(x, 0.0) + jnp.log(1.0 + jnp.exp(-jnp.abs(x)))


def _one_minus_exp(y):
    series = -y * (1.0 + y * (0.5 + y * (1.0 / 6.0)))
    return jnp.where(y > -1e-2, series, 1.0 - jnp.exp(y))


def _acc(ref, val, first):
    @pl.when(first)
    def _():
        ref[...] = val

    @pl.when(jnp.logical_not(first))
    def _():
        ref[...] += val


_DIMS = {"nn": (((1,), (0,)), ((), ())), "nt": (((1,), (1,)), ((), ())), "tn": (((0,), (0,)), ((), ()))}


def matmul(a, b, *, dims, M, N, K, tm, tn, tk, out_dtype, name, a_blk0=(0, 0), b_blk0=(0, 0), vmem_mb=48):
    nk = K // tk
    assert M % tm == 0 and N % tn == 0 and K % tk == 0
    dn = _DIMS[dims]

    def body(a_ref, b_ref, o_ref, *scr):
        part = lax.dot_general(a_ref[...].astype(bf16), b_ref[...].astype(bf16), dn, preferred_element_type=f32)
        if nk == 1:
            o_ref[...] = part.astype(out_dtype)
        else:
            acc = scr[0]
            k = pl.program_id(2)
            _acc(acc, part, k == 0)

            @pl.when(k == nk - 1)
            def _():
                o_ref[...] = acc[...].astype(out_dtype)

    ar, ac = a_blk0
    br, bc = b_blk0
    if dims == "nn":
        a_spec = pl.BlockSpec((tm, tk), lambda i, j, k: (i + ar, k + ac))
        b_spec = pl.BlockSpec((tk, tn), lambda i, j, k: (k + br, j + bc))
    elif dims == "nt":
        a_spec = pl.BlockSpec((tm, tk), lambda i, j, k: (i + ar, k + ac))
        b_spec = pl.BlockSpec((tn, tk), lambda i, j, k: (j + br, k + bc))
    else:
        a_spec = pl.BlockSpec((tk, tm), lambda i, j, k: (k + ar, i + ac))
        b_spec = pl.BlockSpec((tk, tn), lambda i, j, k: (k + br, j + bc))
    return pl.pallas_call(
        body, name=name, grid=(M // tm, N // tn, nk),
        in_specs=[a_spec, b_spec], out_specs=pl.BlockSpec((tm, tn), lambda i, j, k: (i, j)),
        out_shape=SDS((M, N), out_dtype),
        scratch_shapes=[] if nk == 1 else [pltpu.VMEM((tm, tn), f32)],
        compiler_params=_cp(vmem_mb, dimension_semantics=("parallel", "parallel", "arbitrary")),
    )(a, b)


def pre_fwd(x, g, sc, sh, *, tm=256):
    S = x.shape[0]

    def body(x_ref, g_ref, sc_ref, sh_ref, h_ref):
        xv = x_ref[...]
        r = lax.rsqrt(jnp.mean(xv * xv, axis=-1, keepdims=True) + EPS)
        h_ref[...] = (((xv * r) * g_ref[...]) * (1.0 + sc_ref[...]) + sh_ref[...]).astype(bf16)

    row = pl.BlockSpec((tm, D), lambda i: (i, 0))
    vec = pl.BlockSpec((1, D), lambda i: (0, 0))
    return pl.pallas_call(body, name="pre_fwd", grid=(S // tm,), in_specs=[row, vec, vec, vec], out_specs=row,
                          out_shape=SDS((S, D), bf16), compiler_params=_cp(32))(x, g, sc, sh)


def pre_bwd(dh, x, g, sc, dxo, *, tm=256, after=()):
    S = x.shape[0]

    def body(dh_ref, x_ref, g_ref, sc_ref, dxo_ref, *rest):
        dx_ref, dsh_ref, dsc_ref, dg_ref = rest[len(after):]
        first = pl.program_id(0) == 0
        xv, dhv, gv = x_ref[...], dh_ref[...], g_ref[...]
        one_sc = 1.0 + sc_ref[...]
        r = lax.rsqrt(jnp.mean(xv * xv, axis=-1, keepdims=True) + EPS)
        xh = xv * r
        t = dhv * xh
        dxh = dhv * gv * one_sc
        dx_ref[...] = r * (dxh - xh * jnp.mean(dxh * xh, axis=-1, keepdims=True)) + dxo_ref[...]
        _acc(dsh_ref, jnp.sum(dhv, axis=0, keepdims=True), first)
        _acc(dsc_ref, jnp.sum(t * gv, axis=0, keepdims=True), first)
        _acc(dg_ref, jnp.sum(t * one_sc, axis=0, keepdims=True), first)

    row = pl.BlockSpec((tm, D), lambda i: (i, 0))
    vec = pl.BlockSpec((1, D), lambda i: (0, 0))
    return pl.pallas_call(
        body, name="pre_bwd", grid=(S // tm,),
        in_specs=[row, row, vec, vec, row] + [pl.BlockSpec(memory_space=pl.ANY)] * len(after), out_specs=[row, vec, vec, vec],
        out_shape=[SDS((S, D), f32), SDS((1, D), f32), SDS((1, D), f32), SDS((1, D), f32)],
        compiler_params=_cp(40))(dh, x, g, sc, dxo, *after)


def out_fwd(merged, w_out, x, rg, gp, *, tm=256):
    S = x.shape[0]

    def body(m_ref, w_ref, x_ref, rg_ref, gp_ref, y_ref, xn_ref):
        y = jnp.dot(m_ref[...], w_ref[...], preferred_element_type=f32)
        y_ref[...] = y
        r = lax.rsqrt(jnp.mean(y * y, axis=-1, keepdims=True) + EPS)
        xn_ref[...] = x_ref[...] + (1.0 + rg_ref[...]) * ((y * r) * gp_ref[...])

    row = pl.BlockSpec((tm, D), lambda i: (i, 0))
    vec = pl.BlockSpec((1, D), lambda i: (0, 0))
    return pl.pallas_call(
        body, name="out_fwd", grid=(S // tm,),
        in_specs=[row, pl.BlockSpec((D, D), lambda i: (0, 0)), row, vec, vec], out_specs=[row, row],
        out_shape=[SDS((S, D), f32), SDS((S, D), f32)], compiler_params=_cp(48))(merged, w_out, x, rg, gp)


def out_bwd(dxo, y, rg, gp, *, tm=256):
    S = y.shape[0]

    def body(dxo_ref, y_ref, rg_ref, gp_ref, dy_ref, drg_ref, dgp_ref):
        first = pl.program_id(0) == 0
        yv, dv, gv = y_ref[...], dxo_ref[...], gp_ref[...]
        r = lax.rsqrt(jnp.mean(yv * yv, axis=-1, keepdims=True) + EPS)
        yh = yv * r
        dn = dv * (1.0 + rg_ref[...])
        dyh = dn * gv
        dy_ref[...] = (r * (dyh - yh * jnp.mean(dyh * yh, axis=-1, keepdims=True))).astype(bf16)
        _acc(drg_ref, jnp.sum(dv * (yh * gv), axis=0, keepdims=True), first)
        _acc(dgp_ref, jnp.sum(dn * yh, axis=0, keepdims=True), first)

    row = pl.BlockSpec((tm, D), lambda i: (i, 0))
    vec = pl.BlockSpec((1, D), lambda i: (0, 0))
    return pl.pallas_call(
        body, name="out_bwd", grid=(S // tm,), in_specs=[row, row, vec, vec], out_specs=[row, vec, vec],
        out_shape=[SDS((S, D), bf16), SDS((1, D), f32), SDS((1, D), f32)], compiler_params=_cp(40))(dxo, y, rg, gp)


def _ml_spec(b, tm):
    return pl.BlockSpec((tm, LANE), lambda i, j: (i, ML0 + b * (D // LANE) + j))


def gate_fwd(proj, us, *, tm=2048):
    S = proj.shape[0]
    tm = min(tm, S)

    def body(ml0, ml1, ml2, u0, u1, u2, m_ref):
        acc = None
        for ml, u in ((ml0, u0), (ml1, u1), (ml2, u2)):
            t = _sigmoid(ml[...]) * u[...].astype(f32)
            acc = t if acc is None else acc + t
        m_ref[...] = acc.astype(bf16)

    blk = pl.BlockSpec((tm, LANE), lambda i, j: (i, j))
    return pl.pallas_call(
        body, name="gate_fwd", grid=(S // tm, D // LANE),
        in_specs=[_ml_spec(0, tm), _ml_spec(1, tm), _ml_spec(2, tm), blk, blk, blk], out_specs=blk,
        out_shape=SDS((S, D), bf16), compiler_params=_cp(32),
    )(proj, proj, proj, *us)


def gate_bwd(dmerged, proj, us, *, tm=2048):
    S = proj.shape[0]
    tm = min(tm, S)

    def body(dm_ref, ml0, ml1, ml2, u0, u1, u2, du0, du1, du2, dl0, dl1, dl2):
        dm = dm_ref[...]
        for ml, u, du, dl in ((ml0, u0, du0, dl0), (ml1, u1, du1, dl1), (ml2, u2, du2, dl2)):
            s = _sigmoid(ml[...])
            du[...] = (dm * s).astype(bf16)
            dl[...] = (dm * u[...].astype(f32) * (s * (1.0 - s))).astype(bf16)

    blk = pl.BlockSpec((tm, LANE), lambda i, j: (i, j))
    outs = pl.pallas_call(
        body, name="gate_bwd", grid=(S // tm, D // LANE),
        in_specs=[blk, _ml_spec(0, tm), _ml_spec(1, tm), _ml_spec(2, tm), blk, blk, blk], out_specs=[blk] * 6,
        out_shape=[SDS((S, D), bf16)] * 6, compiler_params=_cp(40),
    )(dmerged, proj, proj, proj, *us)
    return outs[:3], outs[3:]


def rope_tables(positions):
    pos = positions.astype(f32)[:, None]

    def cs(dim):
        inv = 10000.0 ** (-jnp.arange(0, dim, 2, dtype=f32) / dim)
        ang = pos * inv
        return jnp.cos(ang), jnp.sin(ang)

    c, s = cs(128)
    ret = (jnp.concatenate([c, c], 1), jnp.concatenate([-s, s], 1))
    c, s = cs(64)
    z32, z64 = jnp.zeros_like(c), jnp.zeros((c.shape[0], 64), f32)
    mla = (jnp.concatenate([c, c, z64], 1), jnp.concatenate([-s, z32, z64], 1), jnp.concatenate([z32, s, z64], 1))
    return ret, mla


def _rope_ret(x, c, s):
    return x * c + pltpu.roll(x, 64, 1) * s


def _rope_ret_t(dy, c, s):
    return dy * c + pltpu.roll(dy * s, 64, 1)


def _rope_mla(x, c, sa, sb):
    return x * c + pltpu.roll(x, 96, 1) * sa + pltpu.roll(x, 32, 1) * sb


def _rope_mla_t(dy, c, sa, sb):
    return dy * c + pltpu.roll(dy * sa, 32, 1) + pltpu.roll(dy * sb, 96, 1)


def _softmax_rows(q, k):
    bq, nk = q.shape[0], k.shape[0]
    s = lax.dot_general(q, k, _DIMS["nt"], preferred_element_type=f32)
    i = lax.broadcasted_iota(jnp.int32, (bq, bq), 0)
    j = lax.broadcasted_iota(jnp.int32, (bq, bq), 1)
    own = s[:, nk - bq:] + jnp.where((j // CHUNK) <= (i // CHUNK), 0.0, -1e30)
    s = own if nk == bq else jnp.concatenate([s[:, :nk - bq], own], axis=1)
    e = jnp.exp(s - jnp.max(s, axis=-1, keepdims=True))
    return e / jnp.sum(e, axis=-1, keepdims=True)


def _decay_rows(lg, bq, S):
    nq = S // bq
    i = lax.broadcasted_iota(jnp.int32, (bq, S), 0)
    col = lax.broadcasted_iota(jnp.int32, (bq, S), 1)
    j = col % bq
    back = nq - 1 - col // bq
    dist = back * bq + i - j
    seen = jnp.logical_or(back > 0, (j // CHUNK) <= (i // CHUNK))
    return jnp.where(seen, jnp.exp(lg * jnp.abs(dist).astype(f32)), 0.0)


def _per_query_block(nq, fn):
    for qi in range(nq):
        pl.when(pl.program_id(1) == qi)(functools.partial(fn, qi))


def mla_attn_fwd(q, k, v, proj, *, bq=256):
    S = q.shape[0]
    bq = min(bq, S)
    assert bq % CHUNK == 0

    def body(q_ref, k_ref, v_ref, g_ref, o_ref, y_ref):
        def block(qi):
            keys = pl.ds(0, (qi + 1) * bq)
            p = _softmax_rows(q_ref[...], k_ref[keys, :])
            o = jnp.dot(p.astype(bf16), v_ref[keys, :], preferred_element_type=f32)
            o_ref[...] = o
            y_ref[...] = (o * _silu(g_ref[...])).astype(bf16)

        _per_query_block(S // bq, block)

    o_spec = pl.BlockSpec((bq, DH), lambda h, i: (i, h))
    return pl.pallas_call(
        body, name="mla_attn_fwd", grid=(HEADS, S // bq),
        in_specs=[pl.BlockSpec((bq, QPAD), lambda h, i: (i, h)), pl.BlockSpec((S, QPAD), lambda h, i: (0, h)),
                  pl.BlockSpec((S, DH), lambda h, i: (0, h)), pl.BlockSpec((bq, DH), lambda h, i: (i, MG0 + h))],
        out_specs=[o_spec, o_spec], out_shape=[SDS((S, BW), f32), SDS((S, BW), bf16)],
        compiler_params=_cp(48))(q, k, v, proj)


def mla_attn_bwd(q, k, v, do, o, *, bq=256):
    S = q.shape[0]
    bq = min(bq, S)

    def body(q_ref, k_ref, v_ref, do_ref, o_ref, dq_ref, dk_ref, dv_ref):
        def block(qi):
            keys = pl.ds(0, (qi + 1) * bq)
            if qi == 0:
                dk_ref[...] = jnp.zeros_like(dk_ref)
                dv_ref[...] = jnp.zeros_like(dv_ref)
            qv, kv, dov = q_ref[...], k_ref[keys, :], do_ref[...]
            p = _softmax_rows(qv, kv)
            dp = lax.dot_general(dov, v_ref[keys, :], _DIMS["nt"], preferred_element_type=f32)
            delta = jnp.sum(dov.astype(f32) * o_ref[...], axis=-1, keepdims=True)
            dsb = (p * (dp - delta)).astype(bf16)
            dq_ref[...] = jnp.dot(dsb, kv, preferred_element_type=f32)
            dv_ref[keys, :] += lax.dot_general(p.astype(bf16), dov, _DIMS["tn"], preferred_element_type=f32)
            dk_ref[keys, :] += lax.dot_general(dsb, qv, _DIMS["tn"], preferred_element_type=f32)

        _per_query_block(S // bq, block)

    o_spec = pl.BlockSpec((bq, DH), lambda h, i: (i, h))
    return pl.pallas_call(
        body, name="mla_attn_bwd", grid=(HEADS, S // bq),
        in_specs=[pl.BlockSpec((bq, QPAD), lambda h, i: (i, h)), pl.BlockSpec((S, QPAD), lambda h, i: (0, h)),
                  pl.BlockSpec((S, DH), lambda h, i: (0, h)), o_spec, o_spec],
        out_specs=[pl.BlockSpec((bq, QPAD), lambda h, i: (i, h)), pl.BlockSpec((S, QPAD), lambda h, i: (0, h)),
                   pl.BlockSpec((S, DH), lambda h, i: (0, h))],
        out_shape=[SDS((S, HEADS * QPAD), f32), SDS((S, HEADS * QPAD), f32), SDS((S, BW), f32)],
        compiler_params=_cp(52))(q, k, v, do, o)


def ret_attn_fwd(proj, tabs, lgam, gn, *, bq=256):
    S = proj.shape[0]
    bq = min(bq, S)
    assert bq % CHUNK == 0

    def body(lg_ref, q_ref, k_ref, v_ref, g_ref, gn_ref, cq_ref, sq_ref, ck_ref, sk_ref, o_ref, y_ref, k_s, w_s):
        lg = lg_ref[pl.program_id(0)]

        def block(qi):
            if qi == 0:
                k_s[...] = _rope_ret(k_ref[...], ck_ref[...], sk_ref[...]).astype(bf16)
                w_s[...] = _decay_rows(lg, bq, S)
            keys = pl.ds(0, (qi + 1) * bq)
            qv = (_rope_ret(q_ref[...], cq_ref[...], sq_ref[...]) * RET_SCALE).astype(bf16)
            p = lax.dot_general(qv, k_s[keys, :], _DIMS["nt"], preferred_element_type=f32) * w_s[:, pl.ds(S - (qi + 1) * bq, (qi + 1) * bq)]
            o = jnp.dot(p.astype(bf16), v_ref[keys, :].astype(bf16), preferred_element_type=f32)
            o_ref[...] = o
            oc = o - jnp.mean(o, axis=-1, keepdims=True)
            z = oc * lax.rsqrt(jnp.mean(oc * oc, axis=-1, keepdims=True) + EPS) * gn_ref[...]
            y_ref[...] = (z * _silu(g_ref[...])).astype(bf16)

        _per_query_block(S // bq, block)

    row = lambda b0: pl.BlockSpec((bq, DH), lambda h, i: (i, b0 + h))
    full = lambda b0: pl.BlockSpec((S, DH), lambda h, i: (0, b0 + h))
    tq, tk = pl.BlockSpec((bq, DH), lambda h, i: (i, 0)), pl.BlockSpec((S, DH), lambda h, i: (0, 0))
    o_spec = pl.BlockSpec((bq, DH), lambda h, i: (i, h))
    return pl.pallas_call(
        body, name="ret_attn_fwd", grid=(HEADS, S // bq),
        in_specs=[pl.BlockSpec(memory_space=pltpu.SMEM), row(Q0), full(K0), full(V0), row(RG0),
                  pl.BlockSpec((1, DH), lambda h, i: (0, h)), tq, tq, tk, tk],
        out_specs=[o_spec, o_spec], out_shape=[SDS((S, BW), f32), SDS((S, BW), bf16)],
        scratch_shapes=[pltpu.VMEM((S, DH), bf16), pltpu.VMEM((bq, S), f32)], compiler_params=_cp(48),
    )(lgam, proj, proj, proj, proj, gn, tabs[0], tabs[1], tabs[0], tabs[1])


def ret_attn_bwd(proj, do, tabs, lgam, *, bq=256):
    S = proj.shape[0]
    bq = min(bq, S)
    nq = S // bq

    def body(lg_ref, q_ref, k_ref, v_ref, do_ref, cq_ref, sq_ref, ck_ref, sk_ref, dq_ref, dk_ref, dv_ref, k_s, dk_s, dv_s, w_s):
        lg = lg_ref[pl.program_id(0)]

        def block(qi):
            if qi == 0:
                k_s[...] = _rope_ret(k_ref[...], ck_ref[...], sk_ref[...]).astype(bf16)
                w_s[...] = _decay_rows(lg, bq, S)
                dk_s[...] = jnp.zeros_like(dk_s)
                dv_s[...] = jnp.zeros_like(dv_s)
            keys = pl.ds(0, (qi + 1) * bq)
            cq, sq = cq_ref[...], sq_ref[...]
            qv = (_rope_ret(q_ref[...], cq, sq) * RET_SCALE).astype(bf16)
            kv, dov = k_s[keys, :], do_ref[...]
            w = w_s[:, pl.ds(S - (qi + 1) * bq, (qi + 1) * bq)]
            p = lax.dot_general(qv, kv, _DIMS["nt"], preferred_element_type=f32) * w
            dp = lax.dot_general(dov, v_ref[keys, :].astype(bf16), _DIMS["nt"], preferred_element_type=f32)
            dsb = (dp * w).astype(bf16)
            dq_ref[...] = _rope_ret_t(jnp.dot(dsb, kv, preferred_element_type=f32) * RET_SCALE, cq, sq).astype(bf16)
            dv_s[keys, :] += lax.dot_general(p.astype(bf16), dov, _DIMS["tn"], preferred_element_type=f32)
            dk_s[keys, :] += lax.dot_general(dsb, qv, _DIMS["tn"], preferred_element_type=f32)
            if qi == nq - 1:
                dk_ref[...] = _rope_ret_t(dk_s[...], ck_ref[...], sk_ref[...]).astype(bf16)
                dv_ref[...] = dv_s[...].astype(bf16)

        _per_query_block(nq, block)

    row = lambda b0: pl.BlockSpec((bq, DH), lambda h, i: (i, b0 + h))
    full = lambda b0: pl.BlockSpec((S, DH), lambda h, i: (0, b0 + h))
    tq, tk = pl.BlockSpec((bq, DH), lambda h, i: (i, 0)), pl.BlockSpec((S, DH), lambda h, i: (0, 0))
    return pl.pallas_call(
        body, name="ret_attn_bwd", grid=(HEADS, nq),
        in_specs=[pl.BlockSpec(memory_space=pltpu.SMEM), row(Q0), full(K0), full(V0), row(0), tq, tq, tk, tk],
        out_specs=[row(0), full(0), full(0)], out_shape=[SDS((S, BW), bf16)] * 3,
        scratch_shapes=[pltpu.VMEM((S, DH), bf16), pltpu.VMEM((S, DH), f32), pltpu.VMEM((S, DH), f32),
                        pltpu.VMEM((bq, S), f32)],
        compiler_params=_cp(52),
    )(lgam, proj, proj, proj, do, tabs[0], tabs[1], tabs[0], tabs[1])


def ret_post_bwd(dy, o, proj, gn, *, tm=512):
    S = dy.shape[0]
    tm = min(tm, S)

    def body(dy_ref, o_ref, g_ref, gn_ref, do_ref, drg_ref, dgn_ref):
        ov, g, gnv, dyv = o_ref[...], g_ref[...], gn_ref[...], dy_ref[...]
        oc = ov - jnp.mean(ov, axis=-1, keepdims=True)
        rs = lax.rsqrt(jnp.mean(oc * oc, axis=-1, keepdims=True) + EPS)
        oh = oc * rs
        dz = dyv * _silu(g)
        drg_ref[...] = (dyv * (oh * gnv) * _dsilu(g)).astype(bf16)
        doh = dz * gnv
        do_ref[...] = (rs * (doh - jnp.mean(doh, axis=-1, keepdims=True)
                             - oh * jnp.mean(doh * oh, axis=-1, keepdims=True))).astype(bf16)
        _acc(dgn_ref, jnp.sum(dz * oh, axis=0, keepdims=True), pl.program_id(1) == 0)

    blk = pl.BlockSpec((tm, DH), lambda h, i: (i, h))
    vec = pl.BlockSpec((1, DH), lambda h, i: (0, h))
    return pl.pallas_call(
        body, name="ret_post_bwd", grid=(HEADS, S // tm),
        in_specs=[blk, blk, pl.BlockSpec((tm, DH), lambda h, i: (i, RG0 + h)), vec], out_specs=[blk, blk, vec],
        out_shape=[SDS((S, BW), bf16), SDS((S, BW), bf16), SDS((1, BW), f32)], compiler_params=_cp(32))(dy, o, proj, gn)


def mla_post_bwd(dy, o, proj, *, tm=512):
    S = dy.shape[0]
    tm = min(tm, S)

    def body(dy_ref, o_ref, g_ref, do_ref, dg_ref):
        g, dyv = g_ref[...], dy_ref[...]
        do_ref[...] = (dyv * _silu(g)).astype(bf16)
        dg_ref[...] = (dyv * o_ref[...] * _dsilu(g)).astype(bf16)

    blk = pl.BlockSpec((tm, DH), lambda i, h: (i, h))
    return pl.pallas_call(
        body, name="mla_post_bwd", grid=(S // tm, HEADS),
        in_specs=[blk, blk, pl.BlockSpec((tm, DH), lambda i, h: (i, MG0 + h))], out_specs=[blk, blk],
        out_shape=[SDS((S, BW), bf16), SDS((S, BW), bf16)], compiler_params=_cp(32))(dy, o, proj)


def mla_prep_fwd(proj, qnorm, kvnorm, wuq, wukv, tabs, *, tm=256, kv_blk=0):
    S = proj.shape[0]
    tm = min(tm, S)

    def body(mq_ref, mkv_ref, mkr_ref, qn_ref, kvn_ref, wuq_ref, wukv_ref, c_ref, sa_ref, sb_ref, q_ref, k_ref, v_ref):
        c, sa, sb = c_ref[...], sa_ref[...], sb_ref[...]
        mq, mkv = mq_ref[...], mkv_ref[...]
        qn = (mq * lax.rsqrt(jnp.mean(mq * mq, axis=-1, keepdims=True) + EPS) * qn_ref[...]).astype(bf16)
        kvn = (mkv * lax.rsqrt(jnp.mean(mkv * mkv, axis=-1, keepdims=True) + EPS) * kvn_ref[...]).astype(bf16)
        kr = _rope_mla(mkr_ref[...], c, sa, sb).astype(bf16)
        for h in range(HEADS):
            qh = jnp.dot(qn, wuq_ref[h], preferred_element_type=f32)
            q_ref[:, pl.ds(h * QPAD, DH)] = (qh[:, :DH] * MLA_SCALE).astype(bf16)
            q_ref[:, pl.ds(h * QPAD + DH, DH)] = (_rope_mla(qh[:, DH:], c, sa, sb) * MLA_SCALE).astype(bf16)
            kvh = jnp.dot(kvn, wukv_ref[h], preferred_element_type=f32)
            k_ref[:, pl.ds(h * QPAD, DH)] = kvh[:, :DH].astype(bf16)
            k_ref[:, pl.ds(h * QPAD + DH, DH)] = kr
            v_ref[:, pl.ds(h * DH, DH)] = kvh[:, DH:].astype(bf16)

    lat = lambda b: pl.BlockSpec((tm, LORA), lambda i: (i, b))
    tab = pl.BlockSpec((tm, DH), lambda i: (i, 0))
    vec = pl.BlockSpec((1, LORA), lambda i: (0, 0))
    wsp = pl.BlockSpec((HEADS, LORA, QPAD), lambda i: (0, 0, 0))
    wkv = pl.BlockSpec((HEADS, LORA, QPAD), lambda i: (0, 0, kv_blk))
    return pl.pallas_call(
        body, name="mla_prep_fwd", grid=(S // tm,),
        in_specs=[lat(MQ0 // 4), lat(MKV0 // 4), pl.BlockSpec((tm, DH), lambda i: (i, MKR0)), vec, vec, wsp, wkv,
                  tab, tab, tab],
        out_specs=[pl.BlockSpec((tm, HEADS * QPAD), lambda i: (i, 0))] * 2 + [pl.BlockSpec((tm, BW), lambda i: (i, 0))],
        out_shape=[SDS((S, HEADS * QPAD), bf16)] * 2 + [SDS((S, BW), bf16)], compiler_params=_cp(48),
    )(proj, proj, proj, qnorm, kvnorm, wuq, wukv, *tabs)


def mla_prep_bwd(dq256, dk256, dv, proj, qnorm, kvnorm, wuq, wukv, tabs, *, tm=256, kv_blk=0):
    S = proj.shape[0]
    tm = min(tm, S)

    def body(dq_ref, dk_ref, dv_ref, mq_ref, mkv_ref, qn_ref, kvn_ref, wuq_ref, wukv_ref, c_ref, sa_ref, sb_ref,
             dm_ref, dwuq_ref, dwukv_ref, dqn_ref, dkvn_ref):
        first = pl.program_id(0) == 0
        c, sa, sb = c_ref[...], sa_ref[...], sb_ref[...]
        mq, mkv = mq_ref[...], mkv_ref[...]
        rq = lax.rsqrt(jnp.mean(mq * mq, axis=-1, keepdims=True) + EPS)
        rkv = lax.rsqrt(jnp.mean(mkv * mkv, axis=-1, keepdims=True) + EPS)
        mqh, mkvh = mq * rq, mkv * rkv
        qn = (mqh * qn_ref[...]).astype(bf16)
        kvn = (mkvh * kvn_ref[...]).astype(bf16)
        dqn = jnp.zeros((tm, LORA), f32)
        dkvn = jnp.zeros((tm, LORA), f32)
        dkr = jnp.zeros((tm, DH), f32)
        for h in range(HEADS):
            da = dq_ref[:, pl.ds(h * QPAD, DH)] * MLA_SCALE
            db = _rope_mla_t(dq_ref[:, pl.ds(h * QPAD + DH, DH)] * MLA_SCALE, c, sa, sb)
            dqh = jnp.concatenate([da, db], axis=1).astype(bf16)
            dqn += lax.dot_general(dqh, wuq_ref[h], _DIMS["nt"], preferred_element_type=f32)
            _acc(dwuq_ref.at[h], lax.dot_general(qn, dqh, _DIMS["tn"], preferred_element_type=f32), first)
            dkr += dk_ref[:, pl.ds(h * QPAD + DH, DH)]
            dkvh = jnp.concatenate([dk_ref[:, pl.ds(h * QPAD, DH)], dv_ref[:, pl.ds(h * DH, DH)]], axis=1).astype(bf16)
            dkvn += lax.dot_general(dkvh, wukv_ref[h], _DIMS["nt"], preferred_element_type=f32)
            _acc(dwukv_ref.at[h], lax.dot_general(kvn, dkvh, _DIMS["tn"], preferred_element_type=f32), first)
        dmh = dqn * qn_ref[...]
        dm_ref[:, pl.ds(0, LORA)] = (rq * (dmh - mqh * jnp.mean(dmh * mqh, axis=-1, keepdims=True))).astype(bf16)
        dmh = dkvn * kvn_ref[...]
        dm_ref[:, pl.ds(LORA, LORA)] = (rkv * (dmh - mkvh * jnp.mean(dmh * mkvh, axis=-1, keepdims=True))).astype(bf16)
        dm_ref[:, pl.ds(2 * LORA, DH)] = _rope_mla_t(dkr, c, sa, sb).astype(bf16)
        _acc(dqn_ref, jnp.sum(dqn * mqh, axis=0, keepdims=True), first)
        _acc(dkvn_ref, jnp.sum(dkvn * mkvh, axis=0, keepdims=True), first)

    lat = lambda b: pl.BlockSpec((tm, LORA), lambda i: (i, b))
    tab = pl.BlockSpec((tm, DH), lambda i: (i, 0))
    vec = pl.BlockSpec((1, LORA), lambda i: (0, 0))
    wsp = pl.BlockSpec((HEADS, LORA, QPAD), lambda i: (0, 0, 0))
    wkv = pl.BlockSpec((HEADS, LORA, QPAD), lambda i: (0, 0, kv_blk))
    wide = pl.BlockSpec((tm, HEADS * QPAD), lambda i: (i, 0))
    return pl.pallas_call(
        body, name="mla_prep_bwd", grid=(S // tm,),
        in_specs=[wide, wide, pl.BlockSpec((tm, BW), lambda i: (i, 0)), lat(MQ0 // 4), lat(MKV0 // 4), vec, vec, wsp, wkv,
                  tab, tab, tab],
        out_specs=[pl.BlockSpec((tm, 2 * LORA + DH), lambda i: (i, 0)), wsp, wsp, vec, vec],
        out_shape=[SDS((S, 2 * LORA + DH), bf16), SDS((HEADS, LORA, QPAD), f32), SDS((HEADS, LORA, QPAD), f32),
                   SDS((1, LORA), f32), SDS((1, LORA), f32)],
        compiler_params=_cp(52),
    )(dq256, dk256, dv, proj, proj, qnorm, kvnorm, wuq, wukv, *tabs)


SUB = 8


def _scan_tiles(a_s, b_s, out, S, reverse):
    nt = S // SUB
    rows = lax.broadcasted_iota(jnp.int32, (SUB, LANE), 0)

    def tile(t, carry):
        base = pl.multiple_of((nt - 1 - t if reverse else t) * SUB, SUB)
        a, b = a_s[pl.ds(base, SUB), :], b_s[pl.ds(base, SUB), :]
        for d in (1, 2, 4):
            sh = SUB - d if reverse else d
            inside = rows < SUB - d if reverse else rows >= d
            a_n = jnp.where(inside, pltpu.roll(a, sh, 0), 1.0)
            b_n = jnp.where(inside, pltpu.roll(b, sh, 0), 0.0)
            b = a * b_n + b
            a = a * a_n
        res = a * carry + b
        out[pl.ds(base, SUB), :] = res
        edge = res[0:1, :] if reverse else res[SUB - 1:SUB, :]
        return jnp.broadcast_to(edge, (SUB, LANE))

    lax.fori_loop(0, nt, tile, jnp.zeros((SUB, LANE), f32))


def _shift_down(x, n, rows):
    return x if n == 0 else jnp.where(rows >= n, pltpu.roll(x, n, 0), 0.0)


def _shift_up(x, n, rows, S):
    return x if n == 0 else jnp.where(rows < S - n, pltpu.roll(x, S - n, 0), 0.0)


def _lru_gates(xb, cw, cb, wa, ba, wx, bx, lam, rows):
    xc = cb + cw[3:4, :] * xb
    for w in range(3):
        xc = xc + cw[w:w + 1, :] * _shift_down(xb, 3 - w, rows)
    xcb = xc.astype(bf16)
    r = _sigmoid(jnp.dot(xcb, wa, preferred_element_type=f32) + ba)
    i = _sigmoid(jnp.dot(xcb, wx, preferred_element_type=f32) + bx)
    sp = _softplus(-lam)
    la = (-LRU_C * r) * sp
    return xc, xcb, r, i, sp, la, jnp.exp(la)


def _lru_specs(S):
    col = lambda b0: pl.BlockSpec((S, LANE), lambda n: (0, b0 + n))
    vec = pl.BlockSpec((1, LANE), lambda n: (0, n))
    return col, vec, pl.BlockSpec((4, LANE), lambda n: (0, n)), pl.BlockSpec((1, LANE, LANE), lambda n: (n, 0, 0))


def lru_fwd(proj, cw, cb, wa, ba, wx, bx, lam):
    S = proj.shape[0]

    def body(x_ref, g_ref, cw_ref, cb_ref, wa_ref, ba_ref, wx_ref, bx_ref, lam_ref, h_ref, y_ref, a_s, b_s):
        rows = lax.broadcasted_iota(jnp.int32, (S, LANE), 0)
        xc, _, _, i, _, la, a = _lru_gates(x_ref[...], cw_ref[...], cb_ref[...], wa_ref[0].astype(bf16), ba_ref[...],
                                           wx_ref[0].astype(bf16), bx_ref[...], lam_ref[...], rows)
        a_s[...] = a
        b_s[...] = jnp.sqrt(_one_minus_exp(2.0 * la)) * (i * xc)
        _scan_tiles(a_s, b_s, h_ref, S, reverse=False)
        y_ref[...] = (h_ref[...] * _silu(g_ref[...])).astype(bf16)

    col, vec, cws, wsp = _lru_specs(S)
    return pl.pallas_call(
        body, name="lru_fwd", grid=(HEADS,),
        in_specs=[col(LX0), col(LG0), cws, vec, wsp, vec, wsp, vec, vec], out_specs=[col(0), col(0)],
        out_shape=[SDS((S, BW), f32), SDS((S, BW), bf16)],
        scratch_shapes=[pltpu.VMEM((S, LANE), f32), pltpu.VMEM((S, LANE), f32)], compiler_params=_cp(40),
    )(proj, proj, cw, cb, wa, ba, wx, bx, lam)


def lru_bwd(dy, h, proj, cw, cb, wa, ba, wx, bx, lam):
    S = proj.shape[0]

    def body(dy_ref, h_ref, x_ref, g_ref, cw_ref, cb_ref, wa_ref, ba_ref, wx_ref, bx_ref, lam_ref,
             dx_ref, dg_ref, dcw_ref, dcb_ref, dba_ref, dbx_ref, dlam_ref, dwa_ref, dwx_ref, a_s, b_s, l_s):
        rows = lax.broadcasted_iota(jnp.int32, (S, LANE), 0)
        xb, g, hv, dyv, cw, lam = x_ref[...], g_ref[...], h_ref[...], dy_ref[...], cw_ref[...], lam_ref[...]
        wa, wx = wa_ref[0].astype(bf16), wx_ref[0].astype(bf16)
        xc, xcb, r, i, sp, la, a = _lru_gates(xb, cw, cb_ref[...], wa, ba_ref[...], wx, bx_ref[...], lam, rows)
        dg_ref[...] = (dyv * hv * _dsilu(g)).astype(bf16)
        a_s[...] = _shift_up(a, 1, rows, S)
        b_s[...] = dyv * _silu(g)
        _scan_tiles(a_s, b_s, l_s, S, reverse=True)
        lmb = l_s[...]
        gated = i * xc
        sq = jnp.sqrt(_one_minus_exp(2.0 * la))
        dla = lmb * _shift_down(hv, 1, rows) * a - (lmb * gated) * (a * a) / sq
        dgated = lmb * sq
        dzr = (dla * (-LRU_C * sp)) * (r * (1.0 - r))
        dzi = (dgated * xc) * (i * (1.0 - i))
        dzrb, dzib = dzr.astype(bf16), dzi.astype(bf16)
        dxc = (dgated * i + lax.dot_general(dzrb, wa, _DIMS["nt"], preferred_element_type=f32)
               + lax.dot_general(dzib, wx, _DIMS["nt"], preferred_element_type=f32))
        dwa_ref[0] = lax.dot_general(xcb, dzrb, _DIMS["tn"], preferred_element_type=f32)
        dwx_ref[0] = lax.dot_general(xcb, dzib, _DIMS["tn"], preferred_element_type=f32)
        dba_ref[...] = jnp.sum(dzr, axis=0, keepdims=True)
        dbx_ref[...] = jnp.sum(dzi, axis=0, keepdims=True)
        dlam_ref[...] = jnp.sum(dla * (-LRU_C * r), axis=0, keepdims=True) * (-_sigmoid(-lam))
        dcb_ref[...] = jnp.sum(dxc, axis=0, keepdims=True)
        dxb = cw[3:4, :] * dxc
        dcw_ref[3:4, :] = jnp.sum(dxc * xb, axis=0, keepdims=True)
        for w in range(3):
            dxb = dxb + cw[w:w + 1, :] * _shift_up(dxc, 3 - w, rows, S)
            dcw_ref[w:w + 1, :] = jnp.sum(dxc * _shift_down(xb, 3 - w, rows), axis=0, keepdims=True)
        dx_ref[...] = dxb.astype(bf16)

    col, vec, cws, wsp = _lru_specs(S)
    scr = pltpu.VMEM((S, LANE), f32)
    return pl.pallas_call(
        body, name="lru_bwd", grid=(HEADS,),
        in_specs=[col(0), col(0), col(LX0), col(LG0), cws, vec, wsp, vec, wsp, vec, vec],
        out_specs=[col(0), col(0), cws, vec, vec, vec, vec, wsp, wsp],
        out_shape=[SDS((S, BW), bf16), SDS((S, BW), bf16), SDS((4, BW), f32)] + [SDS((1, BW), f32)] * 4
        + [SDS((HEADS, LANE, LANE), f32)] * 2,
        scratch_shapes=[scr, scr, scr], compiler_params=_cp(48),
    )(dy, h, proj, proj, cw, cb, wa, ba, wx, bx, lam)


def loss_head(y, target, *, tm=256):
    S = y.shape[0]

    def body(y_ref, t_ref, l_ref, d_ref):
        err = y_ref[...] - t_ref[...]
        d_ref[...] = err * (1.0 / D)
        part = jnp.sum(jnp.sum(err * err, axis=1, keepdims=True), axis=0, keepdims=True) * (0.5 / D)
        _acc(l_ref, jnp.broadcast_to(part, (1, LANE)), pl.program_id(0) == 0)

    row = pl.BlockSpec((tm, D), lambda i: (i, 0))
    return pl.pallas_call(
        body, name="loss_head", grid=(S // tm,), in_specs=[row, row],
        out_specs=[pl.BlockSpec((1, LANE), lambda i: (0, 0)), row],
        out_shape=[SDS((1, LANE), f32), SDS((S, D), f32)], compiler_params=_cp(32))(y, target)


def _adam_math(w, g, m, v):
    mn = ADAM_B1 * m + (1.0 - ADAM_B1) * g
    vn = ADAM_B2 * v + (1.0 - ADAM_B2) * (g * g)
    m_hat = mn / (1.0 - ADAM_B1 ** ADAM_STEP)
    v_hat = vn / (1.0 - ADAM_B2 ** ADAM_STEP)
    return -ADAM_LR * (m_hat / (jnp.sqrt(v_hat) + ADAM_EPS) + ADAM_WD * w), mn, vn


def _adam_rows(rows, cols):
    for cand in (2048, 1024, 512, 256, 128, 64, 32, 16, 8):
        if rows % cand == 0 and rows > cand and cand * cols <= ADAM_BLOCK_ELEMS:
            return cand
    return rows


def adamw(w, g, m, v):
    shape = w.shape
    cols = shape[-1]
    rows = math.prod(shape[:-1])
    tr = _adam_rows(rows, cols)

    def body(w_ref, g_ref, m_ref, v_ref, d_ref, mo_ref, vo_ref):
        d_ref[...], mo_ref[...], vo_ref[...] = _adam_math(w_ref[...], g_ref[...], m_ref[...], v_ref[...])

    blk = pl.BlockSpec((tr, cols), lambda i: (i, 0))
    flat = [t.reshape(rows, cols) for t in (w, g, m, v)]
    outs = pl.pallas_call(
        body, name="adamw", grid=(rows // tr,), in_specs=[blk] * 4, out_specs=[blk] * 3,
        out_shape=[SDS((rows, cols), f32)] * 3, compiler_params=_cp(48))(*flat)
    return tuple(o.reshape(shape) for o in outs)


ADA_SHARD = 3 * D // NDEV
ROWS16 = 16


def ada_fwd(c_all, ada_w, ada_b_mine):
    def body(c_ref, w_ref, b_ref, o_ref):
        o_ref[0] = jnp.dot(_silu(c_ref[...]).astype(bf16), w_ref[0].astype(bf16), preferred_element_type=f32) + b_ref[0]

    return pl.pallas_call(
        body, name="ada_fwd", grid=(DEPTH,),
        in_specs=[pl.BlockSpec((ROWS16, D), lambda l: (0, 0)), pl.BlockSpec((1, D, ADA_SHARD), lambda l: (l, 0, 0)),
                  pl.BlockSpec((1, 1, ADA_SHARD), lambda l: (l, 0, 0))],
        out_specs=pl.BlockSpec((1, ROWS16, ADA_SHARD), lambda l: (l, 0, 0)),
        out_shape=SDS((DEPTH, ROWS16, ADA_SHARD), f32), compiler_params=_cp(40))(c_all, ada_w, ada_b_mine)


def ada_bwd(c_all, dmod):
    def body(c_ref, d_ref, o_ref):
        o_ref[0] = lax.dot_general(_silu(c_ref[...]).astype(bf16), d_ref[0].astype(bf16), _DIMS["tn"],
                                   preferred_element_type=f32)

    return pl.pallas_call(
        body, name="ada_bwd", grid=(DEPTH,),
        in_specs=[pl.BlockSpec((ROWS16, D), lambda l: (0, 0)), pl.BlockSpec((1, ROWS16, ADA_SHARD), lambda l: (l, 0, 0))],
        out_specs=pl.BlockSpec((1, D, ADA_SHARD), lambda l: (l, 0, 0)),
        out_shape=SDS((DEPTH, D, ADA_SHARD), f32), compiler_params=_cp(40))(c_all, dmod)


def shift_params(me):
    start = SHARD_W * me
    return jnp.stack([start % LANE, (start + GAP) % LANE, jnp.clip(GAP_COL - start, 0, SHARD_W)]).astype(jnp.int32)


def win_pack(wt, sidx, *, after=()):
    def body(s_ref, w_ref, *rest):
        o_ref, scr = rest[len(after):]
        s1, s2, gi = s_ref[0], s_ref[1], s_ref[2]
        scr[pl.ds(SHARD_W, WIN - SHARD_W), :] = jnp.zeros((WIN - SHARD_W, LANE), f32)
        scr[pl.ds(0, SHARD_W), :] = w_ref[0]
        v = scr[...].T
        j = lax.broadcasted_iota(jnp.int32, v.shape, 1)
        o_ref[0] = jnp.where(j - s1 < gi, pltpu.roll(v, s1, 1),
                             jnp.where(j - s2 >= gi, pltpu.roll(v, s2, 1), 0.0)).astype(bf16)

    return pl.pallas_call(
        body, name="win_pack", grid=(DEPTH, D // LANE),
        in_specs=[pl.BlockSpec(memory_space=pltpu.SMEM), pl.BlockSpec((1, SHARD_W, LANE), lambda l, i: (l, 0, i))]
        + [pl.BlockSpec(memory_space=pl.ANY)] * len(after),
        out_specs=pl.BlockSpec((1, LANE, WIN), lambda l, i: (l, i, 0)),
        out_shape=SDS((DEPTH, D, WIN), bf16), scratch_shapes=[pltpu.VMEM((WIN, LANE), f32)],
        compiler_params=_cp(32))(sidx, wt, *after)


def win_assemble(g, *, after=()):
    own = WIN_STRIDE * LANE
    tail = NP - NDEV * own
    assert tail == 2 * LANE

    def body(a_ref, b_ref, *rest):
        o_ref = rest[len(after)]
        o_ref[...] = a_ref[0]

        @pl.when(pl.program_id(0) > 0)
        def _():
            o_ref[:, pl.ds(0, LANE)] = a_ref[0, :, pl.ds(0, LANE)] + b_ref[0]

    main = pl.pallas_call(
        body, name="win_assemble", grid=(NDEV,),
        in_specs=[pl.BlockSpec((1, D, own), lambda k: (k, 0, 0)),
                  pl.BlockSpec((1, D, LANE), lambda k: (jnp.maximum(k - 1, 0), 0, WIN_BLKS - 1))]
        + [pl.BlockSpec(memory_space=pl.ANY)] * len(after),
        out_specs=pl.BlockSpec((D, own), lambda k: (0, k)),
        out_shape=SDS((D, NP), bf16), compiler_params=_cp(48))(g, g, *after)

    def tail_body(_, b_ref, o_ref):
        o_ref[:, pl.ds(0, LANE)] = b_ref[0]
        o_ref[:, pl.ds(LANE, LANE)] = jnp.zeros((D, LANE), bf16)

    return pl.pallas_call(
        tail_body, name="win_assemble_tail", grid=(1,),
        in_specs=[pl.BlockSpec(memory_space=pl.ANY), pl.BlockSpec((1, D, LANE), lambda t: (NDEV - 1, 0, WIN_BLKS - 1))],
        out_specs=pl.BlockSpec((D, tail), lambda t: (0, NDEV * own // tail)),
        out_shape=SDS((D, NP), bf16), input_output_aliases={0: 0}, compiler_params=_cp(32))(main, g)


def reduce_adamw(stag, w, m, v, l, prev, name, *, sidx=None, tr=128):
    Cs = stag.shape[2]
    n_prev = 0 if prev is None else 4
    n_lead = 1 if sidx is not None else 0

    def body(*refs):
        refs = list(refs)
        s_ref = refs.pop(0) if sidx is not None else None
        g_ref, w_ref, m_ref, v_ref = refs[:4]
        rest = refs[4 + n_prev:]
        go_ref, d_ref, mo_ref, vo_ref = rest[:4]
        tot = g_ref[0].astype(f32)
        for d in range(1, stag.shape[0]):
            tot = tot + g_ref[d].astype(f32)
        if sidx is not None:
            scr = rest[4]
            s1, s2, gi = s_ref[0], s_ref[1], s_ref[2]
            i = lax.broadcasted_iota(jnp.int32, tot.shape, 1)
            scr[...] = jnp.where(i < gi, pltpu.roll(tot, WIN - s1, 1), pltpu.roll(tot, WIN - s2, 1)).T
            tot = scr[pl.ds(0, SHARD_W), :]
        go_ref[0] = tot
        d_ref[0], mo_ref[0], vo_ref[0] = _adam_math(w_ref[0], tot, m_ref[0], v_ref[0])

    if sidx is not None:
        blk3 = pl.BlockSpec((1, SHARD_W, tr), lambda i: (l, 0, i))
        steps = w.shape[2] // tr
    else:
        blk3 = pl.BlockSpec((1, tr, w.shape[2]), lambda i: (l, i, 0))
        steps = w.shape[1] // tr
    in_specs = ([pl.BlockSpec(memory_space=pltpu.SMEM)] * n_lead
                + [pl.BlockSpec((stag.shape[0], tr, Cs), lambda i: (0, i, 0)), blk3, blk3, blk3]
                + [pl.BlockSpec(memory_space=pl.ANY)] * n_prev)
    args = ([sidx] if sidx is not None else []) + [stag, w, m, v] + list(prev or ())
    return pl.pallas_call(
        body, name=name, grid=(steps,), in_specs=in_specs, out_specs=[blk3] * 4, out_shape=[SDS(w.shape, f32)] * 4,
        scratch_shapes=[pltpu.VMEM((Cs, tr), f32)] if sidx is not None else [],
        input_output_aliases={n_lead + 4 + k: k for k in range(n_prev)}, compiler_params=_cp(48),
    )(*args)


def cast_pad(w, cols_out, name):
    L, R, C = w.shape

    def body(w_ref, o_ref, *scr):
        if cols_out == C:
            o_ref[0] = w_ref[0].astype(bf16)
        else:
            scr[0][...] = jnp.zeros_like(scr[0])
            scr[0][:, pl.ds(0, C)] = w_ref[0]
            o_ref[0] = scr[0][...].astype(bf16)

    return pl.pallas_call(
        body, name=name, grid=(L,), in_specs=[pl.BlockSpec((1, R, C), lambda l: (l, 0, 0))],
        out_specs=pl.BlockSpec((1, R, cols_out), lambda l: (l, 0, 0)), out_shape=SDS((L, R, cols_out), bf16),
        scratch_shapes=[] if cols_out == C else [pltpu.VMEM((R, cols_out), f32)], compiler_params=_cp(32))(w)


def pack_qkv(w_uq, w_ukv):
    L = w_uq.shape[0]

    def body(q_ref, kv_ref, o_ref, scr):
        scr[...] = jnp.zeros_like(scr)
        scr[:, pl.ds(0, WUQ_COLS)] = q_ref[0]
        o_ref[0, :, pl.ds(0, QPAD)] = scr[...].astype(bf16)
        o_ref[0, :, pl.ds(QPAD, QPAD)] = kv_ref[0].astype(bf16)

    return pl.pallas_call(
        body, name="pack_qkv", grid=(L,),
        in_specs=[pl.BlockSpec((1, LORA, WUQ_COLS), lambda l: (l, 0, 0)), pl.BlockSpec((1, LORA, QPAD), lambda l: (l, 0, 0))],
        out_specs=pl.BlockSpec((1, LORA, 2 * QPAD), lambda l: (l, 0, 0)), out_shape=SDS((L, LORA, 2 * QPAD), bf16),
        scratch_shapes=[pltpu.VMEM((LORA, QPAD), f32)], compiler_params=_cp(32))(w_uq, w_ukv)


def sum_parts(stag, cols_out, name, *, tr=None, after=()):
    P, R, C = stag.shape
    tr = R if tr is None else tr

    def body(g_ref, *rest):
        o_ref, *scr = rest[len(after):]
        tot = g_ref[0].astype(f32)
        for d in range(1, P):
            tot = tot + g_ref[d].astype(f32)
        if cols_out == C:
            o_ref[...] = tot
        else:
            scr[0][...] = tot
            o_ref[...] = scr[0][:, pl.ds(0, cols_out)]

    return pl.pallas_call(
        body, name=name, grid=(R // tr,),
        in_specs=[pl.BlockSpec((P, tr, C), lambda i: (0, i, 0))] + [pl.BlockSpec(memory_space=pl.ANY)] * len(after),
        out_specs=pl.BlockSpec((tr, cols_out), lambda i: (i, 0)), out_shape=SDS((R, cols_out), f32),
        scratch_shapes=[] if cols_out == C else [pltpu.VMEM((tr, C), f32)], compiler_params=_cp(40))(stag, *after)


MESH_ID = pl.DeviceIdType.MESH
HBM_SPEC = pl.BlockSpec(memory_space=pltpu.HBM)


def _place():
    return lax.axis_index("x"), lax.axis_index("y"), lax.axis_index("c")


def all_gather(arrs, name):
    n = len(arrs)

    def body(*refs):
        ins, outs = refs[:n], refs[n:2 * n]
        send_sems, recv_sems, local_sems = refs[2 * n:]
        x, y, c = _place()
        me, sibling = (x, y, c), (x, y, 1 - c)
        chips = [(1 - x, y), (x, 1 - y), (1 - x, 1 - y)]

        def copy(a, k, block, to, src=None):
            slot = outs[a].at[4 * block[0] + 2 * block[1] + block[2]]
            return pltpu.make_async_remote_copy(
                src_ref=slot if src is None else src, dst_ref=slot, send_sem=send_sems.at[7 * a + k],
                recv_sem=recv_sems.at[7 * a + k], device_id=to, device_id_type=MESH_ID)

        mine = [pltpu.make_async_copy(ins[a], outs[a].at[4 * x + 2 * y + c], local_sems.at[a]) for a in range(n)]
        for cp in mine:
            cp.start()
        first = []
        for a in range(n):
            first.append(copy(a, 0, me, sibling, src=ins[a]))
            first += [copy(a, 1 + j, me, (*chip, c), src=ins[a]) for j, chip in enumerate(chips)]
        for cp in first:
            cp.start()
        passed = []
        for j, chip in enumerate(chips):
            for a in range(n):
                copy(a, 1 + j, (*chip, c), me).wait_recv()
                cp = copy(a, 4 + j, (*chip, c), sibling)
                cp.start()
                passed.append(cp)
        for a in range(n):
            copy(a, 0, sibling, me).wait_recv()
        for j, chip in enumerate(chips):
            for a in range(n):
                copy(a, 4 + j, (*chip, 1 - c), me).wait_recv()
        for cp in first + passed:
            cp.wait_send()
        for cp in mine:
            cp.wait()

    return pl.pallas_call(
        body, name=name, in_specs=[HBM_SPEC] * n, out_specs=[HBM_SPEC] * n,
        out_shape=[SDS((NDEV,) + a.shape, a.dtype) for a in arrs],
        scratch_shapes=[pltpu.SemaphoreType.DMA((7 * n,)), pltpu.SemaphoreType.DMA((7 * n,)),
                        pltpu.SemaphoreType.DMA((n,))],
    )(*arrs)


AG_COLLECTIVE_ID = 0
RS_COLLECTIVE_ID = 1


def _everyone_else(x, y, c):
    return [(x ^ (r >> 2), y ^ ((r >> 1) & 1), c ^ (r & 1)) for r in range(1, NDEV)]


def _rendezvous(sem, peers):
    for peer in peers:
        pl.semaphore_signal(sem, inc=1, device_id=peer, device_id_type=MESH_ID)
    pl.semaphore_wait(sem, len(peers))


def _sequencer_call(body, arrs, out_types, name, collective_id, remote=7, local=1):
    n = len(arrs)
    return pl.kernel(
        body, name=name, out_type=out_types, mesh=plsc.ScalarSubcoreMesh(axis_name="sequencer", num_cores=1),
        scratch_types=[pltpu.SemaphoreType.DMA((remote * n,)), pltpu.SemaphoreType.DMA((remote * n,)),
                       pltpu.SemaphoreType.DMA((local * n,)), pltpu.SemaphoreType.REGULAR],
        compiler_params=pltpu.CompilerParams(collective_id=collective_id),
    )(*arrs)


def seq_all_gather(arrs, name):
    n = len(arrs)

    def body(*refs):
        ins, outs = refs[:n], refs[n:2 * n]
        send_sems, recv_sems, local_sems, exit_sem = refs[2 * n:]
        x, y, c = _place()
        peers = _everyone_else(x, y, c)
        _rendezvous(pltpu.get_barrier_semaphore(), peers)
        me, sibling = (x, y, c), (x, y, 1 - c)
        chips = [(1 - x, y), (x, 1 - y), (1 - x, 1 - y)]

        def copy(a, k, block, to, src=None):
            slot = outs[a].at[4 * block[0] + 2 * block[1] + block[2]]
            return pltpu.make_async_remote_copy(
                src_ref=slot if src is None else src, dst_ref=slot, send_sem=send_sems.at[7 * a + k],
                recv_sem=recv_sems.at[7 * a + k], device_id=to, device_id_type=MESH_ID)

        mine = [pltpu.make_async_copy(ins[a], outs[a].at[4 * x + 2 * y + c], local_sems.at[a]) for a in range(n)]
        for cp in mine:
            cp.start()
        first = []
        for a in range(n):
            first.append(copy(a, 0, me, sibling, src=ins[a]))
            first += [copy(a, 1 + j, me, (*chip, c), src=ins[a]) for j, chip in enumerate(chips)]
        for cp in first:
            cp.start()
        passed = []
        for j, chip in enumerate(chips):
            for a in range(n):
                copy(a, 1 + j, (*chip, c), me).wait_recv()
                cp = copy(a, 4 + j, (*chip, c), sibling)
                cp.start()
                passed.append(cp)
        for a in range(n):
            copy(a, 0, sibling, me).wait_recv()
        for j, chip in enumerate(chips):
            for a in range(n):
                copy(a, 4 + j, (*chip, 1 - c), me).wait_recv()
        for cp in first + passed:
            cp.wait_send()
        for cp in mine:
            cp.wait()
        _rendezvous(exit_sem, peers)

    return _sequencer_call(body, arrs, [SDS((NDEV,) + a.shape, a.dtype) for a in arrs], name, AG_COLLECTIVE_ID)


def seq_reduce_scatter_parts(arrs, pick, shapes, name):
    n = len(arrs)

    def body(*refs):
        ins, outs = refs[:n], refs[n:2 * n]
        send_sems, recv_sems, local_sems, exit_sem = refs[2 * n:]
        x, y, c = _place()
        peers = _everyone_else(x, y, c)
        _rendezvous(pltpu.get_barrier_semaphore(), peers)
        me = 4 * x + 2 * y + c
        mine = [pltpu.make_async_copy(pick[a](ins[a], me), outs[a].at[me], local_sems.at[a]) for a in range(n)]
        for cp in mine:
            cp.start()
        sent = []
        for r, peer in enumerate(peers):
            pid = 4 * peer[0] + 2 * peer[1] + peer[2]
            for a in range(n):
                cp = pltpu.make_async_remote_copy(
                    src_ref=pick[a](ins[a], pid), dst_ref=outs[a].at[me], send_sem=send_sems.at[7 * a + r],
                    recv_sem=recv_sems.at[7 * a + r], device_id=peer, device_id_type=MESH_ID)
                cp.start()
                sent.append((cp, a, r, pid))
        for cp, a, r, pid in sent:
            pltpu.make_async_remote_copy(
                src_ref=pick[a](ins[a], pid), dst_ref=outs[a].at[pid], send_sem=send_sems.at[7 * a + r],
                recv_sem=recv_sems.at[7 * a + r], device_id=(x, y, c), device_id_type=MESH_ID).wait_recv()
        for cp, _, _, _ in sent:
            cp.wait_send()
        for cp in mine:
            cp.wait()
        _rendezvous(exit_sem, peers)

    return _sequencer_call(body, arrs, [SDS((NDEV,) + tuple(s), a.dtype) for s, a in zip(shapes, arrs)], name,
                           RS_COLLECTIVE_ID)


NCHIP = 4
CHIPS_COLLECTIVE_ID = 2


def pair_exchange(arrs, pick, shapes, name):
    n = len(arrs)

    def body(*refs):
        ins, mine_o, sib_o = refs[:n], refs[n:2 * n], refs[2 * n:3 * n]
        send_sems, recv_sems, local_sems = refs[3 * n:]
        x, y, c = _place()
        sibling = (x, y, 1 - c)
        local, remote = [], []
        for a in range(n):
            for j in range(NCHIP):
                to_mine, to_sibling = 2 * j + c, 2 * j + 1 - c
                local.append(pltpu.make_async_copy(pick[a](ins[a], to_mine), mine_o[a].at[j], local_sems.at[NCHIP * a + j]))
                remote.append(pltpu.make_async_remote_copy(
                    src_ref=pick[a](ins[a], to_sibling), dst_ref=sib_o[a].at[j], send_sem=send_sems.at[NCHIP * a + j],
                    recv_sem=recv_sems.at[NCHIP * a + j], device_id=sibling, device_id_type=MESH_ID))
        for cp in local + remote:
            cp.start()
        for cp in remote + local:
            cp.wait()

    out_types = [SDS((NCHIP,) + tuple(s), a.dtype) for s, a in zip(shapes, arrs)]
    outs = pl.pallas_call(
        body, name=name, in_specs=[HBM_SPEC] * n, out_specs=[HBM_SPEC] * (2 * n), out_shape=out_types + out_types,
        scratch_shapes=[pltpu.SemaphoreType.DMA((NCHIP * n,))] * 3)(*arrs)
    return outs[:n], outs[n:]


def seq_chip_exchange(arrs, name):
    n = len(arrs)

    def body(*refs):
        ins, outs = refs[:n], refs[n:2 * n]
        send_sems, recv_sems, local_sems, exit_sem = refs[2 * n:]
        x, y, c = _place()
        my_chip = 2 * x + y
        peers = [(1 - x, y, c), (x, 1 - y, c), (1 - x, 1 - y, c)]
        _rendezvous(pltpu.get_barrier_semaphore(), peers)
        local = [pltpu.make_async_copy(ins[a].at[my_chip], outs[a].at[my_chip], local_sems.at[a]) for a in range(n)]
        for cp in local:
            cp.start()
        sent = []
        for k, peer in enumerate(peers):
            chip = 2 * peer[0] + peer[1]
            for a in range(n):
                cp = pltpu.make_async_remote_copy(
                    src_ref=ins[a].at[chip], dst_ref=outs[a].at[my_chip], send_sem=send_sems.at[3 * a + k],
                    recv_sem=recv_sems.at[3 * a + k], device_id=peer, device_id_type=MESH_ID)
                cp.start()
                sent.append((cp, a, k, chip))
        for cp, a, k, chip in sent:
            pltpu.make_async_remote_copy(
                src_ref=ins[a].at[chip], dst_ref=outs[a].at[chip], send_sem=send_sems.at[3 * a + k],
                recv_sem=recv_sems.at[3 * a + k], device_id=(x, y, c), device_id_type=MESH_ID).wait_recv()
        for cp, _, _, _ in sent:
            cp.wait_send()
        for cp in local:
            cp.wait()
        _rendezvous(exit_sem, peers)

    return _sequencer_call(body, arrs, [SDS(a.shape, a.dtype) for a in arrs], name, CHIPS_COLLECTIVE_ID, remote=3)


def pair_add(mine, theirs, *, tr=128):
    _, R, C = mine.shape
    tr = min(tr, R)

    def body(a_ref, b_ref, o_ref):
        o_ref[...] = (a_ref[...].astype(f32) + b_ref[...].astype(f32)).astype(bf16)

    blk = pl.BlockSpec((1, tr, C), lambda j, i: (j, i, 0))
    return pl.pallas_call(body, name="pair_add", grid=(NCHIP, R // tr), in_specs=[blk, blk], out_specs=blk,
                          out_shape=SDS(mine.shape, bf16), compiler_params=_cp(32))(mine, theirs)


WEIGHTS = ("ada_w", "ada_b", "norm_pre", "norm_post", "w_in", "ret_gn", "lru_conv_w", "lru_conv_b", "lru_wa", "lru_ba",
           "lru_wx", "lru_bx", "lru_lambda", "mla_q_norm", "mla_w_uq", "mla_kv_norm", "mla_w_ukv", "w_branch", "w_out")
SMALL = ("norm_pre", "norm_post", "ret_gn", "lru_conv_w", "lru_conv_b", "lru_ba", "lru_bx", "lru_lambda", "mla_q_norm",
         "mla_kv_norm")
QKV_ROWS = LORA
QKV_BLOCK = (LORA + LANE, 2 * QPAD)
BR_ROWS = 3 * BW // NDEV
OUT_ROWS = D // NDEV
WUQ_COLS = 192


def _row(v):
    return v.reshape(1, -1)


def layer_fwd(xl, mod, p, wts, tabs, lgam):
    S = xl.shape[0]
    tm = min(S, 2048)
    sh, sc, rg = _row(mod[:D]), _row(mod[D:2 * D]), _row(mod[2 * D:])
    ret_tabs, mla_tabs = tabs
    h = pre_fwd(xl, _row(p["norm_pre"]), sc, sh)
    proj = matmul(h, wts["w_in"], dims="nn", M=S, N=NP, K=D, tm=tm, tn=768, tk=D, out_dtype=f32, name="mm_in")
    o_ret, y_ret = ret_attn_fwd(proj, ret_tabs, lgam, _row(p["ret_gn"]))
    h_lru, y_lru = lru_fwd(proj, p["conv_w"], _row(p["lru_conv_b"]), p["lru_wa"], _row(p["lru_ba"]), p["lru_wx"],
                           _row(p["lru_bx"]), _row(p["lru_lambda"]))
    q256, k256, vm = mla_prep_fwd(proj, _row(p["mla_q_norm"]), _row(p["mla_kv_norm"]), wts["w_qkv"], wts["w_qkv"], mla_tabs,
                                  kv_blk=1)
    o_mla, y_mla = mla_attn_fwd(q256, k256, vm, proj)
    ys = (y_ret, y_lru, y_mla)
    us = [matmul(ys[b], wts["w_branch"], dims="nn", M=S, N=D, K=BW, tm=min(S, 1024), tn=1024, tk=BW,
                 out_dtype=bf16, name="mm_branch", b_blk0=(b, 0)) for b in range(3)]
    merged = gate_fwd(proj, us)
    y, x_next = out_fwd(merged, wts["w_out"], xl, rg, _row(p["norm_post"]))
    saved = dict(x=xl, h=h, proj=proj, o_ret=o_ret, h_lru=h_lru, q256=q256, k256=k256, vm=vm, o_mla=o_mla,
                 ys=ys, us=us, merged=merged, y=y, sc=sc, rg=rg)
    return x_next, saved


def layer_bwd(dx, sv, p, wts, tabs, lgam, exchange):
    S = dx.shape[0]
    tm = min(S, 2048)
    ret_tabs, mla_tabs = tabs
    proj = sv["proj"]
    dy, d_rg, d_gpost = out_bwd(dx, sv["y"], sv["rg"], _row(p["norm_post"]))
    dmerged = matmul(dy, wts["w_out"], dims="nt", M=S, N=D, K=D, tm=min(S, 1024), tn=1024, tk=D, out_dtype=f32, name="mm_dmerged")
    dw_out = matmul(sv["merged"], dy, dims="tn", M=D, N=D, K=S, tm=1024, tn=1024, tk=min(S, 1024), out_dtype=bf16, name="mm_dwout")
    du, dml = gate_bwd(dmerged, proj, sv["us"])
    tmb, tkb = min(S, 1024), min(S, 1024)
    dys = [matmul(du[b], wts["w_branch"], dims="nt", M=S, N=BW, K=D, tm=tmb, tn=BW, tk=D, out_dtype=f32, name="mm_dybranch",
                  b_blk0=(b, 0)) for b in range(3)]
    dw_branch = jnp.concatenate(
        [matmul(sv["ys"][b], du[b], dims="tn", M=BW, N=D, K=S, tm=BW, tn=1024, tk=tkb, out_dtype=bf16, name="mm_dwbranch")
         for b in range(3)], axis=0)
    do, d_rgate, d_gn = ret_post_bwd(dys[0], sv["o_ret"], proj, _row(p["ret_gn"]))
    d_q, d_k, d_v = ret_attn_bwd(proj, do, ret_tabs, lgam)
    d_lx, d_lg, d_cw, d_cb, d_ba, d_bx, d_lam, d_wa, d_wx = lru_bwd(
        dys[1], sv["h_lru"], proj, p["conv_w"], _row(p["lru_conv_b"]), p["lru_wa"], _row(p["lru_ba"]), p["lru_wx"],
        _row(p["lru_bx"]), _row(p["lru_lambda"]))
    do, d_mg = mla_post_bwd(dys[2], sv["o_mla"], proj)
    dq256, dk256, dvm = mla_attn_bwd(sv["q256"], sv["k256"], sv["vm"], do, sv["o_mla"])
    d_lat, dw_uq, dw_ukv, d_qn, d_kvn = mla_prep_bwd(dq256, dk256, dvm, proj, _row(p["mla_q_norm"]), _row(p["mla_kv_norm"]),
                                                       wts["w_qkv"], wts["w_qkv"], mla_tabs, kv_blk=1)
    dproj = jnp.concatenate([d_q, d_k, d_v, d_rgate, d_lx, d_lg, d_lat, d_mg, *dml, jnp.zeros((S, LANE), bf16)], axis=1)
    dh = matmul(dproj, wts["w_in"], dims="nt", M=S, N=D, K=NP, tm=min(S, 1024), tn=1024, tk=NP // 6, out_dtype=f32, name="mm_dh")
    dw_in = matmul(sv["h"], dproj, dims="tn", M=D, N=NP, K=S, tm=D, tn=768, tk=tm, out_dtype=bf16, name="mm_dwin")
    dw_qkv = jnp.concatenate([jnp.concatenate([dw_uq, dw_ukv], axis=2),
                              jnp.concatenate([d_wa, d_wx, jnp.zeros((HEADS, LANE, QPAD), f32)], axis=2)], axis=1).astype(bf16)
    staged, launched_from = exchange(dict(w_in=dw_in, w_branch=dw_branch, w_out=dw_out, w_qkv=dw_qkv))
    dxl, d_sh, d_sc, d_gpre = pre_bwd(dh, sv["x"], _row(p["norm_pre"]), sv["sc"], dx, after=tuple(launched_from))
    dmod = jnp.concatenate([d_sh, d_sc, d_rg], axis=1).reshape(-1)
    small = dict(norm_pre=d_gpre, norm_post=d_gpost, ret_gn=d_gn, lru_conv_w=d_cw, lru_conv_b=d_cb, lru_ba=d_ba,
                 lru_bx=d_bx, lru_lambda=d_lam, mla_q_norm=d_qn, mla_kv_norm=d_kvn)
    return dxl, dmod, staged, small


def exchange_grads(big, l, *, via_chips=False):
    picks = [lambda r, d: r.at[:, pl.ds(pl.multiple_of(d * (WIN_STRIDE * LANE), LANE), WIN)],
             lambda r, d: r.at[pl.ds(pl.multiple_of(d * BR_ROWS, 8), BR_ROWS), :],
             lambda r, d: r.at[pl.ds(pl.multiple_of(d * OUT_ROWS, 8), OUT_ROWS), :],
             lambda r, d: r.at[d]]
    shapes = [(D, WIN), (BR_ROWS, D), (OUT_ROWS, D), QKV_BLOCK]
    arrs = [big["w_in"], big["w_branch"], big["w_out"], big["w_qkv"]]
    if not via_chips:
        return seq_reduce_scatter_parts(arrs, picks, shapes, f"rs_grads_{l}"), arrs
    mine, theirs = pair_exchange(arrs, picks, shapes, f"rs_pair_{l}")
    sums = [pair_add(m, t) for m, t in zip(mine, theirs)]
    return seq_chip_exchange(sums, f"rs_chips_{l}"), sums


def kernel(x, c, positions, ada_w, ada_b, norm_pre, norm_post, w_in, ret_gn, lru_conv_w, lru_conv_b, lru_wa, lru_ba, lru_wx, lru_bx, lru_lambda, mla_q_norm, mla_w_uq, mla_kv_norm, mla_w_ukv, w_branch, w_out, loss_target, m_ada_w, m_ada_b, m_norm_pre, m_norm_post, m_w_in, m_ret_gn, m_lru_conv_w, m_lru_conv_b, m_lru_wa, m_lru_ba, m_lru_wx, m_lru_bx, m_lru_lambda, m_mla_q_norm, m_mla_w_uq, m_mla_kv_norm, m_mla_w_ukv, m_w_branch, m_w_out, v_ada_w, v_ada_b, v_norm_pre, v_norm_post, v_w_in, v_ret_gn, v_lru_conv_w, v_lru_conv_b, v_lru_wa, v_lru_ba, v_lru_wx, v_lru_bx, v_lru_lambda, v_mla_q_norm, v_mla_w_uq, v_mla_kv_norm, v_mla_w_ukv, v_w_branch, v_w_out):
    given = dict(locals())
    xi, yi, ci = _place()
    me = 4 * xi + 2 * yi + ci
    S = x.shape[1]
    sidx = shift_params(me)
    lgam = jnp.asarray(np.log1p(-np.exp2(-5.0 - np.arange(HEADS))), f32)
    tabs = rope_tables(positions[0])

    (g_small,) = all_gather([jnp.concatenate([c.reshape(16, LANE), lru_conv_w.reshape(16, LANE)], axis=0)], "ag_small")
    c16 = jnp.concatenate([g_small[:, :16].reshape(NDEV, D), jnp.zeros((ROWS16 - NDEV, D), f32)], axis=0)
    conv_w_all = g_small[:, 16:].reshape(NDEV, DEPTH, 4, LANE).transpose(1, 2, 0, 3).reshape(DEPTH, 4, BW)
    ada_b_mine = lax.dynamic_slice_in_dim(ada_b, me * ADA_SHARD, ADA_SHARD, axis=1).reshape(DEPTH, 1, ADA_SHARD)
    (g_mod,) = all_gather([ada_fwd(c16, ada_w, ada_b_mine)[:, :NDEV]], "ag_mod")
    mods = lax.dynamic_index_in_dim(g_mod, me, axis=2, keepdims=False).transpose(1, 0, 2).reshape(DEPTH, 3 * D)
    w_in_t = {n: jnp.swapaxes(given[n], 1, 2) for n in ("w_in", "m_w_in", "v_w_in")}
    packed = (win_pack(w_in_t["w_in"], sidx, after=(g_mod,)), cast_pad(w_branch, D, "pack_wbranch"), cast_pad(w_out, D, "pack_wout"),
              pack_qkv(mla_w_uq, mla_w_ukv))
    first, rest = seq_all_gather([packed[0][0], packed[3][0]], "ag_weights_0a"), seq_all_gather([packed[1][0], packed[2][0]], "ag_weights_0b")
    gathered = [(first[0], rest[0], rest[1], first[1])]
    gathered += [seq_all_gather([t[l] for t in packed], f"ag_weights_{l}") for l in range(1, DEPTH)]

    params, wts = [], []
    for l in range(DEPTH):
        p = {n: given[n][l] for n in SMALL + ("lru_wa", "lru_wx") if n != "lru_conv_w"}
        p["conv_w"] = conv_w_all[l]
        params.append(p)

    xl, saved = x[0], []
    for l in range(DEPTH):
        g_win, g_br, g_out, g_qkv = gathered[l]
        wts.append(dict(w_in=win_assemble(g_win, after=(xl,)), w_qkv=g_qkv, w_branch=g_br.reshape(3 * BW, D),
                        w_out=g_out.reshape(D, D)))
        xl, sv = layer_fwd(xl, mods[l], params[l], wts[l], tabs, lgam)
        saved.append(sv)
    my_loss, dx = loss_head(xl, loss_target[0])
    loss = lax.psum(my_loss[0, 0], ("x", "y", "c"))

    dmods, smalls, staged, grads = [None] * DEPTH, [None] * DEPTH, [None] * DEPTH, {n: [None] * DEPTH for n in WEIGHTS}
    for l in reversed(range(DEPTH)):
        dx, dmods[l], staged[l], smalls[l] = layer_bwd(dx, saved[l], params[l], wts[l], tabs, lgam,
                                                       functools.partial(exchange_grads, l=l, via_chips=(l == 0)))
    chained, lru_blocks = {"w_in": None, "w_branch": None, "w_out": None}, [None] * DEPTH
    for l in reversed(range(DEPTH)):
        st_win, st_br, st_out, st_qkv = staged[l]
        chained["w_in"] = reduce_adamw(st_win, w_in_t["w_in"], w_in_t["m_w_in"], w_in_t["v_w_in"], l, chained["w_in"],
                                       "update_w_in", sidx=sidx)
        for n, st in (("w_branch", st_br), ("w_out", st_out)):
            chained[n] = reduce_adamw(st, given[n], given["m_" + n], given["v_" + n], l, chained[n], "update_" + n)
        g_qkv = sum_parts(st_qkv, 2 * QPAD, "sum_wqkv", after=(chained["w_in"][0],) if l == 0 else ())
        grads["mla_w_uq"][l] = g_qkv[:QKV_ROWS, :WUQ_COLS]
        grads["mla_w_ukv"][l] = g_qkv[:QKV_ROWS, QPAD:]
        lru_blocks[l] = g_qkv[QKV_ROWS:, :2 * LANE]
    (g_lru,) = all_gather([jnp.stack(lru_blocks)], "ag_lru_w")
    grads["lru_wa"] = g_lru[..., :LANE].transpose(1, 0, 2, 3)
    grads["lru_wx"] = g_lru[..., LANE:].transpose(1, 0, 2, 3)

    flat = [jnp.stack(dmods).reshape(-1)] + [smalls[l][n].reshape(-1) for l in range(DEPTH) for n in SMALL]
    sizes = [int(t.shape[0]) for t in flat]
    (g_pack,) = all_gather([jnp.concatenate(flat).reshape(-1, LANE)], "ag_small_grads")
    rows = g_pack.shape[1]
    tot = sum_parts(g_pack, LANE, "sum_small_grads", tr=rows // 8).reshape(-1)
    offs = np.concatenate([[0], np.cumsum(sizes)])
    pieces = [tot[int(offs[i]):int(offs[i + 1])] for i in range(len(sizes))]
    grads["ada_b"] = pieces[0].reshape(DEPTH, 3 * D)
    for l in range(DEPTH):
        for j, n in enumerate(SMALL):
            piece = pieces[1 + l * len(SMALL) + j]
            if n == "lru_conv_w":
                piece = lax.dynamic_slice_in_dim(piece.reshape(4, BW), me * LANE, LANE, axis=1)
            grads[n][l] = piece.reshape(given[n].shape[1:])
    dmod_all = g_pack[:, :DEPTH * 3 * D // LANE].reshape(NDEV, DEPTH, 3 * D)
    dmod_mine = lax.dynamic_slice_in_dim(dmod_all, me * ADA_SHARD, ADA_SHARD, axis=2).transpose(1, 0, 2)
    dmod16 = jnp.concatenate([dmod_mine, jnp.zeros((DEPTH, ROWS16 - NDEV, ADA_SHARD), f32)], axis=1)
    grads["ada_w"] = ada_bwd(c16, dmod16)

    outs = {"grad": [], "delta": [], "m": [], "v": []}
    for n in WEIGHTS:
        if n in chained:
            g, delta, new_m, new_v = (jnp.swapaxes(t, 1, 2) for t in chained[n]) if n == "w_in" else chained[n]
        else:
            g = grads[n] if not isinstance(grads[n], list) else jnp.stack(grads[n])
            delta, new_m, new_v = adamw(given[n], g, given["m_" + n], given["v_" + n])
        outs["grad"].append(g)
        outs["delta"].append(delta)
        outs["m"].append(new_m)
        outs["v"].append(new_v)
    return (loss, dx[None], *outs["grad"], *outs["delta"], *outs["m"], *outs["v"])
```

```python
import functools
import math

import numpy as np
import jax
import jax.numpy as jnp
from jax import lax
from jax.experimental import pallas as pl
from jax.experimental.pallas import tpu as pltpu
from jax.experimental.pallas import tpu_sc as plsc

f32 = jnp.float32
bf16 = jnp.bfloat16
SDS = jax.ShapeDtypeStruct

DEPTH = 4
D = 2048
HEADS = 8
DH = 128
BW = HEADS * DH
LANE = 128
CHUNK = 64
EPS = 1e-6
LRU_C = 8.0
NDEV = 8
VMEM_V7X = 64 * 1024 * 1024

Q0, K0, V0, RG0, LX0, LG0, MQ0, MKV0, MKR0, MG0, ML0 = 0, 8, 16, 24, 32, 40, 48, 52, 56, 57, 65
NB = 114
NP = NB * LANE
IN_W = 14400
SHARD_W = IN_W // NDEV
GAP_COL = 7232
GAP = 64
WIN = 1920
WIN_BLKS = WIN // LANE
WIN_STRIDE = 14
LORA = 512
QPAD = 256

ADAM_LR, ADAM_B1, ADAM_B2, ADAM_EPS, ADAM_WD, ADAM_STEP = 0.001, 0.9, 0.999, 1e-08, 0.01, 10
ADAM_BLOCK_ELEMS = 256 * 1024
MLA_SCALE = (128 + 64) ** -0.5
RET_SCALE = 128 ** -0.5


def _cp(vmem_mb=None, **kw):
    if vmem_mb is not None:
        kw["vmem_limit_bytes"] = min(vmem_mb * 1024 * 1024, VMEM_V7X - 8 * 1024 * 1024)
    return pltpu.CompilerParams(**kw)


def _sigmoid(x):
    return 1.0 / (1.0 + jnp.exp(-x))


def _silu(x):
    return x * _sigmoid(x)


def _dsilu(x):
    s = _sigmoid(x)
    return s * (1.0 + x * (1.0 - s))


def _softplus(x):
    return jnp.maximum(x, 0.0) + jnp.log(1.0 + jnp.exp(-jnp.abs(x)))


def _one_minus_exp(y):
    series = -y * (1.0 + y * (0.5 + y * (1.0 / 6.0)))
    return jnp.where(y > -1e-2, series, 1.0 - jnp.exp(y))


def _acc(ref, val, first):
    @pl.when(first)
    def _():
        ref[...] = val

    @pl.when(jnp.logical_not(first))
    def _():
        ref[...] += val


_DIMS = {"nn": (((1,), (0,)), ((), ())), "nt": (((1,), (1,)), ((), ())), "tn": (((0,), (0,)), ((), ()))}


def matmul(a, b, *, dims, M, N, K, tm, tn, tk, out_dtype, name, a_blk0=(0, 0), b_blk0=(0, 0), vmem_mb=48):
    nk = K // tk
    assert M % tm == 0 and N % tn == 0 and K % tk == 0
    dn = _DIMS[dims]

    def body(a_ref, b_ref, o_ref, *scr):
        part = lax.dot_general(a_ref[...].astype(bf16), b_ref[...].astype(bf16), dn, preferred_element_type=f32)
        if nk == 1:
            o_ref[...] = part.astype(out_dtype)
        else:
            acc = scr[0]
            k = pl.program_id(2)
            _acc(acc, part, k == 0)

            @pl.when(k == nk - 1)
            def _():
                o_ref[...] = acc[...].astype(out_dtype)

    ar, ac = a_blk0
    br, bc = b_blk0
    if dims == "nn":
        a_spec = pl.BlockSpec((tm, tk), lambda i, j, k: (i + ar, k + ac))
        b_spec = pl.BlockSpec((tk, tn), lambda i, j, k: (k + br, j + bc))
    elif dims == "nt":
        a_spec = pl.BlockSpec((tm, tk), lambda i, j, k: (i + ar, k + ac))
        b_spec = pl.BlockSpec((tn, tk), lambda i, j, k: (j + br, k + bc))
    else:
        a_spec = pl.BlockSpec((tk, tm), lambda i, j, k: (k + ar, i + ac))
        b_spec = pl.BlockSpec((tk, tn), lambda i, j, k: (k + br, j + bc))
    return pl.pallas_call(
        body, name=name, grid=(M // tm, N // tn, nk),
        in_specs=[a_spec, b_spec], out_specs=pl.BlockSpec((tm, tn), lambda i, j, k: (i, j)),
        out_shape=SDS((M, N), out_dtype),
        scratch_shapes=[] if nk == 1 else [pltpu.VMEM((tm, tn), f32)],
        compiler_params=_cp(vmem_mb, dimension_semantics=("parallel", "parallel", "arbitrary")),
    )(a, b)


def pre_fwd(x, g, sc, sh, *, tm=256):
    S = x.shape[0]

    def body(x_ref, g_ref, sc_ref, sh_ref, h_ref):
        xv = x_ref[...]
        r = lax.rsqrt(jnp.mean(xv * xv, axis=-1, keepdims=True) + EPS)
        h_ref[...] = (((xv * r) * g_ref[...]) * (1.0 + sc_ref[...]) + sh_ref[...]).astype(bf16)

    row = pl.BlockSpec((tm, D), lambda i: (i, 0))
    vec = pl.BlockSpec((1, D), lambda i: (0, 0))
    return pl.pallas_call(body, name="pre_fwd", grid=(S // tm,), in_specs=[row, vec, vec, vec], out_specs=row,
                          out_shape=SDS((S, D), bf16), compiler_params=_cp(32))(x, g, sc, sh)


def pre_bwd(dh, x, g, sc, dxo, *, tm=256, after=()):
    S = x.shape[0]

    def body(dh_ref, x_ref, g_ref, sc_ref, dxo_ref, *rest):
        dx_ref, dsh_ref, dsc_ref, dg_ref = rest[len(after):]
        first = pl.program_id(0) == 0
        xv, dhv, gv = x_ref[...], dh_ref[...], g_ref[...]
        one_sc = 1.0 + sc_ref[...]
        r = lax.rsqrt(jnp.mean(xv * xv, axis=-1, keepdims=True) + EPS)
        xh = xv * r
        t = dhv * xh
        dxh = dhv * gv * one_sc
        dx_ref[...] = r * (dxh - xh * jnp.mean(dxh * xh, axis=-1, keepdims=True)) + dxo_ref[...]
        _acc(dsh_ref, jnp.sum(dhv, axis=0, keepdims=True), first)
        _acc(dsc_ref, jnp.sum(t * gv, axis=0, keepdims=True), first)
        _acc(dg_ref, jnp.sum(t * one_sc, axis=0, keepdims=True), first)

    row = pl.BlockSpec((tm, D), lambda i: (i, 0))
    vec = pl.BlockSpec((1, D), lambda i: (0, 0))
    return pl.pallas_call(
        body, name="pre_bwd", grid=(S // tm,),
        in_specs=[row, row, vec, vec, row] + [pl.BlockSpec(memory_space=pl.ANY)] * len(after), out_specs=[row, vec, vec, vec],
        out_shape=[SDS((S, D), f32), SDS((1, D), f32), SDS((1, D), f32), SDS((1, D), f32)],
        compiler_params=_cp(40))(dh, x, g, sc, dxo, *after)


def out_fwd(merged, w_out, x, rg, gp, *, tm=256):
    S = x.shape[0]

    def body(m_ref, w_ref, x_ref, rg_ref, gp_ref, y_ref, xn_ref):
        y = jnp.dot(m_ref[...], w_ref[...], preferred_element_type=f32)
        y_ref[...] = y
        r = lax.rsqrt(jnp.mean(y * y, axis=-1, keepdims=True) + EPS)
        xn_ref[...] = x_ref[...] + (1.0 + rg_ref[...]) * ((y * r) * gp_ref[...])

    row = pl.BlockSpec((tm, D), lambda i: (i, 0))
    vec = pl.BlockSpec((1, D), lambda i: (0, 0))
    return pl.pallas_call(
        body, name="out_fwd", grid=(S // tm,),
        in_specs=[row, pl.BlockSpec((D, D), lambda i: (0, 0)), row, vec, vec], out_specs=[row, row],
        out_shape=[SDS((S, D), f32), SDS((S, D), f32)], compiler_params=_cp(48))(merged, w_out, x, rg, gp)


def out_bwd(dxo, y, rg, gp, *, tm=256):
    S = y.shape[0]

    def body(dxo_ref, y_ref, rg_ref, gp_ref, dy_ref, drg_ref, dgp_ref):
        first = pl.program_id(0) == 0
        yv, dv, gv = y_ref[...], dxo_ref[...], gp_ref[...]
        r = lax.rsqrt(jnp.mean(yv * yv, axis=-1, keepdims=True) + EPS)
        yh = yv * r
        dn = dv * (1.0 + rg_ref[...])
        dyh = dn * gv
        dy_ref[...] = (r * (dyh - yh * jnp.mean(dyh * yh, axis=-1, keepdims=True))).astype(bf16)
        _acc(drg_ref, jnp.sum(dv * (yh * gv), axis=0, keepdims=True), first)
        _acc(dgp_ref, jnp.sum(dn * yh, axis=0, keepdims=True), first)

    row = pl.BlockSpec((tm, D), lambda i: (i, 0))
    vec = pl.BlockSpec((1, D), lambda i: (0, 0))
    return pl.pallas_call(
        body, name="out_bwd", grid=(S // tm,), in_specs=[row, row, vec, vec], out_specs=[row, vec, vec],
        out_shape=[SDS((S, D), bf16), SDS((1, D), f32), SDS((1, D), f32)], compiler_params=_cp(40))(dxo, y, rg, gp)


def _ml_spec(b, tm):
    return pl.BlockSpec((tm, LANE), lambda i, j: (i, ML0 + b * (D // LANE) + j))


def gate_fwd(proj, us, *, tm=2048):
    S = proj.shape[0]
    tm = min(tm, S)

    def body(ml0, ml1, ml2, u0, u1, u2, m_ref):
        acc = None
        for ml, u in ((ml0, u0), (ml1, u1), (ml2, u2)):
            t = _sigmoid(ml[...]) * u[...].astype(f32)
            acc = t if acc is None else acc + t
        m_ref[...] = acc.astype(bf16)

    blk = pl.BlockSpec((tm, LANE), lambda i, j: (i, j))
    return pl.pallas_call(
        body, name="gate_fwd", grid=(S // tm, D // LANE),
        in_specs=[_ml_spec(0, tm), _ml_spec(1, tm), _ml_spec(2, tm), blk, blk, blk], out_specs=blk,
        out_shape=SDS((S, D), bf16), compiler_params=_cp(32),
    )(proj, proj, proj, *us)


def gate_bwd(dmerged, proj, us, *, tm=2048):
    S = proj.shape[0]
    tm = min(tm, S)

    def body(dm_ref, ml0, ml1, ml2, u0, u1, u2, du0, du1, du2, dl0, dl1, dl2):
        dm = dm_ref[...]
        for ml, u, du, dl in ((ml0, u0, du0, dl0), (ml1, u1, du1, dl1), (ml2, u2, du2, dl2)):
            s = _sigmoid(ml[...])
            du[...] = (dm * s).astype(bf16)
            dl[...] = (dm * u[...].astype(f32) * (s * (1.0 - s))).astype(bf16)

    blk = pl.BlockSpec((tm, LANE), lambda i, j: (i, j))
    outs = pl.pallas_call(
        body, name="gate_bwd", grid=(S // tm, D // LANE),
        in_specs=[blk, _ml_spec(0, tm), _ml_spec(1, tm), _ml_spec(2, tm), blk, blk, blk], out_specs=[blk] * 6,
        out_shape=[SDS((S, D), bf16)] * 6, compiler_params=_cp(40),
    )(dmerged, proj, proj, proj, *us)
    return outs[:3], outs[3:]


def rope_tables(positions):
    pos = positions.astype(f32)[:, None]

    def cs(dim):
        inv = 10000.0 ** (-jnp.arange(0, dim, 2, dtype=f32) / dim)
        ang = pos * inv
        return jnp.cos(ang), jnp.sin(ang)

    c, s = cs(128)
    ret = (jnp.concatenate([c, c], 1), jnp.concatenate([-s, s], 1))
    c, s = cs(64)
    z32, z64 = jnp.zeros_like(c), jnp.zeros((c.shape[0], 64), f32)
    mla = (jnp.concatenate([c, c, z64], 1), jnp.concatenate([-s, z32, z64], 1), jnp.concatenate([z32, s, z64], 1))
    return ret, mla


def _rope_ret(x, c, s):
    return x * c + pltpu.roll(x, 64, 1) * s


def _rope_ret_t(dy, c, s):
    return dy * c + pltpu.roll(dy * s, 64, 1)


def _rope_mla(x, c, sa, sb):
    return x * c + pltpu.roll(x, 96, 1) * sa + pltpu.roll(x, 32, 1) * sb


def _rope_mla_t(dy, c, sa, sb):
    return dy * c + pltpu.roll(dy * sa, 32, 1) + pltpu.roll(dy * sb, 96, 1)


def _softmax_rows(q, k):
    bq, nk = q.shape[0], k.shape[0]
    s = lax.dot_general(q, k, _DIMS["nt"], preferred_element_type=f32)
    i = lax.broadcasted_iota(jnp.int32, (bq, bq), 0)
    j = lax.broadcasted_iota(jnp.int32, (bq, bq), 1)
    own = s[:, nk - bq:] + jnp.where((j // CHUNK) <= (i // CHUNK), 0.0, -1e30)
    s = own if nk == bq else jnp.concatenate([s[:, :nk - bq], own], axis=1)
    e = jnp.exp(s - jnp.max(s, axis=-1, keepdims=True))
    return e / jnp.sum(e, axis=-1, keepdims=True)


def _decay_rows(lg, bq, S):
    nq = S // bq
    i = lax.broadcasted_iota(jnp.int32, (bq, S), 0)
    col = lax.broadcasted_iota(jnp.int32, (bq, S), 1)
    j = col % bq
    back = nq - 1 - col // bq
    dist = back * bq + i - j
    seen = jnp.logical_or(back > 0, (j // CHUNK) <= (i // CHUNK))
    return jnp.where(seen, jnp.exp(lg * jnp.abs(dist).astype(f32)), 0.0)


def _per_query_block(nq, fn):
    for qi in range(nq):
        pl.when(pl.program_id(1) == qi)(functools.partial(fn, qi))


def mla_attn_fwd(q, k, v, proj, *, bq=256):
    S = q.shape[0]
    bq = min(bq, S)
    assert bq % CHUNK == 0

    def body(q_ref, k_ref, v_ref, g_ref, o_ref, y_ref):
        def block(qi):
            keys = pl.ds(0, (qi + 1) * bq)
            p = _softmax_rows(q_ref[...], k_ref[keys, :])
            o = jnp.dot(p.astype(bf16), v_ref[keys, :], preferred_element_type=f32)
            o_ref[...] = o
            y_ref[...] = (o * _silu(g_ref[...])).astype(bf16)

        _per_query_block(S // bq, block)

    o_spec = pl.BlockSpec((bq, DH), lambda h, i: (i, h))
    return pl.pallas_call(
        body, name="mla_attn_fwd", grid=(HEADS, S // bq),
        in_specs=[pl.BlockSpec((bq, QPAD), lambda h, i: (i, h)), pl.BlockSpec((S, QPAD), lambda h, i: (0, h)),
                  pl.BlockSpec((S, DH), lambda h, i: (0, h)), pl.BlockSpec((bq, DH), lambda h, i: (i, MG0 + h))],
        out_specs=[o_spec, o_spec], out_shape=[SDS((S, BW), f32), SDS((S, BW), bf16)],
        compiler_params=_cp(48))(q, k, v, proj)


def mla_attn_bwd(q, k, v, do, o, *, bq=256):
    S = q.shape[0]
    bq = min(bq, S)

    def body(q_ref, k_ref, v_ref, do_ref, o_ref, dq_ref, dk_ref, dv_ref):
        def block(qi):
            keys = pl.ds(0, (qi + 1) * bq)
            if qi == 0:
                dk_ref[...] = jnp.zeros_like(dk_ref)
                dv_ref[...] = jnp.zeros_like(dv_ref)
            qv, kv, dov = q_ref[...], k_ref[keys, :], do_ref[...]
            p = _softmax_rows(qv, kv)
            dp = lax.dot_general(dov, v_ref[keys, :], _DIMS["nt"], preferred_element_type=f32)
            delta = jnp.sum(dov.astype(f32) * o_ref[...], axis=-1, keepdims=True)
            dsb = (p * (dp - delta)).astype(bf16)
            dq_ref[...] = jnp.dot(dsb, kv, preferred_element_type=f32)
            dv_ref[keys, :] += lax.dot_general(p.astype(bf16), dov, _DIMS["tn"], preferred_element_type=f32)
            dk_ref[keys, :] += lax.dot_general(dsb, qv, _DIMS["tn"], preferred_element_type=f32)

        _per_query_block(S // bq, block)

    o_spec = pl.BlockSpec((bq, DH), lambda h, i: (i, h))
    return pl.pallas_call(
        body, name="mla_attn_bwd", grid=(HEADS, S // bq),
        in_specs=[pl.BlockSpec((bq, QPAD), lambda h, i: (i, h)), pl.BlockSpec((S, QPAD), lambda h, i: (0, h)),
                  pl.BlockSpec((S, DH), lambda h, i: (0, h)), o_spec, o_spec],
        out_specs=[pl.BlockSpec((bq, QPAD), lambda h, i: (i, h)), pl.BlockSpec((S, QPAD), lambda h, i: (0, h)),
                   pl.BlockSpec((S, DH), lambda h, i: (0, h))],
        out_shape=[SDS((S, HEADS * QPAD), f32), SDS((S, HEADS * QPAD), f32), SDS((S, BW), f32)],
        compiler_params=_cp(52))(q, k, v, do, o)


def ret_attn_fwd(proj, tabs, lgam, gn, *, bq=256):
    S = proj.shape[0]
    bq = min(bq, S)
    assert bq % CHUNK == 0

    def body(lg_ref, q_ref, k_ref, v_ref, g_ref, gn_ref, cq_ref, sq_ref, ck_ref, sk_ref, o_ref, y_ref, k_s, w_s):
        lg = lg_ref[pl.program_id(0)]

        def block(qi):
            if qi == 0:
                k_s[...] = _rope_ret(k_ref[...], ck_ref[...], sk_ref[...]).astype(bf16)
                w_s[...] = _decay_rows(lg, bq, S)
            keys = pl.ds(0, (qi + 1) * bq)
            qv = (_rope_ret(q_ref[...], cq_ref[...], sq_ref[...]) * RET_SCALE).astype(bf16)
            p = lax.dot_general(qv, k_s[keys, :], _DIMS["nt"], preferred_element_type=f32) * w_s[:, pl.ds(S - (qi + 1) * bq, (qi + 1) * bq)]
            o = jnp.dot(p.astype(bf16), v_ref[keys, :].astype(bf16), preferred_element_type=f32)
            o_ref[...] = o
            oc = o - jnp.mean(o, axis=-1, keepdims=True)
            z = oc * lax.rsqrt(jnp.mean(oc * oc, axis=-1, keepdims=True) + EPS) * gn_ref[...]
            y_ref[...] = (z * _silu(g_ref[...])).astype(bf16)

        _per_query_block(S // bq, block)

    row = lambda b0: pl.BlockSpec((bq, DH), lambda h, i: (i, b0 + h))
    full = lambda b0: pl.BlockSpec((S, DH), lambda h, i: (0, b0 + h))
    tq, tk = pl.BlockSpec((bq, DH), lambda h, i: (i, 0)), pl.BlockSpec((S, DH), lambda h, i: (0, 0))
    o_spec = pl.BlockSpec((bq, DH), lambda h, i: (i, h))
    return pl.pallas_call(
        body, name="ret_attn_fwd", grid=(HEADS, S // bq),
        in_specs=[pl.BlockSpec(memory_space=pltpu.SMEM), row(Q0), full(K0), full(V0), row(RG0),
                  pl.BlockSpec((1, DH), lambda h, i: (0, h)), tq, tq, tk, tk],
        out_specs=[o_spec, o_spec], out_shape=[SDS((S, BW), f32), SDS((S, BW), bf16)],
        scratch_shapes=[pltpu.VMEM((S, DH), bf16), pltpu.VMEM((bq, S), f32)], compiler_params=_cp(48),
    )(lgam, proj, proj, proj, proj, gn, tabs[0], tabs[1], tabs[0], tabs[1])


def ret_attn_bwd(proj, do, tabs, lgam, *, bq=256):
    S = proj.shape[0]
    bq = min(bq, S)
    nq = S // bq

    def body(lg_ref, q_ref, k_ref, v_ref, do_ref, cq_ref, sq_ref, ck_ref, sk_ref, dq_ref, dk_ref, dv_ref, k_s, dk_s, dv_s, w_s):
        lg = lg_ref[pl.program_id(0)]

        def block(qi):
            if qi == 0:
                k_s[...] = _rope_ret(k_ref[...], ck_ref[...], sk_ref[...]).astype(bf16)
                w_s[...] = _decay_rows(lg, bq, S)
                dk_s[...] = jnp.zeros_like(dk_s)
                dv_s[...] = jnp.zeros_like(dv_s)
            keys = pl.ds(0, (qi + 1) * bq)
            cq, sq = cq_ref[...], sq_ref[...]
            qv = (_rope_ret(q_ref[...], cq, sq) * RET_SCALE).astype(bf16)
            kv, dov = k_s[keys, :], do_ref[...]
            w = w_s[:, pl.ds(S - (qi + 1) * bq, (qi + 1) * bq)]
            p = lax.dot_general(qv, kv, _DIMS["nt"], preferred_element_type=f32) * w
            dp = lax.dot_general(dov, v_ref[keys, :].astype(bf16), _DIMS["nt"], preferred_element_type=f32)
            dsb = (dp * w).astype(bf16)
            dq_ref[...] = _rope_ret_t(jnp.dot(dsb, kv, preferred_element_type=f32) * RET_SCALE, cq, sq).astype(bf16)
            dv_s[keys, :] += lax.dot_general(p.astype(bf16), dov, _DIMS["tn"], preferred_element_type=f32)
            dk_s[keys, :] += lax.dot_general(dsb, qv, _DIMS["tn"], preferred_element_type=f32)
            if qi == nq - 1:
                dk_ref[...] = _rope_ret_t(dk_s[...], ck_ref[...], sk_ref[...]).astype(bf16)
                dv_ref[...] = dv_s[...].astype(bf16)

        _per_query_block(nq, block)

    row = lambda b0: pl.BlockSpec((bq, DH), lambda h, i: (i, b0 + h))
    full = lambda b0: pl.BlockSpec((S, DH), lambda h, i: (0, b0 + h))
    tq, tk = pl.BlockSpec((bq, DH), lambda h, i: (i, 0)), pl.BlockSpec((S, DH), lambda h, i: (0, 0))
    return pl.pallas_call(
        body, name="ret_attn_bwd", grid=(HEADS, nq),
        in_specs=[pl.BlockSpec(memory_space=pltpu.SMEM), row(Q0), full(K0), full(V0), row(0), tq, tq, tk, tk],
        out_specs=[row(0), full(0), full(0)], out_shape=[SDS((S, BW), bf16)] * 3,
        scratch_shapes=[pltpu.VMEM((S, DH), bf16), pltpu.VMEM((S, DH), f32), pltpu.VMEM((S, DH), f32),
                        pltpu.VMEM((bq, S), f32)],
        compiler_params=_cp(52),
    )(lgam, proj, proj, proj, do, tabs[0], tabs[1], tabs[0], tabs[1])


def ret_post_bwd(dy, o, proj, gn, *, tm=512):
    S = dy.shape[0]
    tm = min(tm, S)

    def body(dy_ref, o_ref, g_ref, gn_ref, do_ref, drg_ref, dgn_ref):
        ov, g, gnv, dyv = o_ref[...], g_ref[...], gn_ref[...], dy_ref[...]
        oc = ov - jnp.mean(ov, axis=-1, keepdims=True)
        rs = lax.rsqrt(jnp.mean(oc * oc, axis=-1, keepdims=True) + EPS)
        oh = oc * rs
        dz = dyv * _silu(g)
        drg_ref[...] = (dyv * (oh * gnv) * _dsilu(g)).astype(bf16)
        doh = dz * gnv
        do_ref[...] = (rs * (doh - jnp.mean(doh, axis=-1, keepdims=True)
                             - oh * jnp.mean(doh * oh, axis=-1, keepdims=True))).astype(bf16)
        _acc(dgn_ref, jnp.sum(dz * oh, axis=0, keepdims=True), pl.program_id(1) == 0)

    blk = pl.BlockSpec((tm, DH), lambda h, i: (i, h))
    vec = pl.BlockSpec((1, DH), lambda h, i: (0, h))
    return pl.pallas_call(
        body, name="ret_post_bwd", grid=(HEADS, S // tm),
        in_specs=[blk, blk, pl.BlockSpec((tm, DH), lambda h, i: (i, RG0 + h)), vec], out_specs=[blk, blk, vec],
        out_shape=[SDS((S, BW), bf16), SDS((S, BW), bf16), SDS((1, BW), f32)], compiler_params=_cp(32))(dy, o, proj, gn)


def mla_post_bwd(dy, o, proj, *, tm=512):
    S = dy.shape[0]
    tm = min(tm, S)

    def body(dy_ref, o_ref, g_ref, do_ref, dg_ref):
        g, dyv = g_ref[...], dy_ref[...]
        do_ref[...] = (dyv * _silu(g)).astype(bf16)
        dg_ref[...] = (dyv * o_ref[...] * _dsilu(g)).astype(bf16)

    blk = pl.BlockSpec((tm, DH), lambda i, h: (i, h))
    return pl.pallas_call(
        body, name="mla_post_bwd", grid=(S // tm, HEADS),
        in_specs=[blk, blk, pl.BlockSpec((tm, DH), lambda i, h: (i, MG0 + h))], out_specs=[blk, blk],
        out_shape=[SDS((S, BW), bf16), SDS((S, BW), bf16)], compiler_params=_cp(32))(dy, o, proj)


def mla_prep_fwd(proj, qnorm, kvnorm, wuq, wukv, tabs, *, tm=256, kv_blk=0):
    S = proj.shape[0]
    tm = min(tm, S)

    def body(mq_ref, mkv_ref, mkr_ref, qn_ref, kvn_ref, wuq_ref, wukv_ref, c_ref, sa_ref, sb_ref, q_ref, k_ref, v_ref):
        c, sa, sb = c_ref[...], sa_ref[...], sb_ref[...]
        mq, mkv = mq_ref[...], mkv_ref[...]
        qn = (mq * lax.rsqrt(jnp.mean(mq * mq, axis=-1, keepdims=True) + EPS) * qn_ref[...]).astype(bf16)
        kvn = (mkv * lax.rsqrt(jnp.mean(mkv * mkv, axis=-1, keepdims=True) + EPS) * kvn_ref[...]).astype(bf16)
        kr = _rope_mla(mkr_ref[...], c, sa, sb).astype(bf16)
        for h in range(HEADS):
            qh = jnp.dot(qn, wuq_ref[h], preferred_element_type=f32)
            q_ref[:, pl.ds(h * QPAD, DH)] = (qh[:, :DH] * MLA_SCALE).astype(bf16)
            q_ref[:, pl.ds(h * QPAD + DH, DH)] = (_rope_mla(qh[:, DH:], c, sa, sb) * MLA_SCALE).astype(bf16)
            kvh = jnp.dot(kvn, wukv_ref[h], preferred_element_type=f32)
            k_ref[:, pl.ds(h * QPAD, DH)] = kvh[:, :DH].astype(bf16)
            k_ref[:, pl.ds(h * QPAD + DH, DH)] = kr
            v_ref[:, pl.ds(h * DH, DH)] = kvh[:, DH:].astype(bf16)

    lat = lambda b: pl.BlockSpec((tm, LORA), lambda i: (i, b))
    tab = pl.BlockSpec((tm, DH), lambda i: (i, 0))
    vec = pl.BlockSpec((1, LORA), lambda i: (0, 0))
    wsp = pl.BlockSpec((HEADS, LORA, QPAD), lambda i: (0, 0, 0))
    wkv = pl.BlockSpec((HEADS, LORA, QPAD), lambda i: (0, 0, kv_blk))
    return pl.pallas_call(
        body, name="mla_prep_fwd", grid=(S // tm,),
        in_specs=[lat(MQ0 // 4), lat(MKV0 // 4), pl.BlockSpec((tm, DH), lambda i: (i, MKR0)), vec, vec, wsp, wkv,
                  tab, tab, tab],
        out_specs=[pl.BlockSpec((tm, HEADS * QPAD), lambda i: (i, 0))] * 2 + [pl.BlockSpec((tm, BW), lambda i: (i, 0))],
        out_shape=[SDS((S, HEADS * QPAD), bf16)] * 2 + [SDS((S, BW), bf16)], compiler_params=_cp(48),
    )(proj, proj, proj, qnorm, kvnorm, wuq, wukv, *tabs)


def mla_prep_bwd(dq256, dk256, dv, proj, qnorm, kvnorm, wuq, wukv, tabs, *, tm=256, kv_blk=0):
    S = proj.shape[0]
    tm = min(tm, S)

    def body(dq_ref, dk_ref, dv_ref, mq_ref, mkv_ref, qn_ref, kvn_ref, wuq_ref, wukv_ref, c_ref, sa_ref, sb_ref,
             dm_ref, dwuq_ref, dwukv_ref, dqn_ref, dkvn_ref):
        first = pl.program_id(0) == 0
        c, sa, sb = c_ref[...], sa_ref[...], sb_ref[...]
        mq, mkv = mq_ref[...], mkv_ref[...]
        rq = lax.rsqrt(jnp.mean(mq * mq, axis=-1, keepdims=True) + EPS)
        rkv = lax.rsqrt(jnp.mean(mkv * mkv, axis=-1, keepdims=True) + EPS)
        mqh, mkvh = mq * rq, mkv * rkv
        qn = (mqh * qn_ref[...]).astype(bf16)
        kvn = (mkvh * kvn_ref[...]).astype(bf16)
        dqn = jnp.zeros((tm, LORA), f32)
        dkvn = jnp.zeros((tm, LORA), f32)
        dkr = jnp.zeros((tm, DH), f32)
        for h in range(HEADS):
            da = dq_ref[:, pl.ds(h * QPAD, DH)] * MLA_SCALE
            db = _rope_mla_t(dq_ref[:, pl.ds(h * QPAD + DH, DH)] * MLA_SCALE, c, sa, sb)
            dqh = jnp.concatenate([da, db], axis=1).astype(bf16)
            dqn += lax.dot_general(dqh, wuq_ref[h], _DIMS["nt"], preferred_element_type=f32)
            _acc(dwuq_ref.at[h], lax.dot_general(qn, dqh, _DIMS["tn"], preferred_element_type=f32), first)
            dkr += dk_ref[:, pl.ds(h * QPAD + DH, DH)]
            dkvh = jnp.concatenate([dk_ref[:, pl.ds(h * QPAD, DH)], dv_ref[:, pl.ds(h * DH, DH)]], axis=1).astype(bf16)
            dkvn += lax.dot_general(dkvh, wukv_ref[h], _DIMS["nt"], preferred_element_type=f32)
            _acc(dwukv_ref.at[h], lax.dot_general(kvn, dkvh, _DIMS["tn"], preferred_element_type=f32), first)
        dmh = dqn * qn_ref[...]
        dm_ref[:, pl.ds(0, LORA)] = (rq * (dmh - mqh * jnp.mean(dmh * mqh, axis=-1, keepdims=True))).astype(bf16)
        dmh = dkvn * kvn_ref[...]
        dm_ref[:, pl.ds(LORA, LORA)] = (rkv * (dmh - mkvh * jnp.mean(dmh * mkvh, axis=-1, keepdims=True))).astype(bf16)
        dm_ref[:, pl.ds(2 * LORA, DH)] = _rope_mla_t(dkr, c, sa, sb).astype(bf16)
        _acc(dqn_ref, jnp.sum(dqn * mqh, axis=0, keepdims=True), first)
        _acc(dkvn_ref, jnp.sum(dkvn * mkvh, axis=0, keepdims=True), first)

    lat = lambda b: pl.BlockSpec((tm, LORA), lambda i: (i, b))
    tab = pl.BlockSpec((tm, DH), lambda i: (i, 0))
    vec = pl.BlockSpec((1, LORA), lambda i: (0, 0))
    wsp = pl.BlockSpec((HEADS, LORA, QPAD), lambda i: (0, 0, 0))
    wkv = pl.BlockSpec((HEADS, LORA, QPAD), lambda i: (0, 0, kv_blk))
    wide = pl.BlockSpec((tm, HEADS * QPAD), lambda i: (i, 0))
    return pl.pallas_call(
        body, name="mla_prep_bwd", grid=(S // tm,),
        in_specs=[wide, wide, pl.BlockSpec((tm, BW), lambda i: (i, 0)), lat(MQ0 // 4), lat(MKV0 // 4), vec, vec, wsp, wkv,
                  tab, tab, tab],
        out_specs=[pl.BlockSpec((tm, 2 * LORA + DH), lambda i: (i, 0)), wsp, wsp, vec, vec],
        out_shape=[SDS((S, 2 * LORA + DH), bf16), SDS((HEADS, LORA, QPAD), f32), SDS((HEADS, LORA, QPAD), f32),
                   SDS((1, LORA), f32), SDS((1, LORA), f32)],
        compiler_params=_cp(52),
    )(dq256, dk256, dv, proj, proj, qnorm, kvnorm, wuq, wukv, *tabs)


SUB = 8


def _scan_tiles(a_s, b_s, out, S, reverse):
    nt = S // SUB
    rows = lax.broadcasted_iota(jnp.int32, (SUB, LANE), 0)

    def tile(t, carry):
        base = pl.multiple_of((nt - 1 - t if reverse else t) * SUB, SUB)
        a, b = a_s[pl.ds(base, SUB), :], b_s[pl.ds(base, SUB), :]
        for d in (1, 2, 4):
            sh = SUB - d if reverse else d
            inside = rows < SUB - d if reverse else rows >= d
            a_n = jnp.where(inside, pltpu.roll(a, sh, 0), 1.0)
            b_n = jnp.where(inside, pltpu.roll(b, sh, 0), 0.0)
            b = a * b_n + b
            a = a * a_n
        res = a * carry + b
        out[pl.ds(base, SUB), :] = res
        edge = res[0:1, :] if reverse else res[SUB - 1:SUB, :]
        return jnp.broadcast_to(edge, (SUB, LANE))

    lax.fori_loop(0, nt, tile, jnp.zeros((SUB, LANE), f32))


def _shift_down(x, n, rows):
    return x if n == 0 else jnp.where(rows >= n, pltpu.roll(x, n, 0), 0.0)


def _shift_up(x, n, rows, S):
    return x if n == 0 else jnp.where(rows < S - n, pltpu.roll(x, S - n, 0), 0.0)


def _lru_gates(xb, cw, cb, wa, ba, wx, bx, lam, rows):
    xc = cb + cw[3:4, :] * xb
    for w in range(3):
        xc = xc + cw[w:w + 1, :] * _shift_down(xb, 3 - w, rows)
    xcb = xc.astype(bf16)
    r = _sigmoid(jnp.dot(xcb, wa, preferred_element_type=f32) + ba)
    i = _sigmoid(jnp.dot(xcb, wx, preferred_element_type=f32) + bx)
    sp = _softplus(-lam)
    la = (-LRU_C * r) * sp
    return xc, xcb, r, i, sp, la, jnp.exp(la)


def _lru_specs(S):
    col = lambda b0: pl.BlockSpec((S, LANE), lambda n: (0, b0 + n))
    vec = pl.BlockSpec((1, LANE), lambda n: (0, n))
    return col, vec, pl.BlockSpec((4, LANE), lambda n: (0, n)), pl.BlockSpec((1, LANE, LANE), lambda n: (n, 0, 0))


def lru_fwd(proj, cw, cb, wa, ba, wx, bx, lam):
    S = proj.shape[0]

    def body(x_ref, g_ref, cw_ref, cb_ref, wa_ref, ba_ref, wx_ref, bx_ref, lam_ref, h_ref, y_ref, a_s, b_s):
        rows = lax.broadcasted_iota(jnp.int32, (S, LANE), 0)
        xc, _, _, i, _, la, a = _lru_gates(x_ref[...], cw_ref[...], cb_ref[...], wa_ref[0].astype(bf16), ba_ref[...],
                                           wx_ref[0].astype(bf16), bx_ref[...], lam_ref[...], rows)
        a_s[...] = a
        b_s[...] = jnp.sqrt(_one_minus_exp(2.0 * la)) * (i * xc)
        _scan_tiles(a_s, b_s, h_ref, S, reverse=False)
        y_ref[...] = (h_ref[...] * _silu(g_ref[...])).astype(bf16)

    col, vec, cws, wsp = _lru_specs(S)
    return pl.pallas_call(
        body, name="lru_fwd", grid=(HEADS,),
        in_specs=[col(LX0), col(LG0), cws, vec, wsp, vec, wsp, vec, vec], out_specs=[col(0), col(0)],
        out_shape=[SDS((S, BW), f32), SDS((S, BW), bf16)],
        scratch_shapes=[pltpu.VMEM((S, LANE), f32), pltpu.VMEM((S, LANE), f32)], compiler_params=_cp(40),
    )(proj, proj, cw, cb, wa, ba, wx, bx, lam)


def lru_bwd(dy, h, proj, cw, cb, wa, ba, wx, bx, lam):
    S = proj.shape[0]

    def body(dy_ref, h_ref, x_ref, g_ref, cw_ref, cb_ref, wa_ref, ba_ref, wx_ref, bx_ref, lam_ref,
             dx_ref, dg_ref, dcw_ref, dcb_ref, dba_ref, dbx_ref, dlam_ref, dwa_ref, dwx_ref, a_s, b_s, l_s):
        rows = lax.broadcasted_iota(jnp.int32, (S, LANE), 0)
        xb, g, hv, dyv, cw, lam = x_ref[...], g_ref[...], h_ref[...], dy_ref[...], cw_ref[...], lam_ref[...]
        wa, wx = wa_ref[0].astype(bf16), wx_ref[0].astype(bf16)
        xc, xcb, r, i, sp, la, a = _lru_gates(xb, cw, cb_ref[...], wa, ba_ref[...], wx, bx_ref[...], lam, rows)
        dg_ref[...] = (dyv * hv * _dsilu(g)).astype(bf16)
        a_s[...] = _shift_up(a, 1, rows, S)
        b_s[...] = dyv * _silu(g)
        _scan_tiles(a_s, b_s, l_s, S, reverse=True)
        lmb = l_s[...]
        gated = i * xc
        sq = jnp.sqrt(_one_minus_exp(2.0 * la))
        dla = lmb * _shift_down(hv, 1, rows) * a - (lmb * gated) * (a * a) / sq
        dgated = lmb * sq
        dzr = (dla * (-LRU_C * sp)) * (r * (1.0 - r))
        dzi = (dgated * xc) * (i * (1.0 - i))
        dzrb, dzib = dzr.astype(bf16), dzi.astype(bf16)
        dxc = (dgated * i + lax.dot_general(dzrb, wa, _DIMS["nt"], preferred_element_type=f32)
               + lax.dot_general(dzib, wx, _DIMS["nt"], preferred_element_type=f32))
        dwa_ref[0] = lax.dot_general(xcb, dzrb, _DIMS["tn"], preferred_element_type=f32)
        dwx_ref[0] = lax.dot_general(xcb, dzib, _DIMS["tn"], preferred_element_type=f32)
        dba_ref[...] = jnp.sum(dzr, axis=0, keepdims=True)
        dbx_ref[...] = jnp.sum(dzi, axis=0, keepdims=True)
        dlam_ref[...] = jnp.sum(dla * (-LRU_C * r), axis=0, keepdims=True) * (-_sigmoid(-lam))
        dcb_ref[...] = jnp.sum(dxc, axis=0, keepdims=True)
        dxb = cw[3:4, :] * dxc
        dcw_ref[3:4, :] = jnp.sum(dxc * xb, axis=0, keepdims=True)
        for w in range(3):
            dxb = dxb + cw[w:w + 1, :] * _shift_up(dxc, 3 - w, rows, S)
            dcw_ref[w:w + 1, :] = jnp.sum(dxc * _shift_down(xb, 3 - w, rows), axis=0, keepdims=True)
        dx_ref[...] = dxb.astype(bf16)

    col, vec, cws, wsp = _lru_specs(S)
    scr = pltpu.VMEM((S, LANE), f32)
    return pl.pallas_call(
        body, name="lru_bwd", grid=(HEADS,),
        in_specs=[col(0), col(0), col(LX0), col(LG0), cws, vec, wsp, vec, wsp, vec, vec],
        out_specs=[col(0), col(0), cws, vec, vec, vec, vec, wsp, wsp],
        out_shape=[SDS((S, BW), bf16), SDS((S, BW), bf16), SDS((4, BW), f32)] + [SDS((1, BW), f32)] * 4
        + [SDS((HEADS, LANE, LANE), f32)] * 2,
        scratch_shapes=[scr, scr, scr], compiler_params=_cp(48),
    )(dy, h, proj, proj, cw, cb, wa, ba, wx, bx, lam)


def loss_head(y, target, *, tm=256):
    S = y.shape[0]

    def body(y_ref, t_ref, l_ref, d_ref):
        err = y_ref[...] - t_ref[...]
        d_ref[...] = err * (1.0 / D)
        part = jnp.sum(jnp.sum(err * err, axis=1, keepdims=True), axis=0, keepdims=True) * (0.5 / D)
        _acc(l_ref, jnp.broadcast_to(part, (1, LANE)), pl.program_id(0) == 0)

    row = pl.BlockSpec((tm, D), lambda i: (i, 0))
    return pl.pallas_call(
        body, name="loss_head", grid=(S // tm,), in_specs=[row, row],
        out_specs=[pl.BlockSpec((1, LANE), lambda i: (0, 0)), row],
        out_shape=[SDS((1, LANE), f32), SDS((S, D), f32)], compiler_params=_cp(32))(y, target)


def _adam_math(w, g, m, v):
    mn = ADAM_B1 * m + (1.0 - ADAM_B1) * g
    vn = ADAM_B2 * v + (1.0 - ADAM_B2) * (g * g)
    m_hat = mn / (1.0 - ADAM_B1 ** ADAM_STEP)
    v_hat = vn / (1.0 - ADAM_B2 ** ADAM_STEP)
    return -ADAM_LR * (m_hat / (jnp.sqrt(v_hat) + ADAM_EPS) + ADAM_WD * w), mn, vn


def _adam_rows(rows, cols):
    for cand in (2048, 1024, 512, 256, 128, 64, 32, 16, 8):
        if rows % cand == 0 and rows > cand and cand * cols <= ADAM_BLOCK_ELEMS:
            return cand
    return rows


def adamw(w, g, m, v):
    shape = w.shape
    cols = shape[-1]
    rows = math.prod(shape[:-1])
    tr = _adam_rows(rows, cols)

    def body(w_ref, g_ref, m_ref, v_ref, d_ref, mo_ref, vo_ref):
        d_ref[...], mo_ref[...], vo_ref[...] = _adam_math(w_ref[...], g_ref[...], m_ref[...], v_ref[...])

    blk = pl.BlockSpec((tr, cols), lambda i: (i, 0))
    flat = [t.reshape(rows, cols) for t in (w, g, m, v)]
    outs = pl.pallas_call(
        body, name="adamw", grid=(rows // tr,), in_specs=[blk] * 4, out_specs=[blk] * 3,
        out_shape=[SDS((rows, cols), f32)] * 3, compiler_params=_cp(48))(*flat)
    return tuple(o.reshape(shape) for o in outs)


ADA_SHARD = 3 * D // NDEV
ROWS16 = 16


def ada_fwd(c_all, ada_w, ada_b_mine):
    def body(c_ref, w_ref, b_ref, o_ref):
        o_ref[0] = jnp.dot(_silu(c_ref[...]).astype(bf16), w_ref[0].astype(bf16), preferred_element_type=f32) + b_ref[0]

    return pl.pallas_call(
        body, name="ada_fwd", grid=(DEPTH,),
        in_specs=[pl.BlockSpec((ROWS16, D), lambda l: (0, 0)), pl.BlockSpec((1, D, ADA_SHARD), lambda l: (l, 0, 0)),
                  pl.BlockSpec((1, 1, ADA_SHARD), lambda l: (l, 0, 0))],
        out_specs=pl.BlockSpec((1, ROWS16, ADA_SHARD), lambda l: (l, 0, 0)),
        out_shape=SDS((DEPTH, ROWS16, ADA_SHARD), f32), compiler_params=_cp(40))(c_all, ada_w, ada_b_mine)


def ada_bwd(c_all, dmod):
    def body(c_ref, d_ref, o_ref):
        o_ref[0] = lax.dot_general(_silu(c_ref[...]).astype(bf16), d_ref[0].astype(bf16), _DIMS["tn"],
                                   preferred_element_type=f32)

    return pl.pallas_call(
        body, name="ada_bwd", grid=(DEPTH,),
        in_specs=[pl.BlockSpec((ROWS16, D), lambda l: (0, 0)), pl.BlockSpec((1, ROWS16, ADA_SHARD), lambda l: (l, 0, 0))],
        out_specs=pl.BlockSpec((1, D, ADA_SHARD), lambda l: (l, 0, 0)),
        out_shape=SDS((DEPTH, D, ADA_SHARD), f32), compiler_params=_cp(40))(c_all, dmod)


def shift_params(me):
    start = SHARD_W * me
    return jnp.stack([start % LANE, (start + GAP) % LANE, jnp.clip(GAP_COL - start, 0, SHARD_W)]).astype(jnp.int32)


def win_pack(wt, sidx, *, after=()):
    def body(s_ref, w_ref, *rest):
        o_ref, scr = rest[len(after):]
        s1, s2, gi = s_ref[0], s_ref[1], s_ref[2]
        scr[pl.ds(SHARD_W, WIN - SHARD_W), :] = jnp.zeros((WIN - SHARD_W, LANE), f32)
        scr[pl.ds(0, SHARD_W), :] = w_ref[0]
        v = scr[...].T
        j = lax.broadcasted_iota(jnp.int32, v.shape, 1)
        o_ref[0] = jnp.where(j - s1 < gi, pltpu.roll(v, s1, 1),
                             jnp.where(j - s2 >= gi, pltpu.roll(v, s2, 1), 0.0)).astype(bf16)

    return pl.pallas_call(
        body, name="win_pack", grid=(DEPTH, D // LANE),
        in_specs=[pl.BlockSpec(memory_space=pltpu.SMEM), pl.BlockSpec((1, SHARD_W, LANE), lambda l, i: (l, 0, i))]
        + [pl.BlockSpec(memory_space=pl.ANY)] * len(after),
        out_specs=pl.BlockSpec((1, LANE, WIN), lambda l, i: (l, i, 0)),
        out_shape=SDS((DEPTH, D, WIN), bf16), scratch_shapes=[pltpu.VMEM((WIN, LANE), f32)],
        compiler_params=_cp(32))(sidx, wt, *after)


def win_assemble(g, *, after=()):
    own = WIN_STRIDE * LANE
    tail = NP - NDEV * own
    assert tail == 2 * LANE

    def body(a_ref, b_ref, *rest):
        o_ref = rest[len(after)]
        o_ref[...] = a_ref[0]

        @pl.when(pl.program_id(0) > 0)
        def _():
            o_ref[:, pl.ds(0, LANE)] = a_ref[0, :, pl.ds(0, LANE)] + b_ref[0]

    main = pl.pallas_call(
        body, name="win_assemble", grid=(NDEV,),
        in_specs=[pl.BlockSpec((1, D, own), lambda k: (k, 0, 0)),
                  pl.BlockSpec((1, D, LANE), lambda k: (jnp.maximum(k - 1, 0), 0, WIN_BLKS - 1))]
        + [pl.BlockSpec(memory_space=pl.ANY)] * len(after),
        out_specs=pl.BlockSpec((D, own), lambda k: (0, k)),
        out_shape=SDS((D, NP), bf16), compiler_params=_cp(48))(g, g, *after)

    def tail_body(_, b_ref, o_ref):
        o_ref[:, pl.ds(0, LANE)] = b_ref[0]
        o_ref[:, pl.ds(LANE, LANE)] = jnp.zeros((D, LANE), bf16)

    return pl.pallas_call(
        tail_body, name="win_assemble_tail", grid=(1,),
        in_specs=[pl.BlockSpec(memory_space=pl.ANY), pl.BlockSpec((1, D, LANE), lambda t: (NDEV - 1, 0, WIN_BLKS - 1))],
        out_specs=pl.BlockSpec((D, tail), lambda t: (0, NDEV * own // tail)),
        out_shape=SDS((D, NP), bf16), input_output_aliases={0: 0}, compiler_params=_cp(32))(main, g)


def reduce_adamw(stag, w, m, v, l, prev, name, *, sidx=None, tr=128):
    Cs = stag.shape[2]
    n_prev = 0 if prev is None else 4
    n_lead = 1 if sidx is not None else 0

    def body(*refs):
        refs = list(refs)
        s_ref = refs.pop(0) if sidx is not None else None
        g_ref, w_ref, m_ref, v_ref = refs[:4]
        rest = refs[4 + n_prev:]
        go_ref, d_ref, mo_ref, vo_ref = rest[:4]
        tot = g_ref[0].astype(f32)
        for d in range(1, stag.shape[0]):
            tot = tot + g_ref[d].astype(f32)
        if sidx is not None:
            scr = rest[4]
            s1, s2, gi = s_ref[0], s_ref[1], s_ref[2]
            i = lax.broadcasted_iota(jnp.int32, tot.shape, 1)
            scr[...] = jnp.where(i < gi, pltpu.roll(tot, WIN - s1, 1), pltpu.roll(tot, WIN - s2, 1)).T
            tot = scr[pl.ds(0, SHARD_W), :]
        go_ref[0] = tot
        d_ref[0], mo_ref[0], vo_ref[0] = _adam_math(w_ref[0], tot, m_ref[0], v_ref[0])

    if sidx is not None:
        blk3 = pl.BlockSpec((1, SHARD_W, tr), lambda i: (l, 0, i))
        steps = w.shape[2] // tr
    else:
        blk3 = pl.BlockSpec((1, tr, w.shape[2]), lambda i: (l, i, 0))
        steps = w.shape[1] // tr
    in_specs = ([pl.BlockSpec(memory_space=pltpu.SMEM)] * n_lead
                + [pl.BlockSpec((stag.shape[0], tr, Cs), lambda i: (0, i, 0)), blk3, blk3, blk3]
                + [pl.BlockSpec(memory_space=pl.ANY)] * n_prev)
    args = ([sidx] if sidx is not None else []) + [stag, w, m, v] + list(prev or ())
    return pl.pallas_call(
        body, name=name, grid=(steps,), in_specs=in_specs, out_specs=[blk3] * 4, out_shape=[SDS(w.shape, f32)] * 4,
        scratch_shapes=[pltpu.VMEM((Cs, tr), f32)] if sidx is not None else [],
        input_output_aliases={n_lead + 4 + k: k for k in range(n_prev)}, compiler_params=_cp(48),
    )(*args)


def cast_pad(w, cols_out, name):
    L, R, C = w.shape

    def body(w_ref, o_ref, *scr):
        if cols_out == C:
            o_ref[0] = w_ref[0].astype(bf16)
        else:
            scr[0][...] = jnp.zeros_like(scr[0])
            scr[0][:, pl.ds(0, C)] = w_ref[0]
            o_ref[0] = scr[0][...].astype(bf16)

    return pl.pallas_call(
        body, name=name, grid=(L,), in_specs=[pl.BlockSpec((1, R, C), lambda l: (l, 0, 0))],
        out_specs=pl.BlockSpec((1, R, cols_out), lambda l: (l, 0, 0)), out_shape=SDS((L, R, cols_out), bf16),
        scratch_shapes=[] if cols_out == C else [pltpu.VMEM((R, cols_out), f32)], compiler_params=_cp(32))(w)


def pack_qkv(w_uq, w_ukv):
    L = w_uq.shape[0]

    def body(q_ref, kv_ref, o_ref, scr):
        scr[...] = jnp.zeros_like(scr)
        scr[:, pl.ds(0, WUQ_COLS)] = q_ref[0]
        o_ref[0, :, pl.ds(0, QPAD)] = scr[...].astype(bf16)
        o_ref[0, :, pl.ds(QPAD, QPAD)] = kv_ref[0].astype(bf16)

    return pl.pallas_call(
        body, name="pack_qkv", grid=(L,),
        in_specs=[pl.BlockSpec((1, LORA, WUQ_COLS), lambda l: (l, 0, 0)), pl.BlockSpec((1, LORA, QPAD), lambda l: (l, 0, 0))],
        out_specs=pl.BlockSpec((1, LORA, 2 * QPAD), lambda l: (l, 0, 0)), out_shape=SDS((L, LORA, 2 * QPAD), bf16),
        scratch_shapes=[pltpu.VMEM((LORA, QPAD), f32)], compiler_params=_cp(32))(w_uq, w_ukv)


def sum_parts(stag, cols_out, name, *, tr=None, after=()):
    P, R, C = stag.shape
    tr = R if tr is None else tr

    def body(g_ref, *rest):
        o_ref, *scr = rest[len(after):]
        tot = g_ref[0].astype(f32)
        for d in range(1, P):
            tot = tot + g_ref[d].astype(f32)
        if cols_out == C:
            o_ref[...] = tot
        else:
            scr[0][...] = tot
            o_ref[...] = scr[0][:, pl.ds(0, cols_out)]

    return pl.pallas_call(
        body, name=name, grid=(R // tr,),
        in_specs=[pl.BlockSpec((P, tr, C), lambda i: (0, i, 0))] + [pl.BlockSpec(memory_space=pl.ANY)] * len(after),
        out_specs=pl.BlockSpec((tr, cols_out), lambda i: (i, 0)), out_shape=SDS((R, cols_out), f32),
        scratch_shapes=[] if cols_out == C else [pltpu.VMEM((tr, C), f32)], compiler_params=_cp(40))(stag, *after)


MESH_ID = pl.DeviceIdType.MESH
HBM_SPEC = pl.BlockSpec(memory_space=pltpu.HBM)


def _place():
    return lax.axis_index("x"), lax.axis_index("y"), lax.axis_index("c")


def all_gather(arrs, name):
    n = len(arrs)

    def body(*refs):
        ins, outs = refs[:n], refs[n:2 * n]
        send_sems, recv_sems, local_sems = refs[2 * n:]
        x, y, c = _place()
        me, sibling = (x, y, c), (x, y, 1 - c)
        chips = [(1 - x, y), (x, 1 - y), (1 - x, 1 - y)]

        def copy(a, k, block, to, src=None):
            slot = outs[a].at[4 * block[0] + 2 * block[1] + block[2]]
            return pltpu.make_async_remote_copy(
                src_ref=slot if src is None else src, dst_ref=slot, send_sem=send_sems.at[7 * a + k],
                recv_sem=recv_sems.at[7 * a + k], device_id=to, device_id_type=MESH_ID)

        mine = [pltpu.make_async_copy(ins[a], outs[a].at[4 * x + 2 * y + c], local_sems.at[a]) for a in range(n)]
        for cp in mine:
            cp.start()
        first = []
        for a in range(n):
            first.append(copy(a, 0, me, sibling, src=ins[a]))
            first += [copy(a, 1 + j, me, (*chip, c), src=ins[a]) for j, chip in enumerate(chips)]
        for cp in first:
            cp.start()
        passed = []
        for j, chip in enumerate(chips):
            for a in range(n):
                copy(a, 1 + j, (*chip, c), me).wait_recv()
                cp = copy(a, 4 + j, (*chip, c), sibling)
                cp.start()
                passed.append(cp)
        for a in range(n):
            copy(a, 0, sibling, me).wait_recv()
        for j, chip in enumerate(chips):
            for a in range(n):
                copy(a, 4 + j, (*chip, 1 - c), me).wait_recv()
        for cp in first + passed:
            cp.wait_send()
        for cp in mine:
            cp.wait()

    return pl.pallas_call(
        body, name=name, in_specs=[HBM_SPEC] * n, out_specs=[HBM_SPEC] * n,
        out_shape=[SDS((NDEV,) + a.shape, a.dtype) for a in arrs],
        scratch_shapes=[pltpu.SemaphoreType.DMA((7 * n,)), pltpu.SemaphoreType.DMA((7 * n,)),
                        pltpu.SemaphoreType.DMA((n,))],
    )(*arrs)


AG_COLLECTIVE_ID = 0
RS_COLLECTIVE_ID = 1


def _everyone_else(x, y, c):
    return [(x ^ (r >> 2), y ^ ((r >> 1) & 1), c ^ (r & 1)) for r in range(1, NDEV)]


def _rendezvous(sem, peers):
    for peer in peers:
        pl.semaphore_signal(sem, inc=1, device_id=peer, device_id_type=MESH_ID)
    pl.semaphore_wait(sem, len(peers))


def _sequencer_call(body, arrs, out_types, name, collective_id, remote=7, local=1):
    n = len(arrs)
    return pl.kernel(
        body, name=name, out_type=out_types, mesh=plsc.ScalarSubcoreMesh(axis_name="sequencer", num_cores=1),
        scratch_types=[pltpu.SemaphoreType.DMA((remote * n,)), pltpu.SemaphoreType.DMA((remote * n,)),
                       pltpu.SemaphoreType.DMA((local * n,)), pltpu.SemaphoreType.REGULAR],
        compiler_params=pltpu.CompilerParams(collective_id=collective_id),
    )(*arrs)


def seq_all_gather(arrs, name):
    n = len(arrs)

    def body(*refs):
        ins, outs = refs[:n], refs[n:2 * n]
        send_sems, recv_sems, local_sems, exit_sem = refs[2 * n:]
        x, y, c = _place()
        peers = _everyone_else(x, y, c)
        _rendezvous(pltpu.get_barrier_semaphore(), peers)
        me, sibling = (x, y, c), (x, y, 1 - c)
        chips = [(1 - x, y), (x, 1 - y), (1 - x, 1 - y)]

        def copy(a, k, block, to, src=None):
            slot = outs[a].at[4 * block[0] + 2 * block[1] + block[2]]
            return pltpu.make_async_remote_copy(
                src_ref=slot if src is None else src, dst_ref=slot, send_sem=send_sems.at[7 * a + k],
                recv_sem=recv_sems.at[7 * a + k], device_id=to, device_id_type=MESH_ID)

        mine = [pltpu.make_async_copy(ins[a], outs[a].at[4 * x + 2 * y + c], local_sems.at[a]) for a in range(n)]
        for cp in mine:
            cp.start()
        first = []
        for a in range(n):
            first.append(copy(a, 0, me, sibling, src=ins[a]))
            first += [copy(a, 1 + j, me, (*chip, c), src=ins[a]) for j, chip in enumerate(chips)]
        for cp in first:
            cp.start()
        passed = []
        for j, chip in enumerate(chips):
            for a in range(n):
                copy(a, 1 + j, (*chip, c), me).wait_recv()
                cp = copy(a, 4 + j, (*chip, c), sibling)
                cp.start()
                passed.append(cp)
        for a in range(n):
            copy(a, 0, sibling, me).wait_recv()
        for j, chip in enumerate(chips):
            for a in range(n):
                copy(a, 4 + j, (*chip, 1 - c), me).wait_recv()
        for cp in first + passed:
            cp.wait_send()
        for cp in mine:
            cp.wait()
        _rendezvous(exit_sem, peers)

    return _sequencer_call(body, arrs, [SDS((NDEV,) + a.shape, a.dtype) for a in arrs], name, AG_COLLECTIVE_ID)


def seq_reduce_scatter_parts(arrs, pick, shapes, name):
    n = len(arrs)

    def body(*refs):
        ins, outs = refs[:n], refs[n:2 * n]
        send_sems, recv_sems, local_sems, exit_sem = refs[2 * n:]
        x, y, c = _place()
        peers = _everyone_else(x, y, c)
        _rendezvous(pltpu.get_barrier_semaphore(), peers)
        me = 4 * x + 2 * y + c
        mine = [pltpu.make_async_copy(pick[a](ins[a], me), outs[a].at[me], local_sems.at[a]) for a in range(n)]
        for cp in mine:
            cp.start()
        sent = []
        for r, peer in enumerate(peers):
            pid = 4 * peer[0] + 2 * peer[1] + peer[2]
            for a in range(n):
                cp = pltpu.make_async_remote_copy(
                    src_ref=pick[a](ins[a], pid), dst_ref=outs[a].at[me], send_sem=send_sems.at[7 * a + r],
                    recv_sem=recv_sems.at[7 * a + r], device_id=peer, device_id_type=MESH_ID)
                cp.start()
                sent.append((cp, a, r, pid))
        for cp, a, r, pid in sent:
            pltpu.make_async_remote_copy(
                src_ref=pick[a](ins[a], pid), dst_ref=outs[a].at[pid], send_sem=send_sems.at[7 * a + r],
                recv_sem=recv_sems.at[7 * a + r], device_id=(x, y, c), device_id_type=MESH_ID).wait_recv()
        for cp, _, _, _ in sent:
            cp.wait_send()
        for cp in mine:
            cp.wait()
        _rendezvous(exit_sem, peers)

    return _sequencer_call(body, arrs, [SDS((NDEV,) + tuple(s), a.dtype) for s, a in zip(shapes, arrs)], name,
                           RS_COLLECTIVE_ID)


WEIGHTS = ("ada_w", "ada_b", "norm_pre", "norm_post", "w_in", "ret_gn", "lru_conv_w", "lru_conv_b", "lru_wa", "lru_ba",
           "lru_wx", "lru_bx", "lru_lambda", "mla_q_norm", "mla_w_uq", "mla_kv_norm", "mla_w_ukv", "w_branch", "w_out")
SMALL = ("norm_pre", "norm_post", "ret_gn", "lru_conv_w", "lru_conv_b", "lru_ba", "lru_bx", "lru_lambda", "mla_q_norm",
         "mla_kv_norm")
QKV_ROWS = LORA
QKV_BLOCK = (LORA + LANE, 2 * QPAD)
BR_ROWS = 3 * BW // NDEV
OUT_ROWS = D // NDEV
WUQ_COLS = 192


def _row(v):
    return v.reshape(1, -1)


def layer_fwd(xl, mod, p, wts, tabs, lgam):
    S = xl.shape[0]
    tm = min(S, 2048)
    sh, sc, rg = _row(mod[:D]), _row(mod[D:2 * D]), _row(mod[2 * D:])
    ret_tabs, mla_tabs = tabs
    h = pre_fwd(xl, _row(p["norm_pre"]), sc, sh)
    proj = matmul(h, wts["w_in"], dims="nn", M=S, N=NP, K=D, tm=tm, tn=768, tk=D, out_dtype=f32, name="mm_in")
    o_ret, y_ret = ret_attn_fwd(proj, ret_tabs, lgam, _row(p["ret_gn"]))
    h_lru, y_lru = lru_fwd(proj, p["conv_w"], _row(p["lru_conv_b"]), p["lru_wa"], _row(p["lru_ba"]), p["lru_wx"],
                           _row(p["lru_bx"]), _row(p["lru_lambda"]))
    q256, k256, vm = mla_prep_fwd(proj, _row(p["mla_q_norm"]), _row(p["mla_kv_norm"]), wts["w_qkv"], wts["w_qkv"], mla_tabs,
                                  kv_blk=1)
    o_mla, y_mla = mla_attn_fwd(q256, k256, vm, proj)
    ys = (y_ret, y_lru, y_mla)
    us = [matmul(ys[b], wts["w_branch"], dims="nn", M=S, N=D, K=BW, tm=min(S, 1024), tn=1024, tk=BW,
                 out_dtype=bf16, name="mm_branch", b_blk0=(b, 0)) for b in range(3)]
    merged = gate_fwd(proj, us)
    y, x_next = out_fwd(merged, wts["w_out"], xl, rg, _row(p["norm_post"]))
    saved = dict(x=xl, h=h, proj=proj, o_ret=o_ret, h_lru=h_lru, q256=q256, k256=k256, vm=vm, o_mla=o_mla,
                 ys=ys, us=us, merged=merged, y=y, sc=sc, rg=rg)
    return x_next, saved


def layer_bwd(dx, sv, p, wts, tabs, lgam, exchange):
    S = dx.shape[0]
    tm = min(S, 2048)
    ret_tabs, mla_tabs = tabs
    proj = sv["proj"]
    dy, d_rg, d_gpost = out_bwd(dx, sv["y"], sv["rg"], _row(p["norm_post"]))
    dmerged = matmul(dy, wts["w_out"], dims="nt", M=S, N=D, K=D, tm=min(S, 1024), tn=1024, tk=D, out_dtype=f32, name="mm_dmerged")
    dw_out = matmul(sv["merged"], dy, dims="tn", M=D, N=D, K=S, tm=1024, tn=1024, tk=min(S, 1024), out_dtype=bf16, name="mm_dwout")
    du, dml = gate_bwd(dmerged, proj, sv["us"])
    tmb, tkb = min(S, 1024), min(S, 1024)
    dys = [matmul(du[b], wts["w_branch"], dims="nt", M=S, N=BW, K=D, tm=tmb, tn=BW, tk=D, out_dtype=f32, name="mm_dybranch",
                  b_blk0=(b, 0)) for b in range(3)]
    dw_branch = jnp.concatenate(
        [matmul(sv["ys"][b], du[b], dims="tn", M=BW, N=D, K=S, tm=BW, tn=1024, tk=tkb, out_dtype=bf16, name="mm_dwbranch")
         for b in range(3)], axis=0)
    do, d_rgate, d_gn = ret_post_bwd(dys[0], sv["o_ret"], proj, _row(p["ret_gn"]))
    d_q, d_k, d_v = ret_attn_bwd(proj, do, ret_tabs, lgam)
    d_lx, d_lg, d_cw, d_cb, d_ba, d_bx, d_lam, d_wa, d_wx = lru_bwd(
        dys[1], sv["h_lru"], proj, p["conv_w"], _row(p["lru_conv_b"]), p["lru_wa"], _row(p["lru_ba"]), p["lru_wx"],
        _row(p["lru_bx"]), _row(p["lru_lambda"]))
    do, d_mg = mla_post_bwd(dys[2], sv["o_mla"], proj)
    dq256, dk256, dvm = mla_attn_bwd(sv["q256"], sv["k256"], sv["vm"], do, sv["o_mla"])
    d_lat, dw_uq, dw_ukv, d_qn, d_kvn = mla_prep_bwd(dq256, dk256, dvm, proj, _row(p["mla_q_norm"]), _row(p["mla_kv_norm"]),
                                                       wts["w_qkv"], wts["w_qkv"], mla_tabs, kv_blk=1)
    dproj = jnp.concatenate([d_q, d_k, d_v, d_rgate, d_lx, d_lg, d_lat, d_mg, *dml, jnp.zeros((S, LANE), bf16)], axis=1)
    dh = matmul(dproj, wts["w_in"], dims="nt", M=S, N=D, K=NP, tm=min(S, 1024), tn=1024, tk=NP // 6, out_dtype=f32, name="mm_dh")
    dw_in = matmul(sv["h"], dproj, dims="tn", M=D, N=NP, K=S, tm=D, tn=768, tk=tm, out_dtype=bf16, name="mm_dwin")
    dw_qkv = jnp.concatenate([jnp.concatenate([dw_uq, dw_ukv], axis=2),
                              jnp.concatenate([d_wa, d_wx, jnp.zeros((HEADS, LANE, QPAD), f32)], axis=2)], axis=1).astype(bf16)
    staged, launched_from = exchange(dict(w_in=dw_in, w_branch=dw_branch, w_out=dw_out, w_qkv=dw_qkv))
    dxl, d_sh, d_sc, d_gpre = pre_bwd(dh, sv["x"], _row(p["norm_pre"]), sv["sc"], dx, after=tuple(launched_from))
    dmod = jnp.concatenate([d_sh, d_sc, d_rg], axis=1).reshape(-1)
    small = dict(norm_pre=d_gpre, norm_post=d_gpost, ret_gn=d_gn, lru_conv_w=d_cw, lru_conv_b=d_cb, lru_ba=d_ba,
                 lru_bx=d_bx, lru_lambda=d_lam, mla_q_norm=d_qn, mla_kv_norm=d_kvn)
    return dxl, dmod, staged, small


def exchange_grads(big, l):
    picks = [lambda r, d: r.at[:, pl.ds(pl.multiple_of(d * (WIN_STRIDE * LANE), LANE), WIN)],
             lambda r, d: r.at[pl.ds(pl.multiple_of(d * BR_ROWS, 8), BR_ROWS), :],
             lambda r, d: r.at[pl.ds(pl.multiple_of(d * OUT_ROWS, 8), OUT_ROWS), :],
             lambda r, d: r.at[d]]
    shapes = [(D, WIN), (BR_ROWS, D), (OUT_ROWS, D), QKV_BLOCK]
    arrs = [big["w_in"], big["w_branch"], big["w_out"], big["w_qkv"]]
    return seq_reduce_scatter_parts(arrs, picks, shapes, f"rs_grads_{l}"), arrs


def kernel(x, c, positions, ada_w, ada_b, norm_pre, norm_post, w_in, ret_gn, lru_conv_w, lru_conv_b, lru_wa, lru_ba, lru_wx, lru_bx, lru_lambda, mla_q_norm, mla_w_uq, mla_kv_norm, mla_w_ukv, w_branch, w_out, loss_target, m_ada_w, m_ada_b, m_norm_pre, m_norm_post, m_w_in, m_ret_gn, m_lru_conv_w, m_lru_conv_b, m_lru_wa, m_lru_ba, m_lru_wx, m_lru_bx, m_lru_lambda, m_mla_q_norm, m_mla_w_uq, m_mla_kv_norm, m_mla_w_ukv, m_w_branch, m_w_out, v_ada_w, v_ada_b, v_norm_pre, v_norm_post, v_w_in, v_ret_gn, v_lru_conv_w, v_lru_conv_b, v_lru_wa, v_lru_ba, v_lru_wx, v_lru_bx, v_lru_lambda, v_mla_q_norm, v_mla_w_uq, v_mla_kv_norm, v_mla_w_ukv, v_w_branch, v_w_out):
    given = dict(locals())
    xi, yi, ci = _place()
    me = 4 * xi + 2 * yi + ci
    S = x.shape[1]
    sidx = shift_params(me)
    lgam = jnp.asarray(np.log1p(-np.exp2(-5.0 - np.arange(HEADS))), f32)
    tabs = rope_tables(positions[0])

    (g_small,) = all_gather([jnp.concatenate([c.reshape(16, LANE), lru_conv_w.reshape(16, LANE)], axis=0)], "ag_small")
    c16 = jnp.concatenate([g_small[:, :16].reshape(NDEV, D), jnp.zeros((ROWS16 - NDEV, D), f32)], axis=0)
    conv_w_all = g_small[:, 16:].reshape(NDEV, DEPTH, 4, LANE).transpose(1, 2, 0, 3).reshape(DEPTH, 4, BW)
    ada_b_mine = lax.dynamic_slice_in_dim(ada_b, me * ADA_SHARD, ADA_SHARD, axis=1).reshape(DEPTH, 1, ADA_SHARD)
    (g_mod,) = all_gather([ada_fwd(c16, ada_w, ada_b_mine)[:, :NDEV]], "ag_mod")
    mods = lax.dynamic_index_in_dim(g_mod, me, axis=2, keepdims=False).transpose(1, 0, 2).reshape(DEPTH, 3 * D)
    w_in_t = {n: jnp.swapaxes(given[n], 1, 2) for n in ("w_in", "m_w_in", "v_w_in")}
    packed = (win_pack(w_in_t["w_in"], sidx, after=(g_mod,)), cast_pad(w_branch, D, "pack_wbranch"), cast_pad(w_out, D, "pack_wout"),
              pack_qkv(mla_w_uq, mla_w_ukv))
    first, rest = seq_all_gather([packed[0][0], packed[3][0]], "ag_weights_0a"), seq_all_gather([packed[1][0], packed[2][0]], "ag_weights_0b")
    gathered = [(first[0], rest[0], rest[1], first[1])]
    gathered += [seq_all_gather([t[l] for t in packed], f"ag_weights_{l}") for l in range(1, DEPTH)]

    params, wts = [], []
    for l in range(DEPTH):
        p = {n: given[n][l] for n in SMALL + ("lru_wa", "lru_wx") if n != "lru_conv_w"}
        p["conv_w"] = conv_w_all[l]
        params.append(p)

    xl, saved = x[0], []
    for l in range(DEPTH):
        g_win, g_br, g_out, g_qkv = gathered[l]
        wts.append(dict(w_in=win_assemble(g_win, after=(xl,)), w_qkv=g_qkv, w_branch=g_br.reshape(3 * BW, D),
                        w_out=g_out.reshape(D, D)))
        xl, sv = layer_fwd(xl, mods[l], params[l], wts[l], tabs, lgam)
        saved.append(sv)
    my_loss, dx = loss_head(xl, loss_target[0])
    loss = lax.psum(my_loss[0, 0], ("x", "y", "c"))

    dmods, smalls, staged, grads = [None] * DEPTH, [None] * DEPTH, [None] * DEPTH, {n: [None] * DEPTH for n in WEIGHTS}
    for l in reversed(range(DEPTH)):
        dx, dmods[l], staged[l], smalls[l] = layer_bwd(dx, saved[l], params[l], wts[l], tabs, lgam,
                                                       functools.partial(exchange_grads, l=l))
    chained, lru_blocks = {"w_in": None, "w_branch": None, "w_out": None}, [None] * DEPTH
    for l in reversed(range(DEPTH)):
        st_win, st_br, st_out, st_qkv = staged[l]
        chained["w_in"] = reduce_adamw(st_win, w_in_t["w_in"], w_in_t["m_w_in"], w_in_t["v_w_in"], l, chained["w_in"],
                                       "update_w_in", sidx=sidx)
        for n, st in (("w_branch", st_br), ("w_out", st_out)):
            chained[n] = reduce_adamw(st, given[n], given["m_" + n], given["v_" + n], l, chained[n], "update_" + n)
        g_qkv = sum_parts(st_qkv, 2 * QPAD, "sum_wqkv", after=(chained["w_in"][0],) if l == 0 else ())
        grads["mla_w_uq"][l] = g_qkv[:QKV_ROWS, :WUQ_COLS]
        grads["mla_w_ukv"][l] = g_qkv[:QKV_ROWS, QPAD:]
        lru_blocks[l] = g_qkv[QKV_ROWS:, :2 * LANE]
    (g_lru,) = all_gather([jnp.stack(lru_blocks)], "ag_lru_w")
    grads["lru_wa"] = g_lru[..., :LANE].transpose(1, 0, 2, 3)
    grads["lru_wx"] = g_lru[..., LANE:].transpose(1, 0, 2, 3)

    flat = [jnp.stack(dmods).reshape(-1)] + [smalls[l][n].reshape(-1) for n in SMALL for l in range(DEPTH)]
    (g_pack,) = all_gather([jnp.concatenate(flat).reshape(-1, LANE)], "ag_small_grads")
    rows = g_pack.shape[1]
    tot = sum_parts(g_pack, LANE, "sum_small_grads", tr=rows // 8).reshape(-1)
    grads["ada_b"] = tot[:DEPTH * 3 * D].reshape(DEPTH, 3 * D)
    off = DEPTH * 3 * D
    for n in SMALL:
        size = DEPTH * int(smalls[0][n].size)
        piece = tot[off:off + size]
        off += size
        if n == "lru_conv_w":
            grads[n] = lax.dynamic_slice_in_dim(piece.reshape(DEPTH, 4, BW), me * LANE, LANE, axis=2)
        else:
            grads[n] = piece.reshape(given[n].shape)
    dmod_all = g_pack[:, :DEPTH * 3 * D // LANE].reshape(NDEV, DEPTH, 3 * D)
    dmod_mine = lax.dynamic_slice_in_dim(dmod_all, me * ADA_SHARD, ADA_SHARD, axis=2).transpose(1, 0, 2)
    dmod16 = jnp.concatenate([dmod_mine, jnp.zeros((DEPTH, ROWS16 - NDEV, ADA_SHARD), f32)], axis=1)
    grads["ada_w"] = ada_bwd(c16, dmod16)

    outs = {"grad": [], "delta": [], "m": [], "v": []}
    for n in WEIGHTS:
        if n in chained:
            g, delta, new_m, new_v = (jnp.swapaxes(t, 1, 2) for t in chained[n]) if n == "w_in" else chained[n]
        else:
            g = grads[n] if not isinstance(grads[n], list) else jnp.stack(grads[n])
            delta, new_m, new_v = adamw(given[n], g, given["m_" + n], given["v_" + n])
        outs["grad"].append(g)
        outs["delta"].append(delta)
        outs["m"].append(new_m)
        outs["v"].append(new_v)
    return (loss, dx[None], *outs["grad"], *outs["delta"], *outs["m"], *outs["v"])
```

```python
import functools
import math

import numpy as np
import jax
import jax.numpy as jnp
from jax import lax
from jax.experimental import pallas as pl
from jax.experimental.pallas import tpu as pltpu
from jax.experimental.pallas import tpu_sc as plsc

f32 = jnp.float32
bf16 = jnp.bfloat16
SDS = jax.ShapeDtypeStruct

DEPTH = 4
D = 2048
HEADS = 8
DH = 128
BW = HEADS * DH
LANE = 128
CHUNK = 64
EPS = 1e-6
LRU_C = 8.0
NDEV = 8
VMEM_V7X = 64 * 1024 * 1024

Q0, K0, V0, RG0, LX0, LG0, MQ0, MKV0, MKR0, MG0, ML0 = 0, 8, 16, 24, 32, 40, 48, 52, 56, 57, 65
NB = 114
NP = NB * LANE
IN_W = 14400
SHARD_W = IN_W // NDEV
GAP_COL = 7232
GAP = 64
WIN = 1920
WIN_BLKS = WIN // LANE
WIN_STRIDE = 14
LORA = 512
QPAD = 256

ADAM_LR, ADAM_B1, ADAM_B2, ADAM_EPS, ADAM_WD, ADAM_STEP = 0.001, 0.9, 0.999, 1e-08, 0.01, 10
ADAM_BLOCK_ELEMS = 256 * 1024
MLA_SCALE = (128 + 64) ** -0.5
RET_SCALE = 128 ** -0.5


def _cp(vmem_mb=None, **kw):
    if vmem_mb is not None:
        kw["vmem_limit_bytes"] = min(vmem_mb * 1024 * 1024, VMEM_V7X - 8 * 1024 * 1024)
    return pltpu.CompilerParams(**kw)


def _sigmoid(x):
    return 1.0 / (1.0 + jnp.exp(-x))


def _silu(x):
    return x * _sigmoid(x)


def _dsilu(x):
    s = _sigmoid(x)
    return s * (1.0 + x * (1.0 - s))


def _softplus(x):
    return jnp.maximum(x, 0.0) + jnp.log(1.0 + jnp.exp(-jnp.abs(x)))


def _one_minus_exp(y):
    series = -y * (1.0 + y * (0.5 + y * (1.0 / 6.0)))
    return jnp.where(y > -1e-2, series, 1.0 - jnp.exp(y))


def _acc(ref, val, first):
    @pl.when(first)
    def _():
        ref[...] = val

    @pl.when(jnp.logical_not(first))
    def _():
        ref[...] += val


_DIMS = {"nn": (((1,), (0,)), ((), ())), "nt": (((1,), (1,)), ((), ())), "tn": (((0,), (0,)), ((), ()))}


def matmul(a, b, *, dims, M, N, K, tm, tn, tk, out_dtype, name, a_blk0=(0, 0), b_blk0=(0, 0), vmem_mb=48):
    nk = K // tk
    assert M % tm == 0 and N % tn == 0 and K % tk == 0
    dn = _DIMS[dims]

    def body(a_ref, b_ref, o_ref, *scr):
        part = lax.dot_general(a_ref[...].astype(bf16), b_ref[...].astype(bf16), dn, preferred_element_type=f32)
        if nk == 1:
            o_ref[...] = part.astype(out_dtype)
        else:
            acc = scr[0]
            k = pl.program_id(2)
            _acc(acc, part, k == 0)

            @pl.when(k == nk - 1)
            def _():
                o_ref[...] = acc[...].astype(out_dtype)

    ar, ac = a_blk0
    br, bc = b_blk0
    if dims == "nn":
        a_spec = pl.BlockSpec((tm, tk), lambda i, j, k: (i + ar, k + ac))
        b_spec = pl.BlockSpec((tk, tn), lambda i, j, k: (k + br, j + bc))
    elif dims == "nt":
        a_spec = pl.BlockSpec((tm, tk), lambda i, j, k: (i + ar, k + ac))
        b_spec = pl.BlockSpec((tn, tk), lambda i, j, k: (j + br, k + bc))
    else:
        a_spec = pl.BlockSpec((tk, tm), lambda i, j, k: (k + ar, i + ac))
        b_spec = pl.BlockSpec((tk, tn), lambda i, j, k: (k + br, j + bc))
    return pl.pallas_call(
        body, name=name, grid=(M // tm, N // tn, nk),
        in_specs=[a_spec, b_spec], out_specs=pl.BlockSpec((tm, tn), lambda i, j, k: (i, j)),
        out_shape=SDS((M, N), out_dtype),
        scratch_shapes=[] if nk == 1 else [pltpu.VMEM((tm, tn), f32)],
        compiler_params=_cp(vmem_mb, dimension_semantics=("parallel", "parallel", "arbitrary")),
    )(a, b)


def pre_fwd(x, g, sc, sh, *, tm=256):
    S = x.shape[0]

    def body(x_ref, g_ref, sc_ref, sh_ref, h_ref):
        xv = x_ref[...]
        r = lax.rsqrt(jnp.mean(xv * xv, axis=-1, keepdims=True) + EPS)
        h_ref[...] = (((xv * r) * g_ref[...]) * (1.0 + sc_ref[...]) + sh_ref[...]).astype(bf16)

    row = pl.BlockSpec((tm, D), lambda i: (i, 0))
    vec = pl.BlockSpec((1, D), lambda i: (0, 0))
    return pl.pallas_call(body, name="pre_fwd", grid=(S // tm,), in_specs=[row, vec, vec, vec], out_specs=row,
                          out_shape=SDS((S, D), bf16), compiler_params=_cp(32))(x, g, sc, sh)


def pre_bwd(dh, x, g, sc, dxo, *, tm=256, after=()):
    S = x.shape[0]

    def body(dh_ref, x_ref, g_ref, sc_ref, dxo_ref, *rest):
        dx_ref, dsh_ref, dsc_ref, dg_ref = rest[len(after):]
        first = pl.program_id(0) == 0
        xv, dhv, gv = x_ref[...], dh_ref[...], g_ref[...]
        one_sc = 1.0 + sc_ref[...]
        r = lax.rsqrt(jnp.mean(xv * xv, axis=-1, keepdims=True) + EPS)
        xh = xv * r
        t = dhv * xh
        dxh = dhv * gv * one_sc
        dx_ref[...] = r * (dxh - xh * jnp.mean(dxh * xh, axis=-1, keepdims=True)) + dxo_ref[...]
        _acc(dsh_ref, jnp.sum(dhv, axis=0, keepdims=True), first)
        _acc(dsc_ref, jnp.sum(t * gv, axis=0, keepdims=True), first)
        _acc(dg_ref, jnp.sum(t * one_sc, axis=0, keepdims=True), first)

    row = pl.BlockSpec((tm, D), lambda i: (i, 0))
    vec = pl.BlockSpec((1, D), lambda i: (0, 0))
    return pl.pallas_call(
        body, name="pre_bwd", grid=(S // tm,),
        in_specs=[row, row, vec, vec, row] + [pl.BlockSpec(memory_space=pl.ANY)] * len(after), out_specs=[row, vec, vec, vec],
        out_shape=[SDS((S, D), f32), SDS((1, D), f32), SDS((1, D), f32), SDS((1, D), f32)],
        compiler_params=_cp(40))(dh, x, g, sc, dxo, *after)


def out_fwd(merged, w_out, x, rg, gp, *, tm=256):
    S = x.shape[0]

    def body(m_ref, w_ref, x_ref, rg_ref, gp_ref, y_ref, xn_ref):
        y = jnp.dot(m_ref[...], w_ref[...], preferred_element_type=f32)
        y_ref[...] = y
        r = lax.rsqrt(jnp.mean(y * y, axis=-1, keepdims=True) + EPS)
        xn_ref[...] = x_ref[...] + (1.0 + rg_ref[...]) * ((y * r) * gp_ref[...])

    row = pl.BlockSpec((tm, D), lambda i: (i, 0))
    vec = pl.BlockSpec((1, D), lambda i: (0, 0))
    return pl.pallas_call(
        body, name="out_fwd", grid=(S // tm,),
        in_specs=[row, pl.BlockSpec((D, D), lambda i: (0, 0)), row, vec, vec], out_specs=[row, row],
        out_shape=[SDS((S, D), f32), SDS((S, D), f32)], compiler_params=_cp(48))(merged, w_out, x, rg, gp)


def out_bwd(dxo, y, rg, gp, *, tm=256):
    S = y.shape[0]

    def body(dxo_ref, y_ref, rg_ref, gp_ref, dy_ref, drg_ref, dgp_ref):
        first = pl.program_id(0) == 0
        yv, dv, gv = y_ref[...], dxo_ref[...], gp_ref[...]
        r = lax.rsqrt(jnp.mean(yv * yv, axis=-1, keepdims=True) + EPS)
        yh = yv * r
        dn = dv * (1.0 + rg_ref[...])
        dyh = dn * gv
        dy_ref[...] = (r * (dyh - yh * jnp.mean(dyh * yh, axis=-1, keepdims=True))).astype(bf16)
        _acc(drg_ref, jnp.sum(dv * (yh * gv), axis=0, keepdims=True), first)
        _acc(dgp_ref, jnp.sum(dn * yh, axis=0, keepdims=True), first)

    row = pl.BlockSpec((tm, D), lambda i: (i, 0))
    vec = pl.BlockSpec((1, D), lambda i: (0, 0))
    return pl.pallas_call(
        body, name="out_bwd", grid=(S // tm,), in_specs=[row, row, vec, vec], out_specs=[row, vec, vec],
        out_shape=[SDS((S, D), bf16), SDS((1, D), f32), SDS((1, D), f32)], compiler_params=_cp(40))(dxo, y, rg, gp)


def _ml_spec(b, tm):
    return pl.BlockSpec((tm, LANE), lambda i, j: (i, ML0 + b * (D // LANE) + j))


def gate_fwd(proj, us, *, tm=2048):
    S = proj.shape[0]
    tm = min(tm, S)

    def body(ml0, ml1, ml2, u0, u1, u2, m_ref):
        acc = None
        for ml, u in ((ml0, u0), (ml1, u1), (ml2, u2)):
            t = _sigmoid(ml[...]) * u[...].astype(f32)
            acc = t if acc is None else acc + t
        m_ref[...] = acc.astype(bf16)

    blk = pl.BlockSpec((tm, LANE), lambda i, j: (i, j))
    return pl.pallas_call(
        body, name="gate_fwd", grid=(S // tm, D // LANE),
        in_specs=[_ml_spec(0, tm), _ml_spec(1, tm), _ml_spec(2, tm), blk, blk, blk], out_specs=blk,
        out_shape=SDS((S, D), bf16), compiler_params=_cp(32),
    )(proj, proj, proj, *us)


def gate_bwd(dmerged, proj, us, *, tm=2048):
    S = proj.shape[0]
    tm = min(tm, S)

    def body(dm_ref, ml0, ml1, ml2, u0, u1, u2, du0, du1, du2, dl0, dl1, dl2):
        dm = dm_ref[...]
        for ml, u, du, dl in ((ml0, u0, du0, dl0), (ml1, u1, du1, dl1), (ml2, u2, du2, dl2)):
            s = _sigmoid(ml[...])
            du[...] = (dm * s).astype(bf16)
            dl[...] = (dm * u[...].astype(f32) * (s * (1.0 - s))).astype(bf16)

    blk = pl.BlockSpec((tm, LANE), lambda i, j: (i, j))
    outs = pl.pallas_call(
        body, name="gate_bwd", grid=(S // tm, D // LANE),
        in_specs=[blk, _ml_spec(0, tm), _ml_spec(1, tm), _ml_spec(2, tm), blk, blk, blk], out_specs=[blk] * 6,
        out_shape=[SDS((S, D), bf16)] * 6, compiler_params=_cp(40),
    )(dmerged, proj, proj, proj, *us)
    return outs[:3], outs[3:]


def rope_tables(positions):
    pos = positions.astype(f32)[:, None]

    def cs(dim):
        inv = 10000.0 ** (-jnp.arange(0, dim, 2, dtype=f32) / dim)
        ang = pos * inv
        return jnp.cos(ang), jnp.sin(ang)

    c, s = cs(128)
    ret = (jnp.concatenate([c, c], 1), jnp.concatenate([-s, s], 1))
    c, s = cs(64)
    z32, z64 = jnp.zeros_like(c), jnp.zeros((c.shape[0], 64), f32)
    mla = (jnp.concatenate([c, c, z64], 1), jnp.concatenate([-s, z32, z64], 1), jnp.concatenate([z32, s, z64], 1))
    return ret, mla


def _rope_ret(x, c, s):
    return x * c + pltpu.roll(x, 64, 1) * s


def _rope_ret_t(dy, c, s):
    return dy * c + pltpu.roll(dy * s, 64, 1)


def _rope_mla(x, c, sa, sb):
    return x * c + pltpu.roll(x, 96, 1) * sa + pltpu.roll(x, 32, 1) * sb


def _rope_mla_t(dy, c, sa, sb):
    return dy * c + pltpu.roll(dy * sa, 32, 1) + pltpu.roll(dy * sb, 96, 1)


def _softmax_rows(q, k):
    bq, nk = q.shape[0], k.shape[0]
    s = lax.dot_general(q, k, _DIMS["nt"], preferred_element_type=f32)
    i = lax.broadcasted_iota(jnp.int32, (bq, bq), 0)
    j = lax.broadcasted_iota(jnp.int32, (bq, bq), 1)
    own = s[:, nk - bq:] + jnp.where((j // CHUNK) <= (i // CHUNK), 0.0, -1e30)
    s = own if nk == bq else jnp.concatenate([s[:, :nk - bq], own], axis=1)
    e = jnp.exp(s - jnp.max(s, axis=-1, keepdims=True))
    return e / jnp.sum(e, axis=-1, keepdims=True)


def _decay_rows(lg, bq, S):
    nq = S // bq
    i = lax.broadcasted_iota(jnp.int32, (bq, S), 0)
    col = lax.broadcasted_iota(jnp.int32, (bq, S), 1)
    j = col % bq
    back = nq - 1 - col // bq
    dist = back * bq + i - j
    seen = jnp.logical_or(back > 0, (j // CHUNK) <= (i // CHUNK))
    return jnp.where(seen, jnp.exp(lg * jnp.abs(dist).astype(f32)), 0.0)


def _per_query_block(nq, fn):
    for qi in range(nq):
        pl.when(pl.program_id(1) == qi)(functools.partial(fn, qi))


def mla_attn_fwd(q, k, v, proj, *, bq=256):
    S = q.shape[0]
    bq = min(bq, S)
    assert bq % CHUNK == 0

    def body(q_ref, k_ref, v_ref, g_ref, o_ref, y_ref):
        def block(qi):
            keys = pl.ds(0, (qi + 1) * bq)
            p = _softmax_rows(q_ref[...], k_ref[keys, :])
            o = jnp.dot(p.astype(bf16), v_ref[keys, :], preferred_element_type=f32)
            o_ref[...] = o
            y_ref[...] = (o * _silu(g_ref[...])).astype(bf16)

        _per_query_block(S // bq, block)

    o_spec = pl.BlockSpec((bq, DH), lambda h, i: (i, h))
    return pl.pallas_call(
        body, name="mla_attn_fwd", grid=(HEADS, S // bq),
        in_specs=[pl.BlockSpec((bq, QPAD), lambda h, i: (i, h)), pl.BlockSpec((S, QPAD), lambda h, i: (0, h)),
                  pl.BlockSpec((S, DH), lambda h, i: (0, h)), pl.BlockSpec((bq, DH), lambda h, i: (i, MG0 + h))],
        out_specs=[o_spec, o_spec], out_shape=[SDS((S, BW), f32), SDS((S, BW), bf16)],
        compiler_params=_cp(48))(q, k, v, proj)


def mla_attn_bwd(q, k, v, do, o, *, bq=256):
    S = q.shape[0]
    bq = min(bq, S)

    def body(q_ref, k_ref, v_ref, do_ref, o_ref, dq_ref, dk_ref, dv_ref):
        def block(qi):
            keys = pl.ds(0, (qi + 1) * bq)
            if qi == 0:
                dk_ref[...] = jnp.zeros_like(dk_ref)
                dv_ref[...] = jnp.zeros_like(dv_ref)
            qv, kv, dov = q_ref[...], k_ref[keys, :], do_ref[...]
            p = _softmax_rows(qv, kv)
            dp = lax.dot_general(dov, v_ref[keys, :], _DIMS["nt"], preferred_element_type=f32)
            delta = jnp.sum(dov.astype(f32) * o_ref[...], axis=-1, keepdims=True)
            dsb = (p * (dp - delta)).astype(bf16)
            dq_ref[...] = jnp.dot(dsb, kv, preferred_element_type=f32)
            dv_ref[keys, :] += lax.dot_general(p.astype(bf16), dov, _DIMS["tn"], preferred_element_type=f32)
            dk_ref[keys, :] += lax.dot_general(dsb, qv, _DIMS["tn"], preferred_element_type=f32)

        _per_query_block(S // bq, block)

    o_spec = pl.BlockSpec((bq, DH), lambda h, i: (i, h))
    return pl.pallas_call(
        body, name="mla_attn_bwd", grid=(HEADS, S // bq),
        in_specs=[pl.BlockSpec((bq, QPAD), lambda h, i: (i, h)), pl.BlockSpec((S, QPAD), lambda h, i: (0, h)),
                  pl.BlockSpec((S, DH), lambda h, i: (0, h)), o_spec, o_spec],
        out_specs=[pl.BlockSpec((bq, QPAD), lambda h, i: (i, h)), pl.BlockSpec((S, QPAD), lambda h, i: (0, h)),
                   pl.BlockSpec((S, DH), lambda h, i: (0, h))],
        out_shape=[SDS((S, HEADS * QPAD), f32), SDS((S, HEADS * QPAD), f32), SDS((S, BW), f32)],
        compiler_params=_cp(52))(q, k, v, do, o)


def ret_attn_fwd(proj, tabs, lgam, gn, *, bq=256):
    S = proj.shape[0]
    bq = min(bq, S)
    assert bq % CHUNK == 0

    def body(lg_ref, q_ref, k_ref, v_ref, g_ref, gn_ref, cq_ref, sq_ref, ck_ref, sk_ref, o_ref, y_ref, k_s, w_s):
        lg = lg_ref[pl.program_id(0)]

        def block(qi):
            if qi == 0:
                k_s[...] = _rope_ret(k_ref[...], ck_ref[...], sk_ref[...]).astype(bf16)
                w_s[...] = _decay_rows(lg, bq, S)
            keys = pl.ds(0, (qi + 1) * bq)
            qv = (_rope_ret(q_ref[...], cq_ref[...], sq_ref[...]) * RET_SCALE).astype(bf16)
            p = lax.dot_general(qv, k_s[keys, :], _DIMS["nt"], preferred_element_type=f32) * w_s[:, pl.ds(S - (qi + 1) * bq, (qi + 1) * bq)]
            o = jnp.dot(p.astype(bf16), v_ref[keys, :].astype(bf16), preferred_element_type=f32)
            o_ref[...] = o
            oc = o - jnp.mean(o, axis=-1, keepdims=True)
            z = oc * lax.rsqrt(jnp.mean(oc * oc, axis=-1, keepdims=True) + EPS) * gn_ref[...]
            y_ref[...] = (z * _silu(g_ref[...])).astype(bf16)

        _per_query_block(S // bq, block)

    row = lambda b0: pl.BlockSpec((bq, DH), lambda h, i: (i, b0 + h))
    full = lambda b0: pl.BlockSpec((S, DH), lambda h, i: (0, b0 + h))
    tq, tk = pl.BlockSpec((bq, DH), lambda h, i: (i, 0)), pl.BlockSpec((S, DH), lambda h, i: (0, 0))
    o_spec = pl.BlockSpec((bq, DH), lambda h, i: (i, h))
    return pl.pallas_call(
        body, name="ret_attn_fwd", grid=(HEADS, S // bq),
        in_specs=[pl.BlockSpec(memory_space=pltpu.SMEM), row(Q0), full(K0), full(V0), row(RG0),
                  pl.BlockSpec((1, DH), lambda h, i: (0, h)), tq, tq, tk, tk],
        out_specs=[o_spec, o_spec], out_shape=[SDS((S, BW), f32), SDS((S, BW), bf16)],
        scratch_shapes=[pltpu.VMEM((S, DH), bf16), pltpu.VMEM((bq, S), f32)], compiler_params=_cp(48),
    )(lgam, proj, proj, proj, proj, gn, tabs[0], tabs[1], tabs[0], tabs[1])


def ret_attn_bwd(proj, do, tabs, lgam, *, bq=256):
    S = proj.shape[0]
    bq = min(bq, S)
    nq = S // bq

    def body(lg_ref, q_ref, k_ref, v_ref, do_ref, cq_ref, sq_ref, ck_ref, sk_ref, dq_ref, dk_ref, dv_ref, k_s, dk_s, dv_s, w_s):
        lg = lg_ref[pl.program_id(0)]

        def block(qi):
            if qi == 0:
                k_s[...] = _rope_ret(k_ref[...], ck_ref[...], sk_ref[...]).astype(bf16)
                w_s[...] = _decay_rows(lg, bq, S)
                dk_s[...] = jnp.zeros_like(dk_s)
                dv_s[...] = jnp.zeros_like(dv_s)
            keys = pl.ds(0, (qi + 1) * bq)
            cq, sq = cq_ref[...], sq_ref[...]
            qv = (_rope_ret(q_ref[...], cq, sq) * RET_SCALE).astype(bf16)
            kv, dov = k_s[keys, :], do_ref[...]
            w = w_s[:, pl.ds(S - (qi + 1) * bq, (qi + 1) * bq)]
            p = lax.dot_general(qv, kv, _DIMS["nt"], preferred_element_type=f32) * w
            dp = lax.dot_general(dov, v_ref[keys, :].astype(bf16), _DIMS["nt"], preferred_element_type=f32)
            dsb = (dp * w).astype(bf16)
            dq_ref[...] = _rope_ret_t(jnp.dot(dsb, kv, preferred_element_type=f32) * RET_SCALE, cq, sq).astype(bf16)
            dv_s[keys, :] += lax.dot_general(p.astype(bf16), dov, _DIMS["tn"], preferred_element_type=f32)
            dk_s[keys, :] += lax.dot_general(dsb, qv, _DIMS["tn"], preferred_element_type=f32)
            if qi == nq - 1:
                dk_ref[...] = _rope_ret_t(dk_s[...], ck_ref[...], sk_ref[...]).astype(bf16)
                dv_ref[...] = dv_s[...].astype(bf16)

        _per_query_block(nq, block)

    row = lambda b0: pl.BlockSpec((bq, DH), lambda h, i: (i, b0 + h))
    full = lambda b0: pl.BlockSpec((S, DH), lambda h, i: (0, b0 + h))
    tq, tk = pl.BlockSpec((bq, DH), lambda h, i: (i, 0)), pl.BlockSpec((S, DH), lambda h, i: (0, 0))
    return pl.pallas_call(
        body, name="ret_attn_bwd", grid=(HEADS, nq),
        in_specs=[pl.BlockSpec(memory_space=pltpu.SMEM), row(Q0), full(K0), full(V0), row(0), tq, tq, tk, tk],
        out_specs=[row(0), full(0), full(0)], out_shape=[SDS((S, BW), bf16)] * 3,
        scratch_shapes=[pltpu.VMEM((S, DH), bf16), pltpu.VMEM((S, DH), f32), pltpu.VMEM((S, DH), f32),
                        pltpu.VMEM((bq, S), f32)],
        compiler_params=_cp(52),
    )(lgam, proj, proj, proj, do, tabs[0], tabs[1], tabs[0], tabs[1])


def ret_post_bwd(dy, o, proj, gn, *, tm=512):
    S = dy.shape[0]
    tm = min(tm, S)

    def body(dy_ref, o_ref, g_ref, gn_ref, do_ref, drg_ref, dgn_ref):
        ov, g, gnv, dyv = o_ref[...], g_ref[...], gn_ref[...], dy_ref[...]
        oc = ov - jnp.mean(ov, axis=-1, keepdims=True)
        rs = lax.rsqrt(jnp.mean(oc * oc, axis=-1, keepdims=True) + EPS)
        oh = oc * rs
        dz = dyv * _silu(g)
        drg_ref[...] = (dyv * (oh * gnv) * _dsilu(g)).astype(bf16)
        doh = dz * gnv
        do_ref[...] = (rs * (doh - jnp.mean(doh, axis=-1, keepdims=True)
                             - oh * jnp.mean(doh * oh, axis=-1, keepdims=True))).astype(bf16)
        _acc(dgn_ref, jnp.sum(dz * oh, axis=0, keepdims=True), pl.program_id(1) == 0)

    blk = pl.BlockSpec((tm, DH), lambda h, i: (i, h))
    vec = pl.BlockSpec((1, DH), lambda h, i: (0, h))
    return pl.pallas_call(
        body, name="ret_post_bwd", grid=(HEADS, S // tm),
        in_specs=[blk, blk, pl.BlockSpec((tm, DH), lambda h, i: (i, RG0 + h)), vec], out_specs=[blk, blk, vec],
        out_shape=[SDS((S, BW), bf16), SDS((S, BW), bf16), SDS((1, BW), f32)], compiler_params=_cp(32))(dy, o, proj, gn)


def mla_post_bwd(dy, o, proj, *, tm=512):
    S = dy.shape[0]
    tm = min(tm, S)

    def body(dy_ref, o_ref, g_ref, do_ref, dg_ref):
        g, dyv = g_ref[...], dy_ref[...]
        do_ref[...] = (dyv * _silu(g)).astype(bf16)
        dg_ref[...] = (dyv * o_ref[...] * _dsilu(g)).astype(bf16)

    blk = pl.BlockSpec((tm, DH), lambda i, h: (i, h))
    return pl.pallas_call(
        body, name="mla_post_bwd", grid=(S // tm, HEADS),
        in_specs=[blk, blk, pl.BlockSpec((tm, DH), lambda i, h: (i, MG0 + h))], out_specs=[blk, blk],
        out_shape=[SDS((S, BW), bf16), SDS((S, BW), bf16)], compiler_params=_cp(32))(dy, o, proj)


def mla_prep_fwd(proj, qnorm, kvnorm, wuq, wukv, tabs, *, tm=256, kv_blk=0):
    S = proj.shape[0]
    tm = min(tm, S)

    def body(mq_ref, mkv_ref, mkr_ref, qn_ref, kvn_ref, wuq_ref, wukv_ref, c_ref, sa_ref, sb_ref, q_ref, k_ref, v_ref):
        c, sa, sb = c_ref[...], sa_ref[...], sb_ref[...]
        mq, mkv = mq_ref[...], mkv_ref[...]
        qn = (mq * lax.rsqrt(jnp.mean(mq * mq, axis=-1, keepdims=True) + EPS) * qn_ref[...]).astype(bf16)
        kvn = (mkv * lax.rsqrt(jnp.mean(mkv * mkv, axis=-1, keepdims=True) + EPS) * kvn_ref[...]).astype(bf16)
        kr = _rope_mla(mkr_ref[...], c, sa, sb).astype(bf16)
        for h in range(HEADS):
            qh = jnp.dot(qn, wuq_ref[h], preferred_element_type=f32)
            q_ref[:, pl.ds(h * QPAD, DH)] = (qh[:, :DH] * MLA_SCALE).astype(bf16)
            q_ref[:, pl.ds(h * QPAD + DH, DH)] = (_rope_mla(qh[:, DH:], c, sa, sb) * MLA_SCALE).astype(bf16)
            kvh = jnp.dot(kvn, wukv_ref[h], preferred_element_type=f32)
            k_ref[:, pl.ds(h * QPAD, DH)] = kvh[:, :DH].astype(bf16)
            k_ref[:, pl.ds(h * QPAD + DH, DH)] = kr
            v_ref[:, pl.ds(h * DH, DH)] = kvh[:, DH:].astype(bf16)

    lat = lambda b: pl.BlockSpec((tm, LORA), lambda i: (i, b))
    tab = pl.BlockSpec((tm, DH), lambda i: (i, 0))
    vec = pl.BlockSpec((1, LORA), lambda i: (0, 0))
    wsp = pl.BlockSpec((HEADS, LORA, QPAD), lambda i: (0, 0, 0))
    wkv = pl.BlockSpec((HEADS, LORA, QPAD), lambda i: (0, 0, kv_blk))
    return pl.pallas_call(
        body, name="mla_prep_fwd", grid=(S // tm,),
        in_specs=[lat(MQ0 // 4), lat(MKV0 // 4), pl.BlockSpec((tm, DH), lambda i: (i, MKR0)), vec, vec, wsp, wkv,
                  tab, tab, tab],
        out_specs=[pl.BlockSpec((tm, HEADS * QPAD), lambda i: (i, 0))] * 2 + [pl.BlockSpec((tm, BW), lambda i: (i, 0))],
        out_shape=[SDS((S, HEADS * QPAD), bf16)] * 2 + [SDS((S, BW), bf16)], compiler_params=_cp(48),
    )(proj, proj, proj, qnorm, kvnorm, wuq, wukv, *tabs)


def mla_prep_bwd(dq256, dk256, dv, proj, qnorm, kvnorm, wuq, wukv, tabs, *, tm=256, kv_blk=0):
    S = proj.shape[0]
    tm = min(tm, S)

    def body(dq_ref, dk_ref, dv_ref, mq_ref, mkv_ref, qn_ref, kvn_ref, wuq_ref, wukv_ref, c_ref, sa_ref, sb_ref,
             dm_ref, dwuq_ref, dwukv_ref, dqn_ref, dkvn_ref):
        first = pl.program_id(0) == 0
        c, sa, sb = c_ref[...], sa_ref[...], sb_ref[...]
        mq, mkv = mq_ref[...], mkv_ref[...]
        rq = lax.rsqrt(jnp.mean(mq * mq, axis=-1, keepdims=True) + EPS)
        rkv = lax.rsqrt(jnp.mean(mkv * mkv, axis=-1, keepdims=True) + EPS)
        mqh, mkvh = mq * rq, mkv * rkv
        qn = (mqh * qn_ref[...]).astype(bf16)
        kvn = (mkvh * kvn_ref[...]).astype(bf16)
        dqn = jnp.zeros((tm, LORA), f32)
        dkvn = jnp.zeros((tm, LORA), f32)
        dkr = jnp.zeros((tm, DH), f32)
        for h in range(HEADS):
            da = dq_ref[:, pl.ds(h * QPAD, DH)] * MLA_SCALE
            db = _rope_mla_t(dq_ref[:, pl.ds(h * QPAD + DH, DH)] * MLA_SCALE, c, sa, sb)
            dqh = jnp.concatenate([da, db], axis=1).astype(bf16)
            dqn += lax.dot_general(dqh, wuq_ref[h], _DIMS["nt"], preferred_element_type=f32)
            _acc(dwuq_ref.at[h], lax.dot_general(qn, dqh, _DIMS["tn"], preferred_element_type=f32), first)
            dkr += dk_ref[:, pl.ds(h * QPAD + DH, DH)]
            dkvh = jnp.concatenate([dk_ref[:, pl.ds(h * QPAD, DH)], dv_ref[:, pl.ds(h * DH, DH)]], axis=1).astype(bf16)
            dkvn += lax.dot_general(dkvh, wukv_ref[h], _DIMS["nt"], preferred_element_type=f32)
            _acc(dwukv_ref.at[h], lax.dot_general(kvn, dkvh, _DIMS["tn"], preferred_element_type=f32), first)
        dmh = dqn * qn_ref[...]
        dm_ref[:, pl.ds(0, LORA)] = (rq * (dmh - mqh * jnp.mean(dmh * mqh, axis=-1, keepdims=True))).astype(bf16)
        dmh = dkvn * kvn_ref[...]
        dm_ref[:, pl.ds(LORA, LORA)] = (rkv * (dmh - mkvh * jnp.mean(dmh * mkvh, axis=-1, keepdims=True))).astype(bf16)
        dm_ref[:, pl.ds(2 * LORA, DH)] = _rope_mla_t(dkr, c, sa, sb).astype(bf16)
        _acc(dqn_ref, jnp.sum(dqn * mqh, axis=0, keepdims=True), first)
        _acc(dkvn_ref, jnp.sum(dkvn * mkvh, axis=0, keepdims=True), first)

    lat = lambda b: pl.BlockSpec((tm, LORA), lambda i: (i, b))
    tab = pl.BlockSpec((tm, DH), lambda i: (i, 0))
    vec = pl.BlockSpec((1, LORA), lambda i: (0, 0))
    wsp = pl.BlockSpec((HEADS, LORA, QPAD), lambda i: (0, 0, 0))
    wkv = pl.BlockSpec((HEADS, LORA, QPAD), lambda i: (0, 0, kv_blk))
    wide = pl.BlockSpec((tm, HEADS * QPAD), lambda i: (i, 0))
    return pl.pallas_call(
        body, name="mla_prep_bwd", grid=(S // tm,),
        in_specs=[wide, wide, pl.BlockSpec((tm, BW), lambda i: (i, 0)), lat(MQ0 // 4), lat(MKV0 // 4), vec, vec, wsp, wkv,
                  tab, tab, tab],
        out_specs=[pl.BlockSpec((tm, 2 * LORA + DH), lambda i: (i, 0)), wsp, wsp, vec, vec],
        out_shape=[SDS((S, 2 * LORA + DH), bf16), SDS((HEADS, LORA, QPAD), f32), SDS((HEADS, LORA, QPAD), f32),
                   SDS((1, LORA), f32), SDS((1, LORA), f32)],
        compiler_params=_cp(52),
    )(dq256, dk256, dv, proj, proj, qnorm, kvnorm, wuq, wukv, *tabs)


SUB = 8


def _scan_tiles(a_s, b_s, out, S, reverse):
    nt = S // SUB
    rows = lax.broadcasted_iota(jnp.int32, (SUB, LANE), 0)

    def tile(t, carry):
        base = pl.multiple_of((nt - 1 - t if reverse else t) * SUB, SUB)
        a, b = a_s[pl.ds(base, SUB), :], b_s[pl.ds(base, SUB), :]
        for d in (1, 2, 4):
            sh = SUB - d if reverse else d
            inside = rows < SUB - d if reverse else rows >= d
            a_n = jnp.where(inside, pltpu.roll(a, sh, 0), 1.0)
            b_n = jnp.where(inside, pltpu.roll(b, sh, 0), 0.0)
            b = a * b_n + b
            a = a * a_n
        res = a * carry + b
        out[pl.ds(base, SUB), :] = res
        edge = res[0:1, :] if reverse else res[SUB - 1:SUB, :]
        return jnp.broadcast_to(edge, (SUB, LANE))

    lax.fori_loop(0, nt, tile, jnp.zeros((SUB, LANE), f32))


def _shift_down(x, n, rows):
    return x if n == 0 else jnp.where(rows >= n, pltpu.roll(x, n, 0), 0.0)


def _shift_up(x, n, rows, S):
    return x if n == 0 else jnp.where(rows < S - n, pltpu.roll(x, S - n, 0), 0.0)


def _lru_gates(xb, cw, cb, wa, ba, wx, bx, lam, rows):
    xc = cb + cw[3:4, :] * xb
    for w in range(3):
        xc = xc + cw[w:w + 1, :] * _shift_down(xb, 3 - w, rows)
    xcb = xc.astype(bf16)
    r = _sigmoid(jnp.dot(xcb, wa, preferred_element_type=f32) + ba)
    i = _sigmoid(jnp.dot(xcb, wx, preferred_element_type=f32) + bx)
    sp = _softplus(-lam)
    la = (-LRU_C * r) * sp
    return xc, xcb, r, i, sp, la, jnp.exp(la)


def _lru_specs(S):
    col = lambda b0: pl.BlockSpec((S, LANE), lambda n: (0, b0 + n))
    vec = pl.BlockSpec((1, LANE), lambda n: (0, n))
    return col, vec, pl.BlockSpec((4, LANE), lambda n: (0, n)), pl.BlockSpec((1, LANE, LANE), lambda n: (n, 0, 0))


def lru_fwd(proj, cw, cb, wa, ba, wx, bx, lam):
    S = proj.shape[0]

    def body(x_ref, g_ref, cw_ref, cb_ref, wa_ref, ba_ref, wx_ref, bx_ref, lam_ref, h_ref, y_ref, a_s, b_s):
        rows = lax.broadcasted_iota(jnp.int32, (S, LANE), 0)
        xc, _, _, i, _, la, a = _lru_gates(x_ref[...], cw_ref[...], cb_ref[...], wa_ref[0].astype(bf16), ba_ref[...],
                                           wx_ref[0].astype(bf16), bx_ref[...], lam_ref[...], rows)
        a_s[...] = a
        b_s[...] = jnp.sqrt(_one_minus_exp(2.0 * la)) * (i * xc)
        _scan_tiles(a_s, b_s, h_ref, S, reverse=False)
        y_ref[...] = (h_ref[...] * _silu(g_ref[...])).astype(bf16)

    col, vec, cws, wsp = _lru_specs(S)
    return pl.pallas_call(
        body, name="lru_fwd", grid=(HEADS,),
        in_specs=[col(LX0), col(LG0), cws, vec, wsp, vec, wsp, vec, vec], out_specs=[col(0), col(0)],
        out_shape=[SDS((S, BW), f32), SDS((S, BW), bf16)],
        scratch_shapes=[pltpu.VMEM((S, LANE), f32), pltpu.VMEM((S, LANE), f32)], compiler_params=_cp(40),
    )(proj, proj, cw, cb, wa, ba, wx, bx, lam)


def lru_bwd(dy, h, proj, cw, cb, wa, ba, wx, bx, lam):
    S = proj.shape[0]

    def body(dy_ref, h_ref, x_ref, g_ref, cw_ref, cb_ref, wa_ref, ba_ref, wx_ref, bx_ref, lam_ref,
             dx_ref, dg_ref, dcw_ref, dcb_ref, dba_ref, dbx_ref, dlam_ref, dwa_ref, dwx_ref, a_s, b_s, l_s):
        rows = lax.broadcasted_iota(jnp.int32, (S, LANE), 0)
        xb, g, hv, dyv, cw, lam = x_ref[...], g_ref[...], h_ref[...], dy_ref[...], cw_ref[...], lam_ref[...]
        wa, wx = wa_ref[0].astype(bf16), wx_ref[0].astype(bf16)
        xc, xcb, r, i, sp, la, a = _lru_gates(xb, cw, cb_ref[...], wa, ba_ref[...], wx, bx_ref[...], lam, rows)
        dg_ref[...] = (dyv * hv * _dsilu(g)).astype(bf16)
        a_s[...] = _shift_up(a, 1, rows, S)
        b_s[...] = dyv * _silu(g)
        _scan_tiles(a_s, b_s, l_s, S, reverse=True)
        lmb = l_s[...]
        gated = i * xc
        sq = jnp.sqrt(_one_minus_exp(2.0 * la))
        dla = lmb * _shift_down(hv, 1, rows) * a - (lmb * gated) * (a * a) / sq
        dgated = lmb * sq
        dzr = (dla * (-LRU_C * sp)) * (r * (1.0 - r))
        dzi = (dgated * xc) * (i * (1.0 - i))
        dzrb, dzib = dzr.astype(bf16), dzi.astype(bf16)
        dxc = (dgated * i + lax.dot_general(dzrb, wa, _DIMS["nt"], preferred_element_type=f32)
               + lax.dot_general(dzib, wx, _DIMS["nt"], preferred_element_type=f32))
        dwa_ref[0] = lax.dot_general(xcb, dzrb, _DIMS["tn"], preferred_element_type=f32)
        dwx_ref[0] = lax.dot_general(xcb, dzib, _DIMS["tn"], preferred_element_type=f32)
        dba_ref[...] = jnp.sum(dzr, axis=0, keepdims=True)
        dbx_ref[...] = jnp.sum(dzi, axis=0, keepdims=True)
        dlam_ref[...] = jnp.sum(dla * (-LRU_C * r), axis=0, keepdims=True) * (-_sigmoid(-lam))
        dcb_ref[...] = jnp.sum(dxc, axis=0, keepdims=True)
        dxb = cw[3:4, :] * dxc
        dcw_ref[3:4, :] = jnp.sum(dxc * xb, axis=0, keepdims=True)
        for w in range(3):
            dxb = dxb + cw[w:w + 1, :] * _shift_up(dxc, 3 - w, rows, S)
            dcw_ref[w:w + 1, :] = jnp.sum(dxc * _shift_down(xb, 3 - w, rows), axis=0, keepdims=True)
        dx_ref[...] = dxb.astype(bf16)

    col, vec, cws, wsp = _lru_specs(S)
    scr = pltpu.VMEM((S, LANE), f32)
    return pl.pallas_call(
        body, name="lru_bwd", grid=(HEADS,),
        in_specs=[col(0), col(0), col(LX0), col(LG0), cws, vec, wsp, vec, wsp, vec, vec],
        out_specs=[col(0), col(0), cws, vec, vec, vec, vec, wsp, wsp],
        out_shape=[SDS((S, BW), bf16), SDS((S, BW), bf16), SDS((4, BW), f32)] + [SDS((1, BW), f32)] * 4
        + [SDS((HEADS, LANE, LANE), f32)] * 2,
        scratch_shapes=[scr, scr, scr], compiler_params=_cp(48),
    )(dy, h, proj, proj, cw, cb, wa, ba, wx, bx, lam)


def loss_head(y, target, *, tm=256):
    S = y.shape[0]

    def body(y_ref, t_ref, l_ref, d_ref):
        err = y_ref[...] - t_ref[...]
        d_ref[...] = err * (1.0 / D)
        part = jnp.sum(jnp.sum(err * err, axis=1, keepdims=True), axis=0, keepdims=True) * (0.5 / D)
        _acc(l_ref, jnp.broadcast_to(part, (1, LANE)), pl.program_id(0) == 0)

    row = pl.BlockSpec((tm, D), lambda i: (i, 0))
    return pl.pallas_call(
        body, name="loss_head", grid=(S // tm,), in_specs=[row, row],
        out_specs=[pl.BlockSpec((1, LANE), lambda i: (0, 0)), row],
        out_shape=[SDS((1, LANE), f32), SDS((S, D), f32)], compiler_params=_cp(32))(y, target)


def _adam_math(w, g, m, v):
    mn = ADAM_B1 * m + (1.0 - ADAM_B1) * g
    vn = ADAM_B2 * v + (1.0 - ADAM_B2) * (g * g)
    m_hat = mn / (1.0 - ADAM_B1 ** ADAM_STEP)
    v_hat = vn / (1.0 - ADAM_B2 ** ADAM_STEP)
    return -ADAM_LR * (m_hat / (jnp.sqrt(v_hat) + ADAM_EPS) + ADAM_WD * w), mn, vn


def _adam_rows(rows, cols):
    for cand in (2048, 1024, 512, 256, 128, 64, 32, 16, 8):
        if rows % cand == 0 and rows > cand and cand * cols <= ADAM_BLOCK_ELEMS:
            return cand
    return rows


def adamw(w, g, m, v):
    shape = w.shape
    cols = shape[-1]
    rows = math.prod(shape[:-1])
    tr = _adam_rows(rows, cols)

    def body(w_ref, g_ref, m_ref, v_ref, d_ref, mo_ref, vo_ref):
        d_ref[...], mo_ref[...], vo_ref[...] = _adam_math(w_ref[...], g_ref[...], m_ref[...], v_ref[...])

    blk = pl.BlockSpec((tr, cols), lambda i: (i, 0))
    flat = [t.reshape(rows, cols) for t in (w, g, m, v)]
    outs = pl.pallas_call(
        body, name="adamw", grid=(rows // tr,), in_specs=[blk] * 4, out_specs=[blk] * 3,
        out_shape=[SDS((rows, cols), f32)] * 3, compiler_params=_cp(48))(*flat)
    return tuple(o.reshape(shape) for o in outs)


ADA_SHARD = 3 * D // NDEV
ROWS16 = 16


def ada_fwd(c_all, ada_w, ada_b_mine):
    def body(c_ref, w_ref, b_ref, o_ref):
        o_ref[0] = jnp.dot(_silu(c_ref[...]).astype(bf16), w_ref[0].astype(bf16), preferred_element_type=f32) + b_ref[0]

    return pl.pallas_call(
        body, name="ada_fwd", grid=(DEPTH,),
        in_specs=[pl.BlockSpec((ROWS16, D), lambda l: (0, 0)), pl.BlockSpec((1, D, ADA_SHARD), lambda l: (l, 0, 0)),
                  pl.BlockSpec((1, 1, ADA_SHARD), lambda l: (l, 0, 0))],
        out_specs=pl.BlockSpec((1, ROWS16, ADA_SHARD), lambda l: (l, 0, 0)),
        out_shape=SDS((DEPTH, ROWS16, ADA_SHARD), f32), compiler_params=_cp(40))(c_all, ada_w, ada_b_mine)


def ada_bwd(c_all, dmod):
    def body(c_ref, d_ref, o_ref):
        o_ref[0] = lax.dot_general(_silu(c_ref[...]).astype(bf16), d_ref[0].astype(bf16), _DIMS["tn"],
                                   preferred_element_type=f32)

    return pl.pallas_call(
        body, name="ada_bwd", grid=(DEPTH,),
        in_specs=[pl.BlockSpec((ROWS16, D), lambda l: (0, 0)), pl.BlockSpec((1, ROWS16, ADA_SHARD), lambda l: (l, 0, 0))],
        out_specs=pl.BlockSpec((1, D, ADA_SHARD), lambda l: (l, 0, 0)),
        out_shape=SDS((DEPTH, D, ADA_SHARD), f32), compiler_params=_cp(40))(c_all, dmod)


def shift_params(me):
    start = SHARD_W * me
    return jnp.stack([start % LANE, (start + GAP) % LANE, jnp.clip(GAP_COL - start, 0, SHARD_W)]).astype(jnp.int32)


def win_pack(wt, sidx, *, first=0, count=DEPTH, after=()):
    def body(s_ref, w_ref, *rest):
        o_ref, scr = rest[len(after):]
        s1, s2, gi = s_ref[0], s_ref[1], s_ref[2]
        scr[pl.ds(SHARD_W, WIN - SHARD_W), :] = jnp.zeros((WIN - SHARD_W, LANE), f32)
        scr[pl.ds(0, SHARD_W), :] = w_ref[0]
        v = scr[...].T
        j = lax.broadcasted_iota(jnp.int32, v.shape, 1)
        o_ref[0] = jnp.where(j - s1 < gi, pltpu.roll(v, s1, 1),
                             jnp.where(j - s2 >= gi, pltpu.roll(v, s2, 1), 0.0)).astype(bf16)

    return pl.pallas_call(
        body, name="win_pack", grid=(count, D // LANE),
        in_specs=[pl.BlockSpec(memory_space=pltpu.SMEM), pl.BlockSpec((1, SHARD_W, LANE), lambda l, i: (first + l, 0, i))]
        + [pl.BlockSpec(memory_space=pl.ANY)] * len(after),
        out_specs=pl.BlockSpec((1, LANE, WIN), lambda l, i: (l, i, 0)),
        out_shape=SDS((count, D, WIN), bf16), scratch_shapes=[pltpu.VMEM((WIN, LANE), f32)],
        compiler_params=_cp(32))(sidx, wt, *after)


def win_assemble(g, *, after=()):
    own = WIN_STRIDE * LANE
    tail = NP - NDEV * own
    assert tail == 2 * LANE

    def body(a_ref, b_ref, *rest):
        o_ref = rest[len(after)]
        o_ref[...] = a_ref[0]

        @pl.when(pl.program_id(0) > 0)
        def _():
            o_ref[:, pl.ds(0, LANE)] = a_ref[0, :, pl.ds(0, LANE)] + b_ref[0]

    main = pl.pallas_call(
        body, name="win_assemble", grid=(NDEV,),
        in_specs=[pl.BlockSpec((1, D, own), lambda k: (k, 0, 0)),
                  pl.BlockSpec((1, D, LANE), lambda k: (jnp.maximum(k - 1, 0), 0, WIN_BLKS - 1))]
        + [pl.BlockSpec(memory_space=pl.ANY)] * len(after),
        out_specs=pl.BlockSpec((D, own), lambda k: (0, k)),
        out_shape=SDS((D, NP), bf16), compiler_params=_cp(48))(g, g, *after)

    def tail_body(_, b_ref, o_ref):
        o_ref[:, pl.ds(0, LANE)] = b_ref[0]
        o_ref[:, pl.ds(LANE, LANE)] = jnp.zeros((D, LANE), bf16)

    return pl.pallas_call(
        tail_body, name="win_assemble_tail", grid=(1,),
        in_specs=[pl.BlockSpec(memory_space=pl.ANY), pl.BlockSpec((1, D, LANE), lambda t: (NDEV - 1, 0, WIN_BLKS - 1))],
        out_specs=pl.BlockSpec((D, tail), lambda t: (0, NDEV * own // tail)),
        out_shape=SDS((D, NP), bf16), input_output_aliases={0: 0}, compiler_params=_cp(32))(main, g)


def reduce_adamw(stag, w, m, v, l, prev, name, *, sidx=None, tr=128):
    Cs = stag.shape[2]
    n_prev = 0 if prev is None else 4
    n_lead = 1 if sidx is not None else 0

    def body(*refs):
        refs = list(refs)
        s_ref = refs.pop(0) if sidx is not None else None
        g_ref, w_ref, m_ref, v_ref = refs[:4]
        rest = refs[4 + n_prev:]
        go_ref, d_ref, mo_ref, vo_ref = rest[:4]
        tot = g_ref[0].astype(f32)
        for d in range(1, stag.shape[0]):
            tot = tot + g_ref[d].astype(f32)
        if sidx is not None:
            scr = rest[4]
            s1, s2, gi = s_ref[0], s_ref[1], s_ref[2]
            i = lax.broadcasted_iota(jnp.int32, tot.shape, 1)
            scr[...] = jnp.where(i < gi, pltpu.roll(tot, WIN - s1, 1), pltpu.roll(tot, WIN - s2, 1)).T
            tot = scr[pl.ds(0, SHARD_W), :]
        go_ref[0] = tot
        d_ref[0], mo_ref[0], vo_ref[0] = _adam_math(w_ref[0], tot, m_ref[0], v_ref[0])

    if sidx is not None:
        blk3 = pl.BlockSpec((1, SHARD_W, tr), lambda i: (l, 0, i))
        steps = w.shape[2] // tr
    else:
        blk3 = pl.BlockSpec((1, tr, w.shape[2]), lambda i: (l, i, 0))
        steps = w.shape[1] // tr
    in_specs = ([pl.BlockSpec(memory_space=pltpu.SMEM)] * n_lead
                + [pl.BlockSpec((stag.shape[0], tr, Cs), lambda i: (0, i, 0)), blk3, blk3, blk3]
                + [pl.BlockSpec(memory_space=pl.ANY)] * n_prev)
    args = ([sidx] if sidx is not None else []) + [stag, w, m, v] + list(prev or ())
    return pl.pallas_call(
        body, name=name, grid=(steps,), in_specs=in_specs, out_specs=[blk3] * 4, out_shape=[SDS(w.shape, f32)] * 4,
        scratch_shapes=[pltpu.VMEM((Cs, tr), f32)] if sidx is not None else [],
        input_output_aliases={n_lead + 4 + k: k for k in range(n_prev)}, compiler_params=_cp(48),
    )(*args)


def cast_pad(w, cols_out, name):
    L, R, C = w.shape

    def body(w_ref, o_ref, *scr):
        if cols_out == C:
            o_ref[0] = w_ref[0].astype(bf16)
        else:
            scr[0][...] = jnp.zeros_like(scr[0])
            scr[0][:, pl.ds(0, C)] = w_ref[0]
            o_ref[0] = scr[0][...].astype(bf16)

    return pl.pallas_call(
        body, name=name, grid=(L,), in_specs=[pl.BlockSpec((1, R, C), lambda l: (l, 0, 0))],
        out_specs=pl.BlockSpec((1, R, cols_out), lambda l: (l, 0, 0)), out_shape=SDS((L, R, cols_out), bf16),
        scratch_shapes=[] if cols_out == C else [pltpu.VMEM((R, cols_out), f32)], compiler_params=_cp(32))(w)


def pack_qkv(w_uq, w_ukv):
    L = w_uq.shape[0]

    def body(q_ref, kv_ref, o_ref, scr):
        scr[...] = jnp.zeros_like(scr)
        scr[:, pl.ds(0, WUQ_COLS)] = q_ref[0]
        o_ref[0, :, pl.ds(0, QPAD)] = scr[...].astype(bf16)
        o_ref[0, :, pl.ds(QPAD, QPAD)] = kv_ref[0].astype(bf16)

    return pl.pallas_call(
        body, name="pack_qkv", grid=(L,),
        in_specs=[pl.BlockSpec((1, LORA, WUQ_COLS), lambda l: (l, 0, 0)), pl.BlockSpec((1, LORA, QPAD), lambda l: (l, 0, 0))],
        out_specs=pl.BlockSpec((1, LORA, 2 * QPAD), lambda l: (l, 0, 0)), out_shape=SDS((L, LORA, 2 * QPAD), bf16),
        scratch_shapes=[pltpu.VMEM((LORA, QPAD), f32)], compiler_params=_cp(32))(w_uq, w_ukv)


def sum_parts(stag, cols_out, name, *, tr=None, after=()):
    P, R, C = stag.shape
    tr = R if tr is None else tr

    def body(g_ref, *rest):
        o_ref, *scr = rest[len(after):]
        tot = g_ref[0].astype(f32)
        for d in range(1, P):
            tot = tot + g_ref[d].astype(f32)
        if cols_out == C:
            o_ref[...] = tot
        else:
            scr[0][...] = tot
            o_ref[...] = scr[0][:, pl.ds(0, cols_out)]

    return pl.pallas_call(
        body, name=name, grid=(R // tr,),
        in_specs=[pl.BlockSpec((P, tr, C), lambda i: (0, i, 0))] + [pl.BlockSpec(memory_space=pl.ANY)] * len(after),
        out_specs=pl.BlockSpec((tr, cols_out), lambda i: (i, 0)), out_shape=SDS((R, cols_out), f32),
        scratch_shapes=[] if cols_out == C else [pltpu.VMEM((tr, C), f32)], compiler_params=_cp(40))(stag, *after)


MESH_ID = pl.DeviceIdType.MESH
HBM_SPEC = pl.BlockSpec(memory_space=pltpu.HBM)


def _place():
    return lax.axis_index("x"), lax.axis_index("y"), lax.axis_index("c")


def all_gather(arrs, name):
    n = len(arrs)

    def body(*refs):
        ins, outs = refs[:n], refs[n:2 * n]
        send_sems, recv_sems, local_sems = refs[2 * n:]
        x, y, c = _place()
        me, sibling = (x, y, c), (x, y, 1 - c)
        chips = [(1 - x, y), (x, 1 - y), (1 - x, 1 - y)]

        def copy(a, k, block, to, src=None):
            slot = outs[a].at[4 * block[0] + 2 * block[1] + block[2]]
            return pltpu.make_async_remote_copy(
                src_ref=slot if src is None else src, dst_ref=slot, send_sem=send_sems.at[7 * a + k],
                recv_sem=recv_sems.at[7 * a + k], device_id=to, device_id_type=MESH_ID)

        mine = [pltpu.make_async_copy(ins[a], outs[a].at[4 * x + 2 * y + c], local_sems.at[a]) for a in range(n)]
        for cp in mine:
            cp.start()
        first = []
        for a in range(n):
            first.append(copy(a, 0, me, sibling, src=ins[a]))
            first += [copy(a, 1 + j, me, (*chip, c), src=ins[a]) for j, chip in enumerate(chips)]
        for cp in first:
            cp.start()
        passed = []
        for j, chip in enumerate(chips):
            for a in range(n):
                copy(a, 1 + j, (*chip, c), me).wait_recv()
                cp = copy(a, 4 + j, (*chip, c), sibling)
                cp.start()
                passed.append(cp)
        for a in range(n):
            copy(a, 0, sibling, me).wait_recv()
        for j, chip in enumerate(chips):
            for a in range(n):
                copy(a, 4 + j, (*chip, 1 - c), me).wait_recv()
        for cp in first + passed:
            cp.wait_send()
        for cp in mine:
            cp.wait()

    return pl.pallas_call(
        body, name=name, in_specs=[HBM_SPEC] * n, out_specs=[HBM_SPEC] * n,
        out_shape=[SDS((NDEV,) + a.shape, a.dtype) for a in arrs],
        scratch_shapes=[pltpu.SemaphoreType.DMA((7 * n,)), pltpu.SemaphoreType.DMA((7 * n,)),
                        pltpu.SemaphoreType.DMA((n,))],
    )(*arrs)


AG_COLLECTIVE_ID = 0
RS_COLLECTIVE_ID = 1


def _everyone_else(x, y, c):
    return [(x ^ (r >> 2), y ^ ((r >> 1) & 1), c ^ (r & 1)) for r in range(1, NDEV)]


def _rendezvous(sem, peers):
    for peer in peers:
        pl.semaphore_signal(sem, inc=1, device_id=peer, device_id_type=MESH_ID)
    pl.semaphore_wait(sem, len(peers))


def _sequencer_call(body, arrs, out_types, name, collective_id, remote=7, local=1):
    n = len(arrs)
    return pl.kernel(
        body, name=name, out_type=out_types, mesh=plsc.ScalarSubcoreMesh(axis_name="sequencer", num_cores=1),
        scratch_types=[pltpu.SemaphoreType.DMA((remote * n,)), pltpu.SemaphoreType.DMA((remote * n,)),
                       pltpu.SemaphoreType.DMA((local * n,)), pltpu.SemaphoreType.REGULAR],
        compiler_params=pltpu.CompilerParams(collective_id=collective_id),
    )(*arrs)


def seq_all_gather(arrs, name):
    n = len(arrs)

    def body(*refs):
        ins, outs = refs[:n], refs[n:2 * n]
        send_sems, recv_sems, local_sems, exit_sem = refs[2 * n:]
        x, y, c = _place()
        peers = _everyone_else(x, y, c)
        _rendezvous(pltpu.get_barrier_semaphore(), peers)
        me, sibling = (x, y, c), (x, y, 1 - c)
        chips = [(1 - x, y), (x, 1 - y), (1 - x, 1 - y)]

        def copy(a, k, block, to, src=None):
            slot = outs[a].at[4 * block[0] + 2 * block[1] + block[2]]
            return pltpu.make_async_remote_copy(
                src_ref=slot if src is None else src, dst_ref=slot, send_sem=send_sems.at[7 * a + k],
                recv_sem=recv_sems.at[7 * a + k], device_id=to, device_id_type=MESH_ID)

        mine = [pltpu.make_async_copy(ins[a], outs[a].at[4 * x + 2 * y + c], local_sems.at[a]) for a in range(n)]
        for cp in mine:
            cp.start()
        first = []
        for a in range(n):
            first.append(copy(a, 0, me, sibling, src=ins[a]))
            first += [copy(a, 1 + j, me, (*chip, c), src=ins[a]) for j, chip in enumerate(chips)]
        for cp in first:
            cp.start()
        passed = []
        for j, chip in enumerate(chips):
            for a in range(n):
                copy(a, 1 + j, (*chip, c), me).wait_recv()
                cp = copy(a, 4 + j, (*chip, c), sibling)
                cp.start()
                passed.append(cp)
        for a in range(n):
            copy(a, 0, sibling, me).wait_recv()
        for j, chip in enumerate(chips):
            for a in range(n):
                copy(a, 4 + j, (*chip, 1 - c), me).wait_recv()
        for cp in first + passed:
            cp.wait_send()
        for cp in mine:
            cp.wait()
        _rendezvous(exit_sem, peers)

    return _sequencer_call(body, arrs, [SDS((NDEV,) + a.shape, a.dtype) for a in arrs], name, AG_COLLECTIVE_ID)


def seq_reduce_scatter_parts(arrs, pick, shapes, name):
    n = len(arrs)

    def body(*refs):
        ins, outs = refs[:n], refs[n:2 * n]
        send_sems, recv_sems, local_sems, exit_sem = refs[2 * n:]
        x, y, c = _place()
        peers = _everyone_else(x, y, c)
        _rendezvous(pltpu.get_barrier_semaphore(), peers)
        me = 4 * x + 2 * y + c
        mine = [pltpu.make_async_copy(pick[a](ins[a], me), outs[a].at[me], local_sems.at[a]) for a in range(n)]
        for cp in mine:
            cp.start()
        sent = []
        for r, peer in enumerate(peers):
            pid = 4 * peer[0] + 2 * peer[1] + peer[2]
            for a in range(n):
                cp = pltpu.make_async_remote_copy(
                    src_ref=pick[a](ins[a], pid), dst_ref=outs[a].at[me], send_sem=send_sems.at[7 * a + r],
                    recv_sem=recv_sems.at[7 * a + r], device_id=peer, device_id_type=MESH_ID)
                cp.start()
                sent.append((cp, a, r, pid))
        for cp, a, r, pid in sent:
            pltpu.make_async_remote_copy(
                src_ref=pick[a](ins[a], pid), dst_ref=outs[a].at[pid], send_sem=send_sems.at[7 * a + r],
                recv_sem=recv_sems.at[7 * a + r], device_id=(x, y, c), device_id_type=MESH_ID).wait_recv()
        for cp, _, _, _ in sent:
            cp.wait_send()
        for cp in mine:
            cp.wait()
        _rendezvous(exit_sem, peers)

    return _sequencer_call(body, arrs, [SDS((NDEV,) + tuple(s), a.dtype) for s, a in zip(shapes, arrs)], name,
                           RS_COLLECTIVE_ID)


WEIGHTS = ("ada_w", "ada_b", "norm_pre", "norm_post", "w_in", "ret_gn", "lru_conv_w", "lru_conv_b", "lru_wa", "lru_ba",
           "lru_wx", "lru_bx", "lru_lambda", "mla_q_norm", "mla_w_uq", "mla_kv_norm", "mla_w_ukv", "w_branch", "w_out")
SMALL = ("norm_pre", "norm_post", "ret_gn", "lru_conv_w", "lru_conv_b", "lru_ba", "lru_bx", "lru_lambda", "mla_q_norm",
         "mla_kv_norm")
QKV_ROWS = LORA
QKV_BLOCK = (LORA + LANE, 2 * QPAD)
BR_ROWS = 3 * BW // NDEV
OUT_ROWS = D // NDEV
WUQ_COLS = 192


def _row(v):
    return v.reshape(1, -1)


def layer_fwd(xl, mod, p, wts, tabs, lgam):
    S = xl.shape[0]
    tm = min(S, 2048)
    sh, sc, rg = _row(mod[:D]), _row(mod[D:2 * D]), _row(mod[2 * D:])
    ret_tabs, mla_tabs = tabs
    h = pre_fwd(xl, _row(p["norm_pre"]), sc, sh)
    proj = matmul(h, wts["w_in"], dims="nn", M=S, N=NP, K=D, tm=tm, tn=768, tk=D, out_dtype=f32, name="mm_in")
    o_ret, y_ret = ret_attn_fwd(proj, ret_tabs, lgam, _row(p["ret_gn"]))
    h_lru, y_lru = lru_fwd(proj, p["conv_w"], _row(p["lru_conv_b"]), p["lru_wa"], _row(p["lru_ba"]), p["lru_wx"],
                           _row(p["lru_bx"]), _row(p["lru_lambda"]))
    q256, k256, vm = mla_prep_fwd(proj, _row(p["mla_q_norm"]), _row(p["mla_kv_norm"]), wts["w_qkv"], wts["w_qkv"], mla_tabs,
                                  kv_blk=1)
    o_mla, y_mla = mla_attn_fwd(q256, k256, vm, proj)
    ys = (y_ret, y_lru, y_mla)
    us = [matmul(ys[b], wts["w_branch"], dims="nn", M=S, N=D, K=BW, tm=min(S, 1024), tn=1024, tk=BW,
                 out_dtype=bf16, name="mm_branch", b_blk0=(b, 0)) for b in range(3)]
    merged = gate_fwd(proj, us)
    y, x_next = out_fwd(merged, wts["w_out"], xl, rg, _row(p["norm_post"]))
    saved = dict(x=xl, h=h, proj=proj, o_ret=o_ret, h_lru=h_lru, q256=q256, k256=k256, vm=vm, o_mla=o_mla,
                 ys=ys, us=us, merged=merged, y=y, sc=sc, rg=rg)
    return x_next, saved


def layer_bwd(dx, sv, p, wts, tabs, lgam, exchange):
    S = dx.shape[0]
    tm = min(S, 2048)
    ret_tabs, mla_tabs = tabs
    proj = sv["proj"]
    dy, d_rg, d_gpost = out_bwd(dx, sv["y"], sv["rg"], _row(p["norm_post"]))
    dmerged = matmul(dy, wts["w_out"], dims="nt", M=S, N=D, K=D, tm=min(S, 1024), tn=1024, tk=D, out_dtype=f32, name="mm_dmerged")
    dw_out = matmul(sv["merged"], dy, dims="tn", M=D, N=D, K=S, tm=1024, tn=1024, tk=min(S, 1024), out_dtype=bf16, name="mm_dwout")
    du, dml = gate_bwd(dmerged, proj, sv["us"])
    tmb, tkb = min(S, 1024), min(S, 1024)
    dys = [matmul(du[b], wts["w_branch"], dims="nt", M=S, N=BW, K=D, tm=tmb, tn=BW, tk=D, out_dtype=f32, name="mm_dybranch",
                  b_blk0=(b, 0)) for b in range(3)]
    dw_branch = jnp.concatenate(
        [matmul(sv["ys"][b], du[b], dims="tn", M=BW, N=D, K=S, tm=BW, tn=1024, tk=tkb, out_dtype=bf16, name="mm_dwbranch")
         for b in range(3)], axis=0)
    do, d_rgate, d_gn = ret_post_bwd(dys[0], sv["o_ret"], proj, _row(p["ret_gn"]))
    d_q, d_k, d_v = ret_attn_bwd(proj, do, ret_tabs, lgam)
    d_lx, d_lg, d_cw, d_cb, d_ba, d_bx, d_lam, d_wa, d_wx = lru_bwd(
        dys[1], sv["h_lru"], proj, p["conv_w"], _row(p["lru_conv_b"]), p["lru_wa"], _row(p["lru_ba"]), p["lru_wx"],
        _row(p["lru_bx"]), _row(p["lru_lambda"]))
    do, d_mg = mla_post_bwd(dys[2], sv["o_mla"], proj)
    dq256, dk256, dvm = mla_attn_bwd(sv["q256"], sv["k256"], sv["vm"], do, sv["o_mla"])
    d_lat, dw_uq, dw_ukv, d_qn, d_kvn = mla_prep_bwd(dq256, dk256, dvm, proj, _row(p["mla_q_norm"]), _row(p["mla_kv_norm"]),
                                                       wts["w_qkv"], wts["w_qkv"], mla_tabs, kv_blk=1)
    dproj = jnp.concatenate([d_q, d_k, d_v, d_rgate, d_lx, d_lg, d_lat, d_mg, *dml, jnp.zeros((S, LANE), bf16)], axis=1)
    dh = matmul(dproj, wts["w_in"], dims="nt", M=S, N=D, K=NP, tm=min(S, 1024), tn=1024, tk=NP // 6, out_dtype=f32, name="mm_dh")
    dw_in = matmul(sv["h"], dproj, dims="tn", M=D, N=NP, K=S, tm=D, tn=768, tk=tm, out_dtype=bf16, name="mm_dwin")
    dw_qkv = jnp.concatenate([jnp.concatenate([dw_uq, dw_ukv], axis=2),
                              jnp.concatenate([d_wa, d_wx, jnp.zeros((HEADS, LANE, QPAD), f32)], axis=2)], axis=1).astype(bf16)
    staged, launched_from = exchange(dict(w_in=dw_in, w_branch=dw_branch, w_out=dw_out, w_qkv=dw_qkv))
    dxl, d_sh, d_sc, d_gpre = pre_bwd(dh, sv["x"], _row(p["norm_pre"]), sv["sc"], dx, after=tuple(launched_from))
    dmod = jnp.concatenate([d_sh, d_sc, d_rg], axis=1).reshape(-1)
    small = dict(norm_pre=d_gpre, norm_post=d_gpost, ret_gn=d_gn, lru_conv_w=d_cw, lru_conv_b=d_cb, lru_ba=d_ba,
                 lru_bx=d_bx, lru_lambda=d_lam, mla_q_norm=d_qn, mla_kv_norm=d_kvn)
    return dxl, dmod, staged, small


def exchange_grads(big, l):
    picks = [lambda r, d: r.at[:, pl.ds(pl.multiple_of(d * (WIN_STRIDE * LANE), LANE), WIN)],
             lambda r, d: r.at[pl.ds(pl.multiple_of(d * BR_ROWS, 8), BR_ROWS), :],
             lambda r, d: r.at[pl.ds(pl.multiple_of(d * OUT_ROWS, 8), OUT_ROWS), :],
             lambda r, d: r.at[d]]
    shapes = [(D, WIN), (BR_ROWS, D), (OUT_ROWS, D), QKV_BLOCK]
    arrs = [big["w_in"], big["w_branch"], big["w_out"], big["w_qkv"]]
    return seq_reduce_scatter_parts(arrs, picks, shapes, f"rs_grads_{l}"), arrs


def kernel(x, c, positions, ada_w, ada_b, norm_pre, norm_post, w_in, ret_gn, lru_conv_w, lru_conv_b, lru_wa, lru_ba, lru_wx, lru_bx, lru_lambda, mla_q_norm, mla_w_uq, mla_kv_norm, mla_w_ukv, w_branch, w_out, loss_target, m_ada_w, m_ada_b, m_norm_pre, m_norm_post, m_w_in, m_ret_gn, m_lru_conv_w, m_lru_conv_b, m_lru_wa, m_lru_ba, m_lru_wx, m_lru_bx, m_lru_lambda, m_mla_q_norm, m_mla_w_uq, m_mla_kv_norm, m_mla_w_ukv, m_w_branch, m_w_out, v_ada_w, v_ada_b, v_norm_pre, v_norm_post, v_w_in, v_ret_gn, v_lru_conv_w, v_lru_conv_b, v_lru_wa, v_lru_ba, v_lru_wx, v_lru_bx, v_lru_lambda, v_mla_q_norm, v_mla_w_uq, v_mla_kv_norm, v_mla_w_ukv, v_w_branch, v_w_out):
    given = dict(locals())
    xi, yi, ci = _place()
    me = 4 * xi + 2 * yi + ci
    S = x.shape[1]
    sidx = shift_params(me)
    lgam = jnp.asarray(np.log1p(-np.exp2(-5.0 - np.arange(HEADS))), f32)
    tabs = rope_tables(positions[0])

    (g_small,) = all_gather([jnp.concatenate([c.reshape(16, LANE), lru_conv_w.reshape(16, LANE)], axis=0)], "ag_small")
    c16 = jnp.concatenate([g_small[:, :16].reshape(NDEV, D), jnp.zeros((ROWS16 - NDEV, D), f32)], axis=0)
    conv_w_all = g_small[:, 16:].reshape(NDEV, DEPTH, 4, LANE).transpose(1, 2, 0, 3).reshape(DEPTH, 4, BW)
    ada_b_mine = lax.dynamic_slice_in_dim(ada_b, me * ADA_SHARD, ADA_SHARD, axis=1).reshape(DEPTH, 1, ADA_SHARD)
    (g_mod,) = all_gather([ada_fwd(c16, ada_w, ada_b_mine)[:, :NDEV]], "ag_mod")
    mods = lax.dynamic_index_in_dim(g_mod, me, axis=2, keepdims=False).transpose(1, 0, 2).reshape(DEPTH, 3 * D)
    w_in_t = {n: jnp.swapaxes(given[n], 1, 2) for n in ("w_in", "m_w_in", "v_w_in")}
    win0 = win_pack(w_in_t["w_in"], sidx, count=1, after=(g_mod,))
    win_rest = win_pack(w_in_t["w_in"], sidx, first=1, count=DEPTH - 1, after=(win0,))
    wbr_p, wout_p, qkv_p = cast_pad(w_branch, D, "pack_wbranch"), cast_pad(w_out, D, "pack_wout"), pack_qkv(mla_w_uq, mla_w_ukv)
    first, rest = seq_all_gather([win0[0], qkv_p[0]], "ag_weights_0a"), seq_all_gather([wbr_p[0], wout_p[0]], "ag_weights_0b")
    gathered = [(first[0], rest[0], rest[1], first[1])]
    gathered += [seq_all_gather([win_rest[l - 1], wbr_p[l], wout_p[l], qkv_p[l]], f"ag_weights_{l}") for l in range(1, DEPTH)]

    params, wts = [], []
    for l in range(DEPTH):
        p = {n: given[n][l] for n in SMALL + ("lru_wa", "lru_wx") if n != "lru_conv_w"}
        p["conv_w"] = conv_w_all[l]
        params.append(p)

    xl, saved = x[0], []
    for l in range(DEPTH):
        g_win, g_br, g_out, g_qkv = gathered[l]
        wts.append(dict(w_in=win_assemble(g_win, after=(xl,)), w_qkv=g_qkv, w_branch=g_br.reshape(3 * BW, D),
                        w_out=g_out.reshape(D, D)))
        xl, sv = layer_fwd(xl, mods[l], params[l], wts[l], tabs, lgam)
        saved.append(sv)
    my_loss, dx = loss_head(xl, loss_target[0])
    loss = lax.psum(my_loss[0, 0], ("x", "y", "c"))

    dmods, smalls, staged, grads = [None] * DEPTH, [None] * DEPTH, [None] * DEPTH, {n: [None] * DEPTH for n in WEIGHTS}
    for l in reversed(range(DEPTH)):
        dx, dmods[l], staged[l], smalls[l] = layer_bwd(dx, saved[l], params[l], wts[l], tabs, lgam,
                                                       functools.partial(exchange_grads, l=l))
    chained, lru_blocks = {"w_in": None, "w_branch": None, "w_out": None}, [None] * DEPTH
    for l in reversed(range(DEPTH)):
        st_win, st_br, st_out, st_qkv = staged[l]
        chained["w_in"] = reduce_adamw(st_win, w_in_t["w_in"], w_in_t["m_w_in"], w_in_t["v_w_in"], l, chained["w_in"],
                                       "update_w_in", sidx=sidx)
        for n, st in (("w_branch", st_br), ("w_out", st_out)):
            chained[n] = reduce_adamw(st, given[n], given["m_" + n], given["v_" + n], l, chained[n], "update_" + n)
        g_qkv = sum_parts(st_qkv, 2 * QPAD, "sum_wqkv", after=(chained["w_in"][0],) if l == 0 else ())
        grads["mla_w_uq"][l] = g_qkv[:QKV_ROWS, :WUQ_COLS]
        grads["mla_w_ukv"][l] = g_qkv[:QKV_ROWS, QPAD:]
        lru_blocks[l] = g_qkv[QKV_ROWS:, :2 * LANE]
    (g_lru,) = all_gather([jnp.stack(lru_blocks)], "ag_lru_w")
    grads["lru_wa"] = g_lru[..., :LANE].transpose(1, 0, 2, 3)
    grads["lru_wx"] = g_lru[..., LANE:].transpose(1, 0, 2, 3)

    flat = [jnp.stack(dmods).reshape(-1)] + [smalls[l][n].reshape(-1) for n in SMALL for l in range(DEPTH)]
    (g_pack,) = all_gather([jnp.concatenate(flat).reshape(-1, LANE)], "ag_small_grads")
    rows = g_pack.shape[1]
    tot = sum_parts(g_pack, LANE, "sum_small_grads", tr=rows // 8).reshape(-1)
    grads["ada_b"] = tot[:DEPTH * 3 * D].reshape(DEPTH, 3 * D)
    off = DEPTH * 3 * D
    for n in SMALL:
        size = DEPTH * int(smalls[0][n].size)
        piece = tot[off:off + size]
        off += size
        if n == "lru_conv_w":
            grads[n] = lax.dynamic_slice_in_dim(piece.reshape(DEPTH, 4, BW), me * LANE, LANE, axis=2)
        else:
            grads[n] = piece.reshape(given[n].shape)
    dmod_all = g_pack[:, :DEPTH * 3 * D // LANE].reshape(NDEV, DEPTH, 3 * D)
    dmod_mine = lax.dynamic_slice_in_dim(dmod_all, me * ADA_SHARD, ADA_SHARD, axis=2).transpose(1, 0, 2)
    dmod16 = jnp.concatenate([dmod_mine, jnp.zeros((DEPTH, ROWS16 - NDEV, ADA_SHARD), f32)], axis=1)
    grads["ada_w"] = ada_bwd(c16, dmod16)

    outs = {"grad": [], "delta": [], "m": [], "v": []}
    for n in WEIGHTS:
        if n in chained:
            g, delta, new_m, new_v = (jnp.swapaxes(t, 1, 2) for t in chained[n]) if n == "w_in" else chained[n]
        else:
            g = grads[n] if not isinstance(grads[n], list) else jnp.stack(grads[n])
            delta, new_m, new_v = adamw(given[n], g, given["m_" + n], given["v_" + n])
        outs["grad"].append(g)
        outs["delta"].append(delta)
        outs["m"].append(new_m)
        outs["v"].append(new_v)
    return (loss, dx[None], *outs["grad"], *outs["delta"], *outs["m"], *outs["v"])
```

```python
import functools
import math

import numpy as np
import jax
import jax.numpy as jnp
from jax import lax
from jax.experimental import pallas as pl
from jax.experimental.pallas import tpu as pltpu
from jax.experimental.pallas import tpu_sc as plsc

f32 = jnp.float32
bf16 = jnp.bfloat16
SDS = jax.ShapeDtypeStruct

DEPTH = 4
D = 2048
HEADS = 8
DH = 128
BW = HEADS * DH
LANE = 128
CHUNK = 64
EPS = 1e-6
LRU_C = 8.0
NDEV = 8
VMEM_V7X = 64 * 1024 * 1024

Q0, K0, V0, RG0, LX0, LG0, MQ0, MKV0, MKR0, MG0, ML0 = 0, 8, 16, 24, 32, 40, 48, 52, 56, 57, 65
NB = 114
NP = NB * LANE
IN_W = 14400
SHARD_W = IN_W // NDEV
GAP_COL = 7232
GAP = 64
WIN = 1920
WIN_BLKS = WIN // LANE
WIN_STRIDE = 14
LORA = 512
QPAD = 256

ADAM_LR, ADAM_B1, ADAM_B2, ADAM_EPS, ADAM_WD, ADAM_STEP = 0.001, 0.9, 0.999, 1e-08, 0.01, 10
ADAM_BLOCK_ELEMS = 256 * 1024
MLA_SCALE = (128 + 64) ** -0.5
RET_SCALE = 128 ** -0.5


def _cp(vmem_mb=None, **kw):
    if vmem_mb is not None:
        kw["vmem_limit_bytes"] = min(vmem_mb * 1024 * 1024, VMEM_V7X - 8 * 1024 * 1024)
    return pltpu.CompilerParams(**kw)


def _sigmoid(x):
    return 1.0 / (1.0 + jnp.exp(-x))


def _silu(x):
    return x * _sigmoid(x)


def _dsilu(x):
    s = _sigmoid(x)
    return s * (1.0 + x * (1.0 - s))


def _softplus(x):
    return jnp.maximum(x, 0.0) + jnp.log(1.0 + jnp.exp(-jnp.abs(x)))


def _one_minus_exp(y):
    series = -y * (1.0 + y * (0.5 + y * (1.0 / 6.0)))
    return jnp.where(y > -1e-2, series, 1.0 - jnp.exp(y))


def _acc(ref, val, first):
    @pl.when(first)
    def _():
        ref[...] = val

    @pl.when(jnp.logical_not(first))
    def _():
        ref[...] += val


_DIMS = {"nn": (((1,), (0,)), ((), ())), "nt": (((1,), (1,)), ((), ())), "tn": (((0,), (0,)), ((), ()))}


def matmul(a, b, *, dims, M, N, K, tm, tn, tk, out_dtype, name, a_blk0=(0, 0), b_blk0=(0, 0), vmem_mb=48):
    nk = K // tk
    assert M % tm == 0 and N % tn == 0 and K % tk == 0
    dn = _DIMS[dims]

    def body(a_ref, b_ref, o_ref, *scr):
        part = lax.dot_general(a_ref[...].astype(bf16), b_ref[...].astype(bf16), dn, preferred_element_type=f32)
        if nk == 1:
            o_ref[...] = part.astype(out_dtype)
        else:
            acc = scr[0]
            k = pl.program_id(2)
            _acc(acc, part, k == 0)

            @pl.when(k == nk - 1)
            def _():
                o_ref[...] = acc[...].astype(out_dtype)

    ar, ac = a_blk0
    br, bc = b_blk0
    if dims == "nn":
        a_spec = pl.BlockSpec((tm, tk), lambda i, j, k: (i + ar, k + ac))
        b_spec = pl.BlockSpec((tk, tn), lambda i, j, k: (k + br, j + bc))
    elif dims == "nt":
        a_spec = pl.BlockSpec((tm, tk), lambda i, j, k: (i + ar, k + ac))
        b_spec = pl.BlockSpec((tn, tk), lambda i, j, k: (j + br, k + bc))
    else:
        a_spec = pl.BlockSpec((tk, tm), lambda i, j, k: (k + ar, i + ac))
        b_spec = pl.BlockSpec((tk, tn), lambda i, j, k: (k + br, j + bc))
    return pl.pallas_call(
        body, name=name, grid=(M // tm, N // tn, nk),
        in_specs=[a_spec, b_spec], out_specs=pl.BlockSpec((tm, tn), lambda i, j, k: (i, j)),
        out_shape=SDS((M, N), out_dtype),
        scratch_shapes=[] if nk == 1 else [pltpu.VMEM((tm, tn), f32)],
        compiler_params=_cp(vmem_mb, dimension_semantics=("parallel", "parallel", "arbitrary")),
    )(a, b)


def pre_fwd(x, g, sc, sh, *, tm=256):
    S = x.shape[0]

    def body(x_ref, g_ref, sc_ref, sh_ref, h_ref):
        xv = x_ref[...]
        r = lax.rsqrt(jnp.mean(xv * xv, axis=-1, keepdims=True) + EPS)
        h_ref[...] = (((xv * r) * g_ref[...]) * (1.0 + sc_ref[...]) + sh_ref[...]).astype(bf16)

    row = pl.BlockSpec((tm, D), lambda i: (i, 0))
    vec = pl.BlockSpec((1, D), lambda i: (0, 0))
    return pl.pallas_call(body, name="pre_fwd", grid=(S // tm,), in_specs=[row, vec, vec, vec], out_specs=row,
                          out_shape=SDS((S, D), bf16), compiler_params=_cp(32))(x, g, sc, sh)


def pre_bwd(dh, x, g, sc, dxo, *, tm=256, after=()):
    S = x.shape[0]

    def body(dh_ref, x_ref, g_ref, sc_ref, dxo_ref, *rest):
        dx_ref, dsh_ref, dsc_ref, dg_ref = rest[len(after):]
        first = pl.program_id(0) == 0
        xv, dhv, gv = x_ref[...], dh_ref[...], g_ref[...]
        one_sc = 1.0 + sc_ref[...]
        r = lax.rsqrt(jnp.mean(xv * xv, axis=-1, keepdims=True) + EPS)
        xh = xv * r
        t = dhv * xh
        dxh = dhv * gv * one_sc
        dx_ref[...] = r * (dxh - xh * jnp.mean(dxh * xh, axis=-1, keepdims=True)) + dxo_ref[...]
        _acc(dsh_ref, jnp.sum(dhv, axis=0, keepdims=True), first)
        _acc(dsc_ref, jnp.sum(t * gv, axis=0, keepdims=True), first)
        _acc(dg_ref, jnp.sum(t * one_sc, axis=0, keepdims=True), first)

    row = pl.BlockSpec((tm, D), lambda i: (i, 0))
    vec = pl.BlockSpec((1, D), lambda i: (0, 0))
    return pl.pallas_call(
        body, name="pre_bwd", grid=(S // tm,),
        in_specs=[row, row, vec, vec, row] + [pl.BlockSpec(memory_space=pl.ANY)] * len(after), out_specs=[row, vec, vec, vec],
        out_shape=[SDS((S, D), f32), SDS((1, D), f32), SDS((1, D), f32), SDS((1, D), f32)],
        compiler_params=_cp(40))(dh, x, g, sc, dxo, *after)


def out_fwd(merged, w_out, x, rg, gp, *, tm=256):
    S = x.shape[0]

    def body(m_ref, w_ref, x_ref, rg_ref, gp_ref, y_ref, xn_ref):
        y = jnp.dot(m_ref[...], w_ref[...], preferred_element_type=f32)
        y_ref[...] = y
        r = lax.rsqrt(jnp.mean(y * y, axis=-1, keepdims=True) + EPS)
        xn_ref[...] = x_ref[...] + (1.0 + rg_ref[...]) * ((y * r) * gp_ref[...])

    row = pl.BlockSpec((tm, D), lambda i: (i, 0))
    vec = pl.BlockSpec((1, D), lambda i: (0, 0))
    return pl.pallas_call(
        body, name="out_fwd", grid=(S // tm,),
        in_specs=[row, pl.BlockSpec((D, D), lambda i: (0, 0)), row, vec, vec], out_specs=[row, row],
        out_shape=[SDS((S, D), f32), SDS((S, D), f32)], compiler_params=_cp(48))(merged, w_out, x, rg, gp)


def out_bwd(dxo, y, rg, gp, *, tm=256):
    S = y.shape[0]

    def body(dxo_ref, y_ref, rg_ref, gp_ref, dy_ref, drg_ref, dgp_ref):
        first = pl.program_id(0) == 0
        yv, dv, gv = y_ref[...], dxo_ref[...], gp_ref[...]
        r = lax.rsqrt(jnp.mean(yv * yv, axis=-1, keepdims=True) + EPS)
        yh = yv * r
        dn = dv * (1.0 + rg_ref[...])
        dyh = dn * gv
        dy_ref[...] = (r * (dyh - yh * jnp.mean(dyh * yh, axis=-1, keepdims=True))).astype(bf16)
        _acc(drg_ref, jnp.sum(dv * (yh * gv), axis=0, keepdims=True), first)
        _acc(dgp_ref, jnp.sum(dn * yh, axis=0, keepdims=True), first)

    row = pl.BlockSpec((tm, D), lambda i: (i, 0))
    vec = pl.BlockSpec((1, D), lambda i: (0, 0))
    return pl.pallas_call(
        body, name="out_bwd", grid=(S // tm,), in_specs=[row, row, vec, vec], out_specs=[row, vec, vec],
        out_shape=[SDS((S, D), bf16), SDS((1, D), f32), SDS((1, D), f32)], compiler_params=_cp(40))(dxo, y, rg, gp)


def _ml_spec(b, tm):
    return pl.BlockSpec((tm, LANE), lambda i, j: (i, ML0 + b * (D // LANE) + j))


def gate_fwd(proj, us, *, tm=2048):
    S = proj.shape[0]
    tm = min(tm, S)

    def body(ml0, ml1, ml2, u0, u1, u2, m_ref):
        acc = None
        for ml, u in ((ml0, u0), (ml1, u1), (ml2, u2)):
            t = _sigmoid(ml[...]) * u[...].astype(f32)
            acc = t if acc is None else acc + t
        m_ref[...] = acc.astype(bf16)

    blk = pl.BlockSpec((tm, LANE), lambda i, j: (i, j))
    return pl.pallas_call(
        body, name="gate_fwd", grid=(S // tm, D // LANE),
        in_specs=[_ml_spec(0, tm), _ml_spec(1, tm), _ml_spec(2, tm), blk, blk, blk], out_specs=blk,
        out_shape=SDS((S, D), bf16), compiler_params=_cp(32),
    )(proj, proj, proj, *us)


def gate_bwd(dmerged, proj, us, *, tm=2048):
    S = proj.shape[0]
    tm = min(tm, S)

    def body(dm_ref, ml0, ml1, ml2, u0, u1, u2, du0, du1, du2, dl0, dl1, dl2):
        dm = dm_ref[...]
        for ml, u, du, dl in ((ml0, u0, du0, dl0), (ml1, u1, du1, dl1), (ml2, u2, du2, dl2)):
            s = _sigmoid(ml[...])
            du[...] = (dm * s).astype(bf16)
            dl[...] = (dm * u[...].astype(f32) * (s * (1.0 - s))).astype(bf16)

    blk = pl.BlockSpec((tm, LANE), lambda i, j: (i, j))
    outs = pl.pallas_call(
        body, name="gate_bwd", grid=(S // tm, D // LANE),
        in_specs=[blk, _ml_spec(0, tm), _ml_spec(1, tm), _ml_spec(2, tm), blk, blk, blk], out_specs=[blk] * 6,
        out_shape=[SDS((S, D), bf16)] * 6, compiler_params=_cp(40),
    )(dmerged, proj, proj, proj, *us)
    return outs[:3], outs[3:]


def rope_tables(positions):
    pos = positions.astype(f32)[:, None]

    def cs(dim):
        inv = 10000.0 ** (-jnp.arange(0, dim, 2, dtype=f32) / dim)
        ang = pos * inv
        return jnp.cos(ang), jnp.sin(ang)

    c, s = cs(128)
    ret = (jnp.concatenate([c, c], 1), jnp.concatenate([-s, s], 1))
    c, s = cs(64)
    z32, z64 = jnp.zeros_like(c), jnp.zeros((c.shape[0], 64), f32)
    mla = (jnp.concatenate([c, c, z64], 1), jnp.concatenate([-s, z32, z64], 1), jnp.concatenate([z32, s, z64], 1))
    return ret, mla


def _rope_ret(x, c, s):
    return x * c + pltpu.roll(x, 64, 1) * s


def _rope_ret_t(dy, c, s):
    return dy * c + pltpu.roll(dy * s, 64, 1)


def _rope_mla(x, c, sa, sb):
    return x * c + pltpu.roll(x, 96, 1) * sa + pltpu.roll(x, 32, 1) * sb


def _rope_mla_t(dy, c, sa, sb):
    return dy * c + pltpu.roll(dy * sa, 32, 1) + pltpu.roll(dy * sb, 96, 1)


def _softmax_rows(q, k):
    bq, nk = q.shape[0], k.shape[0]
    s = lax.dot_general(q, k, _DIMS["nt"], preferred_element_type=f32)
    i = lax.broadcasted_iota(jnp.int32, (bq, bq), 0)
    j = lax.broadcasted_iota(jnp.int32, (bq, bq), 1)
    own = s[:, nk - bq:] + jnp.where((j // CHUNK) <= (i // CHUNK), 0.0, -1e30)
    s = own if nk == bq else jnp.concatenate([s[:, :nk - bq], own], axis=1)
    e = jnp.exp(s - jnp.max(s, axis=-1, keepdims=True))
    return e / jnp.sum(e, axis=-1, keepdims=True)


def _decay_rows(lg, bq, S):
    nq = S // bq
    i = lax.broadcasted_iota(jnp.int32, (bq, S), 0)
    col = lax.broadcasted_iota(jnp.int32, (bq, S), 1)
    j = col % bq
    back = nq - 1 - col // bq
    dist = back * bq + i - j
    seen = jnp.logical_or(back > 0, (j // CHUNK) <= (i // CHUNK))
    return jnp.where(seen, jnp.exp(lg * jnp.abs(dist).astype(f32)), 0.0)


def _per_query_block(nq, fn):
    for qi in range(nq):
        pl.when(pl.program_id(1) == qi)(functools.partial(fn, qi))


def mla_attn_fwd(q, k, v, proj, *, bq=512):
    S = q.shape[0]
    bq = min(bq, S)
    assert bq % CHUNK == 0

    def body(q_ref, k_ref, v_ref, g_ref, o_ref, y_ref):
        def block(qi):
            keys = pl.ds(0, (qi + 1) * bq)
            p = _softmax_rows(q_ref[...], k_ref[keys, :])
            o = jnp.dot(p.astype(bf16), v_ref[keys, :], preferred_element_type=f32)
            o_ref[...] = o
            y_ref[...] = (o * _silu(g_ref[...])).astype(bf16)

        _per_query_block(S // bq, block)

    o_spec = pl.BlockSpec((bq, DH), lambda h, i: (i, h))
    return pl.pallas_call(
        body, name="mla_attn_fwd", grid=(HEADS, S // bq),
        in_specs=[pl.BlockSpec((bq, QPAD), lambda h, i: (i, h)), pl.BlockSpec((S, QPAD), lambda h, i: (0, h)),
                  pl.BlockSpec((S, DH), lambda h, i: (0, h)), pl.BlockSpec((bq, DH), lambda h, i: (i, MG0 + h))],
        out_specs=[o_spec, o_spec], out_shape=[SDS((S, BW), f32), SDS((S, BW), bf16)],
        compiler_params=_cp(48))(q, k, v, proj)


def mla_attn_bwd(q, k, v, do, o, *, bq=512):
    S = q.shape[0]
    bq = min(bq, S)

    def body(q_ref, k_ref, v_ref, do_ref, o_ref, dq_ref, dk_ref, dv_ref):
        def block(qi):
            keys = pl.ds(0, (qi + 1) * bq)
            if qi == 0:
                dk_ref[...] = jnp.zeros_like(dk_ref)
                dv_ref[...] = jnp.zeros_like(dv_ref)
            qv, kv, dov = q_ref[...], k_ref[keys, :], do_ref[...]
            p = _softmax_rows(qv, kv)
            dp = lax.dot_general(dov, v_ref[keys, :], _DIMS["nt"], preferred_element_type=f32)
            delta = jnp.sum(dov.astype(f32) * o_ref[...], axis=-1, keepdims=True)
            dsb = (p * (dp - delta)).astype(bf16)
            dq_ref[...] = jnp.dot(dsb, kv, preferred_element_type=f32)
            dv_ref[keys, :] += lax.dot_general(p.astype(bf16), dov, _DIMS["tn"], preferred_element_type=f32)
            dk_ref[keys, :] += lax.dot_general(dsb, qv, _DIMS["tn"], preferred_element_type=f32)

        _per_query_block(S // bq, block)

    o_spec = pl.BlockSpec((bq, DH), lambda h, i: (i, h))
    return pl.pallas_call(
        body, name="mla_attn_bwd", grid=(HEADS, S // bq),
        in_specs=[pl.BlockSpec((bq, QPAD), lambda h, i: (i, h)), pl.BlockSpec((S, QPAD), lambda h, i: (0, h)),
                  pl.BlockSpec((S, DH), lambda h, i: (0, h)), o_spec, o_spec],
        out_specs=[pl.BlockSpec((bq, QPAD), lambda h, i: (i, h)), pl.BlockSpec((S, QPAD), lambda h, i: (0, h)),
                   pl.BlockSpec((S, DH), lambda h, i: (0, h))],
        out_shape=[SDS((S, HEADS * QPAD), f32), SDS((S, HEADS * QPAD), f32), SDS((S, BW), f32)],
        compiler_params=_cp(52))(q, k, v, do, o)


def ret_attn_fwd(proj, tabs, lgam, gn, *, bq=512):
    S = proj.shape[0]
    bq = min(bq, S)
    assert bq % CHUNK == 0

    def body(lg_ref, q_ref, k_ref, v_ref, g_ref, gn_ref, cq_ref, sq_ref, ck_ref, sk_ref, o_ref, y_ref, k_s, w_s):
        lg = lg_ref[pl.program_id(0)]

        def block(qi):
            if qi == 0:
                k_s[...] = _rope_ret(k_ref[...], ck_ref[...], sk_ref[...]).astype(bf16)
                w_s[...] = _decay_rows(lg, bq, S)
            keys = pl.ds(0, (qi + 1) * bq)
            qv = (_rope_ret(q_ref[...], cq_ref[...], sq_ref[...]) * RET_SCALE).astype(bf16)
            p = lax.dot_general(qv, k_s[keys, :], _DIMS["nt"], preferred_element_type=f32) * w_s[:, pl.ds(S - (qi + 1) * bq, (qi + 1) * bq)]
            o = jnp.dot(p.astype(bf16), v_ref[keys, :].astype(bf16), preferred_element_type=f32)
            o_ref[...] = o
            oc = o - jnp.mean(o, axis=-1, keepdims=True)
            z = oc * lax.rsqrt(jnp.mean(oc * oc, axis=-1, keepdims=True) + EPS) * gn_ref[...]
            y_ref[...] = (z * _silu(g_ref[...])).astype(bf16)

        _per_query_block(S // bq, block)

    row = lambda b0: pl.BlockSpec((bq, DH), lambda h, i: (i, b0 + h))
    full = lambda b0: pl.BlockSpec((S, DH), lambda h, i: (0, b0 + h))
    tq, tk = pl.BlockSpec((bq, DH), lambda h, i: (i, 0)), pl.BlockSpec((S, DH), lambda h, i: (0, 0))
    o_spec = pl.BlockSpec((bq, DH), lambda h, i: (i, h))
    return pl.pallas_call(
        body, name="ret_attn_fwd", grid=(HEADS, S // bq),
        in_specs=[pl.BlockSpec(memory_space=pltpu.SMEM), row(Q0), full(K0), full(V0), row(RG0),
                  pl.BlockSpec((1, DH), lambda h, i: (0, h)), tq, tq, tk, tk],
        out_specs=[o_spec, o_spec], out_shape=[SDS((S, BW), f32), SDS((S, BW), bf16)],
        scratch_shapes=[pltpu.VMEM((S, DH), bf16), pltpu.VMEM((bq, S), f32)], compiler_params=_cp(48),
    )(lgam, proj, proj, proj, proj, gn, tabs[0], tabs[1], tabs[0], tabs[1])


def ret_attn_bwd(proj, do, tabs, lgam, *, bq=512):
    S = proj.shape[0]
    bq = min(bq, S)
    nq = S // bq

    def body(lg_ref, q_ref, k_ref, v_ref, do_ref, cq_ref, sq_ref, ck_ref, sk_ref, dq_ref, dk_ref, dv_ref, k_s, dk_s, dv_s, w_s):
        lg = lg_ref[pl.program_id(0)]

        def block(qi):
            if qi == 0:
                k_s[...] = _rope_ret(k_ref[...], ck_ref[...], sk_ref[...]).astype(bf16)
                w_s[...] = _decay_rows(lg, bq, S)
                dk_s[...] = jnp.zeros_like(dk_s)
                dv_s[...] = jnp.zeros_like(dv_s)
            keys = pl.ds(0, (qi + 1) * bq)
            cq, sq = cq_ref[...], sq_ref[...]
            qv = (_rope_ret(q_ref[...], cq, sq) * RET_SCALE).astype(bf16)
            kv, dov = k_s[keys, :], do_ref[...]
            w = w_s[:, pl.ds(S - (qi + 1) * bq, (qi + 1) * bq)]
            p = lax.dot_general(qv, kv, _DIMS["nt"], preferred_element_type=f32) * w
            dp = lax.dot_general(dov, v_ref[keys, :].astype(bf16), _DIMS["nt"], preferred_element_type=f32)
            dsb = (dp * w).astype(bf16)
            dq_ref[...] = _rope_ret_t(jnp.dot(dsb, kv, preferred_element_type=f32) * RET_SCALE, cq, sq).astype(bf16)
            dv_s[keys, :] += lax.dot_general(p.astype(bf16), dov, _DIMS["tn"], preferred_element_type=f32)
            dk_s[keys, :] += lax.dot_general(dsb, qv, _DIMS["tn"], preferred_element_type=f32)
            if qi == nq - 1:
                dk_ref[...] = _rope_ret_t(dk_s[...], ck_ref[...], sk_ref[...]).astype(bf16)
                dv_ref[...] = dv_s[...].astype(bf16)

        _per_query_block(nq, block)

    row = lambda b0: pl.BlockSpec((bq, DH), lambda h, i: (i, b0 + h))
    full = lambda b0: pl.BlockSpec((S, DH), lambda h, i: (0, b0 + h))
    tq, tk = pl.BlockSpec((bq, DH), lambda h, i: (i, 0)), pl.BlockSpec((S, DH), lambda h, i: (0, 0))
    return pl.pallas_call(
        body, name="ret_attn_bwd", grid=(HEADS, nq),
        in_specs=[pl.BlockSpec(memory_space=pltpu.SMEM), row(Q0), full(K0), full(V0), row(0), tq, tq, tk, tk],
        out_specs=[row(0), full(0), full(0)], out_shape=[SDS((S, BW), bf16)] * 3,
        scratch_shapes=[pltpu.VMEM((S, DH), bf16), pltpu.VMEM((S, DH), f32), pltpu.VMEM((S, DH), f32),
                        pltpu.VMEM((bq, S), f32)],
        compiler_params=_cp(52),
    )(lgam, proj, proj, proj, do, tabs[0], tabs[1], tabs[0], tabs[1])


def ret_post_bwd(dy, o, proj, gn, *, tm=512):
    S = dy.shape[0]
    tm = min(tm, S)

    def body(dy_ref, o_ref, g_ref, gn_ref, do_ref, drg_ref, dgn_ref):
        ov, g, gnv, dyv = o_ref[...], g_ref[...], gn_ref[...], dy_ref[...]
        oc = ov - jnp.mean(ov, axis=-1, keepdims=True)
        rs = lax.rsqrt(jnp.mean(oc * oc, axis=-1, keepdims=True) + EPS)
        oh = oc * rs
        dz = dyv * _silu(g)
        drg_ref[...] = (dyv * (oh * gnv) * _dsilu(g)).astype(bf16)
        doh = dz * gnv
        do_ref[...] = (rs * (doh - jnp.mean(doh, axis=-1, keepdims=True)
                             - oh * jnp.mean(doh * oh, axis=-1, keepdims=True))).astype(bf16)
        _acc(dgn_ref, jnp.sum(dz * oh, axis=0, keepdims=True), pl.program_id(1) == 0)

    blk = pl.BlockSpec((tm, DH), lambda h, i: (i, h))
    vec = pl.BlockSpec((1, DH), lambda h, i: (0, h))
    return pl.pallas_call(
        body, name="ret_post_bwd", grid=(HEADS, S // tm),
        in_specs=[blk, blk, pl.BlockSpec((tm, DH), lambda h, i: (i, RG0 + h)), vec], out_specs=[blk, blk, vec],
        out_shape=[SDS((S, BW), bf16), SDS((S, BW), bf16), SDS((1, BW), f32)], compiler_params=_cp(32))(dy, o, proj, gn)


def mla_post_bwd(dy, o, proj, *, tm=512):
    S = dy.shape[0]
    tm = min(tm, S)

    def body(dy_ref, o_ref, g_ref, do_ref, dg_ref):
        g, dyv = g_ref[...], dy_ref[...]
        do_ref[...] = (dyv * _silu(g)).astype(bf16)
        dg_ref[...] = (dyv * o_ref[...] * _dsilu(g)).astype(bf16)

    blk = pl.BlockSpec((tm, DH), lambda i, h: (i, h))
    return pl.pallas_call(
        body, name="mla_post_bwd", grid=(S // tm, HEADS),
        in_specs=[blk, blk, pl.BlockSpec((tm, DH), lambda i, h: (i, MG0 + h))], out_specs=[blk, blk],
        out_shape=[SDS((S, BW), bf16), SDS((S, BW), bf16)], compiler_params=_cp(32))(dy, o, proj)


def mla_prep_fwd(proj, qnorm, kvnorm, wuq, wukv, tabs, *, tm=256, kv_blk=0):
    S = proj.shape[0]
    tm = min(tm, S)

    def body(mq_ref, mkv_ref, mkr_ref, qn_ref, kvn_ref, wuq_ref, wukv_ref, c_ref, sa_ref, sb_ref, q_ref, k_ref, v_ref):
        c, sa, sb = c_ref[...], sa_ref[...], sb_ref[...]
        mq, mkv = mq_ref[...], mkv_ref[...]
        qn = (mq * lax.rsqrt(jnp.mean(mq * mq, axis=-1, keepdims=True) + EPS) * qn_ref[...]).astype(bf16)
        kvn = (mkv * lax.rsqrt(jnp.mean(mkv * mkv, axis=-1, keepdims=True) + EPS) * kvn_ref[...]).astype(bf16)
        kr = _rope_mla(mkr_ref[...], c, sa, sb).astype(bf16)
        for h in range(HEADS):
            qh = jnp.dot(qn, wuq_ref[h], preferred_element_type=f32)
            q_ref[:, pl.ds(h * QPAD, DH)] = (qh[:, :DH] * MLA_SCALE).astype(bf16)
            q_ref[:, pl.ds(h * QPAD + DH, DH)] = (_rope_mla(qh[:, DH:], c, sa, sb) * MLA_SCALE).astype(bf16)
            kvh = jnp.dot(kvn, wukv_ref[h], preferred_element_type=f32)
            k_ref[:, pl.ds(h * QPAD, DH)] = kvh[:, :DH].astype(bf16)
            k_ref[:, pl.ds(h * QPAD + DH, DH)] = kr
            v_ref[:, pl.ds(h * DH, DH)] = kvh[:, DH:].astype(bf16)

    lat = lambda b: pl.BlockSpec((tm, LORA), lambda i: (i, b))
    tab = pl.BlockSpec((tm, DH), lambda i: (i, 0))
    vec = pl.BlockSpec((1, LORA), lambda i: (0, 0))
    wsp = pl.BlockSpec((HEADS, LORA, QPAD), lambda i: (0, 0, 0))
    wkv = pl.BlockSpec((HEADS, LORA, QPAD), lambda i: (0, 0, kv_blk))
    return pl.pallas_call(
        body, name="mla_prep_fwd", grid=(S // tm,),
        in_specs=[lat(MQ0 // 4), lat(MKV0 // 4), pl.BlockSpec((tm, DH), lambda i: (i, MKR0)), vec, vec, wsp, wkv,
                  tab, tab, tab],
        out_specs=[pl.BlockSpec((tm, HEADS * QPAD), lambda i: (i, 0))] * 2 + [pl.BlockSpec((tm, BW), lambda i: (i, 0))],
        out_shape=[SDS((S, HEADS * QPAD), bf16)] * 2 + [SDS((S, BW), bf16)], compiler_params=_cp(48),
    )(proj, proj, proj, qnorm, kvnorm, wuq, wukv, *tabs)


def mla_prep_bwd(dq256, dk256, dv, proj, qnorm, kvnorm, wuq, wukv, tabs, *, tm=256, kv_blk=0):
    S = proj.shape[0]
    tm = min(tm, S)

    def body(dq_ref, dk_ref, dv_ref, mq_ref, mkv_ref, qn_ref, kvn_ref, wuq_ref, wukv_ref, c_ref, sa_ref, sb_ref,
             dm_ref, dwuq_ref, dwukv_ref, dqn_ref, dkvn_ref):
        first = pl.program_id(0) == 0
        c, sa, sb = c_ref[...], sa_ref[...], sb_ref[...]
        mq, mkv = mq_ref[...], mkv_ref[...]
        rq = lax.rsqrt(jnp.mean(mq * mq, axis=-1, keepdims=True) + EPS)
        rkv = lax.rsqrt(jnp.mean(mkv * mkv, axis=-1, keepdims=True) + EPS)
        mqh, mkvh = mq * rq, mkv * rkv
        qn = (mqh * qn_ref[...]).astype(bf16)
        kvn = (mkvh * kvn_ref[...]).astype(bf16)
        dqn = jnp.zeros((tm, LORA), f32)
        dkvn = jnp.zeros((tm, LORA), f32)
        dkr = jnp.zeros((tm, DH), f32)
        for h in range(HEADS):
            da = dq_ref[:, pl.ds(h * QPAD, DH)] * MLA_SCALE
            db = _rope_mla_t(dq_ref[:, pl.ds(h * QPAD + DH, DH)] * MLA_SCALE, c, sa, sb)
            dqh = jnp.concatenate([da, db], axis=1).astype(bf16)
            dqn += lax.dot_general(dqh, wuq_ref[h], _DIMS["nt"], preferred_element_type=f32)
            _acc(dwuq_ref.at[h], lax.dot_general(qn, dqh, _DIMS["tn"], preferred_element_type=f32), first)
            dkr += dk_ref[:, pl.ds(h * QPAD + DH, DH)]
            dkvh = jnp.concatenate([dk_ref[:, pl.ds(h * QPAD, DH)], dv_ref[:, pl.ds(h * DH, DH)]], axis=1).astype(bf16)
            dkvn += lax.dot_general(dkvh, wukv_ref[h], _DIMS["nt"], preferred_element_type=f32)
            _acc(dwukv_ref.at[h], lax.dot_general(kvn, dkvh, _DIMS["tn"], preferred_element_type=f32), first)
        dmh = dqn * qn_ref[...]
        dm_ref[:, pl.ds(0, LORA)] = (rq * (dmh - mqh * jnp.mean(dmh * mqh, axis=-1, keepdims=True))).astype(bf16)
        dmh = dkvn * kvn_ref[...]
        dm_ref[:, pl.ds(LORA, LORA)] = (rkv * (dmh - mkvh * jnp.mean(dmh * mkvh, axis=-1, keepdims=True))).astype(bf16)
        dm_ref[:, pl.ds(2 * LORA, DH)] = _rope_mla_t(dkr, c, sa, sb).astype(bf16)
        _acc(dqn_ref, jnp.sum(dqn * mqh, axis=0, keepdims=True), first)
        _acc(dkvn_ref, jnp.sum(dkvn * mkvh, axis=0, keepdims=True), first)

    lat = lambda b: pl.BlockSpec((tm, LORA), lambda i: (i, b))
    tab = pl.BlockSpec((tm, DH), lambda i: (i, 0))
    vec = pl.BlockSpec((1, LORA), lambda i: (0, 0))
    wsp = pl.BlockSpec((HEADS, LORA, QPAD), lambda i: (0, 0, 0))
    wkv = pl.BlockSpec((HEADS, LORA, QPAD), lambda i: (0, 0, kv_blk))
    wide = pl.BlockSpec((tm, HEADS * QPAD), lambda i: (i, 0))
    return pl.pallas_call(
        body, name="mla_prep_bwd", grid=(S // tm,),
        in_specs=[wide, wide, pl.BlockSpec((tm, BW), lambda i: (i, 0)), lat(MQ0 // 4), lat(MKV0 // 4), vec, vec, wsp, wkv,
                  tab, tab, tab],
        out_specs=[pl.BlockSpec((tm, 2 * LORA + DH), lambda i: (i, 0)), wsp, wsp, vec, vec],
        out_shape=[SDS((S, 2 * LORA + DH), bf16), SDS((HEADS, LORA, QPAD), f32), SDS((HEADS, LORA, QPAD), f32),
                   SDS((1, LORA), f32), SDS((1, LORA), f32)],
        compiler_params=_cp(52),
    )(dq256, dk256, dv, proj, proj, qnorm, kvnorm, wuq, wukv, *tabs)


SUB = 8


def _scan_tiles(a_s, b_s, out, S, reverse):
    nt = S // SUB
    rows = lax.broadcasted_iota(jnp.int32, (SUB, LANE), 0)

    def tile(t, carry):
        base = pl.multiple_of((nt - 1 - t if reverse else t) * SUB, SUB)
        a, b = a_s[pl.ds(base, SUB), :], b_s[pl.ds(base, SUB), :]
        for d in (1, 2, 4):
            sh = SUB - d if reverse else d
            inside = rows < SUB - d if reverse else rows >= d
            a_n = jnp.where(inside, pltpu.roll(a, sh, 0), 1.0)
            b_n = jnp.where(inside, pltpu.roll(b, sh, 0), 0.0)
            b = a * b_n + b
            a = a * a_n
        res = a * carry + b
        out[pl.ds(base, SUB), :] = res
        edge = res[0:1, :] if reverse else res[SUB - 1:SUB, :]
        return jnp.broadcast_to(edge, (SUB, LANE))

    lax.fori_loop(0, nt, tile, jnp.zeros((SUB, LANE), f32))


def _shift_down(x, n, rows):
    return x if n == 0 else jnp.where(rows >= n, pltpu.roll(x, n, 0), 0.0)


def _shift_up(x, n, rows, S):
    return x if n == 0 else jnp.where(rows < S - n, pltpu.roll(x, S - n, 0), 0.0)


def _lru_gates(xb, cw, cb, wa, ba, wx, bx, lam, rows):
    xc = cb + cw[3:4, :] * xb
    for w in range(3):
        xc = xc + cw[w:w + 1, :] * _shift_down(xb, 3 - w, rows)
    xcb = xc.astype(bf16)
    r = _sigmoid(jnp.dot(xcb, wa, preferred_element_type=f32) + ba)
    i = _sigmoid(jnp.dot(xcb, wx, preferred_element_type=f32) + bx)
    sp = _softplus(-lam)
    la = (-LRU_C * r) * sp
    return xc, xcb, r, i, sp, la, jnp.exp(la)


def _lru_specs(S):
    col = lambda b0: pl.BlockSpec((S, LANE), lambda n: (0, b0 + n))
    vec = pl.BlockSpec((1, LANE), lambda n: (0, n))
    return col, vec, pl.BlockSpec((4, LANE), lambda n: (0, n)), pl.BlockSpec((1, LANE, LANE), lambda n: (n, 0, 0))


def lru_fwd(proj, cw, cb, wa, ba, wx, bx, lam):
    S = proj.shape[0]

    def body(x_ref, g_ref, cw_ref, cb_ref, wa_ref, ba_ref, wx_ref, bx_ref, lam_ref, h_ref, y_ref, a_s, b_s):
        rows = lax.broadcasted_iota(jnp.int32, (S, LANE), 0)
        xc, _, _, i, _, la, a = _lru_gates(x_ref[...], cw_ref[...], cb_ref[...], wa_ref[0].astype(bf16), ba_ref[...],
                                           wx_ref[0].astype(bf16), bx_ref[...], lam_ref[...], rows)
        a_s[...] = a
        b_s[...] = jnp.sqrt(_one_minus_exp(2.0 * la)) * (i * xc)
        _scan_tiles(a_s, b_s, h_ref, S, reverse=False)
        y_ref[...] = (h_ref[...] * _silu(g_ref[...])).astype(bf16)

    col, vec, cws, wsp = _lru_specs(S)
    return pl.pallas_call(
        body, name="lru_fwd", grid=(HEADS,),
        in_specs=[col(LX0), col(LG0), cws, vec, wsp, vec, wsp, vec, vec], out_specs=[col(0), col(0)],
        out_shape=[SDS((S, BW), f32), SDS((S, BW), bf16)],
        scratch_shapes=[pltpu.VMEM((S, LANE), f32), pltpu.VMEM((S, LANE), f32)], compiler_params=_cp(40),
    )(proj, proj, cw, cb, wa, ba, wx, bx, lam)


def lru_bwd(dy, h, proj, cw, cb, wa, ba, wx, bx, lam):
    S = proj.shape[0]

    def body(dy_ref, h_ref, x_ref, g_ref, cw_ref, cb_ref, wa_ref, ba_ref, wx_ref, bx_ref, lam_ref,
             dx_ref, dg_ref, dcw_ref, dcb_ref, dba_ref, dbx_ref, dlam_ref, dwa_ref, dwx_ref, a_s, b_s, l_s):
        rows = lax.broadcasted_iota(jnp.int32, (S, LANE), 0)
        xb, g, hv, dyv, cw, lam = x_ref[...], g_ref[...], h_ref[...], dy_ref[...], cw_ref[...], lam_ref[...]
        wa, wx = wa_ref[0].astype(bf16), wx_ref[0].astype(bf16)
        xc, xcb, r, i, sp, la, a = _lru_gates(xb, cw, cb_ref[...], wa, ba_ref[...], wx, bx_ref[...], lam, rows)
        dg_ref[...] = (dyv * hv * _dsilu(g)).astype(bf16)
        a_s[...] = _shift_up(a, 1, rows, S)
        b_s[...] = dyv * _silu(g)
        _scan_tiles(a_s, b_s, l_s, S, reverse=True)
        lmb = l_s[...]
        gated = i * xc
        sq = jnp.sqrt(_one_minus_exp(2.0 * la))
        dla = lmb * _shift_down(hv, 1, rows) * a - (lmb * gated) * (a * a) / sq
        dgated = lmb * sq
        dzr = (dla * (-LRU_C * sp)) * (r * (1.0 - r))
        dzi = (dgated * xc) * (i * (1.0 - i))
        dzrb, dzib = dzr.astype(bf16), dzi.astype(bf16)
        dxc = (dgated * i + lax.dot_general(dzrb, wa, _DIMS["nt"], preferred_element_type=f32)
               + lax.dot_general(dzib, wx, _DIMS["nt"], preferred_element_type=f32))
        dwa_ref[0] = lax.dot_general(xcb, dzrb, _DIMS["tn"], preferred_element_type=f32)
        dwx_ref[0] = lax.dot_general(xcb, dzib, _DIMS["tn"], preferred_element_type=f32)
        dba_ref[...] = jnp.sum(dzr, axis=0, keepdims=True)
        dbx_ref[...] = jnp.sum(dzi, axis=0, keepdims=True)
        dlam_ref[...] = jnp.sum(dla * (-LRU_C * r), axis=0, keepdims=True) * (-_sigmoid(-lam))
        dcb_ref[...] = jnp.sum(dxc, axis=0, keepdims=True)
        dxb = cw[3:4, :] * dxc
        dcw_ref[3:4, :] = jnp.sum(dxc * xb, axis=0, keepdims=True)
        for w in range(3):
            dxb = dxb + cw[w:w + 1, :] * _shift_up(dxc, 3 - w, rows, S)
            dcw_ref[w:w + 1, :] = jnp.sum(dxc * _shift_down(xb, 3 - w, rows), axis=0, keepdims=True)
        dx_ref[...] = dxb.astype(bf16)

    col, vec, cws, wsp = _lru_specs(S)
    scr = pltpu.VMEM((S, LANE), f32)
    return pl.pallas_call(
        body, name="lru_bwd", grid=(HEADS,),
        in_specs=[col(0), col(0), col(LX0), col(LG0), cws, vec, wsp, vec, wsp, vec, vec],
        out_specs=[col(0), col(0), cws, vec, vec, vec, vec, wsp, wsp],
        out_shape=[SDS((S, BW), bf16), SDS((S, BW), bf16), SDS((4, BW), f32)] + [SDS((1, BW), f32)] * 4
        + [SDS((HEADS, LANE, LANE), f32)] * 2,
        scratch_shapes=[scr, scr, scr], compiler_params=_cp(48),
    )(dy, h, proj, proj, cw, cb, wa, ba, wx, bx, lam)


def loss_head(y, target, *, tm=256):
    S = y.shape[0]

    def body(y_ref, t_ref, l_ref, d_ref):
        err = y_ref[...] - t_ref[...]
        d_ref[...] = err * (1.0 / D)
        part = jnp.sum(jnp.sum(err * err, axis=1, keepdims=True), axis=0, keepdims=True) * (0.5 / D)
        _acc(l_ref, jnp.broadcast_to(part, (1, LANE)), pl.program_id(0) == 0)

    row = pl.BlockSpec((tm, D), lambda i: (i, 0))
    return pl.pallas_call(
        body, name="loss_head", grid=(S // tm,), in_specs=[row, row],
        out_specs=[pl.BlockSpec((1, LANE), lambda i: (0, 0)), row],
        out_shape=[SDS((1, LANE), f32), SDS((S, D), f32)], compiler_params=_cp(32))(y, target)


def _adam_math(w, g, m, v):
    mn = ADAM_B1 * m + (1.0 - ADAM_B1) * g
    vn = ADAM_B2 * v + (1.0 - ADAM_B2) * (g * g)
    m_hat = mn / (1.0 - ADAM_B1 ** ADAM_STEP)
    v_hat = vn / (1.0 - ADAM_B2 ** ADAM_STEP)
    return -ADAM_LR * (m_hat / (jnp.sqrt(v_hat) + ADAM_EPS) + ADAM_WD * w), mn, vn


def _adam_rows(rows, cols):
    for cand in (2048, 1024, 512, 256, 128, 64, 32, 16, 8):
        if rows % cand == 0 and rows > cand and cand * cols <= ADAM_BLOCK_ELEMS:
            return cand
    return rows


def adamw(w, g, m, v):
    shape = w.shape
    cols = shape[-1]
    rows = math.prod(shape[:-1])
    tr = _adam_rows(rows, cols)

    def body(w_ref, g_ref, m_ref, v_ref, d_ref, mo_ref, vo_ref):
        d_ref[...], mo_ref[...], vo_ref[...] = _adam_math(w_ref[...], g_ref[...], m_ref[...], v_ref[...])

    blk = pl.BlockSpec((tr, cols), lambda i: (i, 0))
    flat = [t.reshape(rows, cols) for t in (w, g, m, v)]
    outs = pl.pallas_call(
        body, name="adamw", grid=(rows // tr,), in_specs=[blk] * 4, out_specs=[blk] * 3,
        out_shape=[SDS((rows, cols), f32)] * 3, compiler_params=_cp(48))(*flat)
    return tuple(o.reshape(shape) for o in outs)


ADA_SHARD = 3 * D // NDEV
ROWS16 = 16


def ada_fwd(c_all, ada_w, ada_b_mine):
    def body(c_ref, w_ref, b_ref, o_ref):
        o_ref[0] = jnp.dot(_silu(c_ref[...]).astype(bf16), w_ref[0].astype(bf16), preferred_element_type=f32) + b_ref[0]

    return pl.pallas_call(
        body, name="ada_fwd", grid=(DEPTH,),
        in_specs=[pl.BlockSpec((ROWS16, D), lambda l: (0, 0)), pl.BlockSpec((1, D, ADA_SHARD), lambda l: (l, 0, 0)),
                  pl.BlockSpec((1, 1, ADA_SHARD), lambda l: (l, 0, 0))],
        out_specs=pl.BlockSpec((1, ROWS16, ADA_SHARD), lambda l: (l, 0, 0)),
        out_shape=SDS((DEPTH, ROWS16, ADA_SHARD), f32), compiler_params=_cp(40))(c_all, ada_w, ada_b_mine)


def ada_bwd(c_all, dmod):
    def body(c_ref, d_ref, o_ref):
        o_ref[0] = lax.dot_general(_silu(c_ref[...]).astype(bf16), d_ref[0].astype(bf16), _DIMS["tn"],
                                   preferred_element_type=f32)

    return pl.pallas_call(
        body, name="ada_bwd", grid=(DEPTH,),
        in_specs=[pl.BlockSpec((ROWS16, D), lambda l: (0, 0)), pl.BlockSpec((1, ROWS16, ADA_SHARD), lambda l: (l, 0, 0))],
        out_specs=pl.BlockSpec((1, D, ADA_SHARD), lambda l: (l, 0, 0)),
        out_shape=SDS((DEPTH, D, ADA_SHARD), f32), compiler_params=_cp(40))(c_all, dmod)


def shift_params(me):
    start = SHARD_W * me
    return jnp.stack([start % LANE, (start + GAP) % LANE, jnp.clip(GAP_COL - start, 0, SHARD_W)]).astype(jnp.int32)


def win_pack(wt, sidx, *, first=0, count=DEPTH, after=()):
    def body(s_ref, w_ref, *rest):
        o_ref, scr = rest[len(after):]
        s1, s2, gi = s_ref[0], s_ref[1], s_ref[2]
        scr[pl.ds(SHARD_W, WIN - SHARD_W), :] = jnp.zeros((WIN - SHARD_W, LANE), f32)
        scr[pl.ds(0, SHARD_W), :] = w_ref[0]
        v = scr[...].T
        j = lax.broadcasted_iota(jnp.int32, v.shape, 1)
        o_ref[0] = jnp.where(j - s1 < gi, pltpu.roll(v, s1, 1),
                             jnp.where(j - s2 >= gi, pltpu.roll(v, s2, 1), 0.0)).astype(bf16)

    return pl.pallas_call(
        body, name="win_pack", grid=(count, D // LANE),
        in_specs=[pl.BlockSpec(memory_space=pltpu.SMEM), pl.BlockSpec((1, SHARD_W, LANE), lambda l, i: (first + l, 0, i))]
        + [pl.BlockSpec(memory_space=pl.ANY)] * len(after),
        out_specs=pl.BlockSpec((1, LANE, WIN), lambda l, i: (l, i, 0)),
        out_shape=SDS((count, D, WIN), bf16), scratch_shapes=[pltpu.VMEM((WIN, LANE), f32)],
        compiler_params=_cp(32))(sidx, wt, *after)


def win_assemble(g, *, after=()):
    own = WIN_STRIDE * LANE
    tail = NP - NDEV * own
    assert tail == 2 * LANE

    def body(a_ref, b_ref, *rest):
        o_ref = rest[len(after)]
        o_ref[...] = a_ref[0]

        @pl.when(pl.program_id(0) > 0)
        def _():
            o_ref[:, pl.ds(0, LANE)] = a_ref[0, :, pl.ds(0, LANE)] + b_ref[0]

    main = pl.pallas_call(
        body, name="win_assemble", grid=(NDEV,),
        in_specs=[pl.BlockSpec((1, D, own), lambda k: (k, 0, 0)),
                  pl.BlockSpec((1, D, LANE), lambda k: (jnp.maximum(k - 1, 0), 0, WIN_BLKS - 1))]
        + [pl.BlockSpec(memory_space=pl.ANY)] * len(after),
        out_specs=pl.BlockSpec((D, own), lambda k: (0, k)),
        out_shape=SDS((D, NP), bf16), compiler_params=_cp(48))(g, g, *after)

    def tail_body(_, b_ref, o_ref):
        o_ref[:, pl.ds(0, LANE)] = b_ref[0]
        o_ref[:, pl.ds(LANE, LANE)] = jnp.zeros((D, LANE), bf16)

    return pl.pallas_call(
        tail_body, name="win_assemble_tail", grid=(1,),
        in_specs=[pl.BlockSpec(memory_space=pl.ANY), pl.BlockSpec((1, D, LANE), lambda t: (NDEV - 1, 0, WIN_BLKS - 1))],
        out_specs=pl.BlockSpec((D, tail), lambda t: (0, NDEV * own // tail)),
        out_shape=SDS((D, NP), bf16), input_output_aliases={0: 0}, compiler_params=_cp(32))(main, g)


def reduce_adamw(stag, w, m, v, l, prev, name, *, sidx=None, tr=128):
    Cs = stag.shape[2]
    n_prev = 0 if prev is None else 4
    n_lead = 1 if sidx is not None else 0

    def body(*refs):
        refs = list(refs)
        s_ref = refs.pop(0) if sidx is not None else None
        g_ref, w_ref, m_ref, v_ref = refs[:4]
        rest = refs[4 + n_prev:]
        go_ref, d_ref, mo_ref, vo_ref = rest[:4]
        tot = g_ref[0].astype(f32)
        for d in range(1, stag.shape[0]):
            tot = tot + g_ref[d].astype(f32)
        if sidx is not None:
            scr = rest[4]
            s1, s2, gi = s_ref[0], s_ref[1], s_ref[2]
            i = lax.broadcasted_iota(jnp.int32, tot.shape, 1)
            scr[...] = jnp.where(i < gi, pltpu.roll(tot, WIN - s1, 1), pltpu.roll(tot, WIN - s2, 1)).T
            tot = scr[pl.ds(0, SHARD_W), :]
        go_ref[0] = tot
        d_ref[0], mo_ref[0], vo_ref[0] = _adam_math(w_ref[0], tot, m_ref[0], v_ref[0])

    if sidx is not None:
        blk3 = pl.BlockSpec((1, SHARD_W, tr), lambda i: (l, 0, i))
        steps = w.shape[2] // tr
    else:
        blk3 = pl.BlockSpec((1, tr, w.shape[2]), lambda i: (l, i, 0))
        steps = w.shape[1] // tr
    in_specs = ([pl.BlockSpec(memory_space=pltpu.SMEM)] * n_lead
                + [pl.BlockSpec((stag.shape[0], tr, Cs), lambda i: (0, i, 0)), blk3, blk3, blk3]
                + [pl.BlockSpec(memory_space=pl.ANY)] * n_prev)
    args = ([sidx] if sidx is not None else []) + [stag, w, m, v] + list(prev or ())
    return pl.pallas_call(
        body, name=name, grid=(steps,), in_specs=in_specs, out_specs=[blk3] * 4, out_shape=[SDS(w.shape, f32)] * 4,
        scratch_shapes=[pltpu.VMEM((Cs, tr), f32)] if sidx is not None else [],
        input_output_aliases={n_lead + 4 + k: k for k in range(n_prev)}, compiler_params=_cp(48),
    )(*args)


def cast_pad(w, cols_out, name):
    L, R, C = w.shape

    def body(w_ref, o_ref, *scr):
        if cols_out == C:
            o_ref[0] = w_ref[0].astype(bf16)
        else:
            scr[0][...] = jnp.zeros_like(scr[0])
            scr[0][:, pl.ds(0, C)] = w_ref[0]
            o_ref[0] = scr[0][...].astype(bf16)

    return pl.pallas_call(
        body, name=name, grid=(L,), in_specs=[pl.BlockSpec((1, R, C), lambda l: (l, 0, 0))],
        out_specs=pl.BlockSpec((1, R, cols_out), lambda l: (l, 0, 0)), out_shape=SDS((L, R, cols_out), bf16),
        scratch_shapes=[] if cols_out == C else [pltpu.VMEM((R, cols_out), f32)], compiler_params=_cp(32))(w)


def pack_qkv(w_uq, w_ukv):
    L = w_uq.shape[0]

    def body(q_ref, kv_ref, o_ref, scr):
        scr[...] = jnp.zeros_like(scr)
        scr[:, pl.ds(0, WUQ_COLS)] = q_ref[0]
        o_ref[0, :, pl.ds(0, QPAD)] = scr[...].astype(bf16)
        o_ref[0, :, pl.ds(QPAD, QPAD)] = kv_ref[0].astype(bf16)

    return pl.pallas_call(
        body, name="pack_qkv", grid=(L,),
        in_specs=[pl.BlockSpec((1, LORA, WUQ_COLS), lambda l: (l, 0, 0)), pl.BlockSpec((1, LORA, QPAD), lambda l: (l, 0, 0))],
        out_specs=pl.BlockSpec((1, LORA, 2 * QPAD), lambda l: (l, 0, 0)), out_shape=SDS((L, LORA, 2 * QPAD), bf16),
        scratch_shapes=[pltpu.VMEM((LORA, QPAD), f32)], compiler_params=_cp(32))(w_uq, w_ukv)


def sum_parts(stag, cols_out, name, *, tr=None, after=()):
    P, R, C = stag.shape
    tr = R if tr is None else tr

    def body(g_ref, *rest):
        o_ref, *scr = rest[len(after):]
        tot = g_ref[0].astype(f32)
        for d in range(1, P):
            tot = tot + g_ref[d].astype(f32)
        if cols_out == C:
            o_ref[...] = tot
        else:
            scr[0][...] = tot
            o_ref[...] = scr[0][:, pl.ds(0, cols_out)]

    return pl.pallas_call(
        body, name=name, grid=(R // tr,),
        in_specs=[pl.BlockSpec((P, tr, C), lambda i: (0, i, 0))] + [pl.BlockSpec(memory_space=pl.ANY)] * len(after),
        out_specs=pl.BlockSpec((tr, cols_out), lambda i: (i, 0)), out_shape=SDS((R, cols_out), f32),
        scratch_shapes=[] if cols_out == C else [pltpu.VMEM((tr, C), f32)], compiler_params=_cp(40))(stag, *after)


MESH_ID = pl.DeviceIdType.MESH
HBM_SPEC = pl.BlockSpec(memory_space=pltpu.HBM)


def _place():
    return lax.axis_index("x"), lax.axis_index("y"), lax.axis_index("c")


def all_gather(arrs, name):
    n = len(arrs)

    def body(*refs):
        ins, outs = refs[:n], refs[n:2 * n]
        send_sems, recv_sems, local_sems = refs[2 * n:]
        x, y, c = _place()
        me, sibling = (x, y, c), (x, y, 1 - c)
        chips = [(1 - x, y), (x, 1 - y), (1 - x, 1 - y)]

        def copy(a, k, block, to, src=None):
            slot = outs[a].at[4 * block[0] + 2 * block[1] + block[2]]
            return pltpu.make_async_remote_copy(
                src_ref=slot if src is None else src, dst_ref=slot, send_sem=send_sems.at[7 * a + k],
                recv_sem=recv_sems.at[7 * a + k], device_id=to, device_id_type=MESH_ID)

        mine = [pltpu.make_async_copy(ins[a], outs[a].at[4 * x + 2 * y + c], local_sems.at[a]) for a in range(n)]
        for cp in mine:
            cp.start()
        first = []
        for a in range(n):
            first.append(copy(a, 0, me, sibling, src=ins[a]))
            first += [copy(a, 1 + j, me, (*chip, c), src=ins[a]) for j, chip in enumerate(chips)]
        for cp in first:
            cp.start()
        passed = []
        for j, chip in enumerate(chips):
            for a in range(n):
                copy(a, 1 + j, (*chip, c), me).wait_recv()
                cp = copy(a, 4 + j, (*chip, c), sibling)
                cp.start()
                passed.append(cp)
        for a in range(n):
            copy(a, 0, sibling, me).wait_recv()
        for j, chip in enumerate(chips):
            for a in range(n):
                copy(a, 4 + j, (*chip, 1 - c), me).wait_recv()
        for cp in first + passed:
            cp.wait_send()
        for cp in mine:
            cp.wait()

    return pl.pallas_call(
        body, name=name, in_specs=[HBM_SPEC] * n, out_specs=[HBM_SPEC] * n,
        out_shape=[SDS((NDEV,) + a.shape, a.dtype) for a in arrs],
        scratch_shapes=[pltpu.SemaphoreType.DMA((7 * n,)), pltpu.SemaphoreType.DMA((7 * n,)),
                        pltpu.SemaphoreType.DMA((n,))],
    )(*arrs)


AG_COLLECTIVE_ID = 0
RS_COLLECTIVE_ID = 1


def _everyone_else(x, y, c):
    return [(x ^ (r >> 2), y ^ ((r >> 1) & 1), c ^ (r & 1)) for r in range(1, NDEV)]


def _rendezvous(sem, peers):
    for peer in peers:
        pl.semaphore_signal(sem, inc=1, device_id=peer, device_id_type=MESH_ID)
    pl.semaphore_wait(sem, len(peers))


def _sequencer_call(body, arrs, out_types, name, collective_id, remote=7, local=1):
    n = len(arrs)
    return pl.kernel(
        body, name=name, out_type=out_types, mesh=plsc.ScalarSubcoreMesh(axis_name="sequencer", num_cores=1),
        scratch_types=[pltpu.SemaphoreType.DMA((remote * n,)), pltpu.SemaphoreType.DMA((remote * n,)),
                       pltpu.SemaphoreType.DMA((local * n,)), pltpu.SemaphoreType.REGULAR],
        compiler_params=pltpu.CompilerParams(collective_id=collective_id),
    )(*arrs)


def seq_all_gather(arrs, name):
    n = len(arrs)

    def body(*refs):
        ins, outs = refs[:n], refs[n:2 * n]
        send_sems, recv_sems, local_sems, exit_sem = refs[2 * n:]
        x, y, c = _place()
        peers = _everyone_else(x, y, c)
        _rendezvous(pltpu.get_barrier_semaphore(), peers)
        me, sibling = (x, y, c), (x, y, 1 - c)
        chips = [(1 - x, y), (x, 1 - y), (1 - x, 1 - y)]

        def copy(a, k, block, to, src=None):
            slot = outs[a].at[4 * block[0] + 2 * block[1] + block[2]]
            return pltpu.make_async_remote_copy(
                src_ref=slot if src is None else src, dst_ref=slot, send_sem=send_sems.at[7 * a + k],
                recv_sem=recv_sems.at[7 * a + k], device_id=to, device_id_type=MESH_ID)

        mine = [pltpu.make_async_copy(ins[a], outs[a].at[4 * x + 2 * y + c], local_sems.at[a]) for a in range(n)]
        for cp in mine:
            cp.start()
        first = []
        for a in range(n):
            first.append(copy(a, 0, me, sibling, src=ins[a]))
            first += [copy(a, 1 + j, me, (*chip, c), src=ins[a]) for j, chip in enumerate(chips)]
        for cp in first:
            cp.start()
        passed = []
        for j, chip in enumerate(chips):
            for a in range(n):
                copy(a, 1 + j, (*chip, c), me).wait_recv()
                cp = copy(a, 4 + j, (*chip, c), sibling)
                cp.start()
                passed.append(cp)
        for a in range(n):
            copy(a, 0, sibling, me).wait_recv()
        for j, chip in enumerate(chips):
            for a in range(n):
                copy(a, 4 + j, (*chip, 1 - c), me).wait_recv()
        for cp in first + passed:
            cp.wait_send()
        for cp in mine:
            cp.wait()
        _rendezvous(exit_sem, peers)

    return _sequencer_call(body, arrs, [SDS((NDEV,) + a.shape, a.dtype) for a in arrs], name, AG_COLLECTIVE_ID)


def seq_reduce_scatter_parts(arrs, pick, shapes, name):
    n = len(arrs)

    def body(*refs):
        ins, outs = refs[:n], refs[n:2 * n]
        send_sems, recv_sems, local_sems, exit_sem = refs[2 * n:]
        x, y, c = _place()
        peers = _everyone_else(x, y, c)
        _rendezvous(pltpu.get_barrier_semaphore(), peers)
        me = 4 * x + 2 * y + c
        mine = [pltpu.make_async_copy(pick[a](ins[a], me), outs[a].at[me], local_sems.at[a]) for a in range(n)]
        for cp in mine:
            cp.start()
        sent = []
        for r, peer in enumerate(peers):
            pid = 4 * peer[0] + 2 * peer[1] + peer[2]
            for a in range(n):
                cp = pltpu.make_async_remote_copy(
                    src_ref=pick[a](ins[a], pid), dst_ref=outs[a].at[me], send_sem=send_sems.at[7 * a + r],
                    recv_sem=recv_sems.at[7 * a + r], device_id=peer, device_id_type=MESH_ID)
                cp.start()
                sent.append((cp, a, r, pid))
        for cp, a, r, pid in sent:
            pltpu.make_async_remote_copy(
                src_ref=pick[a](ins[a], pid), dst_ref=outs[a].at[pid], send_sem=send_sems.at[7 * a + r],
                recv_sem=recv_sems.at[7 * a + r], device_id=(x, y, c), device_id_type=MESH_ID).wait_recv()
        for cp, _, _, _ in sent:
            cp.wait_send()
        for cp in mine:
            cp.wait()
        _rendezvous(exit_sem, peers)

    return _sequencer_call(body, arrs, [SDS((NDEV,) + tuple(s), a.dtype) for s, a in zip(shapes, arrs)], name,
                           RS_COLLECTIVE_ID)


WEIGHTS = ("ada_w", "ada_b", "norm_pre", "norm_post", "w_in", "ret_gn", "lru_conv_w", "lru_conv_b", "lru_wa", "lru_ba",
           "lru_wx", "lru_bx", "lru_lambda", "mla_q_norm", "mla_w_uq", "mla_kv_norm", "mla_w_ukv", "w_branch", "w_out")
SMALL = ("norm_pre", "norm_post", "ret_gn", "lru_conv_w", "lru_conv_b", "lru_ba", "lru_bx", "lru_lambda", "mla_q_norm",
         "mla_kv_norm")
QKV_ROWS = LORA
QKV_BLOCK = (LORA + LANE, 2 * QPAD)
BR_ROWS = 3 * BW // NDEV
OUT_ROWS = D // NDEV
WUQ_COLS = 192


def _row(v):
    return v.reshape(1, -1)


def layer_fwd(xl, mod, p, wts, tabs, lgam):
    S = xl.shape[0]
    tm = min(S, 2048)
    sh, sc, rg = _row(mod[:D]), _row(mod[D:2 * D]), _row(mod[2 * D:])
    ret_tabs, mla_tabs = tabs
    h = pre_fwd(xl, _row(p["norm_pre"]), sc, sh)
    proj = matmul(h, wts["w_in"], dims="nn", M=S, N=NP, K=D, tm=tm, tn=768, tk=D, out_dtype=f32, name="mm_in")
    o_ret, y_ret = ret_attn_fwd(proj, ret_tabs, lgam, _row(p["ret_gn"]))
    h_lru, y_lru = lru_fwd(proj, p["conv_w"], _row(p["lru_conv_b"]), p["lru_wa"], _row(p["lru_ba"]), p["lru_wx"],
                           _row(p["lru_bx"]), _row(p["lru_lambda"]))
    q256, k256, vm = mla_prep_fwd(proj, _row(p["mla_q_norm"]), _row(p["mla_kv_norm"]), wts["w_qkv"], wts["w_qkv"], mla_tabs,
                                  kv_blk=1)
    o_mla, y_mla = mla_attn_fwd(q256, k256, vm, proj)
    ys = (y_ret, y_lru, y_mla)
    us = [matmul(ys[b], wts["w_branch"], dims="nn", M=S, N=D, K=BW, tm=min(S, 1024), tn=1024, tk=BW,
                 out_dtype=bf16, name="mm_branch", b_blk0=(b, 0)) for b in range(3)]
    merged = gate_fwd(proj, us)
    y, x_next = out_fwd(merged, wts["w_out"], xl, rg, _row(p["norm_post"]))
    saved = dict(x=xl, h=h, proj=proj, o_ret=o_ret, h_lru=h_lru, q256=q256, k256=k256, vm=vm, o_mla=o_mla,
                 ys=ys, us=us, merged=merged, y=y, sc=sc, rg=rg)
    return x_next, saved


def layer_bwd(dx, sv, p, wts, tabs, lgam, exchange):
    S = dx.shape[0]
    tm = min(S, 2048)
    ret_tabs, mla_tabs = tabs
    proj = sv["proj"]
    dy, d_rg, d_gpost = out_bwd(dx, sv["y"], sv["rg"], _row(p["norm_post"]))
    dmerged = matmul(dy, wts["w_out"], dims="nt", M=S, N=D, K=D, tm=min(S, 1024), tn=1024, tk=D, out_dtype=f32, name="mm_dmerged")
    dw_out = matmul(sv["merged"], dy, dims="tn", M=D, N=D, K=S, tm=1024, tn=1024, tk=min(S, 1024), out_dtype=bf16, name="mm_dwout")
    du, dml = gate_bwd(dmerged, proj, sv["us"])
    tmb, tkb = min(S, 1024), min(S, 1024)
    dys = [matmul(du[b], wts["w_branch"], dims="nt", M=S, N=BW, K=D, tm=tmb, tn=BW, tk=D, out_dtype=f32, name="mm_dybranch",
                  b_blk0=(b, 0)) for b in range(3)]
    dw_branch = jnp.concatenate(
        [matmul(sv["ys"][b], du[b], dims="tn", M=BW, N=D, K=S, tm=BW, tn=1024, tk=tkb, out_dtype=bf16, name="mm_dwbranch")
         for b in range(3)], axis=0)
    do, d_rgate, d_gn = ret_post_bwd(dys[0], sv["o_ret"], proj, _row(p["ret_gn"]))
    d_q, d_k, d_v = ret_attn_bwd(proj, do, ret_tabs, lgam)
    d_lx, d_lg, d_cw, d_cb, d_ba, d_bx, d_lam, d_wa, d_wx = lru_bwd(
        dys[1], sv["h_lru"], proj, p["conv_w"], _row(p["lru_conv_b"]), p["lru_wa"], _row(p["lru_ba"]), p["lru_wx"],
        _row(p["lru_bx"]), _row(p["lru_lambda"]))
    do, d_mg = mla_post_bwd(dys[2], sv["o_mla"], proj)
    dq256, dk256, dvm = mla_attn_bwd(sv["q256"], sv["k256"], sv["vm"], do, sv["o_mla"])
    d_lat, dw_uq, dw_ukv, d_qn, d_kvn = mla_prep_bwd(dq256, dk256, dvm, proj, _row(p["mla_q_norm"]), _row(p["mla_kv_norm"]),
                                                       wts["w_qkv"], wts["w_qkv"], mla_tabs, kv_blk=1)
    dproj = jnp.concatenate([d_q, d_k, d_v, d_rgate, d_lx, d_lg, d_lat, d_mg, *dml, jnp.zeros((S, LANE), bf16)], axis=1)
    dh = matmul(dproj, wts["w_in"], dims="nt", M=S, N=D, K=NP, tm=min(S, 1024), tn=1024, tk=NP // 6, out_dtype=f32, name="mm_dh")
    dw_in = matmul(sv["h"], dproj, dims="tn", M=D, N=NP, K=S, tm=D, tn=768, tk=tm, out_dtype=bf16, name="mm_dwin")
    dw_qkv = jnp.concatenate([jnp.concatenate([dw_uq, dw_ukv], axis=2),
                              jnp.concatenate([d_wa, d_wx, jnp.zeros((HEADS, LANE, QPAD), f32)], axis=2)], axis=1).astype(bf16)
    staged, launched_from = exchange(dict(w_in=dw_in, w_branch=dw_branch, w_out=dw_out, w_qkv=dw_qkv))
    dxl, d_sh, d_sc, d_gpre = pre_bwd(dh, sv["x"], _row(p["norm_pre"]), sv["sc"], dx, after=tuple(launched_from))
    dmod = jnp.concatenate([d_sh, d_sc, d_rg], axis=1).reshape(-1)
    small = dict(norm_pre=d_gpre, norm_post=d_gpost, ret_gn=d_gn, lru_conv_w=d_cw, lru_conv_b=d_cb, lru_ba=d_ba,
                 lru_bx=d_bx, lru_lambda=d_lam, mla_q_norm=d_qn, mla_kv_norm=d_kvn)
    return dxl, dmod, staged, small


def exchange_grads(big, l):
    picks = [lambda r, d: r.at[:, pl.ds(pl.multiple_of(d * (WIN_STRIDE * LANE), LANE), WIN)],
             lambda r, d: r.at[pl.ds(pl.multiple_of(d * BR_ROWS, 8), BR_ROWS), :],
             lambda r, d: r.at[pl.ds(pl.multiple_of(d * OUT_ROWS, 8), OUT_ROWS), :],
             lambda r, d: r.at[d]]
    shapes = [(D, WIN), (BR_ROWS, D), (OUT_ROWS, D), QKV_BLOCK]
    arrs = [big["w_in"], big["w_branch"], big["w_out"], big["w_qkv"]]
    return seq_reduce_scatter_parts(arrs, picks, shapes, f"rs_grads_{l}"), arrs


def kernel(x, c, positions, ada_w, ada_b, norm_pre, norm_post, w_in, ret_gn, lru_conv_w, lru_conv_b, lru_wa, lru_ba, lru_wx, lru_bx, lru_lambda, mla_q_norm, mla_w_uq, mla_kv_norm, mla_w_ukv, w_branch, w_out, loss_target, m_ada_w, m_ada_b, m_norm_pre, m_norm_post, m_w_in, m_ret_gn, m_lru_conv_w, m_lru_conv_b, m_lru_wa, m_lru_ba, m_lru_wx, m_lru_bx, m_lru_lambda, m_mla_q_norm, m_mla_w_uq, m_mla_kv_norm, m_mla_w_ukv, m_w_branch, m_w_out, v_ada_w, v_ada_b, v_norm_pre, v_norm_post, v_w_in, v_ret_gn, v_lru_conv_w, v_lru_conv_b, v_lru_wa, v_lru_ba, v_lru_wx, v_lru_bx, v_lru_lambda, v_mla_q_norm, v_mla_w_uq, v_mla_kv_norm, v_mla_w_ukv, v_w_branch, v_w_out):
    given = dict(locals())
    xi, yi, ci = _place()
    me = 4 * xi + 2 * yi + ci
    S = x.shape[1]
    sidx = shift_params(me)
    lgam = jnp.asarray(np.log1p(-np.exp2(-5.0 - np.arange(HEADS))), f32)
    tabs = rope_tables(positions[0])

    (g_small,) = all_gather([jnp.concatenate([c.reshape(16, LANE), lru_conv_w.reshape(16, LANE)], axis=0)], "ag_small")
    c16 = jnp.concatenate([g_small[:, :16].reshape(NDEV, D), jnp.zeros((ROWS16 - NDEV, D), f32)], axis=0)
    conv_w_all = g_small[:, 16:].reshape(NDEV, DEPTH, 4, LANE).transpose(1, 2, 0, 3).reshape(DEPTH, 4, BW)
    ada_b_mine = lax.dynamic_slice_in_dim(ada_b, me * ADA_SHARD, ADA_SHARD, axis=1).reshape(DEPTH, 1, ADA_SHARD)
    (g_mod,) = all_gather([ada_fwd(c16, ada_w, ada_b_mine)[:, :NDEV]], "ag_mod")
    mods = lax.dynamic_index_in_dim(g_mod, me, axis=2, keepdims=False).transpose(1, 0, 2).reshape(DEPTH, 3 * D)
    w_in_t = {n: jnp.swapaxes(given[n], 1, 2) for n in ("w_in", "m_w_in", "v_w_in")}
    win0 = win_pack(w_in_t["w_in"], sidx, count=1, after=(g_mod,))
    win_rest = win_pack(w_in_t["w_in"], sidx, first=1, count=DEPTH - 1, after=(win0,))
    wbr_p, wout_p, qkv_p = cast_pad(w_branch, D, "pack_wbranch"), cast_pad(w_out, D, "pack_wout"), pack_qkv(mla_w_uq, mla_w_ukv)
    first, rest = seq_all_gather([win0[0], qkv_p[0]], "ag_weights_0a"), seq_all_gather([wbr_p[0], wout_p[0]], "ag_weights_0b")
    gathered = [(first[0], rest[0], rest[1], first[1])]
    gathered += [seq_all_gather([win_rest[l - 1], wbr_p[l], wout_p[l], qkv_p[l]], f"ag_weights_{l}") for l in range(1, DEPTH)]

    params, wts = [], []
    for l in range(DEPTH):
        p = {n: given[n][l] for n in SMALL + ("lru_wa", "lru_wx") if n != "lru_conv_w"}
        p["conv_w"] = conv_w_all[l]
        params.append(p)

    xl, saved = x[0], []
    for l in range(DEPTH):
        g_win, g_br, g_out, g_qkv = gathered[l]
        wts.append(dict(w_in=win_assemble(g_win, after=(xl,)), w_qkv=g_qkv, w_branch=g_br.reshape(3 * BW, D),
                        w_out=g_out.reshape(D, D)))
        xl, sv = layer_fwd(xl, mods[l], params[l], wts[l], tabs, lgam)
        saved.append(sv)
    my_loss, dx = loss_head(xl, loss_target[0])
    loss = lax.psum(my_loss[0, 0], ("x", "y", "c"))

    dmods, smalls, staged, grads = [None] * DEPTH, [None] * DEPTH, [None] * DEPTH, {n: [None] * DEPTH for n in WEIGHTS}
    for l in reversed(range(DEPTH)):
        dx, dmods[l], staged[l], smalls[l] = layer_bwd(dx, saved[l], params[l], wts[l], tabs, lgam,
                                                       functools.partial(exchange_grads, l=l))
    chained, lru_blocks = {"w_in": None, "w_branch": None, "w_out": None}, [None] * DEPTH
    for l in reversed(range(DEPTH)):
        st_win, st_br, st_out, st_qkv = staged[l]
        chained["w_in"] = reduce_adamw(st_win, w_in_t["w_in"], w_in_t["m_w_in"], w_in_t["v_w_in"], l, chained["w_in"],
                                       "update_w_in", sidx=sidx)
        for n, st in (("w_branch", st_br), ("w_out", st_out)):
            chained[n] = reduce_adamw(st, given[n], given["m_" + n], given["v_" + n], l, chained[n], "update_" + n)
        g_qkv = sum_parts(st_qkv, 2 * QPAD, "sum_wqkv", after=(chained["w_in"][0],) if l == 0 else ())
        grads["mla_w_uq"][l] = g_qkv[:QKV_ROWS, :WUQ_COLS]
        grads["mla_w_ukv"][l] = g_qkv[:QKV_ROWS, QPAD:]
        lru_blocks[l] = g_qkv[QKV_ROWS:, :2 * LANE]
    (g_lru,) = all_gather([jnp.stack(lru_blocks)], "ag_lru_w")
    grads["lru_wa"] = g_lru[..., :LANE].transpose(1, 0, 2, 3)
    grads["lru_wx"] = g_lru[..., LANE:].transpose(1, 0, 2, 3)

    flat = [jnp.stack(dmods).reshape(-1)] + [smalls[l][n].reshape(-1) for n in SMALL for l in range(DEPTH)]
    (g_pack,) = all_gather([jnp.concatenate(flat).reshape(-1, LANE)], "ag_small_grads")
    rows = g_pack.shape[1]
    tot = sum_parts(g_pack, LANE, "sum_small_grads", tr=rows // 8).reshape(-1)
    grads["ada_b"] = tot[:DEPTH * 3 * D].reshape(DEPTH, 3 * D)
    off = DEPTH * 3 * D
    for n in SMALL:
        size = DEPTH * int(smalls[0][n].size)
        piece = tot[off:off + size]
        off += size
        if n == "lru_conv_w":
            grads[n] = lax.dynamic_slice_in_dim(piece.reshape(DEPTH, 4, BW), me * LANE, LANE, axis=2)
        else:
            grads[n] = piece.reshape(given[n].shape)
    dmod_all = g_pack[:, :DEPTH * 3 * D // LANE].reshape(NDEV, DEPTH, 3 * D)
    dmod_mine = lax.dynamic_slice_in_dim(dmod_all, me * ADA_SHARD, ADA_SHARD, axis=2).transpose(1, 0, 2)
    dmod16 = jnp.concatenate([dmod_mine, jnp.zeros((DEPTH, ROWS16 - NDEV, ADA_SHARD), f32)], axis=1)
    grads["ada_w"] = ada_bwd(c16, dmod16)

    outs = {"grad": [], "delta": [], "m": [], "v": []}
    for n in WEIGHTS:
        if n in chained:
            g, delta, new_m, new_v = (jnp.swapaxes(t, 1, 2) for t in chained[n]) if n == "w_in" else chained[n]
        else:
            g = grads[n] if not isinstance(grads[n], list) else jnp.stack(grads[n])
            delta, new_m, new_v = adamw(given[n], g, given["m_" + n], given["v_" + n])
        outs["grad"].append(g)
        outs["delta"].append(delta)
        outs["m"].append(new_m)
        outs["v"].append(new_v)
    return (loss, dx[None], *outs["grad"], *outs["delta"], *outs["m"], *outs["v"])
```

```python
import functools
import math

import numpy as np
import jax
import jax.numpy as jnp
from jax import lax
from jax.experimental import pallas as pl
from jax.experimental.pallas import tpu as pltpu
from jax.experimental.pallas import tpu_sc as plsc

f32 = jnp.float32
bf16 = jnp.bfloat16
SDS = jax.ShapeDtypeStruct

DEPTH = 4
D = 2048
HEADS = 8
DH = 128
BW = HEADS * DH
LANE = 128
CHUNK = 64
EPS = 1e-6
LRU_C = 8.0
NDEV = 8
VMEM_V7X = 64 * 1024 * 1024

Q0, K0, V0, RG0, LX0, LG0, MQ0, MKV0, MKR0, MG0, ML0 = 0, 8, 16, 24, 32, 40, 48, 52, 56, 57, 65
NB = 114
NP = NB * LANE
IN_W = 14400
SHARD_W = IN_W // NDEV
GAP_COL = 7232
GAP = 64
WIN = 1920
WIN_BLKS = WIN // LANE
WIN_STRIDE = 14
LORA = 512
QPAD = 256

ADAM_LR, ADAM_B1, ADAM_B2, ADAM_EPS, ADAM_WD, ADAM_STEP = 0.001, 0.9, 0.999, 1e-08, 0.01, 10
ADAM_BLOCK_ELEMS = 256 * 1024
MLA_SCALE = (128 + 64) ** -0.5
RET_SCALE = 128 ** -0.5


def _cp(vmem_mb=None, **kw):
    if vmem_mb is not None:
        kw["vmem_limit_bytes"] = min(vmem_mb * 1024 * 1024, VMEM_V7X - 8 * 1024 * 1024)
    return pltpu.CompilerParams(**kw)


def _sigmoid(x):
    return 1.0 / (1.0 + jnp.exp(-x))


def _silu(x):
    return x * _sigmoid(x)


def _dsilu(x):
    s = _sigmoid(x)
    return s * (1.0 + x * (1.0 - s))


def _softplus(x):
    return jnp.maximum(x, 0.0) + jnp.log(1.0 + jnp.exp(-jnp.abs(x)))


def _one_minus_exp(y):
    series = -y * (1.0 + y * (0.5 + y * (1.0 / 6.0)))
    return jnp.where(y > -1e-2, series, 1.0 - jnp.exp(y))


def _acc(ref, val, first):
    @pl.when(first)
    def _():
        ref[...] = val

    @pl.when(jnp.logical_not(first))
    def _():
        ref[...] += val


_DIMS = {"nn": (((1,), (0,)), ((), ())), "nt": (((1,), (1,)), ((), ())), "tn": (((0,), (0,)), ((), ()))}


def matmul(a, b, *, dims, M, N, K, tm, tn, tk, out_dtype, name, a_blk0=(0, 0), b_blk0=(0, 0), vmem_mb=48):
    nk = K // tk
    assert M % tm == 0 and N % tn == 0 and K % tk == 0
    dn = _DIMS[dims]

    def body(a_ref, b_ref, o_ref, *scr):
        part = lax.dot_general(a_ref[...].astype(bf16), b_ref[...].astype(bf16), dn, preferred_element_type=f32)
        if nk == 1:
            o_ref[...] = part.astype(out_dtype)
        else:
            acc = scr[0]
            k = pl.program_id(2)
            _acc(acc, part, k == 0)

            @pl.when(k == nk - 1)
            def _():
                o_ref[...] = acc[...].astype(out_dtype)

    ar, ac = a_blk0
    br, bc = b_blk0
    if dims == "nn":
        a_spec = pl.BlockSpec((tm, tk), lambda i, j, k: (i + ar, k + ac))
        b_spec = pl.BlockSpec((tk, tn), lambda i, j, k: (k + br, j + bc))
    elif dims == "nt":
        a_spec = pl.BlockSpec((tm, tk), lambda i, j, k: (i + ar, k + ac))
        b_spec = pl.BlockSpec((tn, tk), lambda i, j, k: (j + br, k + bc))
    else:
        a_spec = pl.BlockSpec((tk, tm), lambda i, j, k: (k + ar, i + ac))
        b_spec = pl.BlockSpec((tk, tn), lambda i, j, k: (k + br, j + bc))
    return pl.pallas_call(
        body, name=name, grid=(M // tm, N // tn, nk),
        in_specs=[a_spec, b_spec], out_specs=pl.BlockSpec((tm, tn), lambda i, j, k: (i, j)),
        out_shape=SDS((M, N), out_dtype),
        scratch_shapes=[] if nk == 1 else [pltpu.VMEM((tm, tn), f32)],
        compiler_params=_cp(vmem_mb, dimension_semantics=("parallel", "parallel", "arbitrary")),
    )(a, b)


def pre_fwd(x, g, sc, sh, *, tm=256):
    S = x.shape[0]

    def body(x_ref, g_ref, sc_ref, sh_ref, h_ref):
        xv = x_ref[...]
        r = lax.rsqrt(jnp.mean(xv * xv, axis=-1, keepdims=True) + EPS)
        h_ref[...] = (((xv * r) * g_ref[...]) * (1.0 + sc_ref[...]) + sh_ref[...]).astype(bf16)

    row = pl.BlockSpec((tm, D), lambda i: (i, 0))
    vec = pl.BlockSpec((1, D), lambda i: (0, 0))
    return pl.pallas_call(body, name="pre_fwd", grid=(S // tm,), in_specs=[row, vec, vec, vec], out_specs=row,
                          out_shape=SDS((S, D), bf16), compiler_params=_cp(32))(x, g, sc, sh)


def pre_bwd(dh, x, g, sc, dxo, *, tm=256, after=()):
    S = x.shape[0]

    def body(dh_ref, x_ref, g_ref, sc_ref, dxo_ref, *rest):
        dx_ref, dsh_ref, dsc_ref, dg_ref = rest[len(after):]
        first = pl.program_id(0) == 0
        xv, dhv, gv = x_ref[...], dh_ref[...], g_ref[...]
        one_sc = 1.0 + sc_ref[...]
        r = lax.rsqrt(jnp.mean(xv * xv, axis=-1, keepdims=True) + EPS)
        xh = xv * r
        t = dhv * xh
        dxh = dhv * gv * one_sc
        dx_ref[...] = r * (dxh - xh * jnp.mean(dxh * xh, axis=-1, keepdims=True)) + dxo_ref[...]
        _acc(dsh_ref, jnp.sum(dhv, axis=0, keepdims=True), first)
        _acc(dsc_ref, jnp.sum(t * gv, axis=0, keepdims=True), first)
        _acc(dg_ref, jnp.sum(t * one_sc, axis=0, keepdims=True), first)

    row = pl.BlockSpec((tm, D), lambda i: (i, 0))
    vec = pl.BlockSpec((1, D), lambda i: (0, 0))
    return pl.pallas_call(
        body, name="pre_bwd", grid=(S // tm,),
        in_specs=[row, row, vec, vec, row] + [pl.BlockSpec(memory_space=pl.ANY)] * len(after), out_specs=[row, vec, vec, vec],
        out_shape=[SDS((S, D), f32), SDS((1, D), f32), SDS((1, D), f32), SDS((1, D), f32)],
        compiler_params=_cp(40))(dh, x, g, sc, dxo, *after)


def out_fwd(merged, w_out, x, rg, gp, *, tm=256):
    S = x.shape[0]

    def body(m_ref, w_ref, x_ref, rg_ref, gp_ref, y_ref, xn_ref):
        y = jnp.dot(m_ref[...], w_ref[...], preferred_element_type=f32)
        y_ref[...] = y
        r = lax.rsqrt(jnp.mean(y * y, axis=-1, keepdims=True) + EPS)
        xn_ref[...] = x_ref[...] + (1.0 + rg_ref[...]) * ((y * r) * gp_ref[...])

    row = pl.BlockSpec((tm, D), lambda i: (i, 0))
    vec = pl.BlockSpec((1, D), lambda i: (0, 0))
    return pl.pallas_call(
        body, name="out_fwd", grid=(S // tm,),
        in_specs=[row, pl.BlockSpec((D, D), lambda i: (0, 0)), row, vec, vec], out_specs=[row, row],
        out_shape=[SDS((S, D), f32), SDS((S, D), f32)], compiler_params=_cp(48))(merged, w_out, x, rg, gp)


def out_bwd(dxo, y, rg, gp, *, tm=256):
    S = y.shape[0]

    def body(dxo_ref, y_ref, rg_ref, gp_ref, dy_ref, drg_ref, dgp_ref):
        first = pl.program_id(0) == 0
        yv, dv, gv = y_ref[...], dxo_ref[...], gp_ref[...]
        r = lax.rsqrt(jnp.mean(yv * yv, axis=-1, keepdims=True) + EPS)
        yh = yv * r
        dn = dv * (1.0 + rg_ref[...])
        dyh = dn * gv
        dy_ref[...] = (r * (dyh - yh * jnp.mean(dyh * yh, axis=-1, keepdims=True))).astype(bf16)
        _acc(drg_ref, jnp.sum(dv * (yh * gv), axis=0, keepdims=True), first)
        _acc(dgp_ref, jnp.sum(dn * yh, axis=0, keepdims=True), first)

    row = pl.BlockSpec((tm, D), lambda i: (i, 0))
    vec = pl.BlockSpec((1, D), lambda i: (0, 0))
    return pl.pallas_call(
        body, name="out_bwd", grid=(S // tm,), in_specs=[row, row, vec, vec], out_specs=[row, vec, vec],
        out_shape=[SDS((S, D), bf16), SDS((1, D), f32), SDS((1, D), f32)], compiler_params=_cp(40))(dxo, y, rg, gp)


def _ml_spec(b, tm):
    return pl.BlockSpec((tm, LANE), lambda i, j: (i, ML0 + b * (D // LANE) + j))


def gate_fwd(proj, us, *, tm=2048):
    S = proj.shape[0]
    tm = min(tm, S)

    def body(ml0, ml1, ml2, u0, u1, u2, m_ref):
        acc = None
        for ml, u in ((ml0, u0), (ml1, u1), (ml2, u2)):
            t = _sigmoid(ml[...]) * u[...].astype(f32)
            acc = t if acc is None else acc + t
        m_ref[...] = acc.astype(bf16)

    blk = pl.BlockSpec((tm, LANE), lambda i, j: (i, j))
    return pl.pallas_call(
        body, name="gate_fwd", grid=(S // tm, D // LANE),
        in_specs=[_ml_spec(0, tm), _ml_spec(1, tm), _ml_spec(2, tm), blk, blk, blk], out_specs=blk,
        out_shape=SDS((S, D), bf16), compiler_params=_cp(32),
    )(proj, proj, proj, *us)


def gate_bwd(dmerged, proj, us, *, tm=2048):
    S = proj.shape[0]
    tm = min(tm, S)

    def body(dm_ref, ml0, ml1, ml2, u0, u1, u2, du0, du1, du2, dl0, dl1, dl2):
        dm = dm_ref[...]
        for ml, u, du, dl in ((ml0, u0, du0, dl0), (ml1, u1, du1, dl1), (ml2, u2, du2, dl2)):
            s = _sigmoid(ml[...])
            du[...] = (dm * s).astype(bf16)
            dl[...] = (dm * u[...].astype(f32) * (s * (1.0 - s))).astype(bf16)

    blk = pl.BlockSpec((tm, LANE), lambda i, j: (i, j))
    outs = pl.pallas_call(
        body, name="gate_bwd", grid=(S // tm, D // LANE),
        in_specs=[blk, _ml_spec(0, tm), _ml_spec(1, tm), _ml_spec(2, tm), blk, blk, blk], out_specs=[blk] * 6,
        out_shape=[SDS((S, D), bf16)] * 6, compiler_params=_cp(40),
    )(dmerged, proj, proj, proj, *us)
    return outs[:3], outs[3:]


def rope_tables(positions):
    pos = positions.astype(f32)[:, None]

    def cs(dim):
        inv = 10000.0 ** (-jnp.arange(0, dim, 2, dtype=f32) / dim)
        ang = pos * inv
        return jnp.cos(ang), jnp.sin(ang)

    c, s = cs(128)
    ret = (jnp.concatenate([c, c], 1), jnp.concatenate([-s, s], 1))
    c, s = cs(64)
    z32, z64 = jnp.zeros_like(c), jnp.zeros((c.shape[0], 64), f32)
    mla = (jnp.concatenate([c, c, z64], 1), jnp.concatenate([-s, z32, z64], 1), jnp.concatenate([z32, s, z64], 1))
    return ret, mla


def _rope_ret(x, c, s):
    return x * c + pltpu.roll(x, 64, 1) * s


def _rope_ret_t(dy, c, s):
    return dy * c + pltpu.roll(dy * s, 64, 1)


def _rope_mla(x, c, sa, sb):
    return x * c + pltpu.roll(x, 96, 1) * sa + pltpu.roll(x, 32, 1) * sb


def _rope_mla_t(dy, c, sa, sb):
    return dy * c + pltpu.roll(dy * sa, 32, 1) + pltpu.roll(dy * sb, 96, 1)


def _softmax_rows(q, k):
    bq, nk = q.shape[0], k.shape[0]
    s = lax.dot_general(q, k, _DIMS["nt"], preferred_element_type=f32)
    i = lax.broadcasted_iota(jnp.int32, (bq, bq), 0)
    j = lax.broadcasted_iota(jnp.int32, (bq, bq), 1)
    own = s[:, nk - bq:] + jnp.where((j // CHUNK) <= (i // CHUNK), 0.0, -1e30)
    s = own if nk == bq else jnp.concatenate([s[:, :nk - bq], own], axis=1)
    e = jnp.exp(s - jnp.max(s, axis=-1, keepdims=True))
    return e / jnp.sum(e, axis=-1, keepdims=True)


def _decay_rows(lg, bq, S):
    nq = S // bq
    i = lax.broadcasted_iota(jnp.int32, (bq, S), 0)
    col = lax.broadcasted_iota(jnp.int32, (bq, S), 1)
    j = col % bq
    back = nq - 1 - col // bq
    dist = back * bq + i - j
    seen = jnp.logical_or(back > 0, (j // CHUNK) <= (i // CHUNK))
    return jnp.where(seen, jnp.exp(lg * jnp.abs(dist).astype(f32)), 0.0)


def _per_query_block(nq, fn):
    for qi in range(nq):
        pl.when(pl.program_id(1) == qi)(functools.partial(fn, qi))


def mla_attn_fwd(q, k, v, proj, *, bq=512):
    S = q.shape[0]
    bq = min(bq, S)
    assert bq % CHUNK == 0

    def body(q_ref, k_ref, v_ref, g_ref, o_ref, y_ref):
        def block(qi):
            keys = pl.ds(0, (qi + 1) * bq)
            p = _softmax_rows(q_ref[...], k_ref[keys, :])
            o = jnp.dot(p.astype(bf16), v_ref[keys, :], preferred_element_type=f32)
            o_ref[...] = o
            y_ref[...] = (o * _silu(g_ref[...])).astype(bf16)

        _per_query_block(S // bq, block)

    o_spec = pl.BlockSpec((bq, DH), lambda h, i: (i, h))
    return pl.pallas_call(
        body, name="mla_attn_fwd", grid=(HEADS, S // bq),
        in_specs=[pl.BlockSpec((bq, QPAD), lambda h, i: (i, h)), pl.BlockSpec((S, QPAD), lambda h, i: (0, h)),
                  pl.BlockSpec((S, DH), lambda h, i: (0, h)), pl.BlockSpec((bq, DH), lambda h, i: (i, MG0 + h))],
        out_specs=[o_spec, o_spec], out_shape=[SDS((S, BW), f32), SDS((S, BW), bf16)],
        compiler_params=_cp(48))(q, k, v, proj)


def mla_attn_bwd(q, k, v, do, o, *, bq=512):
    S = q.shape[0]
    bq = min(bq, S)

    def body(q_ref, k_ref, v_ref, do_ref, o_ref, dq_ref, dk_ref, dv_ref):
        def block(qi):
            keys = pl.ds(0, (qi + 1) * bq)
            if qi == 0:
                dk_ref[...] = jnp.zeros_like(dk_ref)
                dv_ref[...] = jnp.zeros_like(dv_ref)
            qv, kv, dov = q_ref[...], k_ref[keys, :], do_ref[...]
            p = _softmax_rows(qv, kv)
            dp = lax.dot_general(dov, v_ref[keys, :], _DIMS["nt"], preferred_element_type=f32)
            delta = jnp.sum(dov.astype(f32) * o_ref[...], axis=-1, keepdims=True)
            dsb = (p * (dp - delta)).astype(bf16)
            dq_ref[...] = jnp.dot(dsb, kv, preferred_element_type=f32)
            dv_ref[keys, :] += lax.dot_general(p.astype(bf16), dov, _DIMS["tn"], preferred_element_type=f32)
            dk_ref[keys, :] += lax.dot_general(dsb, qv, _DIMS["tn"], preferred_element_type=f32)

        _per_query_block(S // bq, block)

    o_spec = pl.BlockSpec((bq, DH), lambda h, i: (i, h))
    return pl.pallas_call(
        body, name="mla_attn_bwd", grid=(HEADS, S // bq),
        in_specs=[pl.BlockSpec((bq, QPAD), lambda h, i: (i, h)), pl.BlockSpec((S, QPAD), lambda h, i: (0, h)),
                  pl.BlockSpec((S, DH), lambda h, i: (0, h)), o_spec, o_spec],
        out_specs=[pl.BlockSpec((bq, QPAD), lambda h, i: (i, h)), pl.BlockSpec((S, QPAD), lambda h, i: (0, h)),
                   pl.BlockSpec((S, DH), lambda h, i: (0, h))],
        out_shape=[SDS((S, HEADS * QPAD), f32), SDS((S, HEADS * QPAD), f32), SDS((S, BW), f32)],
        compiler_params=_cp(52))(q, k, v, do, o)


def ret_attn_fwd(proj, tabs, lgam, gn, *, bq=512):
    S = proj.shape[0]
    bq = min(bq, S)
    assert bq % CHUNK == 0

    def body(lg_ref, q_ref, k_ref, v_ref, g_ref, gn_ref, cq_ref, sq_ref, ck_ref, sk_ref, o_ref, y_ref, k_s, w_s):
        lg = lg_ref[pl.program_id(0)]

        def block(qi):
            if qi == 0:
                k_s[...] = _rope_ret(k_ref[...], ck_ref[...], sk_ref[...]).astype(bf16)
                w_s[...] = _decay_rows(lg, bq, S)
            keys = pl.ds(0, (qi + 1) * bq)
            qv = (_rope_ret(q_ref[...], cq_ref[...], sq_ref[...]) * RET_SCALE).astype(bf16)
            p = lax.dot_general(qv, k_s[keys, :], _DIMS["nt"], preferred_element_type=f32) * w_s[:, pl.ds(S - (qi + 1) * bq, (qi + 1) * bq)]
            o = jnp.dot(p.astype(bf16), v_ref[keys, :].astype(bf16), preferred_element_type=f32)
            o_ref[...] = o
            oc = o - jnp.mean(o, axis=-1, keepdims=True)
            z = oc * lax.rsqrt(jnp.mean(oc * oc, axis=-1, keepdims=True) + EPS) * gn_ref[...]
            y_ref[...] = (z * _silu(g_ref[...])).astype(bf16)

        _per_query_block(S // bq, block)

    row = lambda b0: pl.BlockSpec((bq, DH), lambda h, i: (i, b0 + h))
    full = lambda b0: pl.BlockSpec((S, DH), lambda h, i: (0, b0 + h))
    tq, tk = pl.BlockSpec((bq, DH), lambda h, i: (i, 0)), pl.BlockSpec((S, DH), lambda h, i: (0, 0))
    o_spec = pl.BlockSpec((bq, DH), lambda h, i: (i, h))
    return pl.pallas_call(
        body, name="ret_attn_fwd", grid=(HEADS, S // bq),
        in_specs=[pl.BlockSpec(memory_space=pltpu.SMEM), row(Q0), full(K0), full(V0), row(RG0),
                  pl.BlockSpec((1, DH), lambda h, i: (0, h)), tq, tq, tk, tk],
        out_specs=[o_spec, o_spec], out_shape=[SDS((S, BW), f32), SDS((S, BW), bf16)],
        scratch_shapes=[pltpu.VMEM((S, DH), bf16), pltpu.VMEM((bq, S), f32)], compiler_params=_cp(48),
    )(lgam, proj, proj, proj, proj, gn, tabs[0], tabs[1], tabs[0], tabs[1])


def ret_attn_bwd(proj, do, tabs, lgam, *, bq=512):
    S = proj.shape[0]
    bq = min(bq, S)
    nq = S // bq

    def body(lg_ref, q_ref, k_ref, v_ref, do_ref, cq_ref, sq_ref, ck_ref, sk_ref, dq_ref, dk_ref, dv_ref, k_s, dk_s, dv_s, w_s):
        lg = lg_ref[pl.program_id(0)]

        def block(qi):
            if qi == 0:
                k_s[...] = _rope_ret(k_ref[...], ck_ref[...], sk_ref[...]).astype(bf16)
                w_s[...] = _decay_rows(lg, bq, S)
                dk_s[...] = jnp.zeros_like(dk_s)
                dv_s[...] = jnp.zeros_like(dv_s)
            keys = pl.ds(0, (qi + 1) * bq)
            cq, sq = cq_ref[...], sq_ref[...]
            qv = (_rope_ret(q_ref[...], cq, sq) * RET_SCALE).astype(bf16)
            kv, dov = k_s[keys, :], do_ref[...]
            w = w_s[:, pl.ds(S - (qi + 1) * bq, (qi + 1) * bq)]
            p = lax.dot_general(qv, kv, _DIMS["nt"], preferred_element_type=f32) * w
            dp = lax.dot_general(dov, v_ref[keys, :].astype(bf16), _DIMS["nt"], preferred_element_type=f32)
            dsb = (dp * w).astype(bf16)
            dq_ref[...] = _rope_ret_t(jnp.dot(dsb, kv, preferred_element_type=f32) * RET_SCALE, cq, sq).astype(bf16)
            dv_s[keys, :] += lax.dot_general(p.astype(bf16), dov, _DIMS["tn"], preferred_element_type=f32)
            dk_s[keys, :] += lax.dot_general(dsb, qv, _DIMS["tn"], preferred_element_type=f32)
            if qi == nq - 1:
                dk_ref[...] = _rope_ret_t(dk_s[...], ck_ref[...], sk_ref[...]).astype(bf16)
                dv_ref[...] = dv_s[...].astype(bf16)

        _per_query_block(nq, block)

    row = lambda b0: pl.BlockSpec((bq, DH), lambda h, i: (i, b0 + h))
    full = lambda b0: pl.BlockSpec((S, DH), lambda h, i: (0, b0 + h))
    tq, tk = pl.BlockSpec((bq, DH), lambda h, i: (i, 0)), pl.BlockSpec((S, DH), lambda h, i: (0, 0))
    return pl.pallas_call(
        body, name="ret_attn_bwd", grid=(HEADS, nq),
        in_specs=[pl.BlockSpec(memory_space=pltpu.SMEM), row(Q0), full(K0), full(V0), row(0), tq, tq, tk, tk],
        out_specs=[row(0), full(0), full(0)], out_shape=[SDS((S, BW), bf16)] * 3,
        scratch_shapes=[pltpu.VMEM((S, DH), bf16), pltpu.VMEM((S, DH), f32), pltpu.VMEM((S, DH), f32),
                        pltpu.VMEM((bq, S), f32)],
        compiler_params=_cp(52),
    )(lgam, proj, proj, proj, do, tabs[0], tabs[1], tabs[0], tabs[1])


def ret_post_bwd(dy, o, proj, gn, *, tm=512):
    S = dy.shape[0]
    tm = min(tm, S)

    def body(dy_ref, o_ref, g_ref, gn_ref, do_ref, drg_ref, dgn_ref):
        ov, g, gnv, dyv = o_ref[...], g_ref[...], gn_ref[...], dy_ref[...]
        oc = ov - jnp.mean(ov, axis=-1, keepdims=True)
        rs = lax.rsqrt(jnp.mean(oc * oc, axis=-1, keepdims=True) + EPS)
        oh = oc * rs
        dz = dyv * _silu(g)
        drg_ref[...] = (dyv * (oh * gnv) * _dsilu(g)).astype(bf16)
        doh = dz * gnv
        do_ref[...] = (rs * (doh - jnp.mean(doh, axis=-1, keepdims=True)
                             - oh * jnp.mean(doh * oh, axis=-1, keepdims=True))).astype(bf16)
        _acc(dgn_ref, jnp.sum(dz * oh, axis=0, keepdims=True), pl.program_id(1) == 0)

    blk = pl.BlockSpec((tm, DH), lambda h, i: (i, h))
    vec = pl.BlockSpec((1, DH), lambda h, i: (0, h))
    return pl.pallas_call(
        body, name="ret_post_bwd", grid=(HEADS, S // tm),
        in_specs=[blk, blk, pl.BlockSpec((tm, DH), lambda h, i: (i, RG0 + h)), vec], out_specs=[blk, blk, vec],
        out_shape=[SDS((S, BW), bf16), SDS((S, BW), bf16), SDS((1, BW), f32)], compiler_params=_cp(32))(dy, o, proj, gn)


def mla_post_bwd(dy, o, proj, *, tm=512):
    S = dy.shape[0]
    tm = min(tm, S)

    def body(dy_ref, o_ref, g_ref, do_ref, dg_ref):
        g, dyv = g_ref[...], dy_ref[...]
        do_ref[...] = (dyv * _silu(g)).astype(bf16)
        dg_ref[...] = (dyv * o_ref[...] * _dsilu(g)).astype(bf16)

    blk = pl.BlockSpec((tm, DH), lambda i, h: (i, h))
    return pl.pallas_call(
        body, name="mla_post_bwd", grid=(S // tm, HEADS),
        in_specs=[blk, blk, pl.BlockSpec((tm, DH), lambda i, h: (i, MG0 + h))], out_specs=[blk, blk],
        out_shape=[SDS((S, BW), bf16), SDS((S, BW), bf16)], compiler_params=_cp(32))(dy, o, proj)


def mla_prep_fwd(proj, qnorm, kvnorm, wuq, wukv, tabs, *, tm=512, kv_blk=0):
    S = proj.shape[0]
    tm = min(tm, S)

    def body(mq_ref, mkv_ref, mkr_ref, qn_ref, kvn_ref, wuq_ref, wukv_ref, c_ref, sa_ref, sb_ref, q_ref, k_ref, v_ref):
        c, sa, sb = c_ref[...], sa_ref[...], sb_ref[...]
        mq, mkv = mq_ref[...], mkv_ref[...]
        qn = (mq * lax.rsqrt(jnp.mean(mq * mq, axis=-1, keepdims=True) + EPS) * qn_ref[...]).astype(bf16)
        kvn = (mkv * lax.rsqrt(jnp.mean(mkv * mkv, axis=-1, keepdims=True) + EPS) * kvn_ref[...]).astype(bf16)
        kr = _rope_mla(mkr_ref[...], c, sa, sb).astype(bf16)
        for h in range(HEADS):
            qh = jnp.dot(qn, wuq_ref[h], preferred_element_type=f32)
            q_ref[:, pl.ds(h * QPAD, DH)] = (qh[:, :DH] * MLA_SCALE).astype(bf16)
            q_ref[:, pl.ds(h * QPAD + DH, DH)] = (_rope_mla(qh[:, DH:], c, sa, sb) * MLA_SCALE).astype(bf16)
            kvh = jnp.dot(kvn, wukv_ref[h], preferred_element_type=f32)
            k_ref[:, pl.ds(h * QPAD, DH)] = kvh[:, :DH].astype(bf16)
            k_ref[:, pl.ds(h * QPAD + DH, DH)] = kr
            v_ref[:, pl.ds(h * DH, DH)] = kvh[:, DH:].astype(bf16)

    lat = lambda b: pl.BlockSpec((tm, LORA), lambda i: (i, b))
    tab = pl.BlockSpec((tm, DH), lambda i: (i, 0))
    vec = pl.BlockSpec((1, LORA), lambda i: (0, 0))
    wsp = pl.BlockSpec((HEADS, LORA, QPAD), lambda i: (0, 0, 0))
    wkv = pl.BlockSpec((HEADS, LORA, QPAD), lambda i: (0, 0, kv_blk))
    return pl.pallas_call(
        body, name="mla_prep_fwd", grid=(S // tm,),
        in_specs=[lat(MQ0 // 4), lat(MKV0 // 4), pl.BlockSpec((tm, DH), lambda i: (i, MKR0)), vec, vec, wsp, wkv,
                  tab, tab, tab],
        out_specs=[pl.BlockSpec((tm, HEADS * QPAD), lambda i: (i, 0))] * 2 + [pl.BlockSpec((tm, BW), lambda i: (i, 0))],
        out_shape=[SDS((S, HEADS * QPAD), bf16)] * 2 + [SDS((S, BW), bf16)], compiler_params=_cp(48),
    )(proj, proj, proj, qnorm, kvnorm, wuq, wukv, *tabs)


def mla_prep_bwd(dq256, dk256, dv, proj, qnorm, kvnorm, wuq, wukv, tabs, *, tm=512, kv_blk=0):
    S = proj.shape[0]
    tm = min(tm, S)

    def body(dq_ref, dk_ref, dv_ref, mq_ref, mkv_ref, qn_ref, kvn_ref, wuq_ref, wukv_ref, c_ref, sa_ref, sb_ref,
             dm_ref, dwuq_ref, dwukv_ref, dqn_ref, dkvn_ref):
        first = pl.program_id(0) == 0
        c, sa, sb = c_ref[...], sa_ref[...], sb_ref[...]
        mq, mkv = mq_ref[...], mkv_ref[...]
        rq = lax.rsqrt(jnp.mean(mq * mq, axis=-1, keepdims=True) + EPS)
        rkv = lax.rsqrt(jnp.mean(mkv * mkv, axis=-1, keepdims=True) + EPS)
        mqh, mkvh = mq * rq, mkv * rkv
        qn = (mqh * qn_ref[...]).astype(bf16)
        kvn = (mkvh * kvn_ref[...]).astype(bf16)
        dqn = jnp.zeros((tm, LORA), f32)
        dkvn = jnp.zeros((tm, LORA), f32)
        dkr = jnp.zeros((tm, DH), f32)
        for h in range(HEADS):
            da = dq_ref[:, pl.ds(h * QPAD, DH)] * MLA_SCALE
            db = _rope_mla_t(dq_ref[:, pl.ds(h * QPAD + DH, DH)] * MLA_SCALE, c, sa, sb)
            dqh = jnp.concatenate([da, db], axis=1).astype(bf16)
            dqn += lax.dot_general(dqh, wuq_ref[h], _DIMS["nt"], preferred_element_type=f32)
            _acc(dwuq_ref.at[h], lax.dot_general(qn, dqh, _DIMS["tn"], preferred_element_type=f32), first)
            dkr += dk_ref[:, pl.ds(h * QPAD + DH, DH)]
            dkvh = jnp.concatenate([dk_ref[:, pl.ds(h * QPAD, DH)], dv_ref[:, pl.ds(h * DH, DH)]], axis=1).astype(bf16)
            dkvn += lax.dot_general(dkvh, wukv_ref[h], _DIMS["nt"], preferred_element_type=f32)
            _acc(dwukv_ref.at[h], lax.dot_general(kvn, dkvh, _DIMS["tn"], preferred_element_type=f32), first)
        dmh = dqn * qn_ref[...]
        dm_ref[:, pl.ds(0, LORA)] = (rq * (dmh - mqh * jnp.mean(dmh * mqh, axis=-1, keepdims=True))).astype(bf16)
        dmh = dkvn * kvn_ref[...]
        dm_ref[:, pl.ds(LORA, LORA)] = (rkv * (dmh - mkvh * jnp.mean(dmh * mkvh, axis=-1, keepdims=True))).astype(bf16)
        dm_ref[:, pl.ds(2 * LORA, DH)] = _rope_mla_t(dkr, c, sa, sb).astype(bf16)
        _acc(dqn_ref, jnp.sum(dqn * mqh, axis=0, keepdims=True), first)
        _acc(dkvn_ref, jnp.sum(dkvn * mkvh, axis=0, keepdims=True), first)

    lat = lambda b: pl.BlockSpec((tm, LORA), lambda i: (i, b))
    tab = pl.BlockSpec((tm, DH), lambda i: (i, 0))
    vec = pl.BlockSpec((1, LORA), lambda i: (0, 0))
    wsp = pl.BlockSpec((HEADS, LORA, QPAD), lambda i: (0, 0, 0))
    wkv = pl.BlockSpec((HEADS, LORA, QPAD), lambda i: (0, 0, kv_blk))
    wide = pl.BlockSpec((tm, HEADS * QPAD), lambda i: (i, 0))
    return pl.pallas_call(
        body, name="mla_prep_bwd", grid=(S // tm,),
        in_specs=[wide, wide, pl.BlockSpec((tm, BW), lambda i: (i, 0)), lat(MQ0 // 4), lat(MKV0 // 4), vec, vec, wsp, wkv,
                  tab, tab, tab],
        out_specs=[pl.BlockSpec((tm, 2 * LORA + DH), lambda i: (i, 0)), wsp, wsp, vec, vec],
        out_shape=[SDS((S, 2 * LORA + DH), bf16), SDS((HEADS, LORA, QPAD), f32), SDS((HEADS, LORA, QPAD), f32),
                   SDS((1, LORA), f32), SDS((1, LORA), f32)],
        compiler_params=_cp(52),
    )(dq256, dk256, dv, proj, proj, qnorm, kvnorm, wuq, wukv, *tabs)


SUB = 8


def _scan_tiles(a_s, b_s, out, S, reverse):
    nt = S // SUB
    rows = lax.broadcasted_iota(jnp.int32, (SUB, LANE), 0)

    def tile(t, carry):
        base = pl.multiple_of((nt - 1 - t if reverse else t) * SUB, SUB)
        a, b = a_s[pl.ds(base, SUB), :], b_s[pl.ds(base, SUB), :]
        for d in (1, 2, 4):
            sh = SUB - d if reverse else d
            inside = rows < SUB - d if reverse else rows >= d
            a_n = jnp.where(inside, pltpu.roll(a, sh, 0), 1.0)
            b_n = jnp.where(inside, pltpu.roll(b, sh, 0), 0.0)
            b = a * b_n + b
            a = a * a_n
        res = a * carry + b
        out[pl.ds(base, SUB), :] = res
        edge = res[0:1, :] if reverse else res[SUB - 1:SUB, :]
        return jnp.broadcast_to(edge, (SUB, LANE))

    lax.fori_loop(0, nt, tile, jnp.zeros((SUB, LANE), f32))


def _shift_down(x, n, rows):
    return x if n == 0 else jnp.where(rows >= n, pltpu.roll(x, n, 0), 0.0)


def _shift_up(x, n, rows, S):
    return x if n == 0 else jnp.where(rows < S - n, pltpu.roll(x, S - n, 0), 0.0)


def _lru_gates(xb, cw, cb, wa, ba, wx, bx, lam, rows):
    xc = cb + cw[3:4, :] * xb
    for w in range(3):
        xc = xc + cw[w:w + 1, :] * _shift_down(xb, 3 - w, rows)
    xcb = xc.astype(bf16)
    r = _sigmoid(jnp.dot(xcb, wa, preferred_element_type=f32) + ba)
    i = _sigmoid(jnp.dot(xcb, wx, preferred_element_type=f32) + bx)
    sp = _softplus(-lam)
    la = (-LRU_C * r) * sp
    return xc, xcb, r, i, sp, la, jnp.exp(la)


def _lru_specs(S):
    col = lambda b0: pl.BlockSpec((S, LANE), lambda n: (0, b0 + n))
    vec = pl.BlockSpec((1, LANE), lambda n: (0, n))
    return col, vec, pl.BlockSpec((4, LANE), lambda n: (0, n)), pl.BlockSpec((1, LANE, LANE), lambda n: (n, 0, 0))


def lru_fwd(proj, cw, cb, wa, ba, wx, bx, lam):
    S = proj.shape[0]

    def body(x_ref, g_ref, cw_ref, cb_ref, wa_ref, ba_ref, wx_ref, bx_ref, lam_ref, h_ref, y_ref, a_s, b_s):
        rows = lax.broadcasted_iota(jnp.int32, (S, LANE), 0)
        xc, _, _, i, _, la, a = _lru_gates(x_ref[...], cw_ref[...], cb_ref[...], wa_ref[0].astype(bf16), ba_ref[...],
                                           wx_ref[0].astype(bf16), bx_ref[...], lam_ref[...], rows)
        a_s[...] = a
        b_s[...] = jnp.sqrt(_one_minus_exp(2.0 * la)) * (i * xc)
        _scan_tiles(a_s, b_s, h_ref, S, reverse=False)
        y_ref[...] = (h_ref[...] * _silu(g_ref[...])).astype(bf16)

    col, vec, cws, wsp = _lru_specs(S)
    return pl.pallas_call(
        body, name="lru_fwd", grid=(HEADS,),
        in_specs=[col(LX0), col(LG0), cws, vec, wsp, vec, wsp, vec, vec], out_specs=[col(0), col(0)],
        out_shape=[SDS((S, BW), f32), SDS((S, BW), bf16)],
        scratch_shapes=[pltpu.VMEM((S, LANE), f32), pltpu.VMEM((S, LANE), f32)], compiler_params=_cp(40),
    )(proj, proj, cw, cb, wa, ba, wx, bx, lam)


def lru_bwd(dy, h, proj, cw, cb, wa, ba, wx, bx, lam):
    S = proj.shape[0]

    def body(dy_ref, h_ref, x_ref, g_ref, cw_ref, cb_ref, wa_ref, ba_ref, wx_ref, bx_ref, lam_ref,
             dx_ref, dg_ref, dcw_ref, dcb_ref, dba_ref, dbx_ref, dlam_ref, dwa_ref, dwx_ref, a_s, b_s, l_s):
        rows = lax.broadcasted_iota(jnp.int32, (S, LANE), 0)
        xb, g, hv, dyv, cw, lam = x_ref[...], g_ref[...], h_ref[...], dy_ref[...], cw_ref[...], lam_ref[...]
        wa, wx = wa_ref[0].astype(bf16), wx_ref[0].astype(bf16)
        xc, xcb, r, i, sp, la, a = _lru_gates(xb, cw, cb_ref[...], wa, ba_ref[...], wx, bx_ref[...], lam, rows)
        dg_ref[...] = (dyv * hv * _dsilu(g)).astype(bf16)
        a_s[...] = _shift_up(a, 1, rows, S)
        b_s[...] = dyv * _silu(g)
        _scan_tiles(a_s, b_s, l_s, S, reverse=True)
        lmb = l_s[...]
        gated = i * xc
        sq = jnp.sqrt(_one_minus_exp(2.0 * la))
        dla = lmb * _shift_down(hv, 1, rows) * a - (lmb * gated) * (a * a) / sq
        dgated = lmb * sq
        dzr = (dla * (-LRU_C * sp)) * (r * (1.0 - r))
        dzi = (dgated * xc) * (i * (1.0 - i))
        dzrb, dzib = dzr.astype(bf16), dzi.astype(bf16)
        dxc = (dgated * i + lax.dot_general(dzrb, wa, _DIMS["nt"], preferred_element_type=f32)
               + lax.dot_general(dzib, wx, _DIMS["nt"], preferred_element_type=f32))
        dwa_ref[0] = lax.dot_general(xcb, dzrb, _DIMS["tn"], preferred_element_type=f32)
        dwx_ref[0] = lax.dot_general(xcb, dzib, _DIMS["tn"], preferred_element_type=f32)
        dba_ref[...] = jnp.sum(dzr, axis=0, keepdims=True)
        dbx_ref[...] = jnp.sum(dzi, axis=0, keepdims=True)
        dlam_ref[...] = jnp.sum(dla * (-LRU_C * r), axis=0, keepdims=True) * (-_sigmoid(-lam))
        dcb_ref[...] = jnp.sum(dxc, axis=0, keepdims=True)
        dxb = cw[3:4, :] * dxc
        dcw_ref[3:4, :] = jnp.sum(dxc * xb, axis=0, keepdims=True)
        for w in range(3):
            dxb = dxb + cw[w:w + 1, :] * _shift_up(dxc, 3 - w, rows, S)
            dcw_ref[w:w + 1, :] = jnp.sum(dxc * _shift_down(xb, 3 - w, rows), axis=0, keepdims=True)
        dx_ref[...] = dxb.astype(bf16)

    col, vec, cws, wsp = _lru_specs(S)
    scr = pltpu.VMEM((S, LANE), f32)
    return pl.pallas_call(
        body, name="lru_bwd", grid=(HEADS,),
        in_specs=[col(0), col(0), col(LX0), col(LG0), cws, vec, wsp, vec, wsp, vec, vec],
        out_specs=[col(0), col(0), cws, vec, vec, vec, vec, wsp, wsp],
        out_shape=[SDS((S, BW), bf16), SDS((S, BW), bf16), SDS((4, BW), f32)] + [SDS((1, BW), f32)] * 4
        + [SDS((HEADS, LANE, LANE), f32)] * 2,
        scratch_shapes=[scr, scr, scr], compiler_params=_cp(48),
    )(dy, h, proj, proj, cw, cb, wa, ba, wx, bx, lam)


def loss_head(y, target, *, tm=256):
    S = y.shape[0]

    def body(y_ref, t_ref, l_ref, d_ref):
        err = y_ref[...] - t_ref[...]
        d_ref[...] = err * (1.0 / D)
        part = jnp.sum(jnp.sum(err * err, axis=1, keepdims=True), axis=0, keepdims=True) * (0.5 / D)
        _acc(l_ref, jnp.broadcast_to(part, (1, LANE)), pl.program_id(0) == 0)

    row = pl.BlockSpec((tm, D), lambda i: (i, 0))
    return pl.pallas_call(
        body, name="loss_head", grid=(S // tm,), in_specs=[row, row],
        out_specs=[pl.BlockSpec((1, LANE), lambda i: (0, 0)), row],
        out_shape=[SDS((1, LANE), f32), SDS((S, D), f32)], compiler_params=_cp(32))(y, target)


def _adam_math(w, g, m, v):
    mn = ADAM_B1 * m + (1.0 - ADAM_B1) * g
    vn = ADAM_B2 * v + (1.0 - ADAM_B2) * (g * g)
    m_hat = mn / (1.0 - ADAM_B1 ** ADAM_STEP)
    v_hat = vn / (1.0 - ADAM_B2 ** ADAM_STEP)
    return -ADAM_LR * (m_hat / (jnp.sqrt(v_hat) + ADAM_EPS) + ADAM_WD * w), mn, vn


def _adam_rows(rows, cols):
    for cand in (2048, 1024, 512, 256, 128, 64, 32, 16, 8):
        if rows % cand == 0 and rows > cand and cand * cols <= ADAM_BLOCK_ELEMS:
            return cand
    return rows


def adamw(w, g, m, v):
    shape = w.shape
    cols = shape[-1]
    rows = math.prod(shape[:-1])
    tr = _adam_rows(rows, cols)

    def body(w_ref, g_ref, m_ref, v_ref, d_ref, mo_ref, vo_ref):
        d_ref[...], mo_ref[...], vo_ref[...] = _adam_math(w_ref[...], g_ref[...], m_ref[...], v_ref[...])

    blk = pl.BlockSpec((tr, cols), lambda i: (i, 0))
    flat = [t.reshape(rows, cols) for t in (w, g, m, v)]
    outs = pl.pallas_call(
        body, name="adamw", grid=(rows // tr,), in_specs=[blk] * 4, out_specs=[blk] * 3,
        out_shape=[SDS((rows, cols), f32)] * 3, compiler_params=_cp(48))(*flat)
    return tuple(o.reshape(shape) for o in outs)


ADA_SHARD = 3 * D // NDEV
ROWS16 = 16


def ada_fwd(c_all, ada_w, ada_b_mine):
    def body(c_ref, w_ref, b_ref, o_ref):
        o_ref[0] = jnp.dot(_silu(c_ref[...]).astype(bf16), w_ref[0].astype(bf16), preferred_element_type=f32) + b_ref[0]

    return pl.pallas_call(
        body, name="ada_fwd", grid=(DEPTH,),
        in_specs=[pl.BlockSpec((ROWS16, D), lambda l: (0, 0)), pl.BlockSpec((1, D, ADA_SHARD), lambda l: (l, 0, 0)),
                  pl.BlockSpec((1, 1, ADA_SHARD), lambda l: (l, 0, 0))],
        out_specs=pl.BlockSpec((1, ROWS16, ADA_SHARD), lambda l: (l, 0, 0)),
        out_shape=SDS((DEPTH, ROWS16, ADA_SHARD), f32), compiler_params=_cp(40))(c_all, ada_w, ada_b_mine)


def ada_bwd(c_all, dmod):
    def body(c_ref, d_ref, o_ref):
        o_ref[0] = lax.dot_general(_silu(c_ref[...]).astype(bf16), d_ref[0].astype(bf16), _DIMS["tn"],
                                   preferred_element_type=f32)

    return pl.pallas_call(
        body, name="ada_bwd", grid=(DEPTH,),
        in_specs=[pl.BlockSpec((ROWS16, D), lambda l: (0, 0)), pl.BlockSpec((1, ROWS16, ADA_SHARD), lambda l: (l, 0, 0))],
        out_specs=pl.BlockSpec((1, D, ADA_SHARD), lambda l: (l, 0, 0)),
        out_shape=SDS((DEPTH, D, ADA_SHARD), f32), compiler_params=_cp(40))(c_all, dmod)


def shift_params(me):
    start = SHARD_W * me
    return jnp.stack([start % LANE, (start + GAP) % LANE, jnp.clip(GAP_COL - start, 0, SHARD_W)]).astype(jnp.int32)


def win_pack(wt, sidx, *, first=0, count=DEPTH, after=()):
    def body(s_ref, w_ref, *rest):
        o_ref, scr = rest[len(after):]
        s1, s2, gi = s_ref[0], s_ref[1], s_ref[2]
        scr[pl.ds(SHARD_W, WIN - SHARD_W), :] = jnp.zeros((WIN - SHARD_W, LANE), f32)
        scr[pl.ds(0, SHARD_W), :] = w_ref[0]
        v = scr[...].T
        j = lax.broadcasted_iota(jnp.int32, v.shape, 1)
        o_ref[0] = jnp.where(j - s1 < gi, pltpu.roll(v, s1, 1),
                             jnp.where(j - s2 >= gi, pltpu.roll(v, s2, 1), 0.0)).astype(bf16)

    return pl.pallas_call(
        body, name="win_pack", grid=(count, D // LANE),
        in_specs=[pl.BlockSpec(memory_space=pltpu.SMEM), pl.BlockSpec((1, SHARD_W, LANE), lambda l, i: (first + l, 0, i))]
        + [pl.BlockSpec(memory_space=pl.ANY)] * len(after),
        out_specs=pl.BlockSpec((1, LANE, WIN), lambda l, i: (l, i, 0)),
        out_shape=SDS((count, D, WIN), bf16), scratch_shapes=[pltpu.VMEM((WIN, LANE), f32)],
        compiler_params=_cp(32))(sidx, wt, *after)


def win_assemble(g, *, after=()):
    own = WIN_STRIDE * LANE
    tail = NP - NDEV * own
    assert tail == 2 * LANE

    def body(a_ref, b_ref, *rest):
        o_ref = rest[len(after)]
        o_ref[...] = a_ref[0]

        @pl.when(pl.program_id(0) > 0)
        def _():
            o_ref[:, pl.ds(0, LANE)] = a_ref[0, :, pl.ds(0, LANE)] + b_ref[0]

    main = pl.pallas_call(
        body, name="win_assemble", grid=(NDEV,),
        in_specs=[pl.BlockSpec((1, D, own), lambda k: (k, 0, 0)),
                  pl.BlockSpec((1, D, LANE), lambda k: (jnp.maximum(k - 1, 0), 0, WIN_BLKS - 1))]
        + [pl.BlockSpec(memory_space=pl.ANY)] * len(after),
        out_specs=pl.BlockSpec((D, own), lambda k: (0, k)),
        out_shape=SDS((D, NP), bf16), compiler_params=_cp(48))(g, g, *after)

    def tail_body(_, b_ref, o_ref):
        o_ref[:, pl.ds(0, LANE)] = b_ref[0]
        o_ref[:, pl.ds(LANE, LANE)] = jnp.zeros((D, LANE), bf16)

    return pl.pallas_call(
        tail_body, name="win_assemble_tail", grid=(1,),
        in_specs=[pl.BlockSpec(memory_space=pl.ANY), pl.BlockSpec((1, D, LANE), lambda t: (NDEV - 1, 0, WIN_BLKS - 1))],
        out_specs=pl.BlockSpec((D, tail), lambda t: (0, NDEV * own // tail)),
        out_shape=SDS((D, NP), bf16), input_output_aliases={0: 0}, compiler_params=_cp(32))(main, g)


def reduce_adamw(stag, w, m, v, l, prev, name, *, sidx=None, tr=128):
    Cs = stag.shape[2]
    n_prev = 0 if prev is None else 4
    n_lead = 1 if sidx is not None else 0

    def body(*refs):
        refs = list(refs)
        s_ref = refs.pop(0) if sidx is not None else None
        g_ref, w_ref, m_ref, v_ref = refs[:4]
        rest = refs[4 + n_prev:]
        go_ref, d_ref, mo_ref, vo_ref = rest[:4]
        tot = g_ref[0].astype(f32)
        for d in range(1, stag.shape[0]):
            tot = tot + g_ref[d].astype(f32)
        if sidx is not None:
            scr = rest[4]
            s1, s2, gi = s_ref[0], s_ref[1], s_ref[2]
            i = lax.broadcasted_iota(jnp.int32, tot.shape, 1)
            scr[...] = jnp.where(i < gi, pltpu.roll(tot, WIN - s1, 1), pltpu.roll(tot, WIN - s2, 1)).T
            tot = scr[pl.ds(0, SHARD_W), :]
        go_ref[0] = tot
        d_ref[0], mo_ref[0], vo_ref[0] = _adam_math(w_ref[0], tot, m_ref[0], v_ref[0])

    if sidx is not None:
        blk3 = pl.BlockSpec((1, SHARD_W, tr), lambda i: (l, 0, i))
        steps = w.shape[2] // tr
    else:
        blk3 = pl.BlockSpec((1, tr, w.shape[2]), lambda i: (l, i, 0))
        steps = w.shape[1] // tr
    in_specs = ([pl.BlockSpec(memory_space=pltpu.SMEM)] * n_lead
                + [pl.BlockSpec((stag.shape[0], tr, Cs), lambda i: (0, i, 0)), blk3, blk3, blk3]
                + [pl.BlockSpec(memory_space=pl.ANY)] * n_prev)
    args = ([sidx] if sidx is not None else []) + [stag, w, m, v] + list(prev or ())
    return pl.pallas_call(
        body, name=name, grid=(steps,), in_specs=in_specs, out_specs=[blk3] * 4, out_shape=[SDS(w.shape, f32)] * 4,
        scratch_shapes=[pltpu.VMEM((Cs, tr), f32)] if sidx is not None else [],
        input_output_aliases={n_lead + 4 + k: k for k in range(n_prev)}, compiler_params=_cp(48),
    )(*args)


def cast_pad(w, cols_out, name):
    L, R, C = w.shape

    def body(w_ref, o_ref, *scr):
        if cols_out == C:
            o_ref[0] = w_ref[0].astype(bf16)
        else:
            scr[0][...] = jnp.zeros_like(scr[0])
            scr[0][:, pl.ds(0, C)] = w_ref[0]
            o_ref[0] = scr[0][...].astype(bf16)

    return pl.pallas_call(
        body, name=name, grid=(L,), in_specs=[pl.BlockSpec((1, R, C), lambda l: (l, 0, 0))],
        out_specs=pl.BlockSpec((1, R, cols_out), lambda l: (l, 0, 0)), out_shape=SDS((L, R, cols_out), bf16),
        scratch_shapes=[] if cols_out == C else [pltpu.VMEM((R, cols_out), f32)], compiler_params=_cp(32))(w)


def pack_qkv(w_uq, w_ukv):
    L = w_uq.shape[0]

    def body(q_ref, kv_ref, o_ref, scr):
        scr[...] = jnp.zeros_like(scr)
        scr[:, pl.ds(0, WUQ_COLS)] = q_ref[0]
        o_ref[0, :, pl.ds(0, QPAD)] = scr[...].astype(bf16)
        o_ref[0, :, pl.ds(QPAD, QPAD)] = kv_ref[0].astype(bf16)

    return pl.pallas_call(
        body, name="pack_qkv", grid=(L,),
        in_specs=[pl.BlockSpec((1, LORA, WUQ_COLS), lambda l: (l, 0, 0)), pl.BlockSpec((1, LORA, QPAD), lambda l: (l, 0, 0))],
        out_specs=pl.BlockSpec((1, LORA, 2 * QPAD), lambda l: (l, 0, 0)), out_shape=SDS((L, LORA, 2 * QPAD), bf16),
        scratch_shapes=[pltpu.VMEM((LORA, QPAD), f32)], compiler_params=_cp(32))(w_uq, w_ukv)


def sum_parts(stag, cols_out, name, *, tr=None, after=()):
    P, R, C = stag.shape
    tr = R if tr is None else tr

    def body(g_ref, *rest):
        o_ref, *scr = rest[len(after):]
        tot = g_ref[0].astype(f32)
        for d in range(1, P):
            tot = tot + g_ref[d].astype(f32)
        if cols_out == C:
            o_ref[...] = tot
        else:
            scr[0][...] = tot
            o_ref[...] = scr[0][:, pl.ds(0, cols_out)]

    return pl.pallas_call(
        body, name=name, grid=(R // tr,),
        in_specs=[pl.BlockSpec((P, tr, C), lambda i: (0, i, 0))] + [pl.BlockSpec(memory_space=pl.ANY)] * len(after),
        out_specs=pl.BlockSpec((tr, cols_out), lambda i: (i, 0)), out_shape=SDS((R, cols_out), f32),
        scratch_shapes=[] if cols_out == C else [pltpu.VMEM((tr, C), f32)], compiler_params=_cp(40))(stag, *after)


MESH_ID = pl.DeviceIdType.MESH
HBM_SPEC = pl.BlockSpec(memory_space=pltpu.HBM)


def _place():
    return lax.axis_index("x"), lax.axis_index("y"), lax.axis_index("c")


def all_gather(arrs, name):
    n = len(arrs)

    def body(*refs):
        ins, outs = refs[:n], refs[n:2 * n]
        send_sems, recv_sems, local_sems = refs[2 * n:]
        x, y, c = _place()
        me, sibling = (x, y, c), (x, y, 1 - c)
        chips = [(1 - x, y), (x, 1 - y), (1 - x, 1 - y)]

        def copy(a, k, block, to, src=None):
            slot = outs[a].at[4 * block[0] + 2 * block[1] + block[2]]
            return pltpu.make_async_remote_copy(
                src_ref=slot if src is None else src, dst_ref=slot, send_sem=send_sems.at[7 * a + k],
                recv_sem=recv_sems.at[7 * a + k], device_id=to, device_id_type=MESH_ID)

        mine = [pltpu.make_async_copy(ins[a], outs[a].at[4 * x + 2 * y + c], local_sems.at[a]) for a in range(n)]
        for cp in mine:
            cp.start()
        first = []
        for a in range(n):
            first.append(copy(a, 0, me, sibling, src=ins[a]))
            first += [copy(a, 1 + j, me, (*chip, c), src=ins[a]) for j, chip in enumerate(chips)]
        for cp in first:
            cp.start()
        passed = []
        for j, chip in enumerate(chips):
            for a in range(n):
                copy(a, 1 + j, (*chip, c), me).wait_recv()
                cp = copy(a, 4 + j, (*chip, c), sibling)
                cp.start()
                passed.append(cp)
        for a in range(n):
            copy(a, 0, sibling, me).wait_recv()
        for j, chip in enumerate(chips):
            for a in range(n):
                copy(a, 4 + j, (*chip, 1 - c), me).wait_recv()
        for cp in first + passed:
            cp.wait_send()
        for cp in mine:
            cp.wait()

    return pl.pallas_call(
        body, name=name, in_specs=[HBM_SPEC] * n, out_specs=[HBM_SPEC] * n,
        out_shape=[SDS((NDEV,) + a.shape, a.dtype) for a in arrs],
        scratch_shapes=[pltpu.SemaphoreType.DMA((7 * n,)), pltpu.SemaphoreType.DMA((7 * n,)),
                        pltpu.SemaphoreType.DMA((n,))],
    )(*arrs)


AG_COLLECTIVE_ID = 0
RS_COLLECTIVE_ID = 1


def _everyone_else(x, y, c):
    return [(x ^ (r >> 2), y ^ ((r >> 1) & 1), c ^ (r & 1)) for r in range(1, NDEV)]


def _rendezvous(sem, peers):
    for peer in peers:
        pl.semaphore_signal(sem, inc=1, device_id=peer, device_id_type=MESH_ID)
    pl.semaphore_wait(sem, len(peers))


def _sequencer_call(body, arrs, out_types, name, collective_id, remote=7, local=1):
    n = len(arrs)
    return pl.kernel(
        body, name=name, out_type=out_types, mesh=plsc.ScalarSubcoreMesh(axis_name="sequencer", num_cores=1),
        scratch_types=[pltpu.SemaphoreType.DMA((remote * n,)), pltpu.SemaphoreType.DMA((remote * n,)),
                       pltpu.SemaphoreType.DMA((local * n,)), pltpu.SemaphoreType.REGULAR],
        compiler_params=pltpu.CompilerParams(collective_id=collective_id),
    )(*arrs)


def seq_all_gather(arrs, name):
    n = len(arrs)

    def body(*refs):
        ins, outs = refs[:n], refs[n:2 * n]
        send_sems, recv_sems, local_sems, exit_sem = refs[2 * n:]
        x, y, c = _place()
        peers = _everyone_else(x, y, c)
        _rendezvous(pltpu.get_barrier_semaphore(), peers)
        me, sibling = (x, y, c), (x, y, 1 - c)
        chips = [(1 - x, y), (x, 1 - y), (1 - x, 1 - y)]

        def copy(a, k, block, to, src=None):
            slot = outs[a].at[4 * block[0] + 2 * block[1] + block[2]]
            return pltpu.make_async_remote_copy(
                src_ref=slot if src is None else src, dst_ref=slot, send_sem=send_sems.at[7 * a + k],
                recv_sem=recv_sems.at[7 * a + k], device_id=to, device_id_type=MESH_ID)

        mine = [pltpu.make_async_copy(ins[a], outs[a].at[4 * x + 2 * y + c], local_sems.at[a]) for a in range(n)]
        for cp in mine:
            cp.start()
        first = []
        for a in range(n):
            first.append(copy(a, 0, me, sibling, src=ins[a]))
            first += [copy(a, 1 + j, me, (*chip, c), src=ins[a]) for j, chip in enumerate(chips)]
        for cp in first:
            cp.start()
        passed = []
        for j, chip in enumerate(chips):
            for a in range(n):
                copy(a, 1 + j, (*chip, c), me).wait_recv()
                cp = copy(a, 4 + j, (*chip, c), sibling)
                cp.start()
                passed.append(cp)
        for a in range(n):
            copy(a, 0, sibling, me).wait_recv()
        for j, chip in enumerate(chips):
            for a in range(n):
                copy(a, 4 + j, (*chip, 1 - c), me).wait_recv()
        for cp in first + passed:
            cp.wait_send()
        for cp in mine:
            cp.wait()
        _rendezvous(exit_sem, peers)

    return _sequencer_call(body, arrs, [SDS((NDEV,) + a.shape, a.dtype) for a in arrs], name, AG_COLLECTIVE_ID)


def seq_reduce_scatter_parts(arrs, pick, shapes, name):
    n = len(arrs)

    def body(*refs):
        ins, outs = refs[:n], refs[n:2 * n]
        send_sems, recv_sems, local_sems, exit_sem = refs[2 * n:]
        x, y, c = _place()
        peers = _everyone_else(x, y, c)
        _rendezvous(pltpu.get_barrier_semaphore(), peers)
        me = 4 * x + 2 * y + c
        mine = [pltpu.make_async_copy(pick[a](ins[a], me), outs[a].at[me], local_sems.at[a]) for a in range(n)]
        for cp in mine:
            cp.start()
        sent = []
        for r, peer in enumerate(peers):
            pid = 4 * peer[0] + 2 * peer[1] + peer[2]
            for a in range(n):
                cp = pltpu.make_async_remote_copy(
                    src_ref=pick[a](ins[a], pid), dst_ref=outs[a].at[me], send_sem=send_sems.at[7 * a + r],
                    recv_sem=recv_sems.at[7 * a + r], device_id=peer, device_id_type=MESH_ID)
                cp.start()
                sent.append((cp, a, r, pid))
        for cp, a, r, pid in sent:
            pltpu.make_async_remote_copy(
                src_ref=pick[a](ins[a], pid), dst_ref=outs[a].at[pid], send_sem=send_sems.at[7 * a + r],
                recv_sem=recv_sems.at[7 * a + r], device_id=(x, y, c), device_id_type=MESH_ID).wait_recv()
        for cp, _, _, _ in sent:
            cp.wait_send()
        for cp in mine:
            cp.wait()
        _rendezvous(exit_sem, peers)

    return _sequencer_call(body, arrs, [SDS((NDEV,) + tuple(s), a.dtype) for s, a in zip(shapes, arrs)], name,
                           RS_COLLECTIVE_ID)


WEIGHTS = ("ada_w", "ada_b", "norm_pre", "norm_post", "w_in", "ret_gn", "lru_conv_w", "lru_conv_b", "lru_wa", "lru_ba",
           "lru_wx", "lru_bx", "lru_lambda", "mla_q_norm", "mla_w_uq", "mla_kv_norm", "mla_w_ukv", "w_branch", "w_out")
SMALL = ("norm_pre", "norm_post", "ret_gn", "lru_conv_w", "lru_conv_b", "lru_ba", "lru_bx", "lru_lambda", "mla_q_norm",
         "mla_kv_norm")
QKV_ROWS = LORA
QKV_BLOCK = (LORA + LANE, 2 * QPAD)
BR_ROWS = 3 * BW // NDEV
OUT_ROWS = D // NDEV
WUQ_COLS = 192


def _row(v):
    return v.reshape(1, -1)


def layer_fwd(xl, mod, p, wts, tabs, lgam):
    S = xl.shape[0]
    tm = min(S, 2048)
    sh, sc, rg = _row(mod[:D]), _row(mod[D:2 * D]), _row(mod[2 * D:])
    ret_tabs, mla_tabs = tabs
    h = pre_fwd(xl, _row(p["norm_pre"]), sc, sh)
    proj = matmul(h, wts["w_in"], dims="nn", M=S, N=NP, K=D, tm=tm, tn=768, tk=D, out_dtype=f32, name="mm_in")
    o_ret, y_ret = ret_attn_fwd(proj, ret_tabs, lgam, _row(p["ret_gn"]))
    h_lru, y_lru = lru_fwd(proj, p["conv_w"], _row(p["lru_conv_b"]), p["lru_wa"], _row(p["lru_ba"]), p["lru_wx"],
                           _row(p["lru_bx"]), _row(p["lru_lambda"]))
    q256, k256, vm = mla_prep_fwd(proj, _row(p["mla_q_norm"]), _row(p["mla_kv_norm"]), wts["w_qkv"], wts["w_qkv"], mla_tabs,
                                  kv_blk=1)
    o_mla, y_mla = mla_attn_fwd(q256, k256, vm, proj)
    ys = (y_ret, y_lru, y_mla)
    us = [matmul(ys[b], wts["w_branch"], dims="nn", M=S, N=D, K=BW, tm=min(S, 1024), tn=1024, tk=BW,
                 out_dtype=bf16, name="mm_branch", b_blk0=(b, 0)) for b in range(3)]
    merged = gate_fwd(proj, us)
    y, x_next = out_fwd(merged, wts["w_out"], xl, rg, _row(p["norm_post"]))
    saved = dict(x=xl, h=h, proj=proj, o_ret=o_ret, h_lru=h_lru, q256=q256, k256=k256, vm=vm, o_mla=o_mla,
                 ys=ys, us=us, merged=merged, y=y, sc=sc, rg=rg)
    return x_next, saved


def layer_bwd(dx, sv, p, wts, tabs, lgam, exchange):
    S = dx.shape[0]
    tm = min(S, 2048)
    ret_tabs, mla_tabs = tabs
    proj = sv["proj"]
    dy, d_rg, d_gpost = out_bwd(dx, sv["y"], sv["rg"], _row(p["norm_post"]))
    dmerged = matmul(dy, wts["w_out"], dims="nt", M=S, N=D, K=D, tm=min(S, 1024), tn=1024, tk=D, out_dtype=f32, name="mm_dmerged")
    dw_out = matmul(sv["merged"], dy, dims="tn", M=D, N=D, K=S, tm=1024, tn=1024, tk=min(S, 1024), out_dtype=bf16, name="mm_dwout")
    du, dml = gate_bwd(dmerged, proj, sv["us"])
    tmb, tkb = min(S, 1024), min(S, 1024)
    dys = [matmul(du[b], wts["w_branch"], dims="nt", M=S, N=BW, K=D, tm=tmb, tn=BW, tk=D, out_dtype=f32, name="mm_dybranch",
                  b_blk0=(b, 0)) for b in range(3)]
    dw_branch = jnp.concatenate(
        [matmul(sv["ys"][b], du[b], dims="tn", M=BW, N=D, K=S, tm=BW, tn=1024, tk=tkb, out_dtype=bf16, name="mm_dwbranch")
         for b in range(3)], axis=0)
    do, d_rgate, d_gn = ret_post_bwd(dys[0], sv["o_ret"], proj, _row(p["ret_gn"]))
    d_q, d_k, d_v = ret_attn_bwd(proj, do, ret_tabs, lgam)
    d_lx, d_lg, d_cw, d_cb, d_ba, d_bx, d_lam, d_wa, d_wx = lru_bwd(
        dys[1], sv["h_lru"], proj, p["conv_w"], _row(p["lru_conv_b"]), p["lru_wa"], _row(p["lru_ba"]), p["lru_wx"],
        _row(p["lru_bx"]), _row(p["lru_lambda"]))
    do, d_mg = mla_post_bwd(dys[2], sv["o_mla"], proj)
    dq256, dk256, dvm = mla_attn_bwd(sv["q256"], sv["k256"], sv["vm"], do, sv["o_mla"])
    d_lat, dw_uq, dw_ukv, d_qn, d_kvn = mla_prep_bwd(dq256, dk256, dvm, proj, _row(p["mla_q_norm"]), _row(p["mla_kv_norm"]),
                                                       wts["w_qkv"], wts["w_qkv"], mla_tabs, kv_blk=1)
    dproj = jnp.concatenate([d_q, d_k, d_v, d_rgate, d_lx, d_lg, d_lat, d_mg, *dml, jnp.zeros((S, LANE), bf16)], axis=1)
    dh = matmul(dproj, wts["w_in"], dims="nt", M=S, N=D, K=NP, tm=min(S, 1024), tn=1024, tk=NP // 6, out_dtype=f32, name="mm_dh")
    dw_in = matmul(sv["h"], dproj, dims="tn", M=D, N=NP, K=S, tm=D, tn=768, tk=tm, out_dtype=bf16, name="mm_dwin")
    dw_qkv = jnp.concatenate([jnp.concatenate([dw_uq, dw_ukv], axis=2),
                              jnp.concatenate([d_wa, d_wx, jnp.zeros((HEADS, LANE, QPAD), f32)], axis=2)], axis=1).astype(bf16)
    staged, launched_from = exchange(dict(w_in=dw_in, w_branch=dw_branch, w_out=dw_out, w_qkv=dw_qkv))
    dxl, d_sh, d_sc, d_gpre = pre_bwd(dh, sv["x"], _row(p["norm_pre"]), sv["sc"], dx, after=tuple(launched_from))
    dmod = jnp.concatenate([d_sh, d_sc, d_rg], axis=1).reshape(-1)
    small = dict(norm_pre=d_gpre, norm_post=d_gpost, ret_gn=d_gn, lru_conv_w=d_cw, lru_conv_b=d_cb, lru_ba=d_ba,
                 lru_bx=d_bx, lru_lambda=d_lam, mla_q_norm=d_qn, mla_kv_norm=d_kvn)
    return dxl, dmod, staged, small


def exchange_grads(big, l):
    picks = [lambda r, d: r.at[:, pl.ds(pl.multiple_of(d * (WIN_STRIDE * LANE), LANE), WIN)],
             lambda r, d: r.at[pl.ds(pl.multiple_of(d * BR_ROWS, 8), BR_ROWS), :],
             lambda r, d: r.at[pl.ds(pl.multiple_of(d * OUT_ROWS, 8), OUT_ROWS), :],
             lambda r, d: r.at[d]]
    shapes = [(D, WIN), (BR_ROWS, D), (OUT_ROWS, D), QKV_BLOCK]
    arrs = [big["w_in"], big["w_branch"], big["w_out"], big["w_qkv"]]
    return seq_reduce_scatter_parts(arrs, picks, shapes, f"rs_grads_{l}"), arrs


def kernel(x, c, positions, ada_w, ada_b, norm_pre, norm_post, w_in, ret_gn, lru_conv_w, lru_conv_b, lru_wa, lru_ba, lru_wx, lru_bx, lru_lambda, mla_q_norm, mla_w_uq, mla_kv_norm, mla_w_ukv, w_branch, w_out, loss_target, m_ada_w, m_ada_b, m_norm_pre, m_norm_post, m_w_in, m_ret_gn, m_lru_conv_w, m_lru_conv_b, m_lru_wa, m_lru_ba, m_lru_wx, m_lru_bx, m_lru_lambda, m_mla_q_norm, m_mla_w_uq, m_mla_kv_norm, m_mla_w_ukv, m_w_branch, m_w_out, v_ada_w, v_ada_b, v_norm_pre, v_norm_post, v_w_in, v_ret_gn, v_lru_conv_w, v_lru_conv_b, v_lru_wa, v_lru_ba, v_lru_wx, v_lru_bx, v_lru_lambda, v_mla_q_norm, v_mla_w_uq, v_mla_kv_norm, v_mla_w_ukv, v_w_branch, v_w_out):
    given = dict(locals())
    xi, yi, ci = _place()
    me = 4 * xi + 2 * yi + ci
    S = x.shape[1]
    sidx = shift_params(me)
    lgam = jnp.asarray(np.log1p(-np.exp2(-5.0 - np.arange(HEADS))), f32)
    tabs = rope_tables(positions[0])

    (g_small,) = all_gather([jnp.concatenate([c.reshape(16, LANE), lru_conv_w.reshape(16, LANE)], axis=0)], "ag_small")
    c16 = jnp.concatenate([g_small[:, :16].reshape(NDEV, D), jnp.zeros((ROWS16 - NDEV, D), f32)], axis=0)
    conv_w_all = g_small[:, 16:].reshape(NDEV, DEPTH, 4, LANE).transpose(1, 2, 0, 3).reshape(DEPTH, 4, BW)
    ada_b_mine = lax.dynamic_slice_in_dim(ada_b, me * ADA_SHARD, ADA_SHARD, axis=1).reshape(DEPTH, 1, ADA_SHARD)
    (g_mod,) = all_gather([ada_fwd(c16, ada_w, ada_b_mine)[:, :NDEV]], "ag_mod")
    mods = lax.dynamic_index_in_dim(g_mod, me, axis=2, keepdims=False).transpose(1, 0, 2).reshape(DEPTH, 3 * D)
    w_in_t = {n: jnp.swapaxes(given[n], 1, 2) for n in ("w_in", "m_w_in", "v_w_in")}
    win0 = win_pack(w_in_t["w_in"], sidx, count=1, after=(g_mod,))
    win_rest = win_pack(w_in_t["w_in"], sidx, first=1, count=DEPTH - 1, after=(win0,))
    wbr_p, wout_p, qkv_p = cast_pad(w_branch, D, "pack_wbranch"), cast_pad(w_out, D, "pack_wout"), pack_qkv(mla_w_uq, mla_w_ukv)
    first, rest = seq_all_gather([win0[0], qkv_p[0]], "ag_weights_0a"), seq_all_gather([wbr_p[0], wout_p[0]], "ag_weights_0b")
    gathered = [(first[0], rest[0], rest[1], first[1])]
    gathered += [seq_all_gather([win_rest[l - 1], wbr_p[l], wout_p[l], qkv_p[l]], f"ag_weights_{l}") for l in range(1, DEPTH)]

    params, wts = [], []
    for l in range(DEPTH):
        p = {n: given[n][l] for n in SMALL + ("lru_wa", "lru_wx") if n != "lru_conv_w"}
        p["conv_w"] = conv_w_all[l]
        params.append(p)

    xl, saved = x[0], []
    for l in range(DEPTH):
        g_win, g_br, g_out, g_qkv = gathered[l]
        wts.append(dict(w_in=win_assemble(g_win, after=(xl,)), w_qkv=g_qkv, w_branch=g_br.reshape(3 * BW, D),
                        w_out=g_out.reshape(D, D)))
        xl, sv = layer_fwd(xl, mods[l], params[l], wts[l], tabs, lgam)
        saved.append(sv)
    my_loss, dx = loss_head(xl, loss_target[0])
    loss = lax.psum(my_loss[0, 0], ("x", "y", "c"))

    dmods, smalls, staged, grads = [None] * DEPTH, [None] * DEPTH, [None] * DEPTH, {n: [None] * DEPTH for n in WEIGHTS}
    for l in reversed(range(DEPTH)):
        dx, dmods[l], staged[l], smalls[l] = layer_bwd(dx, saved[l], params[l], wts[l], tabs, lgam,
                                                       functools.partial(exchange_grads, l=l))
    chained, lru_blocks = {"w_in": None, "w_branch": None, "w_out": None}, [None] * DEPTH
    for l in reversed(range(DEPTH)):
        st_win, st_br, st_out, st_qkv = staged[l]
        chained["w_in"] = reduce_adamw(st_win, w_in_t["w_in"], w_in_t["m_w_in"], w_in_t["v_w_in"], l, chained["w_in"],
                                       "update_w_in", sidx=sidx)
        for n, st in (("w_branch", st_br), ("w_out", st_out)):
            chained[n] = reduce_adamw(st, given[n], given["m_" + n], given["v_" + n], l, chained[n], "update_" + n)
        g_qkv = sum_parts(st_qkv, 2 * QPAD, "sum_wqkv", after=(chained["w_in"][0],) if l == 0 else ())
        grads["mla_w_uq"][l] = g_qkv[:QKV_ROWS, :WUQ_COLS]
        grads["mla_w_ukv"][l] = g_qkv[:QKV_ROWS, QPAD:]
        lru_blocks[l] = g_qkv[QKV_ROWS:, :2 * LANE]
    (g_lru,) = all_gather([jnp.stack(lru_blocks)], "ag_lru_w")
    grads["lru_wa"] = g_lru[..., :LANE].transpose(1, 0, 2, 3)
    grads["lru_wx"] = g_lru[..., LANE:].transpose(1, 0, 2, 3)

    flat = [jnp.stack(dmods).reshape(-1)] + [smalls[l][n].reshape(-1) for n in SMALL for l in range(DEPTH)]
    (g_pack,) = all_gather([jnp.concatenate(flat).reshape(-1, LANE)], "ag_small_grads")
    rows = g_pack.shape[1]
    tot = sum_parts(g_pack, LANE, "sum_small_grads", tr=rows // 8).reshape(-1)
    grads["ada_b"] = tot[:DEPTH * 3 * D].reshape(DEPTH, 3 * D)
    off = DEPTH * 3 * D
    for n in SMALL:
        size = DEPTH * int(smalls[0][n].size)
        piece = tot[off:off + size]
        off += size
        if n == "lru_conv_w":
            grads[n] = lax.dynamic_slice_in_dim(piece.reshape(DEPTH, 4, BW), me * LANE, LANE, axis=2)
        else:
            grads[n] = piece.reshape(given[n].shape)
    dmod_all = g_pack[:, :DEPTH * 3 * D // LANE].reshape(NDEV, DEPTH, 3 * D)
    dmod_mine = lax.dynamic_slice_in_dim(dmod_all, me * ADA_SHARD, ADA_SHARD, axis=2).transpose(1, 0, 2)
    dmod16 = jnp.concatenate([dmod_mine, jnp.zeros((DEPTH, ROWS16 - NDEV, ADA_SHARD), f32)], axis=1)
    grads["ada_w"] = ada_bwd(c16, dmod16)

    outs = {"grad": [], "delta": [], "m": [], "v": []}
    for n in WEIGHTS:
        if n in chained:
            g, delta, new_m, new_v = (jnp.swapaxes(t, 1, 2) for t in chained[n]) if n == "w_in" else chained[n]
        else:
            g = grads[n] if not isinstance(grads[n], list) else jnp.stack(grads[n])
            delta, new_m, new_v = adamw(given[n], g, given["m_" + n], given["v_" + n])
        outs["grad"].append(g)
        outs["delta"].append(delta)
        outs["m"].append(new_m)
        outs["v"].append(new_v)
    return (loss, dx[None], *outs["grad"], *outs["delta"], *outs["m"], *outs["v"])
```
